```python
import math
import jax
import jax.numpy as jnp
from jax import lax
import numpy as np

D_MODEL = 2048
BATCH = 4
SEQ = 2048
DEPTH = 2

N_META = 16
CHUNK = 64
HG_CHUNK = 16
LEAD_PAD = CHUNK - N_META
EPS = 1e-6
NEG = -1e30
LB_FLOOR = 1e-30

HG_HEADS = 4
HG_KDIM = 128
HG_VDIM = 128
HG_KEY = HG_HEADS * HG_KDIM
HG_WIDTH = HG_HEADS * HG_VDIM

ML_HEADS = 4
ML_QK = 64
ML_V = 128
ML_QK_W = ML_HEADS * ML_QK
ML_WIDTH = ML_HEADS * ML_V
GATE_CAP = 15.0

SSM_HEADS = 16
SSM_HEADDIM = 64
SSM_WIDTH = SSM_HEADS * SSM_HEADDIM
SSM_STATE = 128
SSM_GROUPS = 4
SSM_BC = SSM_GROUPS * SSM_STATE
CONV_W = 4
CONV_CH = SSM_WIDTH + 2 * SSM_BC

D_MIX = HG_WIDTH + ML_WIDTH + SSM_WIDTH
IN_SPLITS = (HG_KEY, HG_KEY, HG_WIDTH, HG_WIDTH,
             ML_QK_W, ML_QK_W, ML_WIDTH, ML_WIDTH, ML_HEADS, ML_HEADS,
             SSM_WIDTH, CONV_CH, SSM_HEADS)
D_IN = sum(IN_SPLITS)

N_GROUPS_MOE = 4
EXPERTS_PER_GROUP = 8
N_EXPERTS = N_GROUPS_MOE * EXPERTS_PER_GROUP
TOP_K = 2
D_EXPERT = 512
MOE_BLOCK = 128

kernel_name = 'hymba_hgrn2_mlstm_ssd_hmoe'


def rmsnorm(x, w):
    xf = x.astype(jnp.float32)
    y = xf * lax.rsqrt(jnp.mean(xf * xf, axis=-1, keepdims=True) + EPS)
    return (y * w.astype(jnp.float32)).astype(x.dtype)


def group_rmsnorm(y, w, groups):
    shape = y.shape
    yg = y.reshape(shape[:-1] + (groups, shape[-1] // groups))
    yg = yg * lax.rsqrt(jnp.mean(yg * yg, axis=-1, keepdims=True) + EPS)
    return yg.reshape(shape) * w


def softcap(t):
    return GATE_CAP * jnp.tanh(t / GATE_CAP)


def split_last(t, sizes):
    offsets = [int(o) for o in np.cumsum(sizes)[:-1]]
    return jnp.split(t, offsets, axis=-1)


def lead_pad(t, value=0.0):
    widths = [(0, 0)] * t.ndim
    widths[1] = (LEAD_PAD, 0)
    return jnp.pad(t, widths, constant_values=value)


def causal_mask(c):
    return jnp.tril(jnp.ones((c, c), dtype=bool))


def carry_linear_state(decay, local):
    def step(s, inp):
        a, u = inp
        return a * s + u, s
    _, prev = lax.scan(step, jnp.zeros_like(local[0]), (decay, local))
    return prev


def hgrn2_mixer(q, f_logit, inp, g, lower_bound, norm_w):
    b = q.shape[0]
    log_lb = jnp.log(jnp.maximum(lower_bound, LB_FLOOR))
    log_f = jnp.logaddexp(log_lb, jnp.log1p(-lower_bound) + jax.nn.log_sigmoid(f_logit))
    q = lead_pad(q * HG_KDIM ** -0.5)
    log_f = lead_pad(log_f)
    v = lead_pad(inp)
    k = -jnp.expm1(log_f)
    p = q.shape[1]
    nc = p // HG_CHUNK

    def chunks(t, dim):
        return t.reshape(b, nc, HG_CHUNK, HG_HEADS, dim).transpose(0, 3, 1, 2, 4)

    q, k, log_f, v = chunks(q, HG_KDIM), chunks(k, HG_KDIM), chunks(log_f, HG_KDIM), chunks(v, HG_VDIM)
    cum = jnp.cumsum(log_f, axis=3)
    rel = jnp.where(causal_mask(HG_CHUNK)[:, :, None],
                    cum[..., :, None, :] - cum[..., None, :, :], NEG)
    scores = jnp.einsum('bhntd,bhnsd,bhntsd->bhnts', q, k, jnp.exp(rel))
    o_intra = jnp.einsum('bhnts,bhnsv->bhntv', scores, v)
    last = cum[..., -1:, :]
    local = jnp.einsum('bhnsd,bhnsv->bhndv', k * jnp.exp(last - cum), v)
    decay = jnp.exp(last[..., 0, :])[..., None]
    prev = jnp.moveaxis(carry_linear_state(jnp.moveaxis(decay, 2, 0), jnp.moveaxis(local, 2, 0)), 0, 2)
    o_inter = jnp.einsum('bhntd,bhndv->bhntv', q * jnp.exp(cum), prev)
    o = (o_intra + o_inter).transpose(0, 2, 3, 1, 4).reshape(b, p, HG_WIDTH)[:, LEAD_PAD:]
    return group_rmsnorm(o, norm_w, HG_HEADS) * jax.nn.silu(g)


def mlstm_mixer(q, k, v, o_pre, i_pre, f_pre, b_i, b_f, norm_w):
    b = q.shape[0]
    log_i = softcap(i_pre + b_i)
    log_f = jax.nn.log_sigmoid(softcap(f_pre + b_f))
    q = lead_pad(q * ML_QK ** -0.5)
    k = lead_pad(k)
    v = lead_pad(v)
    log_f = lead_pad(log_f)
    log_i = lead_pad(log_i, NEG)
    p = q.shape[1]
    nc = p // CHUNK

    def chunks(t, dim):
        return t.reshape(b, nc, CHUNK, ML_HEADS, dim).transpose(0, 3, 1, 2, 4)

    def gate_chunks(t):
        return t.reshape(b, nc, CHUNK, ML_HEADS).transpose(0, 3, 1, 2)

    q, k, v = chunks(q, ML_QK), chunks(k, ML_QK), chunks(v, ML_V)
    log_f, log_i = gate_chunks(log_f), gate_chunks(log_i)
    cum = jnp.cumsum(log_f, axis=-1)
    intra = jnp.where(causal_mask(CHUNK),
                      cum[..., :, None] - cum[..., None, :] + log_i[..., None, :], NEG)
    to_end = cum[..., -1:] - cum + log_i
    m_loc = jnp.max(to_end, axis=-1)
    w_end = jnp.exp(to_end - m_loc[..., None])
    c_loc = jnp.einsum('bhns,bhnsd,bhnsv->bhndv', w_end, k, v)
    n_loc = jnp.einsum('bhns,bhnsd->bhnd', w_end, k)
    total = cum[..., -1]

    def step(carry, inp):
        c_st, n_st, m_st = carry
        tot, ml, cl, nl = inp
        m_new = jnp.maximum(tot + m_st, ml)
        a = jnp.exp(tot + m_st - m_new)
        s = jnp.exp(ml - m_new)
        c_new = a[..., None, None] * c_st + s[..., None, None] * cl
        n_new = a[..., None] * n_st + s[..., None] * nl
        return (c_new, n_new, m_new), (c_st, n_st, m_st)

    init = (jnp.zeros((b, ML_HEADS, ML_QK, ML_V), c_loc.dtype),
            jnp.zeros((b, ML_HEADS, ML_QK), c_loc.dtype),
            jnp.zeros((b, ML_HEADS), c_loc.dtype))
    xs = (jnp.moveaxis(total, 2, 0), jnp.moveaxis(m_loc, 2, 0),
          jnp.moveaxis(c_loc, 2, 0), jnp.moveaxis(n_loc, 2, 0))
    _, (c_prev, n_prev, m_prev) = lax.scan(step, init, xs)
    c_prev, n_prev, m_prev = jnp.moveaxis(c_prev, 0, 2), jnp.moveaxis(n_prev, 0, 2), jnp.moveaxis(m_prev, 0, 2)
    inter = cum + m_prev[..., None]
    m_t = jnp.maximum(inter, jnp.max(intra, axis=-1))
    w_inter = jnp.exp(inter - m_t)
    qk = jnp.einsum('bhntd,bhnsd->bhnts', q, k) * jnp.exp(intra - m_t[..., None])
    num = jnp.einsum('bhnts,bhnsv->bhntv', qk, v) + w_inter[..., None] * jnp.einsum('bhntd,bhndv->bhntv', q, c_prev)
    den = jnp.sum(qk, axis=-1) + w_inter * jnp.einsum('bhntd,bhnd->bhnt', q, n_prev)
    h = num / jnp.maximum(jnp.abs(den), jnp.exp(-m_t))[..., None]
    h = h.transpose(0, 2, 3, 1, 4).reshape(b, p, ML_WIDTH)[:, LEAD_PAD:]
    return group_rmsnorm(h, norm_w, ML_HEADS) * jax.nn.sigmoid(o_pre)


def causal_depthwise_conv(x, w):
    return lax.conv_general_dilated(x, w.astype(x.dtype)[:, None, :], window_strides=(1,),
                                    padding=[(CONV_W - 1, 0)],
                                    dimension_numbers=('NWC', 'WIO', 'NWC'),
                                    feature_group_count=x.shape[-1])


def mamba2_mixer(z, xbc, dt_pre, conv_w, conv_b, dt_bias, a_log, d_skip, norm_w):
    b = z.shape[0]
    hpg = SSM_HEADS // SSM_GROUPS
    xbc = jax.nn.silu(causal_depthwise_conv(xbc, conv_w) + conv_b)
    x, bm, cm = split_last(xbc, (SSM_WIDTH, SSM_BC, SSM_BC))
    dt = jax.nn.softplus(dt_pre + dt_bias)
    a = -jnp.exp(a_log)
    x, bm, cm, dt = lead_pad(x), lead_pad(bm), lead_pad(cm), lead_pad(dt)
    p = x.shape[1]
    nc = p // CHUNK
    x = x.reshape(b, nc, CHUNK, SSM_GROUPS, hpg, SSM_HEADDIM)
    dt = dt.reshape(b, nc, CHUNK, SSM_GROUPS, hpg)
    bm = bm.reshape(b, nc, CHUNK, SSM_GROUPS, SSM_STATE)
    cm = cm.reshape(b, nc, CHUNK, SSM_GROUPS, SSM_STATE)
    cum = jnp.cumsum(jnp.moveaxis(dt * a.reshape(SSM_GROUPS, hpg), 2, -1), axis=-1)
    xdt = x * dt[..., None]
    decay_in = jnp.exp(jnp.where(causal_mask(CHUNK), cum[..., :, None] - cum[..., None, :], NEG))
    cb = jnp.einsum('bntgk,bnsgk->bngts', cm, bm)
    y_diag = jnp.einsum('bngts,bnghts,bnsghp->bntghp', cb, decay_in, xdt)
    to_end = jnp.exp(cum[..., -1:] - cum)
    local = jnp.einsum('bnsgk,bnghs,bnsghp->bnghpk', bm, to_end, xdt)
    chunk_decay = jnp.exp(cum[..., -1])[..., None, None]
    prev = jnp.moveaxis(carry_linear_state(jnp.moveaxis(chunk_decay, 1, 0), jnp.moveaxis(local, 1, 0)), 0, 1)
    y_off = jnp.einsum('bntgk,bnght,bnghpk->bntghp', cm, jnp.exp(cum), prev)
    y = y_diag + y_off + x * d_skip.reshape(SSM_GROUPS, hpg)[..., None]
    y = y.reshape(b, p, SSM_WIDTH)[:, LEAD_PAD:]
    return group_rmsnorm(y * jax.nn.silu(z), norm_w, SSM_GROUPS)


def hierarchical_moe(u, w_group, b_group, w_router, b_router, w1, w3, w2):
    b, l, d = u.shape
    t = u.reshape(-1, d)
    n_tok = t.shape[0]
    g_prob = jax.nn.softmax((t @ w_group).astype(jnp.float32) + b_group, axis=-1)
    g_sel = jnp.argmax(g_prob, axis=-1)
    g_gate = jnp.max(g_prob, axis=-1)
    e_logits = jnp.einsum('td,gde->tge', t, w_router).astype(jnp.float32) + b_router
    e_logits = jnp.einsum('tg,tge->te', jax.nn.one_hot(g_sel, N_GROUPS_MOE, dtype=jnp.float32), e_logits)
    top_v, top_i = lax.top_k(e_logits, TOP_K)
    top_w = jax.nn.softmax(top_v, axis=-1) * g_gate[:, None]
    flat_e = (g_sel[:, None] * EXPERTS_PER_GROUP + top_i).reshape(-1).astype(jnp.int32)
    flat_w = top_w.reshape(-1)
    tk = flat_e.shape[0]
    order = jnp.argsort(flat_e).astype(jnp.int32)
    sorted_e = flat_e[order]
    counts = jnp.bincount(flat_e, length=N_EXPERTS).astype(jnp.int32)
    starts = jnp.cumsum(counts) - counts
    padded = (counts + MOE_BLOCK - 1) // MOE_BLOCK * MOE_BLOCK
    pad_ends = jnp.cumsum(padded)
    pad_starts = pad_ends - padded
    dest = pad_starts[sorted_e] + jnp.arange(tk, dtype=jnp.int32) - starts[sorted_e]
    n_blocks = -(-tk // MOE_BLOCK) + N_EXPERTS
    n_rows = n_blocks * MOE_BLOCK
    row_tok = jnp.full((n_rows,), n_tok, jnp.int32).at[dest].set(order // TOP_K)
    row_w = jnp.zeros((n_rows,), flat_w.dtype).at[dest].set(flat_w[order])
    block_e = jnp.minimum(jnp.searchsorted(pad_ends, jnp.arange(n_blocks, dtype=jnp.int32) * MOE_BLOCK,
                                           side='right'), N_EXPERTS - 1)
    t_pad = jnp.concatenate([t, jnp.zeros((1, d), t.dtype)], axis=0)
    xs = t_pad[row_tok].reshape(n_blocks, MOE_BLOCK, d)

    def expert_block(args):
        xb, e = args
        hidden = jax.nn.silu(xb @ w1[e]) * (xb @ w3[e])
        return hidden @ w2[e]

    ys = lax.map(expert_block, (xs, block_e)).reshape(n_rows, d)
    out = jnp.zeros((n_tok + 1, d), ys.dtype).at[row_tok].add(ys * row_w[:, None].astype(ys.dtype))
    return out[:n_tok].reshape(b, l, d)


def setup_inputs(seed: int = 0) -> dict:
    key = jax.random.key(seed)
    ks = jax.random.split(key, 25)

    def nrm(k, shape, scale):
        return jax.random.normal(k, shape, jnp.float32) * scale

    def gain(k, shape):
        return 1.0 + nrm(k, shape, 0.02)

    dt0 = jnp.exp(jax.random.uniform(ks[11], (DEPTH, SSM_HEADS), jnp.float32,
                                     minval=math.log(1e-3), maxval=math.log(1e-1)))
    return {
        'x': nrm(ks[0], (BATCH, SEQ, D_MODEL), 1.0),
        'meta_tokens': nrm(ks[1], (N_META, D_MODEL), 1.0),
        'hg_lb_logits': 1.0 + nrm(ks[2], (DEPTH, HG_KEY), 0.1),
        'norm_mix_w': gain(ks[3], (DEPTH, D_MODEL)),
        'w_in': nrm(ks[4], (DEPTH, D_MODEL, D_IN), D_MODEL ** -0.5),
        'hg_norm_w': gain(ks[5], (DEPTH, HG_WIDTH)),
        'ml_b_i': nrm(ks[6], (DEPTH, ML_HEADS), 0.1),
        'ml_b_f': 3.0 + 3.0 * jax.random.uniform(ks[7], (DEPTH, ML_HEADS), jnp.float32),
        'ml_norm_w': gain(ks[8], (DEPTH, ML_WIDTH)),
        'ssm_conv_w': nrm(ks[9], (DEPTH, CONV_W, CONV_CH), CONV_W ** -0.5),
        'ssm_conv_b': nrm(ks[10], (DEPTH, CONV_CH), 0.01),
        'ssm_dt_bias': dt0 + jnp.log(-jnp.expm1(-dt0)),
        'ssm_a_log': jnp.log(jax.random.uniform(ks[12], (DEPTH, SSM_HEADS), jnp.float32, minval=1.0, maxval=16.0)),
        'ssm_d': 1.0 + nrm(ks[13], (DEPTH, SSM_HEADS), 0.1),
        'ssm_norm_w': gain(ks[14], (DEPTH, SSM_WIDTH)),
        'w_out': nrm(ks[15], (DEPTH, D_MIX, D_MODEL), D_MIX ** -0.5),
        'norm_ffn_w': gain(ks[16], (DEPTH, D_MODEL)),
        'moe_w_group': nrm(ks[17], (DEPTH, D_MODEL, N_GROUPS_MOE), D_MODEL ** -0.5),
        'moe_b_group': nrm(ks[18], (DEPTH, N_GROUPS_MOE), 0.01),
        'moe_w_router': nrm(ks[19], (DEPTH, N_GROUPS_MOE, D_MODEL, EXPERTS_PER_GROUP), D_MODEL ** -0.5),
        'moe_b_router': nrm(ks[20], (DEPTH, N_GROUPS_MOE, EXPERTS_PER_GROUP), 0.01),
        'moe_w1': nrm(ks[21], (DEPTH, N_EXPERTS, D_MODEL, D_EXPERT), D_MODEL ** -0.5),
        'moe_w3': nrm(ks[22], (DEPTH, N_EXPERTS, D_MODEL, D_EXPERT), D_MODEL ** -0.5),
        'moe_w2': nrm(ks[23], (DEPTH, N_EXPERTS, D_EXPERT, D_MODEL), D_EXPERT ** -0.5),
        'final_norm_w': gain(ks[24], (D_MODEL,)),
    }


def reference(x, meta_tokens, hg_lb_logits, norm_mix_w, w_in, hg_norm_w, ml_b_i, ml_b_f, ml_norm_w,
              ssm_conv_w, ssm_conv_b, ssm_dt_bias, ssm_a_log, ssm_d, ssm_norm_w, w_out, norm_ffn_w,
              moe_w_group, moe_b_group, moe_w_router, moe_b_router, moe_w1, moe_w3, moe_w2, final_norm_w):
    b = x.shape[0]
    meta = jnp.broadcast_to(meta_tokens.astype(x.dtype)[None], (b, N_META, D_MODEL))
    h = jnp.concatenate([meta, x], axis=1)
    lb_w = jax.nn.softmax(hg_lb_logits.astype(jnp.float32), axis=0)
    lower_bounds = jnp.cumsum(lb_w, axis=0) - lb_w[0]
    for layer in range(DEPTH):
        u = rmsnorm(h, norm_mix_w[layer])
        proj = (u @ w_in[layer]).astype(jnp.float32)
        (hq, hf, hi, hg, mq, mk, mv, mo, mi, mf, sz, sxbc, sdt) = split_last(proj, IN_SPLITS)
        y_a = hgrn2_mixer(hq, hf, hi, hg, lower_bounds[layer], hg_norm_w[layer])
        y_b = mlstm_mixer(mq, mk, mv, mo, mi, mf, ml_b_i[layer], ml_b_f[layer], ml_norm_w[layer])
        y_c = mamba2_mixer(sz, sxbc, sdt, ssm_conv_w[layer], ssm_conv_b[layer], ssm_dt_bias[layer],
                           ssm_a_log[layer], ssm_d[layer], ssm_norm_w[layer])
        mixed = jnp.concatenate([y_a, y_b, y_c], axis=-1).astype(h.dtype)
        h = h + mixed @ w_out[layer]
        h = h + hierarchical_moe(rmsnorm(h, norm_ffn_w[layer]), moe_w_group[layer], moe_b_group[layer],
                                 moe_w_router[layer], moe_b_router[layer], moe_w1[layer], moe_w3[layer],
                                 moe_w2[layer])
    return rmsnorm(h, final_norm_w)[:, N_META:]
```

```python
import functools

import jax
import jax.numpy as jnp
import numpy as np
from jax import lax
from jax.experimental import pallas as pl
from jax.experimental.pallas import tpu as pltpu

F32 = jnp.float32
BF16 = jnp.bfloat16

D_MODEL = 2048
N_META = 16
CHUNK = 64
HG_CHUNK = 16
LEAD_PAD = CHUNK - N_META
EPS = 1e-6
NEG = -1e30
LB_FLOOR = 1e-30

HG_HEADS = 4
HG_KDIM = 128
HG_KEY = HG_HEADS * HG_KDIM
HG_WIDTH = HG_HEADS * 128

ML_HEADS = 4
ML_QK = 64
ML_V = 128
ML_QK_W = ML_HEADS * ML_QK
ML_WIDTH = ML_HEADS * ML_V
GATE_CAP = 15.0

SSM_HEADS = 16
SSM_HEADDIM = 64
SSM_WIDTH = SSM_HEADS * SSM_HEADDIM
SSM_STATE = 128
SSM_GROUPS = 4
SSM_HPG = SSM_HEADS // SSM_GROUPS
SSM_BC = SSM_GROUPS * SSM_STATE
CONV_W = 4

D_MIX = HG_WIDTH + ML_WIDTH + SSM_WIDTH

N_GROUPS_MOE = 4
EXPERTS_PER_GROUP = 8
N_EXPERTS = N_GROUPS_MOE * EXPERTS_PER_GROUP
TOP_K = 2
D_EXPERT = 512
MOE_BM = 128

C_HG = 0
C_SZ = 2048
C_SX = 3072
C_MV = 5120
C_MQ = 6144
C_SMALL = 6656
N_PROJ = 6912
LANE_MI = 0
LANE_MF = ML_HEADS
LANE_DT = 2 * ML_HEADS

VMEM_LIMIT = 56 * 1024 * 1024


def _cparams(sem):
    return pltpu.CompilerParams(dimension_semantics=sem, vmem_limit_bytes=VMEM_LIMIT)


def _row_tile(n, target, mult=16):
    best = None
    for t in range(mult, min(n, target) + 1, mult):
        if n % t == 0:
            best = t
    assert best is not None, (n, target, mult)
    return best


def _split3(x):
    hi = x.astype(BF16)
    r = x - hi.astype(F32)
    mid = r.astype(BF16)
    lo = (r - mid.astype(F32)).astype(BF16)
    return hi, mid, lo


def _dot(a, b):
    return jnp.dot(a, b, preferred_element_type=F32)


def _sel_dot(sel, x):
    hi, mid, lo = _split3(x)
    return _dot(sel, hi) + _dot(sel, mid) + _dot(sel, lo)


def _dot_sel(x, sel):
    hi, mid, lo = _split3(x)
    return _dot(hi, sel) + _dot(mid, sel) + _dot(lo, sel)


def _dot_nt(a, b):
    return lax.dot_general(a, b, (((1,), (1,)), ((), ())), preferred_element_type=F32)


def _dot_tn(a, b):
    return lax.dot_general(a, b, (((0,), (0,)), ((), ())), preferred_element_type=F32)


def _log_sigmoid(x):
    return jnp.minimum(x, 0.0) - jnp.log1p(jnp.exp(-jnp.abs(x)))


def _sigmoid(x):
    return 1.0 / (1.0 + jnp.exp(-x))


def _silu(x):
    return x * _sigmoid(x)


def _norm_kernel(*refs, combine, write_h):
    if combine:
        h_ref, c_ref, wt_ref, nw_ref = refs[:4]
        outs = refs[4:]
        wt = wt_ref[...]
        d = h_ref.shape[-1]
        h = h_ref[...] + wt[:, 0:1] * c_ref[:, :d] + wt[:, 1:2] * c_ref[:, d:]
    else:
        h_ref, nw_ref = refs[:2]
        outs = refs[2:]
        h = h_ref[...]
    if write_h:
        outs[0][...] = h
    ms = jnp.mean(h * h, axis=-1, keepdims=True)
    u_ref = outs[-1]
    u_ref[...] = (h * lax.rsqrt(ms + EPS) * nw_ref[...]).astype(u_ref.dtype)


def _norm_call(h, nw, contrib=None, wts=None, *, write_h, u_dtype, tm_target=264):
    t, d = h.shape
    tm = _row_tile(t, tm_target)
    combine = contrib is not None
    row = lambda i: (i, 0)
    in_specs = [pl.BlockSpec((tm, d), row)]
    args = [h]
    if combine:
        in_specs += [pl.BlockSpec((tm, 2 * d), row), pl.BlockSpec((tm, 2), row)]
        args += [contrib, wts]
    in_specs.append(pl.BlockSpec((1, d), lambda i: (0, 0)))
    args.append(nw.reshape(1, d))
    out_shape, out_specs = [], []
    if write_h:
        out_shape.append(jax.ShapeDtypeStruct((t, d), F32))
        out_specs.append(pl.BlockSpec((tm, d), row))
    out_shape.append(jax.ShapeDtypeStruct((t, d), u_dtype))
    out_specs.append(pl.BlockSpec((tm, d), row))
    return pl.pallas_call(
        functools.partial(_norm_kernel, combine=combine, write_h=write_h),
        grid=(t // tm,), in_specs=in_specs, out_specs=out_specs, out_shape=out_shape,
        compiler_params=_cparams(("arbitrary",)), name="combine_norm",
    )(*args)


def _final_kernel(h_ref, c_ref, wt_ref, nw_ref, o_ref):
    wt = wt_ref[...]
    d = h_ref.shape[-1]
    h = h_ref[...] + wt[:, 0:1] * c_ref[:, :d] + wt[:, 1:2] * c_ref[:, d:]
    ms = jnp.mean(h * h, axis=-1, keepdims=True)
    o_ref[...] = h * lax.rsqrt(ms + EPS) * nw_ref[...]


def _final_call(h, contrib, wts, nw, batch, p):
    t, d = h.shape
    seq = p - CHUNK
    tm = CHUNK
    per_b = p // tm
    n_out = seq // tm
    src = lambda b, i: (b * per_b + 1 + i, 0)
    return pl.pallas_call(
        _final_kernel, grid=(batch, n_out),
        in_specs=[pl.BlockSpec((tm, d), src), pl.BlockSpec((tm, 2 * d), src), pl.BlockSpec((tm, 2), src),
                  pl.BlockSpec((1, d), lambda b, i: (0, 0))],
        out_specs=pl.BlockSpec((None, tm, d), lambda b, i: (b, i, 0)),
        out_shape=jax.ShapeDtypeStruct((batch, seq, d), F32),
        compiler_params=_cparams(("arbitrary", "arbitrary")), name="final_norm",
    )(h, contrib, wts, nw.reshape(1, d))


def _matmul_kernel(x_ref, w_ref, o_ref):
    o_ref[...] = _dot(x_ref[...], w_ref[...])


def _inproj_call(u, w, tm_target=1056, tn=768):
    t, d = u.shape
    n = w.shape[1]
    tm = _row_tile(t, tm_target)
    assert n % tn == 0
    return pl.pallas_call(
        _matmul_kernel, grid=(t // tm, n // tn),
        in_specs=[pl.BlockSpec((tm, d), lambda i, j: (i, 0)), pl.BlockSpec((d, tn), lambda i, j: (0, j))],
        out_specs=pl.BlockSpec((tm, tn), lambda i, j: (i, j)),
        out_shape=jax.ShapeDtypeStruct((t, n), F32),
        compiler_params=_cparams(("arbitrary", "arbitrary")), name="in_proj",
    )(u, w)


def _hgrn2_kernel(q_ref, f_ref, i_ref, g_ref, par_ref, o_ref, st_ref, *, rows):
    s = pl.program_id(1)

    @pl.when(s == 0)
    def _():
        st_ref[...] = jnp.zeros_like(st_ref)

    c = HG_CHUNK
    ones = jnp.ones((HG_KDIM, 128), BF16)
    rid = lax.broadcasted_iota(jnp.int32, (c, 128), 0)
    scale = HG_KDIM ** -0.5

    def chunk(ci, carry):
        r0 = pl.multiple_of(ci * c, c)
        pad = (s * rows + r0 + rid) < LEAD_PAD
        for h in range(HG_HEADS):
            cols = slice(h * 128, (h + 1) * 128)
            a_lb = par_ref[0:1, cols]
            b_lb = par_ref[1:2, cols]
            c_lb = par_ref[2:3, cols]
            nw = par_ref[3:4, cols]
            z = f_ref[pl.ds(r0, c), cols]
            sg = _sigmoid(z)
            f = a_lb + b_lb * sg
            log_f = jnp.where(pad, 0.0, jnp.log(f))
            k = jnp.where(pad, 0.0, b_lb * (1.0 - sg) - c_lb)
            q = q_ref[pl.ds(r0, c), cols] * scale
            v = i_ref[pl.ds(r0, c), cols]
            cum = log_f
            for sh in (1, 2, 4, 8):
                cum = cum + jnp.where(rid >= sh, pltpu.roll(cum, sh, axis=0), 0.0)
            parts = []
            for s_ in range(c):
                rel = jnp.where(rid >= s_, cum - cum[s_:s_ + 1, :], NEG)
                parts.append((q * (k[s_:s_ + 1, :] * jnp.exp(rel))).astype(BF16))
            sc = _dot(jnp.concatenate(parts, axis=0), ones)
            o = jnp.zeros((c, 128), F32)
            for s_ in range(c):
                o = o + sc[s_ * c:(s_ + 1) * c, :] * v[s_:s_ + 1, :]
            st = st_ref[h]
            o = o + _dot_nt((q * jnp.exp(cum)).astype(BF16), st.astype(BF16))
            last = cum[c - 1:c, :]
            kd = (k * jnp.exp(last - cum)).astype(BF16)
            st_ref[h] = st * jnp.exp(last) + _dot_tn(v.astype(BF16), kd)
            ms = jnp.mean(o * o, axis=-1, keepdims=True)
            g = g_ref[pl.ds(r0, c), cols]
            o_ref[pl.ds(r0, c), cols] = (o * lax.rsqrt(ms + EPS) * nw * _silu(g)).astype(o_ref.dtype)
        return carry

    lax.fori_loop(0, rows // c, chunk, 0)


def _hgrn2_call(proj, par, batch, p):
    t = proj.shape[0]
    rows = _row_tile(p, 528)
    nb = p // rows
    w = HG_KEY
    blk = lambda j: pl.BlockSpec((rows, w), lambda b, s, j=j: (b * nb + s, C_HG // w + j))
    return pl.pallas_call(
        functools.partial(_hgrn2_kernel, rows=rows), grid=(batch, nb),
        in_specs=[blk(0), blk(1), blk(2), blk(3), pl.BlockSpec((8, w), lambda b, s: (0, 0))],
        out_specs=pl.BlockSpec((rows, w), lambda b, s: (b * nb + s, 0)),
        out_shape=jax.ShapeDtypeStruct((t, HG_WIDTH), BF16),
        scratch_shapes=[pltpu.VMEM((HG_HEADS, 128, HG_KDIM), F32)],
        compiler_params=_cparams(("arbitrary", "arbitrary")), name="hgrn2",
    )(proj, proj, proj, proj, par)


def _mlstm_kernel(v_ref, o_ref, q_ref, k_ref, sm_ref, par_ref, nw_ref, tri_ref, sel_ref, dg_ref,
                  y_ref, c_ref, m_ref):
    s = pl.program_id(1)

    @pl.when(s == 0)
    def _():
        c_ref[...] = jnp.zeros_like(c_ref)
        m_ref[...] = jnp.zeros_like(m_ref)

    n = CHUNK
    rid = lax.broadcasted_iota(jnp.int32, (n, 128), 0)
    pad = (s * n + rid) < LEAD_PAD
    pre = sm_ref[...] + par_ref[0:1, :]
    cap = GATE_CAP * jnp.tanh(pre * (1.0 / GATE_CAP))
    log_i = jnp.where(pad, NEG, cap)
    log_f = jnp.where(pad, 0.0, _log_sigmoid(cap))
    cum = _sel_dot(tri_ref[...], log_f)
    sel_i = sel_ref[0]
    sel_f = sel_ref[1]
    dg = dg_ref[...]
    ones = jnp.ones((n, n), BF16)
    cum_col = _dot_sel(cum, sel_f)
    cum_row = _sel_dot(ones, cum_col * dg)
    li_row = _sel_dot(ones, _dot_sel(log_i, sel_i) * dg)
    w = ML_HEADS * n
    tt = lax.broadcasted_iota(jnp.int32, (n, w), 0)
    ss = lax.broadcasted_iota(jnp.int32, (n, w), 1) & (n - 1)
    dmat = jnp.where(tt >= ss, cum_col - cum_row + li_row, NEG)
    scale = ML_QK ** -0.5
    lane = lax.broadcasted_iota(jnp.int32, (n, 128), 1)
    one_col = jnp.where(lane == 0, 1.0, 0.0).astype(BF16)
    for h in range(ML_HEADS):
        d_h = dmat[:, h * n:(h + 1) * n]
        cum_h = cum[:, LANE_MF + h:LANE_MF + h + 1]
        li_h = log_i[:, LANE_MI + h:LANE_MI + h + 1]
        m_st = m_ref[h:h + 1, 0:1]
        q = (q_ref[:, h * ML_QK:(h + 1) * ML_QK] * scale).astype(BF16)
        k = k_ref[:, h * ML_QK:(h + 1) * ML_QK]
        v_aug = jnp.concatenate([v_ref[:, h * ML_V:(h + 1) * ML_V].astype(BF16), one_col], axis=1)
        inter = cum_h + m_st
        m_t = jnp.maximum(inter, jnp.max(d_h, axis=-1, keepdims=True))
        w_inter = jnp.exp(inter - m_t)
        qk = _dot_nt(q, k.astype(BF16)) * jnp.exp(d_h - m_t)
        c_prev = c_ref[h]
        nd = _dot(qk.astype(BF16), v_aug) + w_inter * _dot(q, c_prev.astype(BF16))
        num = nd[:, :ML_V]
        den = nd[:, ML_V:ML_V + 1]
        hh = num / jnp.maximum(jnp.abs(den), jnp.exp(-m_t))
        tot = cum_h[n - 1:n, :]
        to_end = tot - cum_h + li_h
        m_loc = jnp.max(to_end, axis=0, keepdims=True)
        kw = (k * jnp.exp(to_end - m_loc)).astype(BF16)
        c_loc = _dot_tn(kw, v_aug)
        m_new = jnp.maximum(tot + m_st, m_loc)
        c_ref[h] = jnp.exp(tot + m_st - m_new) * c_prev + jnp.exp(m_loc - m_new) * c_loc
        m_ref[h:h + 1, :] = jnp.broadcast_to(m_new, (1, 128))
        ms = jnp.mean(hh * hh, axis=-1, keepdims=True)
        cols = slice(h * ML_V, (h + 1) * ML_V)
        y_ref[:, cols] = (hh * lax.rsqrt(ms + EPS) * nw_ref[:, cols] * _sigmoid(o_ref[:, cols])).astype(y_ref.dtype)


def _lane_select(lane0, heads, width):
    m = np.zeros((128, heads * width), np.float32)
    for h in range(heads):
        m[lane0 + h, h * width:(h + 1) * width] = 1.0
    return m


def _diag_mask(n, heads):
    return np.tile(np.eye(n, dtype=np.float32), (1, heads))


def _mlstm_call(proj, par, nw, batch, p):
    t = proj.shape[0]
    n = CHUNK
    nc = p // n
    tri = jnp.asarray(np.tril(np.ones((n, n), np.float32)), BF16)
    sel = jnp.asarray(np.stack([_lane_select(LANE_MI, ML_HEADS, n), _lane_select(LANE_MF, ML_HEADS, n)]), BF16)
    dg = jnp.asarray(_diag_mask(n, ML_HEADS), F32)
    blk = lambda w, off: pl.BlockSpec((n, w), lambda b, s: (b * nc + s, off // w))
    const = lambda shape: pl.BlockSpec(shape, lambda b, s: (0,) * len(shape))
    return pl.pallas_call(
        _mlstm_kernel, grid=(batch, nc),
        in_specs=[blk(ML_WIDTH, C_MV), blk(ML_WIDTH, C_MV + ML_WIDTH), blk(ML_QK_W, C_MQ),
                  blk(ML_QK_W, C_MQ + ML_QK_W), blk(128, C_SMALL),
                  const((8, 128)), const((1, ML_WIDTH)), const((n, n)), const((2, 128, ML_HEADS * n)),
                  const((n, ML_HEADS * n))],
        out_specs=pl.BlockSpec((n, ML_WIDTH), lambda b, s: (b * nc + s, 0)),
        out_shape=jax.ShapeDtypeStruct((t, ML_WIDTH), BF16),
        scratch_shapes=[pltpu.VMEM((ML_HEADS, ML_QK, 2 * ML_V), F32), pltpu.VMEM((8, 128), F32)],
        compiler_params=_cparams(("arbitrary", "arbitrary")), name="mlstm",
    )(proj, proj, proj, proj, proj, par, nw, tri, sel, dg)


def _ssd_kernel(z_ref, x_ref, b_ref, c_ref, sm_ref, cw_ref, cb_ref, par_ref, dsk_ref, nw_ref,
                tri_ref, sel_ref, dg_ref, y_ref, xs_ref, bs_ref, cs_ref, st_ref):
    s = pl.program_id(1)
    n = CHUNK
    tail = 8

    @pl.when(s == 0)
    def _():
        st_ref[...] = jnp.zeros_like(st_ref)
        xs_ref[0:tail, :] = jnp.zeros((tail, xs_ref.shape[1]), F32)
        bs_ref[0:tail, :] = jnp.zeros((tail, bs_ref.shape[1]), F32)
        cs_ref[0:tail, :] = jnp.zeros((tail, cs_ref.shape[1]), F32)

    def conv_silu(src_ref, scr_ref, c0, width, rowmask):
        scr_ref[tail:tail + n, :] = src_ref[...]
        acc = cb_ref[:, c0:c0 + width]
        for j in range(CONV_W):
            off = tail - (CONV_W - 1) + j
            acc = acc + cw_ref[j:j + 1, c0:c0 + width] * scr_ref[off:off + n, :]
        scr_ref[0:tail, :] = scr_ref[n:n + tail, :]
        return jnp.where(rowmask, 0.0, _silu(acc))

    def padmask(width):
        return (s * n + lax.broadcasted_iota(jnp.int32, (n, width), 0)) < LEAD_PAD

    x = conv_silu(x_ref, xs_ref, 0, SSM_WIDTH, padmask(SSM_WIDTH))
    bm = conv_silu(b_ref, bs_ref, SSM_WIDTH, SSM_BC, padmask(SSM_BC))
    cm = conv_silu(c_ref, cs_ref, SSM_WIDTH + SSM_BC, SSM_BC, padmask(SSM_BC))

    pre = sm_ref[...] + par_ref[0:1, :]
    dt = jnp.maximum(pre, 0.0) + jnp.log1p(jnp.exp(-jnp.abs(pre)))
    dt = jnp.where(padmask(128), 0.0, dt)
    da = dt * (-jnp.exp(par_ref[1:2, :]))
    cum = _sel_dot(tri_ref[...], da)
    sel = sel_ref[...]
    ones = jnp.ones((n, n), BF16)
    dt_col = _dot_sel(dt, sel)
    cum_col = _dot_sel(cum, sel)
    cum_row = _sel_dot(ones, cum_col * dg_ref[...])
    w = SSM_WIDTH
    tt = lax.broadcasted_iota(jnp.int32, (n, w), 0)
    ss = lax.broadcasted_iota(jnp.int32, (n, w), 1) & (n - 1)
    decay = jnp.exp(jnp.where(tt >= ss, cum_col - cum_row, NEG))
    xdt = x * dt_col
    last = cum_col[n - 1:n, :]
    wend = (xdt * jnp.exp(last - cum_col)).astype(BF16)
    chunk_decay = jnp.exp(last)
    ecum = jnp.exp(cum_col)
    gw = SSM_HPG * SSM_HEADDIM
    rr = lax.broadcasted_iota(jnp.int32, (gw, gw), 0) // SSM_HEADDIM
    cc = lax.broadcasted_iota(jnp.int32, (gw, gw), 1) // SSM_HEADDIM
    blockdiag = rr == cc
    ys = []
    for g in range(SSM_GROUPS):
        gl = slice(g * gw, (g + 1) * gw)
        sl = slice(g * SSM_STATE, (g + 1) * SSM_STATE)
        cm_g = cm[:, sl].astype(BF16)
        bm_g = bm[:, sl].astype(BF16)
        cb = _dot_nt(cm_g, jnp.concatenate([bm_g] * SSM_HPG, axis=0))
        m = (cb * decay[:, gl]).astype(BF16)
        xdt_g = xdt[:, gl]
        bd = jnp.where(blockdiag, jnp.concatenate([xdt_g] * SSM_HPG, axis=0), 0.0).astype(BF16)
        st = st_ref[g]
        y_g = _dot(m, bd) + ecum[:, gl] * _dot(cm_g, st.astype(BF16))
        st_ref[g] = st * chunk_decay[:, gl] + _dot_tn(bm_g, wend[:, gl])
        ys.append(y_g)
    y = jnp.concatenate(ys, axis=1) + x * dsk_ref[...]
    y = y * _silu(z_ref[...])
    outs = []
    for g in range(SSM_GROUPS):
        gl = slice(g * gw, (g + 1) * gw)
        y_g = y[:, gl]
        ms = jnp.mean(y_g * y_g, axis=-1, keepdims=True)
        outs.append(y_g * lax.rsqrt(ms + EPS))
    y_ref[...] = (jnp.concatenate(outs, axis=1) * nw_ref[...]).astype(y_ref.dtype)


def _ssd_call(proj, cw, cb, par, dsk, nw, batch, p):
    t = proj.shape[0]
    n = CHUNK
    nc = p // n
    tri = jnp.asarray(np.tril(np.ones((n, n), np.float32)), BF16)
    sel = jnp.asarray(_lane_select(LANE_DT, SSM_HEADS, SSM_HEADDIM), BF16)
    dg = jnp.asarray(_diag_mask(n, SSM_HEADS), F32)
    blk = lambda w, off: pl.BlockSpec((n, w), lambda b, s: (b * nc + s, off // w))
    const = lambda shape: pl.BlockSpec(shape, lambda b, s: (0,) * len(shape))
    cch = SSM_WIDTH + 2 * SSM_BC
    return pl.pallas_call(
        _ssd_kernel, grid=(batch, nc),
        in_specs=[blk(SSM_WIDTH, C_SZ), blk(SSM_WIDTH, C_SX), blk(SSM_BC, C_SX + SSM_WIDTH),
                  blk(SSM_BC, C_SX + SSM_WIDTH + SSM_BC), blk(128, C_SMALL),
                  const((CONV_W, cch)), const((1, cch)), const((8, 128)), const((1, SSM_WIDTH)),
                  const((1, SSM_WIDTH)), const((n, n)), const((128, SSM_WIDTH)), const((n, SSM_WIDTH))],
        out_specs=pl.BlockSpec((n, SSM_WIDTH), lambda b, s: (b * nc + s, 0)),
        out_shape=jax.ShapeDtypeStruct((t, SSM_WIDTH), BF16),
        scratch_shapes=[pltpu.VMEM((n + 8, SSM_WIDTH), F32), pltpu.VMEM((n + 8, SSM_BC), F32),
                        pltpu.VMEM((n + 8, SSM_BC), F32),
                        pltpu.VMEM((SSM_GROUPS, SSM_STATE, SSM_HPG * SSM_HEADDIM), F32)],
        compiler_params=_cparams(("arbitrary", "arbitrary")), name="ssd",
    )(proj, proj, proj, proj, proj, cw, cb, par, dsk, nw, tri, sel, dg)


def _outproj_kernel(ya_ref, yb_ref, yc_ref, h_ref, w_ref, nw_ref, rh_ref, rl_ref, hm_ref, u_ref, lg_ref):
    a0, a1 = HG_WIDTH, HG_WIDTH + ML_WIDTH
    h = h_ref[...]
    h = h + _dot(ya_ref[...], w_ref[0:a0, :])
    h = h + _dot(yb_ref[...], w_ref[a0:a1, :])
    h = h + _dot(yc_ref[...], w_ref[a1:, :])
    hm_ref[...] = h
    ms = jnp.mean(h * h, axis=-1, keepdims=True)
    u = h * lax.rsqrt(ms + EPS) * nw_ref[...]
    u_ref[...] = u
    u_hi = u.astype(BF16)
    u_lo = (u - u_hi.astype(F32)).astype(BF16)
    lg_ref[...] = _dot(u_hi, rh_ref[...]) + (_dot(u_lo, rh_ref[...]) + _dot(u_hi, rl_ref[...]))


def _outproj_call(ya, yb, yc, h, w, nw, r_hi, r_lo, tm_target=352):
    t, d = h.shape
    tm = _row_tile(t, tm_target)
    row = lambda i: (i, 0)
    const = lambda shape: pl.BlockSpec(shape, lambda i: (0, 0))
    return pl.pallas_call(
        _outproj_kernel, grid=(t // tm,),
        in_specs=[pl.BlockSpec((tm, HG_WIDTH), row), pl.BlockSpec((tm, ML_WIDTH), row),
                  pl.BlockSpec((tm, SSM_WIDTH), row), pl.BlockSpec((tm, d), row),
                  const((D_MIX, d)), const((1, d)), const((d, 128)), const((d, 128))],
        out_specs=[pl.BlockSpec((tm, d), row), pl.BlockSpec((tm, d), row), pl.BlockSpec((tm, 128), row)],
        out_shape=[jax.ShapeDtypeStruct((t, d), F32), jax.ShapeDtypeStruct((t, d), F32),
                   jax.ShapeDtypeStruct((t, 128), F32)],
        compiler_params=_cparams(("arbitrary",)), name="out_proj_router",
    )(ya, yb, yc, h, w, nw.reshape(1, d), r_hi, r_lo)


def _moe_kernel(be_ref, nu_ref, src0_ref, srcn_ref, dst_ref, u_hbm, w1_ref, w3_ref, w2_ref, o_hbm,
                xbuf, ybuf, w1b, w3b, w2b, gsem, ssem):
    i = pl.program_id(0)
    bm = MOE_BM
    n_used = nu_ref[0]
    slot = i % 2

    def gather(idx_ref, sl):
        def body(r, c):
            tok = idx_ref[0, r]
            pltpu.make_async_copy(u_hbm.at[pl.ds(tok, 1), :], xbuf.at[sl, pl.ds(r, 1), :], gsem.at[sl]).start()
            return c
        lax.fori_loop(0, bm, body, 0)

    def scatter(sl):
        def body(r, c):
            row = dst_ref[0, r]
            pltpu.make_async_copy(ybuf.at[sl, pl.ds(r, 1), :], o_hbm.at[pl.ds(row, 1), :], ssem.at[sl]).start()
            return c
        lax.fori_loop(0, bm, body, 0)

    def wait_gather(sl):
        pltpu.make_async_copy(u_hbm.at[pl.ds(0, bm), :], xbuf.at[sl], gsem.at[sl]).wait()

    def wait_scatter(sl):
        pltpu.make_async_copy(ybuf.at[sl], o_hbm.at[pl.ds(0, bm), :], ssem.at[sl]).wait()

    @pl.when(i == 0)
    def _():
        gather(src0_ref, 0)
        ybuf[...] = jnp.zeros_like(ybuf)
        n_real = o_hbm.shape[0] - 2 * bm
        for sl in range(2):
            fill = pltpu.make_async_copy(ybuf.at[sl], o_hbm.at[pl.ds(n_real + sl * bm, bm), :], ssem.at[sl])
            fill.start()
            fill.wait()

    @pl.when(i + 1 < n_used)
    def _():
        gather(srcn_ref, 1 - slot)

    @pl.when(i < n_used)
    def _():
        first = jnp.logical_or(i == 0, be_ref[i] != be_ref[jnp.maximum(i - 1, 0)])

        @pl.when(first)
        def _():
            w1b[...] = w1_ref[...].astype(BF16)
            w3b[...] = w3_ref[...].astype(BF16)
            w2b[...] = w2_ref[...].astype(BF16)

        wait_gather(slot)
        x = xbuf[slot].astype(BF16)
        hid = _silu(_dot(x, w1b[...])) * _dot(x, w3b[...])
        y = _dot(hid.astype(BF16), w2b[...])

        @pl.when(i >= 2)
        def _():
            wait_scatter(slot)

        ybuf[slot] = y
        scatter(slot)

    @pl.when(i == n_used - 1)
    def _():
        wait_scatter(slot)

        @pl.when(i >= 1)
        def _():
            wait_scatter(1 - slot)


def _moe_call(block_e, n_used, src_tok, dst_row, u, w1, w3, w2, n_out_rows):
    t, d = u.shape
    f = w1.shape[-1]
    bm = MOE_BM
    nb = block_e.shape[0]
    wmap = lambda i, be, nu: (be[i], 0, 0)
    src_tok = src_tok.reshape(nb, 1, bm)
    dst_row = dst_row.reshape(nb, 1, bm)
    idx = lambda fn: pl.BlockSpec((None, 1, bm), fn, memory_space=pltpu.SMEM)
    return pl.pallas_call(
        _moe_kernel,
        grid_spec=pltpu.PrefetchScalarGridSpec(
            num_scalar_prefetch=2, grid=(nb,),
            in_specs=[idx(lambda i, be, nu: (0, 0, 0)),
                      idx(lambda i, be, nu: (jnp.minimum(i + 1, nb - 1), 0, 0)),
                      idx(lambda i, be, nu: (i, 0, 0)),
                      pl.BlockSpec(memory_space=pl.ANY),
                      pl.BlockSpec((None, d, f), wmap), pl.BlockSpec((None, d, f), wmap),
                      pl.BlockSpec((None, f, d), wmap)],
            out_specs=pl.BlockSpec(memory_space=pl.ANY),
            scratch_shapes=[pltpu.VMEM((2, bm, d), F32), pltpu.VMEM((2, bm, d), F32),
                            pltpu.VMEM((d, f), BF16), pltpu.VMEM((d, f), BF16), pltpu.VMEM((f, d), BF16),
                            pltpu.SemaphoreType.DMA((2,)), pltpu.SemaphoreType.DMA((2,))]),
        out_shape=jax.ShapeDtypeStruct((n_out_rows, d), F32),
        compiler_params=_cparams(("arbitrary",)), name="moe_ffn",
    )(block_e, n_used, src_tok, src_tok, dst_row, u, w1, w3, w2)


def _route(logits, b_group, b_router, t):
    bm = MOE_BM
    g_logit = logits[:, :N_GROUPS_MOE] + b_group
    g_prob = jax.nn.softmax(g_logit, axis=-1)
    g_sel = jnp.argmax(g_prob, axis=-1)
    g_gate = jnp.max(g_prob, axis=-1)
    e_all = logits[:, N_GROUPS_MOE:N_GROUPS_MOE + N_EXPERTS].reshape(t, N_GROUPS_MOE, EXPERTS_PER_GROUP) + b_router
    e_logits = jnp.take_along_axis(e_all, g_sel[:, None, None], axis=1)[:, 0]
    top_v, top_i = lax.top_k(e_logits, TOP_K)
    top_w = jax.nn.softmax(top_v, axis=-1) * g_gate[:, None]
    flat_e = (g_sel[:, None] * EXPERTS_PER_GROUP + top_i).reshape(-1).astype(jnp.int32)
    tk = t * TOP_K
    order = jnp.argsort(flat_e).astype(jnp.int32)
    sorted_e = flat_e[order]
    counts = jnp.bincount(flat_e, length=N_EXPERTS).astype(jnp.int32)
    starts = jnp.cumsum(counts) - counts
    padded = (counts + bm - 1) // bm * bm
    pad_ends = jnp.cumsum(padded)
    pad_starts = pad_ends - padded
    dest = pad_starts[sorted_e] + jnp.arange(tk, dtype=jnp.int32) - starts[sorted_e]
    n_blocks = -(-tk // bm) + N_EXPERTS
    n_rows = n_blocks * bm
    dummy = tk + (jnp.arange(n_rows, dtype=jnp.int32) % (2 * bm))
    src_tok = jnp.zeros((n_rows,), jnp.int32).at[dest].set(order // TOP_K)
    dst_row = dummy.at[dest].set(order)
    block_e = jnp.minimum(jnp.searchsorted(pad_ends, jnp.arange(n_blocks, dtype=jnp.int32) * bm, side='right'),
                          N_EXPERTS - 1).astype(jnp.int32)
    n_used = (pad_ends[-1] // bm).astype(jnp.int32).reshape(1)
    last_e = block_e[jnp.maximum(n_used[0] - 1, 0)]
    block_e = jnp.where(jnp.arange(n_blocks) < n_used[0], block_e, last_e)
    return block_e, n_used, src_tok, dst_row, top_w, tk + 2 * bm


def _lane_row(pairs):
    row = jnp.zeros((128,), F32)
    for lane0, vals in pairs:
        row = row.at[lane0:lane0 + vals.shape[0]].set(vals.astype(F32))
    return row


def kernel(x, meta_tokens, hg_lb_logits, norm_mix_w, w_in, hg_norm_w, ml_b_i, ml_b_f, ml_norm_w,
           ssm_conv_w, ssm_conv_b, ssm_dt_bias, ssm_a_log, ssm_d, ssm_norm_w, w_out, norm_ffn_w,
           moe_w_group, moe_b_group, moe_w_router, moe_b_router, moe_w1, moe_w3, moe_w2, final_norm_w):
    batch, seq, d = x.shape
    depth = w_in.shape[0]
    p = LEAD_PAD + N_META + seq
    t = batch * p
    meta = jnp.broadcast_to(meta_tokens.astype(x.dtype)[None], (batch, N_META, d))
    h = jnp.concatenate([jnp.zeros((batch, LEAD_PAD, d), x.dtype), meta, x], axis=1).reshape(t, d)

    lb_w = jax.nn.softmax(hg_lb_logits.astype(F32), axis=0)
    lower_bounds = jnp.cumsum(lb_w, axis=0) - lb_w[0]

    contrib = wts = None
    for layer in range(depth):
        w = w_in[layer]
        o_mq, o_mv, o_mi, o_sz = HG_KEY * 2 + HG_WIDTH * 2, 2560, 3584, 3592
        o_sx = o_sz + SSM_WIDTH
        o_dt = o_sx + SSM_WIDTH + 2 * SSM_BC
        w_perm = jnp.concatenate([
            w[:, :o_mq], w[:, o_sz:o_sx], w[:, o_sx:o_dt], w[:, o_mv:o_mi], w[:, o_mq:o_mv],
            w[:, o_mi:o_sz], w[:, o_dt:], jnp.zeros((d, N_PROJ - C_SMALL - 2 * ML_HEADS - SSM_HEADS), w.dtype)],
            axis=1).astype(BF16)
        lb = lower_bounds[layer]
        lbf = jnp.maximum(lb, LB_FLOOR)
        hg_par = jnp.zeros((8, HG_KEY), F32).at[0].set(lbf).at[1].set(1.0 - lb).at[2].set(lbf - lb).at[3].set(hg_norm_w[layer])
        ml_par = jnp.zeros((8, 128), F32).at[0].set(_lane_row([(LANE_MI, ml_b_i[layer]), (LANE_MF, ml_b_f[layer])]))
        ss_par = jnp.zeros((8, 128), F32).at[0].set(_lane_row([(LANE_DT, ssm_dt_bias[layer])]))
        ss_par = ss_par.at[1].set(_lane_row([(LANE_DT, ssm_a_log[layer])]))
        dskip = jnp.repeat(ssm_d[layer].astype(F32), SSM_HEADDIM).reshape(1, SSM_WIDTH)
        w_r = jnp.concatenate([moe_w_group[layer],
                               moe_w_router[layer].transpose(1, 0, 2).reshape(d, N_EXPERTS),
                               jnp.zeros((d, 128 - N_GROUPS_MOE - N_EXPERTS), F32)], axis=1)
        r_hi = w_r.astype(BF16)
        r_lo = (w_r - r_hi.astype(F32)).astype(BF16)

        if layer == 0:
            (u,) = _norm_call(h, norm_mix_w[layer], write_h=False, u_dtype=BF16)
        else:
            h, u = _norm_call(h, norm_mix_w[layer], contrib, wts, write_h=True, u_dtype=BF16)
        proj = _inproj_call(u, w_perm)
        ya = _hgrn2_call(proj, hg_par, batch, p)
        yb = _mlstm_call(proj, ml_par, ml_norm_w[layer].reshape(1, ML_WIDTH), batch, p)
        yc = _ssd_call(proj, ssm_conv_w[layer], ssm_conv_b[layer].reshape(1, -1), ss_par, dskip,
                       ssm_norm_w[layer].reshape(1, SSM_WIDTH), batch, p)
        h, u_ffn, logits = _outproj_call(ya, yb, yc, h, w_out[layer].astype(BF16), norm_ffn_w[layer], r_hi, r_lo)
        block_e, n_used, src_tok, dst_row, wts, n_out_rows = _route(logits, moe_b_group[layer], moe_b_router[layer], t)
        contrib = _moe_call(block_e, n_used, src_tok, dst_row, u_ffn, moe_w1[layer], moe_w3[layer], moe_w2[layer],
                            n_out_rows)
        contrib = contrib.reshape(-1, TOP_K * d)
    return _final_call(h, contrib, wts, final_norm_w, batch, p)
```

```python
import functools

import jax
import jax.numpy as jnp
import numpy as np
from jax import lax
from jax.experimental import pallas as pl
from jax.experimental.pallas import tpu as pltpu

F32 = jnp.float32
BF16 = jnp.bfloat16

D_MODEL = 2048
N_META = 16
CHUNK = 64
HG_CHUNK = 16
LEAD_PAD = CHUNK - N_META
EPS = 1e-6
NEG = -1e30
LB_FLOOR = 1e-30

HG_HEADS = 4
HG_KDIM = 128
HG_KEY = HG_HEADS * HG_KDIM
HG_WIDTH = HG_HEADS * 128

ML_HEADS = 4
ML_QK = 64
ML_V = 128
ML_QK_W = ML_HEADS * ML_QK
ML_WIDTH = ML_HEADS * ML_V
GATE_CAP = 15.0

SSM_HEADS = 16
SSM_HEADDIM = 64
SSM_WIDTH = SSM_HEADS * SSM_HEADDIM
SSM_STATE = 128
SSM_GROUPS = 4
SSM_HPG = SSM_HEADS // SSM_GROUPS
SSM_BC = SSM_GROUPS * SSM_STATE
CONV_W = 4

D_MIX = HG_WIDTH + ML_WIDTH + SSM_WIDTH

N_GROUPS_MOE = 4
EXPERTS_PER_GROUP = 8
N_EXPERTS = N_GROUPS_MOE * EXPERTS_PER_GROUP
TOP_K = 2
D_EXPERT = 512
MOE_BM = 128

C_HG = 0
C_SZ = 2048
C_SX = 3072
C_MV = 5120
C_MQ = 6144
C_SMALL = 6656
N_PROJ = 6912
LANE_MI = 0
LANE_MF = ML_HEADS
LANE_DT = 2 * ML_HEADS

VMEM_LIMIT = 56 * 1024 * 1024


def _cparams(sem):
    return pltpu.CompilerParams(dimension_semantics=sem, vmem_limit_bytes=VMEM_LIMIT)


def _row_tile(n, target, mult=16):
    best = None
    for t in range(mult, min(n, target) + 1, mult):
        if n % t == 0:
            best = t
    assert best is not None, (n, target, mult)
    return best


def _split3(x):
    hi = x.astype(BF16)
    r = x - hi.astype(F32)
    mid = r.astype(BF16)
    lo = (r - mid.astype(F32)).astype(BF16)
    return hi, mid, lo


def _dot(a, b):
    return jnp.dot(a, b, preferred_element_type=F32)


def _sel_dot(sel, x):
    hi, mid, lo = _split3(x)
    return _dot(sel, hi) + _dot(sel, mid) + _dot(sel, lo)


def _dot_sel(x, sel):
    hi, mid, lo = _split3(x)
    return _dot(hi, sel) + _dot(mid, sel) + _dot(lo, sel)


def _dot_nt(a, b):
    return lax.dot_general(a, b, (((1,), (1,)), ((), ())), preferred_element_type=F32)


def _dot_tn(a, b):
    return lax.dot_general(a, b, (((0,), (0,)), ((), ())), preferred_element_type=F32)


def _log_sigmoid(x):
    return jnp.minimum(x, 0.0) - jnp.log1p(jnp.exp(-jnp.abs(x)))


def _sigmoid(x):
    return 1.0 / (1.0 + jnp.exp(-x))


def _silu(x):
    return x * _sigmoid(x)


def _norm_kernel(*refs, combine, write_h):
    if combine:
        h_ref, c0_ref, c1_ref, wt_ref, nw_ref = refs[:5]
        outs = refs[5:]
        wt = wt_ref[...]
        h = h_ref[...] + wt[:, 0:1] * c0_ref[...] + wt[:, 1:2] * c1_ref[...]
    else:
        h_ref, nw_ref = refs[:2]
        outs = refs[2:]
        h = h_ref[...]
    if write_h:
        outs[0][...] = h
    ms = jnp.mean(h * h, axis=-1, keepdims=True)
    u_ref = outs[-1]
    u_ref[...] = (h * lax.rsqrt(ms + EPS) * nw_ref[...]).astype(u_ref.dtype)


def _norm_call(h, nw, contrib=None, wts=None, *, write_h, u_dtype, tm_target=264):
    t, d = h.shape
    tm = _row_tile(t, tm_target)
    combine = contrib is not None
    row = lambda i: (i, 0)
    in_specs = [pl.BlockSpec((tm, d), row)]
    args = [h]
    if combine:
        in_specs += [pl.BlockSpec((tm, d), row), pl.BlockSpec((tm, d), lambda i: (t // tm + i, 0)),
                     pl.BlockSpec((tm, 2), row)]
        args += [contrib, contrib, wts]
    in_specs.append(pl.BlockSpec((1, d), lambda i: (0, 0)))
    args.append(nw.reshape(1, d))
    out_shape, out_specs = [], []
    if write_h:
        out_shape.append(jax.ShapeDtypeStruct((t, d), F32))
        out_specs.append(pl.BlockSpec((tm, d), row))
    out_shape.append(jax.ShapeDtypeStruct((t, d), u_dtype))
    out_specs.append(pl.BlockSpec((tm, d), row))
    return pl.pallas_call(
        functools.partial(_norm_kernel, combine=combine, write_h=write_h),
        grid=(t // tm,), in_specs=in_specs, out_specs=out_specs, out_shape=out_shape,
        compiler_params=_cparams(("arbitrary",)), name="combine_norm",
    )(*args)


def _final_kernel(h_ref, c0_ref, c1_ref, wt_ref, nw_ref, o_ref):
    wt = wt_ref[...]
    h = h_ref[...] + wt[:, 0:1] * c0_ref[...] + wt[:, 1:2] * c1_ref[...]
    ms = jnp.mean(h * h, axis=-1, keepdims=True)
    o_ref[...] = h * lax.rsqrt(ms + EPS) * nw_ref[...]


def _final_call(h, contrib, wts, nw, batch, p):
    t, d = h.shape
    seq = p - CHUNK
    tm = CHUNK
    per_b = p // tm
    n_out = seq // tm
    src = lambda b, i: (b * per_b + 1 + i, 0)
    src1 = lambda b, i: (t // tm + b * per_b + 1 + i, 0)
    return pl.pallas_call(
        _final_kernel, grid=(batch, n_out),
        in_specs=[pl.BlockSpec((tm, d), src), pl.BlockSpec((tm, d), src), pl.BlockSpec((tm, d), src1),
                  pl.BlockSpec((tm, 2), src), pl.BlockSpec((1, d), lambda b, i: (0, 0))],
        out_specs=pl.BlockSpec((None, tm, d), lambda b, i: (b, i, 0)),
        out_shape=jax.ShapeDtypeStruct((batch, seq, d), F32),
        compiler_params=_cparams(("arbitrary", "arbitrary")), name="final_norm",
    )(h, contrib, contrib, wts, nw.reshape(1, d))


def _matmul_kernel(x_ref, w_ref, o_ref):
    o_ref[...] = _dot(x_ref[...], w_ref[...])


def _inproj_call(u, w, tm_target=1056, tn=768):
    t, d = u.shape
    n = w.shape[1]
    tm = _row_tile(t, tm_target)
    assert n % tn == 0
    return pl.pallas_call(
        _matmul_kernel, grid=(t // tm, n // tn),
        in_specs=[pl.BlockSpec((tm, d), lambda i, j: (i, 0)), pl.BlockSpec((d, tn), lambda i, j: (0, j))],
        out_specs=pl.BlockSpec((tm, tn), lambda i, j: (i, j)),
        out_shape=jax.ShapeDtypeStruct((t, n), F32),
        compiler_params=_cparams(("arbitrary", "arbitrary")), name="in_proj",
    )(u, w)


def _hgrn2_kernel(q_ref, f_ref, i_ref, g_ref, par_ref, o_ref, st_ref, *, rows):
    s = pl.program_id(1)

    @pl.when(s == 0)
    def _():
        st_ref[...] = jnp.zeros_like(st_ref)

    c = HG_CHUNK
    ones = jnp.ones((HG_KDIM, 128), BF16)
    rid = lax.broadcasted_iota(jnp.int32, (c, 128), 0)
    scale = HG_KDIM ** -0.5

    def chunk(ci, carry):
        r0 = pl.multiple_of(ci * c, c)
        pad = (s * rows + r0 + rid) < LEAD_PAD
        for h in range(HG_HEADS):
            cols = slice(h * 128, (h + 1) * 128)
            a_lb = par_ref[0:1, cols]
            b_lb = par_ref[1:2, cols]
            c_lb = par_ref[2:3, cols]
            nw = par_ref[3:4, cols]
            z = f_ref[pl.ds(r0, c), cols]
            sg = _sigmoid(z)
            f = a_lb + b_lb * sg
            log_f = jnp.where(pad, 0.0, jnp.log(f))
            k = jnp.where(pad, 0.0, b_lb * (1.0 - sg) - c_lb)
            q = q_ref[pl.ds(r0, c), cols] * scale
            v = i_ref[pl.ds(r0, c), cols]
            cum = log_f
            for sh in (1, 2, 4, 8):
                cum = cum + jnp.where(rid >= sh, pltpu.roll(cum, sh, axis=0), 0.0)
            parts = []
            for s_ in range(c):
                rel = jnp.where(rid >= s_, cum - cum[s_:s_ + 1, :], NEG)
                parts.append((q * (k[s_:s_ + 1, :] * jnp.exp(rel))).astype(BF16))
            sc = _dot(jnp.concatenate(parts, axis=0), ones)
            o = jnp.zeros((c, 128), F32)
            for s_ in range(c):
                o = o + sc[s_ * c:(s_ + 1) * c, :] * v[s_:s_ + 1, :]
            st = st_ref[h]
            o = o + _dot_nt((q * jnp.exp(cum)).astype(BF16), st.astype(BF16))
            last = cum[c - 1:c, :]
            kd = (k * jnp.exp(last - cum)).astype(BF16)
            st_ref[h] = st * jnp.exp(last) + _dot_tn(v.astype(BF16), kd)
            ms = jnp.mean(o * o, axis=-1, keepdims=True)
            g = g_ref[pl.ds(r0, c), cols]
            o_ref[pl.ds(r0, c), cols] = (o * lax.rsqrt(ms + EPS) * nw * _silu(g)).astype(o_ref.dtype)
        return carry

    lax.fori_loop(0, rows // c, chunk, 0)


def _hgrn2_call(proj, par, batch, p):
    t = proj.shape[0]
    rows = _row_tile(p, 528)
    nb = p // rows
    w = HG_KEY
    blk = lambda j: pl.BlockSpec((rows, w), lambda b, s, j=j: (b * nb + s, C_HG // w + j))
    return pl.pallas_call(
        functools.partial(_hgrn2_kernel, rows=rows), grid=(batch, nb),
        in_specs=[blk(0), blk(1), blk(2), blk(3), pl.BlockSpec((8, w), lambda b, s: (0, 0))],
        out_specs=pl.BlockSpec((rows, w), lambda b, s: (b * nb + s, 0)),
        out_shape=jax.ShapeDtypeStruct((t, HG_WIDTH), BF16),
        scratch_shapes=[pltpu.VMEM((HG_HEADS, 128, HG_KDIM), F32)],
        compiler_params=_cparams(("arbitrary", "arbitrary")), name="hgrn2",
    )(proj, proj, proj, proj, par)


def _mlstm_kernel(v_ref, o_ref, q_ref, k_ref, sm_ref, par_ref, nw_ref, tri_ref, sel_ref, dg_ref,
                  y_ref, c_ref, m_ref):
    s = pl.program_id(1)

    @pl.when(s == 0)
    def _():
        c_ref[...] = jnp.zeros_like(c_ref)
        m_ref[...] = jnp.zeros_like(m_ref)

    n = CHUNK
    rid = lax.broadcasted_iota(jnp.int32, (n, 128), 0)
    pad = (s * n + rid) < LEAD_PAD
    pre = sm_ref[...] + par_ref[0:1, :]
    cap = GATE_CAP * jnp.tanh(pre * (1.0 / GATE_CAP))
    log_i = jnp.where(pad, NEG, cap)
    log_f = jnp.where(pad, 0.0, _log_sigmoid(cap))
    cum = _sel_dot(tri_ref[...], log_f)
    sel_i = sel_ref[0]
    sel_f = sel_ref[1]
    dg = dg_ref[...]
    ones = jnp.ones((n, n), BF16)
    cum_col = _dot_sel(cum, sel_f)
    cum_row = _sel_dot(ones, cum_col * dg)
    li_row = _sel_dot(ones, _dot_sel(log_i, sel_i) * dg)
    w = ML_HEADS * n
    tt = lax.broadcasted_iota(jnp.int32, (n, w), 0)
    ss = lax.broadcasted_iota(jnp.int32, (n, w), 1) & (n - 1)
    dmat = jnp.where(tt >= ss, cum_col - cum_row + li_row, NEG)
    scale = ML_QK ** -0.5
    lane = lax.broadcasted_iota(jnp.int32, (n, 128), 1)
    one_col = jnp.where(lane == 0, 1.0, 0.0).astype(BF16)
    for h in range(ML_HEADS):
        d_h = dmat[:, h * n:(h + 1) * n]
        cum_h = cum[:, LANE_MF + h:LANE_MF + h + 1]
        li_h = log_i[:, LANE_MI + h:LANE_MI + h + 1]
        m_st = m_ref[h:h + 1, 0:1]
        q = (q_ref[:, h * ML_QK:(h + 1) * ML_QK] * scale).astype(BF16)
        k = k_ref[:, h * ML_QK:(h + 1) * ML_QK]
        v_aug = jnp.concatenate([v_ref[:, h * ML_V:(h + 1) * ML_V].astype(BF16), one_col], axis=1)
        inter = cum_h + m_st
        m_t = jnp.maximum(inter, jnp.max(d_h, axis=-1, keepdims=True))
        w_inter = jnp.exp(inter - m_t)
        qk = _dot_nt(q, k.astype(BF16)) * jnp.exp(d_h - m_t)
        c_prev = c_ref[h]
        nd = _dot(qk.astype(BF16), v_aug) + w_inter * _dot(q, c_prev.astype(BF16))
        num = nd[:, :ML_V]
        den = nd[:, ML_V:ML_V + 1]
        hh = num / jnp.maximum(jnp.abs(den), jnp.exp(-m_t))
        tot = cum_h[n - 1:n, :]
        to_end = tot - cum_h + li_h
        m_loc = jnp.max(to_end, axis=0, keepdims=True)
        kw = (k * jnp.exp(to_end - m_loc)).astype(BF16)
        c_loc = _dot_tn(kw, v_aug)
        m_new = jnp.maximum(tot + m_st, m_loc)
        c_ref[h] = jnp.exp(tot + m_st - m_new) * c_prev + jnp.exp(m_loc - m_new) * c_loc
        m_ref[h:h + 1, :] = jnp.broadcast_to(m_new, (1, 128))
        ms = jnp.mean(hh * hh, axis=-1, keepdims=True)
        cols = slice(h * ML_V, (h + 1) * ML_V)
        y_ref[:, cols] = (hh * lax.rsqrt(ms + EPS) * nw_ref[:, cols] * _sigmoid(o_ref[:, cols])).astype(y_ref.dtype)


def _lane_select(lane0, heads, width):
    m = np.zeros((128, heads * width), np.float32)
    for h in range(heads):
        m[lane0 + h, h * width:(h + 1) * width] = 1.0
    return m


def _diag_mask(n, heads):
    return np.tile(np.eye(n, dtype=np.float32), (1, heads))


def _mlstm_call(proj, par, nw, batch, p):
    t = proj.shape[0]
    n = CHUNK
    nc = p // n
    tri = jnp.asarray(np.tril(np.ones((n, n), np.float32)), BF16)
    sel = jnp.asarray(np.stack([_lane_select(LANE_MI, ML_HEADS, n), _lane_select(LANE_MF, ML_HEADS, n)]), BF16)
    dg = jnp.asarray(_diag_mask(n, ML_HEADS), F32)
    blk = lambda w, off: pl.BlockSpec((n, w), lambda b, s: (b * nc + s, off // w))
    const = lambda shape: pl.BlockSpec(shape, lambda b, s: (0,) * len(shape))
    return pl.pallas_call(
        _mlstm_kernel, grid=(batch, nc),
        in_specs=[blk(ML_WIDTH, C_MV), blk(ML_WIDTH, C_MV + ML_WIDTH), blk(ML_QK_W, C_MQ),
                  blk(ML_QK_W, C_MQ + ML_QK_W), blk(128, C_SMALL),
                  const((8, 128)), const((1, ML_WIDTH)), const((n, n)), const((2, 128, ML_HEADS * n)),
                  const((n, ML_HEADS * n))],
        out_specs=pl.BlockSpec((n, ML_WIDTH), lambda b, s: (b * nc + s, 0)),
        out_shape=jax.ShapeDtypeStruct((t, ML_WIDTH), BF16),
        scratch_shapes=[pltpu.VMEM((ML_HEADS, ML_QK, 2 * ML_V), F32), pltpu.VMEM((8, 128), F32)],
        compiler_params=_cparams(("arbitrary", "arbitrary")), name="mlstm",
    )(proj, proj, proj, proj, proj, par, nw, tri, sel, dg)


def _ssd_kernel(z_ref, x_ref, b_ref, c_ref, sm_ref, cw_ref, cb_ref, par_ref, dsk_ref, nw_ref,
                tri_ref, sel_ref, dg_ref, y_ref, xs_ref, bs_ref, cs_ref, st_ref):
    s = pl.program_id(1)
    n = CHUNK
    tail = 8

    @pl.when(s == 0)
    def _():
        st_ref[...] = jnp.zeros_like(st_ref)
        xs_ref[0:tail, :] = jnp.zeros((tail, xs_ref.shape[1]), F32)
        bs_ref[0:tail, :] = jnp.zeros((tail, bs_ref.shape[1]), F32)
        cs_ref[0:tail, :] = jnp.zeros((tail, cs_ref.shape[1]), F32)

    def conv_silu(src_ref, scr_ref, c0, width, rowmask):
        scr_ref[tail:tail + n, :] = src_ref[...]
        acc = cb_ref[:, c0:c0 + width]
        for j in range(CONV_W):
            off = tail - (CONV_W - 1) + j
            acc = acc + cw_ref[j:j + 1, c0:c0 + width] * scr_ref[off:off + n, :]
        scr_ref[0:tail, :] = scr_ref[n:n + tail, :]
        return jnp.where(rowmask, 0.0, _silu(acc))

    def padmask(width):
        return (s * n + lax.broadcasted_iota(jnp.int32, (n, width), 0)) < LEAD_PAD

    x = conv_silu(x_ref, xs_ref, 0, SSM_WIDTH, padmask(SSM_WIDTH))
    bm = conv_silu(b_ref, bs_ref, SSM_WIDTH, SSM_BC, padmask(SSM_BC))
    cm = conv_silu(c_ref, cs_ref, SSM_WIDTH + SSM_BC, SSM_BC, padmask(SSM_BC))

    pre = sm_ref[...] + par_ref[0:1, :]
    dt = jnp.maximum(pre, 0.0) + jnp.log1p(jnp.exp(-jnp.abs(pre)))
    dt = jnp.where(padmask(128), 0.0, dt)
    da = dt * (-jnp.exp(par_ref[1:2, :]))
    cum = _sel_dot(tri_ref[...], da)
    sel = sel_ref[...]
    ones = jnp.ones((n, n), BF16)
    dt_col = _dot_sel(dt, sel)
    cum_col = _dot_sel(cum, sel)
    cum_row = _sel_dot(ones, cum_col * dg_ref[...])
    w = SSM_WIDTH
    tt = lax.broadcasted_iota(jnp.int32, (n, w), 0)
    ss = lax.broadcasted_iota(jnp.int32, (n, w), 1) & (n - 1)
    decay = jnp.exp(jnp.where(tt >= ss, cum_col - cum_row, NEG))
    xdt = x * dt_col
    last = cum_col[n - 1:n, :]
    wend = (xdt * jnp.exp(last - cum_col)).astype(BF16)
    chunk_decay = jnp.exp(last)
    ecum = jnp.exp(cum_col)
    gw = SSM_HPG * SSM_HEADDIM
    rr = lax.broadcasted_iota(jnp.int32, (gw, gw), 0) // SSM_HEADDIM
    cc = lax.broadcasted_iota(jnp.int32, (gw, gw), 1) // SSM_HEADDIM
    blockdiag = rr == cc
    ys = []
    for g in range(SSM_GROUPS):
        gl = slice(g * gw, (g + 1) * gw)
        sl = slice(g * SSM_STATE, (g + 1) * SSM_STATE)
        cm_g = cm[:, sl].astype(BF16)
        bm_g = bm[:, sl].astype(BF16)
        cb = _dot_nt(cm_g, jnp.concatenate([bm_g] * SSM_HPG, axis=0))
        m = (cb * decay[:, gl]).astype(BF16)
        xdt_g = xdt[:, gl]
        bd = jnp.where(blockdiag, jnp.concatenate([xdt_g] * SSM_HPG, axis=0), 0.0).astype(BF16)
        st = st_ref[g]
        y_g = _dot(m, bd) + ecum[:, gl] * _dot(cm_g, st.astype(BF16))
        st_ref[g] = st * chunk_decay[:, gl] + _dot_tn(bm_g, wend[:, gl])
        ys.append(y_g)
    y = jnp.concatenate(ys, axis=1) + x * dsk_ref[...]
    y = y * _silu(z_ref[...])
    outs = []
    for g in range(SSM_GROUPS):
        gl = slice(g * gw, (g + 1) * gw)
        y_g = y[:, gl]
        ms = jnp.mean(y_g * y_g, axis=-1, keepdims=True)
        outs.append(y_g * lax.rsqrt(ms + EPS))
    y_ref[...] = (jnp.concatenate(outs, axis=1) * nw_ref[...]).astype(y_ref.dtype)


def _ssd_call(proj, cw, cb, par, dsk, nw, batch, p):
    t = proj.shape[0]
    n = CHUNK
    nc = p // n
    tri = jnp.asarray(np.tril(np.ones((n, n), np.float32)), BF16)
    sel = jnp.asarray(_lane_select(LANE_DT, SSM_HEADS, SSM_HEADDIM), BF16)
    dg = jnp.asarray(_diag_mask(n, SSM_HEADS), F32)
    blk = lambda w, off: pl.BlockSpec((n, w), lambda b, s: (b * nc + s, off // w))
    const = lambda shape: pl.BlockSpec(shape, lambda b, s: (0,) * len(shape))
    cch = SSM_WIDTH + 2 * SSM_BC
    return pl.pallas_call(
        _ssd_kernel, grid=(batch, nc),
        in_specs=[blk(SSM_WIDTH, C_SZ), blk(SSM_WIDTH, C_SX), blk(SSM_BC, C_SX + SSM_WIDTH),
                  blk(SSM_BC, C_SX + SSM_WIDTH + SSM_BC), blk(128, C_SMALL),
                  const((CONV_W, cch)), const((1, cch)), const((8, 128)), const((1, SSM_WIDTH)),
                  const((1, SSM_WIDTH)), const((n, n)), const((128, SSM_WIDTH)), const((n, SSM_WIDTH))],
        out_specs=pl.BlockSpec((n, SSM_WIDTH), lambda b, s: (b * nc + s, 0)),
        out_shape=jax.ShapeDtypeStruct((t, SSM_WIDTH), BF16),
        scratch_shapes=[pltpu.VMEM((n + 8, SSM_WIDTH), F32), pltpu.VMEM((n + 8, SSM_BC), F32),
                        pltpu.VMEM((n + 8, SSM_BC), F32),
                        pltpu.VMEM((SSM_GROUPS, SSM_STATE, SSM_HPG * SSM_HEADDIM), F32)],
        compiler_params=_cparams(("arbitrary", "arbitrary")), name="ssd",
    )(proj, proj, proj, proj, proj, cw, cb, par, dsk, nw, tri, sel, dg)


def _outproj_kernel(ya_ref, yb_ref, yc_ref, h_ref, w_ref, nw_ref, rh_ref, rl_ref, hm_ref, u_ref, lg_ref):
    a0, a1 = HG_WIDTH, HG_WIDTH + ML_WIDTH
    h = h_ref[...]
    h = h + _dot(ya_ref[...], w_ref[0:a0, :])
    h = h + _dot(yb_ref[...], w_ref[a0:a1, :])
    h = h + _dot(yc_ref[...], w_ref[a1:, :])
    hm_ref[...] = h
    ms = jnp.mean(h * h, axis=-1, keepdims=True)
    u = h * lax.rsqrt(ms + EPS) * nw_ref[...]
    u_ref[...] = u
    u_hi = u.astype(BF16)
    u_lo = (u - u_hi.astype(F32)).astype(BF16)
    lg_ref[...] = _dot(u_hi, rh_ref[...]) + (_dot(u_lo, rh_ref[...]) + _dot(u_hi, rl_ref[...]))


def _outproj_call(ya, yb, yc, h, w, nw, r_hi, r_lo, tm_target=352):
    t, d = h.shape
    tm = _row_tile(t, tm_target)
    row = lambda i: (i, 0)
    const = lambda shape: pl.BlockSpec(shape, lambda i: (0, 0))
    return pl.pallas_call(
        _outproj_kernel, grid=(t // tm,),
        in_specs=[pl.BlockSpec((tm, HG_WIDTH), row), pl.BlockSpec((tm, ML_WIDTH), row),
                  pl.BlockSpec((tm, SSM_WIDTH), row), pl.BlockSpec((tm, d), row),
                  const((D_MIX, d)), const((1, d)), const((d, 128)), const((d, 128))],
        out_specs=[pl.BlockSpec((tm, d), row), pl.BlockSpec((tm, d), row), pl.BlockSpec((tm, 128), row)],
        out_shape=[jax.ShapeDtypeStruct((t, d), F32), jax.ShapeDtypeStruct((t, d), F32),
                   jax.ShapeDtypeStruct((t, 128), F32)],
        compiler_params=_cparams(("arbitrary",)), name="out_proj_router",
    )(ya, yb, yc, h, w, nw.reshape(1, d), r_hi, r_lo)


def _moe_kernel(be_ref, nu_ref, src0_ref, srcn_ref, dst_ref, u_hbm, w1_ref, w3_ref, w2_ref, o_hbm,
                xbuf, ybuf, w1b, w3b, w2b, gsem, ssem):
    i = pl.program_id(0)
    bm = MOE_BM
    n_used = nu_ref[0]
    slot = i % 2

    def gather(idx_ref, sl):
        def body(r, c):
            tok = idx_ref[0, r]
            pltpu.make_async_copy(u_hbm.at[pl.ds(tok, 1), :], xbuf.at[sl, pl.ds(r, 1), :], gsem.at[sl]).start()
            return c
        lax.fori_loop(0, bm, body, 0)

    def scatter(sl):
        def body(r, c):
            row = dst_ref[0, r]
            pltpu.make_async_copy(ybuf.at[sl, pl.ds(r, 1), :], o_hbm.at[pl.ds(row, 1), :], ssem.at[sl]).start()
            return c
        lax.fori_loop(0, bm, body, 0)

    def wait_gather(sl):
        pltpu.make_async_copy(u_hbm.at[pl.ds(0, bm), :], xbuf.at[sl], gsem.at[sl]).wait()

    def wait_scatter(sl):
        pltpu.make_async_copy(ybuf.at[sl], o_hbm.at[pl.ds(0, bm), :], ssem.at[sl]).wait()

    @pl.when(i == 0)
    def _():
        gather(src0_ref, 0)
        ybuf[...] = jnp.zeros_like(ybuf)
        n_real = o_hbm.shape[0] - 2 * bm
        for sl in range(2):
            fill = pltpu.make_async_copy(ybuf.at[sl], o_hbm.at[pl.ds(n_real + sl * bm, bm), :], ssem.at[sl])
            fill.start()
            fill.wait()

    @pl.when(i + 1 < n_used)
    def _():
        gather(srcn_ref, 1 - slot)

    @pl.when(i < n_used)
    def _():
        first = jnp.logical_or(i == 0, be_ref[i] != be_ref[jnp.maximum(i - 1, 0)])

        @pl.when(first)
        def _():
            w1b[...] = w1_ref[...].astype(BF16)
            w3b[...] = w3_ref[...].astype(BF16)
            w2b[...] = w2_ref[...].astype(BF16)

        wait_gather(slot)
        x = xbuf[slot].astype(BF16)
        hid = _silu(_dot(x, w1b[...])) * _dot(x, w3b[...])
        y = _dot(hid.astype(BF16), w2b[...])

        @pl.when(i >= 2)
        def _():
            wait_scatter(slot)

        ybuf[slot] = y
        scatter(slot)

    @pl.when(i == n_used - 1)
    def _():
        wait_scatter(slot)

        @pl.when(i >= 1)
        def _():
            wait_scatter(1 - slot)


def _moe_call(block_e, n_used, src_tok, dst_row, u, w1, w3, w2, layer, n_out_rows):
    t, d = u.shape
    f = w1.shape[-1]
    bm = MOE_BM
    nb = block_e.shape[0]
    wmap = lambda i, be, nu: (layer, be[i], 0, 0)
    src_tok = src_tok.reshape(nb, 1, bm)
    dst_row = dst_row.reshape(nb, 1, bm)
    idx = lambda fn: pl.BlockSpec((None, 1, bm), fn, memory_space=pltpu.SMEM)
    return pl.pallas_call(
        _moe_kernel,
        grid_spec=pltpu.PrefetchScalarGridSpec(
            num_scalar_prefetch=2, grid=(nb,),
            in_specs=[idx(lambda i, be, nu: (0, 0, 0)),
                      idx(lambda i, be, nu: (jnp.minimum(i + 1, nb - 1), 0, 0)),
                      idx(lambda i, be, nu: (i, 0, 0)),
                      pl.BlockSpec(memory_space=pl.ANY),
                      pl.BlockSpec((None, None, d, f), wmap), pl.BlockSpec((None, None, d, f), wmap),
                      pl.BlockSpec((None, None, f, d), wmap)],
            out_specs=pl.BlockSpec(memory_space=pl.ANY),
            scratch_shapes=[pltpu.VMEM((2, bm, d), F32), pltpu.VMEM((2, bm, d), F32),
                            pltpu.VMEM((d, f), BF16), pltpu.VMEM((d, f), BF16), pltpu.VMEM((f, d), BF16),
                            pltpu.SemaphoreType.DMA((2,)), pltpu.SemaphoreType.DMA((2,))]),
        out_shape=jax.ShapeDtypeStruct((n_out_rows, d), F32),
        compiler_params=_cparams(("arbitrary",)), name="moe_ffn",
    )(block_e, n_used, src_tok, src_tok, dst_row, u, w1, w3, w2)


def _route(logits, b_group, b_router, t):
    bm = MOE_BM
    g_logit = logits[:, :N_GROUPS_MOE] + b_group
    g_prob = jax.nn.softmax(g_logit, axis=-1)
    g_sel = jnp.argmax(g_prob, axis=-1)
    g_gate = jnp.max(g_prob, axis=-1)
    e_all = logits[:, N_GROUPS_MOE:N_GROUPS_MOE + N_EXPERTS].reshape(t, N_GROUPS_MOE, EXPERTS_PER_GROUP) + b_router
    e_logits = jnp.take_along_axis(e_all, g_sel[:, None, None], axis=1)[:, 0]
    top_v, top_i = lax.top_k(e_logits, TOP_K)
    top_w = jax.nn.softmax(top_v, axis=-1) * g_gate[:, None]
    flat_e = (g_sel[:, None] * EXPERTS_PER_GROUP + top_i).reshape(-1).astype(jnp.int32)
    tk = t * TOP_K
    order = jnp.argsort(flat_e).astype(jnp.int32)
    sorted_e = flat_e[order]
    counts = jnp.bincount(flat_e, length=N_EXPERTS).astype(jnp.int32)
    starts = jnp.cumsum(counts) - counts
    padded = (counts + bm - 1) // bm * bm
    pad_ends = jnp.cumsum(padded)
    pad_starts = pad_ends - padded
    dest = pad_starts[sorted_e] + jnp.arange(tk, dtype=jnp.int32) - starts[sorted_e]
    n_blocks = -(-tk // bm) + N_EXPERTS
    n_rows = n_blocks * bm
    dummy = tk + (jnp.arange(n_rows, dtype=jnp.int32) % (2 * bm))
    src_tok = jnp.zeros((n_rows,), jnp.int32).at[dest].set(order // TOP_K)
    dst_row = dummy.at[dest].set((order % TOP_K) * t + order // TOP_K)
    block_e = jnp.minimum(jnp.searchsorted(pad_ends, jnp.arange(n_blocks, dtype=jnp.int32) * bm, side='right'),
                          N_EXPERTS - 1).astype(jnp.int32)
    n_used = (pad_ends[-1] // bm).astype(jnp.int32).reshape(1)
    last_e = block_e[jnp.maximum(n_used[0] - 1, 0)]
    block_e = jnp.where(jnp.arange(n_blocks) < n_used[0], block_e, last_e)
    return block_e, n_used, src_tok, dst_row, top_w, tk + 2 * bm


def _lane_row(pairs):
    row = jnp.zeros((128,), F32)
    for lane0, vals in pairs:
        row = row.at[lane0:lane0 + vals.shape[0]].set(vals.astype(F32))
    return row


def kernel(x, meta_tokens, hg_lb_logits, norm_mix_w, w_in, hg_norm_w, ml_b_i, ml_b_f, ml_norm_w,
           ssm_conv_w, ssm_conv_b, ssm_dt_bias, ssm_a_log, ssm_d, ssm_norm_w, w_out, norm_ffn_w,
           moe_w_group, moe_b_group, moe_w_router, moe_b_router, moe_w1, moe_w3, moe_w2, final_norm_w):
    batch, seq, d = x.shape
    depth = w_in.shape[0]
    p = LEAD_PAD + N_META + seq
    t = batch * p
    meta = jnp.broadcast_to(meta_tokens.astype(x.dtype)[None], (batch, N_META, d))
    h = jnp.concatenate([jnp.zeros((batch, LEAD_PAD, d), x.dtype), meta, x], axis=1).reshape(t, d)

    lb_w = jax.nn.softmax(hg_lb_logits.astype(F32), axis=0)
    lower_bounds = jnp.cumsum(lb_w, axis=0) - lb_w[0]

    contrib = wts = None
    for layer in range(depth):
        w = w_in[layer]
        o_mq, o_mv, o_mi, o_sz = HG_KEY * 2 + HG_WIDTH * 2, 2560, 3584, 3592
        o_sx = o_sz + SSM_WIDTH
        o_dt = o_sx + SSM_WIDTH + 2 * SSM_BC
        w_perm = jnp.concatenate([
            w[:, :o_mq], w[:, o_sz:o_sx], w[:, o_sx:o_dt], w[:, o_mv:o_mi], w[:, o_mq:o_mv],
            w[:, o_mi:o_sz], w[:, o_dt:], jnp.zeros((d, N_PROJ - C_SMALL - 2 * ML_HEADS - SSM_HEADS), w.dtype)],
            axis=1).astype(BF16)
        lb = lower_bounds[layer]
        lbf = jnp.maximum(lb, LB_FLOOR)
        hg_par = jnp.zeros((8, HG_KEY), F32).at[0].set(lbf).at[1].set(1.0 - lb).at[2].set(lbf - lb).at[3].set(hg_norm_w[layer])
        ml_par = jnp.zeros((8, 128), F32).at[0].set(_lane_row([(LANE_MI, ml_b_i[layer]), (LANE_MF, ml_b_f[layer])]))
        ss_par = jnp.zeros((8, 128), F32).at[0].set(_lane_row([(LANE_DT, ssm_dt_bias[layer])]))
        ss_par = ss_par.at[1].set(_lane_row([(LANE_DT, ssm_a_log[layer])]))
        dskip = jnp.repeat(ssm_d[layer].astype(F32), SSM_HEADDIM).reshape(1, SSM_WIDTH)
        w_r = jnp.concatenate([moe_w_group[layer],
                               moe_w_router[layer].transpose(1, 0, 2).reshape(d, N_EXPERTS),
                               jnp.zeros((d, 128 - N_GROUPS_MOE - N_EXPERTS), F32)], axis=1)
        r_hi = w_r.astype(BF16)
        r_lo = (w_r - r_hi.astype(F32)).astype(BF16)

        if layer == 0:
            (u,) = _norm_call(h, norm_mix_w[layer], write_h=False, u_dtype=BF16)
        else:
            h, u = _norm_call(h, norm_mix_w[layer], contrib, wts, write_h=True, u_dtype=BF16)
        proj = _inproj_call(u, w_perm)
        ya = _hgrn2_call(proj, hg_par, batch, p)
        yb = _mlstm_call(proj, ml_par, ml_norm_w[layer].reshape(1, ML_WIDTH), batch, p)
        yc = _ssd_call(proj, ssm_conv_w[layer], ssm_conv_b[layer].reshape(1, -1), ss_par, dskip,
                       ssm_norm_w[layer].reshape(1, SSM_WIDTH), batch, p)
        h, u_ffn, logits = _outproj_call(ya, yb, yc, h, w_out[layer].astype(BF16), norm_ffn_w[layer], r_hi, r_lo)
        block_e, n_used, src_tok, dst_row, wts, n_out_rows = _route(logits, moe_b_group[layer], moe_b_router[layer], t)
        contrib = _moe_call(block_e, n_used, src_tok, dst_row, u_ffn, moe_w1, moe_w3, moe_w2, layer, n_out_rows)
    return _final_call(h, contrib, wts, final_norm_w, batch, p)
```

```python
import functools

import jax
import jax.numpy as jnp
import numpy as np
from jax import lax
from jax.experimental import pallas as pl
from jax.experimental.pallas import tpu as pltpu

F32 = jnp.float32
BF16 = jnp.bfloat16

D_MODEL = 2048
N_META = 16
CHUNK = 64
HG_CHUNK = 16
LEAD_PAD = CHUNK - N_META
EPS = 1e-6
NEG = -1e30
LB_FLOOR = 1e-30

HG_HEADS = 4
HG_KDIM = 128
HG_KEY = HG_HEADS * HG_KDIM
HG_WIDTH = HG_HEADS * 128

ML_HEADS = 4
ML_QK = 64
ML_V = 128
ML_QK_W = ML_HEADS * ML_QK
ML_WIDTH = ML_HEADS * ML_V
GATE_CAP = 15.0

SSM_HEADS = 16
SSM_HEADDIM = 64
SSM_WIDTH = SSM_HEADS * SSM_HEADDIM
SSM_STATE = 128
SSM_GROUPS = 4
SSM_HPG = SSM_HEADS // SSM_GROUPS
SSM_BC = SSM_GROUPS * SSM_STATE
CONV_W = 4

D_MIX = HG_WIDTH + ML_WIDTH + SSM_WIDTH

N_GROUPS_MOE = 4
EXPERTS_PER_GROUP = 8
N_EXPERTS = N_GROUPS_MOE * EXPERTS_PER_GROUP
TOP_K = 2
D_EXPERT = 512
MOE_BM = 128

C_HG = 0
C_SZ = 2048
C_SX = 3072
C_MV = 5120
C_MQ = 6144
C_SMALL = 6656
N_PROJ = 6912
LANE_MI = 0
LANE_MF = ML_HEADS
LANE_DT = 2 * ML_HEADS

VMEM_LIMIT = 56 * 1024 * 1024


def _cparams(sem):
    return pltpu.CompilerParams(dimension_semantics=sem, vmem_limit_bytes=VMEM_LIMIT)


def _row_tile(n, target, mult=16):
    best = None
    for t in range(mult, min(n, target) + 1, mult):
        if n % t == 0:
            best = t
    assert best is not None, (n, target, mult)
    return best


def _split3(x):
    hi = x.astype(BF16)
    r = x - hi.astype(F32)
    mid = r.astype(BF16)
    lo = (r - mid.astype(F32)).astype(BF16)
    return hi, mid, lo


def _dot(a, b):
    return jnp.dot(a, b, preferred_element_type=F32)


def _sel_dot(sel, x):
    hi, mid, lo = _split3(x)
    return _dot(sel, hi) + _dot(sel, mid) + _dot(sel, lo)


def _dot_sel(x, sel):
    hi, mid, lo = _split3(x)
    return _dot(hi, sel) + _dot(mid, sel) + _dot(lo, sel)


def _dot_nt(a, b):
    return lax.dot_general(a, b, (((1,), (1,)), ((), ())), preferred_element_type=F32)


def _dot_tn(a, b):
    return lax.dot_general(a, b, (((0,), (0,)), ((), ())), preferred_element_type=F32)


def _log_sigmoid(x):
    return jnp.minimum(x, 0.0) - jnp.log1p(jnp.exp(-jnp.abs(x)))


def _sigmoid(x):
    return 1.0 / (1.0 + jnp.exp(-x))


def _silu(x):
    return x * _sigmoid(x)


def _norm_kernel(*refs, combine, write_h):
    if combine:
        h_ref, c0_ref, c1_ref, wt_ref, nw_ref = refs[:5]
        outs = refs[5:]
        wt = wt_ref[...]
        h = h_ref[...] + wt[:, 0:1] * c0_ref[...] + wt[:, 1:2] * c1_ref[...]
    else:
        h_ref, nw_ref = refs[:2]
        outs = refs[2:]
        h = h_ref[...]
    if write_h:
        outs[0][...] = h
    ms = jnp.mean(h * h, axis=-1, keepdims=True)
    u_ref = outs[-1]
    u_ref[...] = (h * lax.rsqrt(ms + EPS) * nw_ref[...]).astype(u_ref.dtype)


def _norm_call(h, nw, contrib=None, wts=None, *, write_h, u_dtype, tm_target=264):
    t, d = h.shape
    tm = _row_tile(t, tm_target)
    combine = contrib is not None
    row = lambda i: (i, 0)
    in_specs = [pl.BlockSpec((tm, d), row)]
    args = [h]
    if combine:
        in_specs += [pl.BlockSpec((tm, d), row), pl.BlockSpec((tm, d), lambda i: (t // tm + i, 0)),
                     pl.BlockSpec((tm, 2), row)]
        args += [contrib, contrib, wts]
    in_specs.append(pl.BlockSpec((1, d), lambda i: (0, 0)))
    args.append(nw.reshape(1, d))
    out_shape, out_specs = [], []
    if write_h:
        out_shape.append(jax.ShapeDtypeStruct((t, d), F32))
        out_specs.append(pl.BlockSpec((tm, d), row))
    out_shape.append(jax.ShapeDtypeStruct((t, d), u_dtype))
    out_specs.append(pl.BlockSpec((tm, d), row))
    return pl.pallas_call(
        functools.partial(_norm_kernel, combine=combine, write_h=write_h),
        grid=(t // tm,), in_specs=in_specs, out_specs=out_specs, out_shape=out_shape,
        compiler_params=_cparams(("arbitrary",)), name="combine_norm",
    )(*args)


def _final_kernel(h_ref, c0_ref, c1_ref, wt_ref, nw_ref, o_ref):
    wt = wt_ref[...]
    h = h_ref[...] + wt[:, 0:1] * c0_ref[...] + wt[:, 1:2] * c1_ref[...]
    ms = jnp.mean(h * h, axis=-1, keepdims=True)
    o_ref[...] = h * lax.rsqrt(ms + EPS) * nw_ref[...]


def _final_call(h, contrib, wts, nw, batch, p):
    t, d = h.shape
    seq = p - CHUNK
    tm = _row_tile(seq, 256)
    n_out = seq // tm
    src = lambda b, i: (pl.multiple_of(b * p + CHUNK + i * tm, CHUNK), 0)
    src1 = lambda b, i: (pl.multiple_of(t + b * p + CHUNK + i * tm, CHUNK), 0)
    win = lambda w, fn: pl.BlockSpec((pl.Element(tm), pl.Element(w)), fn)
    return pl.pallas_call(
        _final_kernel, grid=(batch, n_out),
        in_specs=[win(d, src), win(d, src), win(d, src1), win(2, src),
                  pl.BlockSpec((1, d), lambda b, i: (0, 0))],
        out_specs=pl.BlockSpec((None, tm, d), lambda b, i: (b, i, 0)),
        out_shape=jax.ShapeDtypeStruct((batch, seq, d), F32),
        compiler_params=_cparams(("arbitrary", "arbitrary")), name="final_norm",
    )(h, contrib, contrib, wts, nw.reshape(1, d))


def _matmul_kernel(x_ref, w_ref, o_ref):
    o_ref[...] = _dot(x_ref[...], w_ref[...])


def _inproj_call(u, w, tm_target=1056, tn=768):
    t, d = u.shape
    n = w.shape[1]
    tm = _row_tile(t, tm_target)
    assert n % tn == 0
    return pl.pallas_call(
        _matmul_kernel, grid=(t // tm, n // tn),
        in_specs=[pl.BlockSpec((tm, d), lambda i, j: (i, 0)), pl.BlockSpec((d, tn), lambda i, j: (0, j))],
        out_specs=pl.BlockSpec((tm, tn), lambda i, j: (i, j)),
        out_shape=jax.ShapeDtypeStruct((t, n), F32),
        compiler_params=_cparams(("arbitrary", "arbitrary")), name="in_proj",
    )(u, w)


def _hgrn2_kernel(q_ref, f_ref, i_ref, g_ref, par_ref, o_ref, st_ref, *, rows):
    s = pl.program_id(1)

    @pl.when(s == 0)
    def _():
        st_ref[...] = jnp.zeros_like(st_ref)

    c = HG_CHUNK
    ones = jnp.ones((HG_KDIM, 128), BF16)
    rid = lax.broadcasted_iota(jnp.int32, (c, 128), 0)
    scale = HG_KDIM ** -0.5

    def chunk(ci, carry):
        r0 = pl.multiple_of(ci * c, c)
        pad = (s * rows + r0 + rid) < LEAD_PAD
        for h in range(HG_HEADS):
            cols = slice(h * 128, (h + 1) * 128)
            a_lb = par_ref[0:1, cols]
            b_lb = par_ref[1:2, cols]
            c_lb = par_ref[2:3, cols]
            nw = par_ref[3:4, cols]
            z = f_ref[pl.ds(r0, c), cols]
            sg = _sigmoid(z)
            f = a_lb + b_lb * sg
            log_f = jnp.where(pad, 0.0, jnp.log(f))
            k = jnp.where(pad, 0.0, b_lb * (1.0 - sg) - c_lb)
            q = q_ref[pl.ds(r0, c), cols] * scale
            v = i_ref[pl.ds(r0, c), cols]
            cum = log_f
            for sh in (1, 2, 4, 8):
                cum = cum + jnp.where(rid >= sh, pltpu.roll(cum, sh, axis=0), 0.0)
            parts = []
            for s_ in range(c):
                rel = jnp.where(rid >= s_, cum - cum[s_:s_ + 1, :], NEG)
                parts.append((q * (k[s_:s_ + 1, :] * jnp.exp(rel))).astype(BF16))
            sc = _dot(jnp.concatenate(parts, axis=0), ones)
            o = jnp.zeros((c, 128), F32)
            for s_ in range(c):
                o = o + sc[s_ * c:(s_ + 1) * c, :] * v[s_:s_ + 1, :]
            st = st_ref[h]
            o = o + _dot_nt((q * jnp.exp(cum)).astype(BF16), st.astype(BF16))
            last = cum[c - 1:c, :]
            kd = (k * jnp.exp(last - cum)).astype(BF16)
            st_ref[h] = st * jnp.exp(last) + _dot_tn(v.astype(BF16), kd)
            ms = jnp.mean(o * o, axis=-1, keepdims=True)
            g = g_ref[pl.ds(r0, c), cols]
            o_ref[pl.ds(r0, c), cols] = (o * lax.rsqrt(ms + EPS) * nw * _silu(g)).astype(o_ref.dtype)
        return carry

    lax.fori_loop(0, rows // c, chunk, 0)


def _hgrn2_call(proj, par, batch, p):
    t = proj.shape[0]
    rows = _row_tile(p, 528)
    nb = p // rows
    w = HG_KEY
    blk = lambda j: pl.BlockSpec((rows, w), lambda b, s, j=j: (b * nb + s, C_HG // w + j))
    return pl.pallas_call(
        functools.partial(_hgrn2_kernel, rows=rows), grid=(batch, nb),
        in_specs=[blk(0), blk(1), blk(2), blk(3), pl.BlockSpec((8, w), lambda b, s: (0, 0))],
        out_specs=pl.BlockSpec((rows, w), lambda b, s: (b * nb + s, 0)),
        out_shape=jax.ShapeDtypeStruct((t, HG_WIDTH), BF16),
        scratch_shapes=[pltpu.VMEM((HG_HEADS, 128, HG_KDIM), F32)],
        compiler_params=_cparams(("arbitrary", "arbitrary")), name="hgrn2",
    )(proj, proj, proj, proj, par)


def _mlstm_kernel(v_ref, o_ref, q_ref, k_ref, sm_ref, par_ref, nw_ref, tri_ref, sel_ref, dg_ref,
                  y_ref, c_ref, m_ref):
    s = pl.program_id(1)

    @pl.when(s == 0)
    def _():
        c_ref[...] = jnp.zeros_like(c_ref)
        m_ref[...] = jnp.zeros_like(m_ref)

    n = CHUNK
    rid = lax.broadcasted_iota(jnp.int32, (n, 128), 0)
    pad = (s * n + rid) < LEAD_PAD
    pre = sm_ref[...] + par_ref[0:1, :]
    cap = GATE_CAP * jnp.tanh(pre * (1.0 / GATE_CAP))
    log_i = jnp.where(pad, NEG, cap)
    log_f = jnp.where(pad, 0.0, _log_sigmoid(cap))
    cum = _sel_dot(tri_ref[...], log_f)
    sel_i = sel_ref[0]
    sel_f = sel_ref[1]
    dg = dg_ref[...]
    ones = jnp.ones((n, n), BF16)
    cum_col = _dot_sel(cum, sel_f)
    cum_row = _sel_dot(ones, cum_col * dg)
    li_row = _sel_dot(ones, _dot_sel(log_i, sel_i) * dg)
    w = ML_HEADS * n
    tt = lax.broadcasted_iota(jnp.int32, (n, w), 0)
    ss = lax.broadcasted_iota(jnp.int32, (n, w), 1) & (n - 1)
    dmat = jnp.where(tt >= ss, cum_col - cum_row + li_row, NEG)
    scale = ML_QK ** -0.5
    lane = lax.broadcasted_iota(jnp.int32, (n, 128), 1)
    one_col = jnp.where(lane == 0, 1.0, 0.0).astype(BF16)
    for h in range(ML_HEADS):
        d_h = dmat[:, h * n:(h + 1) * n]
        cum_h = cum[:, LANE_MF + h:LANE_MF + h + 1]
        li_h = log_i[:, LANE_MI + h:LANE_MI + h + 1]
        m_st = m_ref[h:h + 1, 0:1]
        q = (q_ref[:, h * ML_QK:(h + 1) * ML_QK] * scale).astype(BF16)
        k = k_ref[:, h * ML_QK:(h + 1) * ML_QK]
        v_aug = jnp.concatenate([v_ref[:, h * ML_V:(h + 1) * ML_V].astype(BF16), one_col], axis=1)
        inter = cum_h + m_st
        m_t = jnp.maximum(inter, jnp.max(d_h, axis=-1, keepdims=True))
        w_inter = jnp.exp(inter - m_t)
        qk = _dot_nt(q, k.astype(BF16)) * jnp.exp(d_h - m_t)
        c_prev = c_ref[h]
        nd = _dot(qk.astype(BF16), v_aug) + w_inter * _dot(q, c_prev.astype(BF16))
        num = nd[:, :ML_V]
        den = nd[:, ML_V:ML_V + 1]
        hh = num / jnp.maximum(jnp.abs(den), jnp.exp(-m_t))
        tot = cum_h[n - 1:n, :]
        to_end = tot - cum_h + li_h
        m_loc = jnp.max(to_end, axis=0, keepdims=True)
        kw = (k * jnp.exp(to_end - m_loc)).astype(BF16)
        c_loc = _dot_tn(kw, v_aug)
        m_new = jnp.maximum(tot + m_st, m_loc)
        c_ref[h] = jnp.exp(tot + m_st - m_new) * c_prev + jnp.exp(m_loc - m_new) * c_loc
        m_ref[h:h + 1, :] = jnp.broadcast_to(m_new, (1, 128))
        ms = jnp.mean(hh * hh, axis=-1, keepdims=True)
        cols = slice(h * ML_V, (h + 1) * ML_V)
        y_ref[:, cols] = (hh * lax.rsqrt(ms + EPS) * nw_ref[:, cols] * _sigmoid(o_ref[:, cols])).astype(y_ref.dtype)


def _lane_select(lane0, heads, width):
    m = np.zeros((128, heads * width), np.float32)
    for h in range(heads):
        m[lane0 + h, h * width:(h + 1) * width] = 1.0
    return m


def _diag_mask(n, heads):
    return np.tile(np.eye(n, dtype=np.float32), (1, heads))


def _mlstm_call(proj, par, nw, batch, p):
    t = proj.shape[0]
    n = CHUNK
    nc = p // n
    tri = jnp.asarray(np.tril(np.ones((n, n), np.float32)), BF16)
    sel = jnp.asarray(np.stack([_lane_select(LANE_MI, ML_HEADS, n), _lane_select(LANE_MF, ML_HEADS, n)]), BF16)
    dg = jnp.asarray(_diag_mask(n, ML_HEADS), F32)
    blk = lambda w, off: pl.BlockSpec((n, w), lambda b, s: (b * nc + s, off // w))
    const = lambda shape: pl.BlockSpec(shape, lambda b, s: (0,) * len(shape))
    return pl.pallas_call(
        _mlstm_kernel, grid=(batch, nc),
        in_specs=[blk(ML_WIDTH, C_MV), blk(ML_WIDTH, C_MV + ML_WIDTH), blk(ML_QK_W, C_MQ),
                  blk(ML_QK_W, C_MQ + ML_QK_W), blk(128, C_SMALL),
                  const((8, 128)), const((1, ML_WIDTH)), const((n, n)), const((2, 128, ML_HEADS * n)),
                  const((n, ML_HEADS * n))],
        out_specs=pl.BlockSpec((n, ML_WIDTH), lambda b, s: (b * nc + s, 0)),
        out_shape=jax.ShapeDtypeStruct((t, ML_WIDTH), BF16),
        scratch_shapes=[pltpu.VMEM((ML_HEADS, ML_QK, 2 * ML_V), F32), pltpu.VMEM((8, 128), F32)],
        compiler_params=_cparams(("arbitrary", "arbitrary")), name="mlstm",
    )(proj, proj, proj, proj, proj, par, nw, tri, sel, dg)


def _ssd_kernel(z_ref, x_ref, b_ref, c_ref, sm_ref, cw_ref, cb_ref, par_ref, dsk_ref, nw_ref,
                tri_ref, sel_ref, dg_ref, y_ref, xs_ref, bs_ref, cs_ref, st_ref):
    s = pl.program_id(1)
    n = CHUNK
    tail = 8

    @pl.when(s == 0)
    def _():
        st_ref[...] = jnp.zeros_like(st_ref)
        xs_ref[0:tail, :] = jnp.zeros((tail, xs_ref.shape[1]), F32)
        bs_ref[0:tail, :] = jnp.zeros((tail, bs_ref.shape[1]), F32)
        cs_ref[0:tail, :] = jnp.zeros((tail, cs_ref.shape[1]), F32)

    def conv_silu(src_ref, scr_ref, c0, width, rowmask):
        scr_ref[tail:tail + n, :] = src_ref[...]
        acc = cb_ref[:, c0:c0 + width]
        for j in range(CONV_W):
            off = tail - (CONV_W - 1) + j
            acc = acc + cw_ref[j:j + 1, c0:c0 + width] * scr_ref[off:off + n, :]
        scr_ref[0:tail, :] = scr_ref[n:n + tail, :]
        return jnp.where(rowmask, 0.0, _silu(acc))

    def padmask(width):
        return (s * n + lax.broadcasted_iota(jnp.int32, (n, width), 0)) < LEAD_PAD

    x = conv_silu(x_ref, xs_ref, 0, SSM_WIDTH, padmask(SSM_WIDTH))
    bm = conv_silu(b_ref, bs_ref, SSM_WIDTH, SSM_BC, padmask(SSM_BC))
    cm = conv_silu(c_ref, cs_ref, SSM_WIDTH + SSM_BC, SSM_BC, padmask(SSM_BC))

    pre = sm_ref[...] + par_ref[0:1, :]
    dt = jnp.maximum(pre, 0.0) + jnp.log1p(jnp.exp(-jnp.abs(pre)))
    dt = jnp.where(padmask(128), 0.0, dt)
    da = dt * (-jnp.exp(par_ref[1:2, :]))
    cum = _sel_dot(tri_ref[...], da)
    sel = sel_ref[...]
    ones = jnp.ones((n, n), BF16)
    dt_col = _dot_sel(dt, sel)
    cum_col = _dot_sel(cum, sel)
    cum_row = _sel_dot(ones, cum_col * dg_ref[...])
    w = SSM_WIDTH
    tt = lax.broadcasted_iota(jnp.int32, (n, w), 0)
    ss = lax.broadcasted_iota(jnp.int32, (n, w), 1) & (n - 1)
    decay = jnp.exp(jnp.where(tt >= ss, cum_col - cum_row, NEG))
    xdt = x * dt_col
    last = cum_col[n - 1:n, :]
    wend = (xdt * jnp.exp(last - cum_col)).astype(BF16)
    chunk_decay = jnp.exp(last)
    ecum = jnp.exp(cum_col)
    gw = SSM_HPG * SSM_HEADDIM
    rr = lax.broadcasted_iota(jnp.int32, (gw, gw), 0) // SSM_HEADDIM
    cc = lax.broadcasted_iota(jnp.int32, (gw, gw), 1) // SSM_HEADDIM
    blockdiag = rr == cc
    ys = []
    for g in range(SSM_GROUPS):
        gl = slice(g * gw, (g + 1) * gw)
        sl = slice(g * SSM_STATE, (g + 1) * SSM_STATE)
        cm_g = cm[:, sl].astype(BF16)
        bm_g = bm[:, sl].astype(BF16)
        cb = _dot_nt(cm_g, jnp.concatenate([bm_g] * SSM_HPG, axis=0))
        m = (cb * decay[:, gl]).astype(BF16)
        xdt_g = xdt[:, gl]
        bd = jnp.where(blockdiag, jnp.concatenate([xdt_g] * SSM_HPG, axis=0), 0.0).astype(BF16)
        st = st_ref[g]
        y_g = _dot(m, bd) + ecum[:, gl] * _dot(cm_g, st.astype(BF16))
        st_ref[g] = st * chunk_decay[:, gl] + _dot_tn(bm_g, wend[:, gl])
        ys.append(y_g)
    y = jnp.concatenate(ys, axis=1) + x * dsk_ref[...]
    y = y * _silu(z_ref[...])
    outs = []
    for g in range(SSM_GROUPS):
        gl = slice(g * gw, (g + 1) * gw)
        y_g = y[:, gl]
        ms = jnp.mean(y_g * y_g, axis=-1, keepdims=True)
        outs.append(y_g * lax.rsqrt(ms + EPS))
    y_ref[...] = (jnp.concatenate(outs, axis=1) * nw_ref[...]).astype(y_ref.dtype)


def _ssd_call(proj, cw, cb, par, dsk, nw, batch, p):
    t = proj.shape[0]
    n = CHUNK
    nc = p // n
    tri = jnp.asarray(np.tril(np.ones((n, n), np.float32)), BF16)
    sel = jnp.asarray(_lane_select(LANE_DT, SSM_HEADS, SSM_HEADDIM), BF16)
    dg = jnp.asarray(_diag_mask(n, SSM_HEADS), F32)
    blk = lambda w, off: pl.BlockSpec((n, w), lambda b, s: (b * nc + s, off // w))
    const = lambda shape: pl.BlockSpec(shape, lambda b, s: (0,) * len(shape))
    cch = SSM_WIDTH + 2 * SSM_BC
    return pl.pallas_call(
        _ssd_kernel, grid=(batch, nc),
        in_specs=[blk(SSM_WIDTH, C_SZ), blk(SSM_WIDTH, C_SX), blk(SSM_BC, C_SX + SSM_WIDTH),
                  blk(SSM_BC, C_SX + SSM_WIDTH + SSM_BC), blk(128, C_SMALL),
                  const((CONV_W, cch)), const((1, cch)), const((8, 128)), const((1, SSM_WIDTH)),
                  const((1, SSM_WIDTH)), const((n, n)), const((128, SSM_WIDTH)), const((n, SSM_WIDTH))],
        out_specs=pl.BlockSpec((n, SSM_WIDTH), lambda b, s: (b * nc + s, 0)),
        out_shape=jax.ShapeDtypeStruct((t, SSM_WIDTH), BF16),
        scratch_shapes=[pltpu.VMEM((n + 8, SSM_WIDTH), F32), pltpu.VMEM((n + 8, SSM_BC), F32),
                        pltpu.VMEM((n + 8, SSM_BC), F32),
                        pltpu.VMEM((SSM_GROUPS, SSM_STATE, SSM_HPG * SSM_HEADDIM), F32)],
        compiler_params=_cparams(("arbitrary", "arbitrary")), name="ssd",
    )(proj, proj, proj, proj, proj, cw, cb, par, dsk, nw, tri, sel, dg)


LANE_E0 = N_GROUPS_MOE
RT_E, RT_RANK, RT_W = 0, 2, 4


def _first_max(vals, lane):
    m = jnp.max(vals, axis=-1, keepdims=True)
    idx = jnp.min(jnp.where(vals == m, lane, 128), axis=-1, keepdims=True)
    return m, idx


def _outproj_kernel(ya_ref, yb_ref, yc_ref, h_ref, w_ref, nw_ref, rh_ref, rl_ref, rb_ref, tri_ref,
                    hm_ref, u_ref, rt_ref, cnt_ref):
    a0, a1 = HG_WIDTH, HG_WIDTH + ML_WIDTH
    h = h_ref[...]
    h = h + _dot(ya_ref[...], w_ref[0:a0, :])
    h = h + _dot(yb_ref[...], w_ref[a0:a1, :])
    h = h + _dot(yc_ref[...], w_ref[a1:, :])
    hm_ref[...] = h
    ms = jnp.mean(h * h, axis=-1, keepdims=True)
    u = h * lax.rsqrt(ms + EPS) * nw_ref[...]
    u_ref[...] = u
    u_hi = u.astype(BF16)
    u_lo = (u - u_hi.astype(F32)).astype(BF16)
    lg = _dot(u_hi, rh_ref[...]) + (_dot(u_lo, rh_ref[...]) + _dot(u_hi, rl_ref[...])) + rb_ref[...]

    tm = lg.shape[0]
    lane = lax.broadcasted_iota(jnp.int32, (tm, 128), 1)
    g_mask = lane < N_GROUPS_MOE
    g_max, g_sel = _first_max(jnp.where(g_mask, lg, NEG), lane)
    g_gate = 1.0 / jnp.sum(jnp.where(g_mask, jnp.exp(lg - g_max), 0.0), axis=-1, keepdims=True)
    lo = LANE_E0 + g_sel * EXPERTS_PER_GROUP
    e_vals = jnp.where((lane >= lo) & (lane < lo + EXPERTS_PER_GROUP), lg, NEG)
    v1, i1 = _first_max(e_vals, lane)
    v2, i2 = _first_max(jnp.where(lane == i1, NEG, e_vals), lane)
    a = jnp.exp(v2 - v1)
    w1 = g_gate / (1.0 + a)
    w2 = w1 * a
    @pl.when(pl.program_id(0) == 0)
    def _():
        cnt_ref[...] = jnp.zeros_like(cnt_ref)

    hit1 = lane == i1
    hit2 = lane == i2
    onehot = jnp.where(hit1 | hit2, 1.0, 0.0)
    before = _dot(tri_ref[...], onehot.astype(BF16)) + cnt_ref[0:1, :]
    r1 = jnp.sum(jnp.where(hit1, before, 0.0), axis=-1, keepdims=True)
    r2 = jnp.sum(jnp.where(hit2, before, 0.0), axis=-1, keepdims=True)
    cnt_ref[...] = cnt_ref[...] + jnp.sum(onehot, axis=0, keepdims=True)
    rec = jnp.zeros((tm, 128), F32)
    for ln, val in ((RT_E, (i1 - LANE_E0).astype(F32)), (RT_E + 1, (i2 - LANE_E0).astype(F32)),
                    (RT_RANK, r1), (RT_RANK + 1, r2), (RT_W, w1), (RT_W + 1, w2)):
        rec = jnp.where(lane == ln, val, rec)
    rt_ref[...] = rec


def _outproj_call(ya, yb, yc, h, w, nw, r_hi, r_lo, r_bias, tm_target=352):
    t, d = h.shape
    tm = _row_tile(t, tm_target)
    tri = jnp.asarray(np.tril(np.ones((tm, tm), np.float32), -1), BF16)
    row = lambda i: (i, 0)
    const = lambda shape: pl.BlockSpec(shape, lambda i: (0, 0))
    return pl.pallas_call(
        _outproj_kernel, grid=(t // tm,),
        in_specs=[pl.BlockSpec((tm, HG_WIDTH), row), pl.BlockSpec((tm, ML_WIDTH), row),
                  pl.BlockSpec((tm, SSM_WIDTH), row), pl.BlockSpec((tm, d), row),
                  const((D_MIX, d)), const((1, d)), const((d, 128)), const((d, 128)), const((1, 128)),
                  const((tm, tm))],
        out_specs=[pl.BlockSpec((tm, d), row), pl.BlockSpec((tm, d), row), pl.BlockSpec((tm, 128), row),
                   const((8, 128))],
        out_shape=[jax.ShapeDtypeStruct((t, d), F32), jax.ShapeDtypeStruct((t, d), F32),
                   jax.ShapeDtypeStruct((t, 128), F32), jax.ShapeDtypeStruct((8, 128), F32)],
        compiler_params=_cparams(("arbitrary",)), name="out_proj_router",
    )(ya, yb, yc, h, w, nw.reshape(1, d), r_hi, r_lo, r_bias, tri)


def _moe_kernel(be_ref, nu_ref, src0_ref, srcn_ref, dst_ref, u_hbm, w1_ref, w3_ref, w2_ref, o_hbm,
                xbuf, ybuf, w1b, w3b, w2b, gsem, ssem):
    i = pl.program_id(0)
    bm = MOE_BM
    n_used = nu_ref[0]
    slot = i % 2

    def gather(idx_ref, sl):
        def body(r, c):
            tok = idx_ref[0, r]
            pltpu.make_async_copy(u_hbm.at[pl.ds(tok, 1), :], xbuf.at[sl, pl.ds(r, 1), :], gsem.at[sl]).start()
            return c
        lax.fori_loop(0, bm, body, 0)

    def scatter(sl):
        def body(r, c):
            row = dst_ref[0, r]
            pltpu.make_async_copy(ybuf.at[sl, pl.ds(r, 1), :], o_hbm.at[pl.ds(row, 1), :], ssem.at[sl]).start()
            return c
        lax.fori_loop(0, bm, body, 0)

    def wait_gather(sl):
        pltpu.make_async_copy(u_hbm.at[pl.ds(0, bm), :], xbuf.at[sl], gsem.at[sl]).wait()

    def wait_scatter(sl):
        pltpu.make_async_copy(ybuf.at[sl], o_hbm.at[pl.ds(0, bm), :], ssem.at[sl]).wait()

    @pl.when(i == 0)
    def _():
        gather(src0_ref, 0)
        ybuf[...] = jnp.zeros_like(ybuf)
        n_real = o_hbm.shape[0] - 2 * bm
        for sl in range(2):
            fill = pltpu.make_async_copy(ybuf.at[sl], o_hbm.at[pl.ds(n_real + sl * bm, bm), :], ssem.at[sl])
            fill.start()
            fill.wait()

    @pl.when(i + 1 < n_used)
    def _():
        gather(srcn_ref, 1 - slot)

    @pl.when(i < n_used)
    def _():
        first = jnp.logical_or(i == 0, be_ref[i] != be_ref[jnp.maximum(i - 1, 0)])

        @pl.when(first)
        def _():
            w1b[...] = w1_ref[...].astype(BF16)
            w3b[...] = w3_ref[...].astype(BF16)
            w2b[...] = w2_ref[...].astype(BF16)

        wait_gather(slot)
        x = xbuf[slot].astype(BF16)
        hid = _silu(_dot(x, w1b[...])) * _dot(x, w3b[...])
        y = _dot(hid.astype(BF16), w2b[...])

        @pl.when(i >= 2)
        def _():
            wait_scatter(slot)

        ybuf[slot] = y
        scatter(slot)

    @pl.when(i == n_used - 1)
    def _():
        wait_scatter(slot)

        @pl.when(i >= 1)
        def _():
            wait_scatter(1 - slot)


def _moe_call(block_e, n_used, src_tok, dst_row, u, w1, w3, w2, layer, n_out_rows):
    t, d = u.shape
    f = w1.shape[-1]
    bm = MOE_BM
    nb = block_e.shape[0]
    wmap = lambda i, be, nu: (layer, be[i], 0, 0)
    src_tok = src_tok.reshape(nb, 1, bm)
    dst_row = dst_row.reshape(nb, 1, bm)
    idx = lambda fn: pl.BlockSpec((None, 1, bm), fn, memory_space=pltpu.SMEM)
    return pl.pallas_call(
        _moe_kernel,
        grid_spec=pltpu.PrefetchScalarGridSpec(
            num_scalar_prefetch=2, grid=(nb,),
            in_specs=[idx(lambda i, be, nu: (0, 0, 0)),
                      idx(lambda i, be, nu: (jnp.minimum(i + 1, nb - 1), 0, 0)),
                      idx(lambda i, be, nu: (i, 0, 0)),
                      pl.BlockSpec(memory_space=pl.ANY),
                      pl.BlockSpec((None, None, d, f), wmap), pl.BlockSpec((None, None, d, f), wmap),
                      pl.BlockSpec((None, None, f, d), wmap)],
            out_specs=pl.BlockSpec(memory_space=pl.ANY),
            scratch_shapes=[pltpu.VMEM((2, bm, d), F32), pltpu.VMEM((2, bm, d), F32),
                            pltpu.VMEM((d, f), BF16), pltpu.VMEM((d, f), BF16), pltpu.VMEM((f, d), BF16),
                            pltpu.SemaphoreType.DMA((2,)), pltpu.SemaphoreType.DMA((2,))]),
        out_shape=jax.ShapeDtypeStruct((n_out_rows, d), F32),
        compiler_params=_cparams(("arbitrary",)), name="moe_ffn",
    )(block_e, n_used, src_tok, src_tok, dst_row, u, w1, w3, w2)


def _route_tables(rt, cnt, t):
    bm = MOE_BM
    tk = t * TOP_K
    e = rt[:, RT_E:RT_E + TOP_K].astype(jnp.int32)
    rank = rt[:, RT_RANK:RT_RANK + TOP_K].astype(jnp.int32)
    wts = rt[:, RT_W:RT_W + TOP_K]
    counts = cnt[0, LANE_E0:LANE_E0 + N_EXPERTS].astype(jnp.int32)
    padded = (counts + bm - 1) // bm * bm
    pad_ends = jnp.cumsum(padded)
    pad_starts = pad_ends - padded
    dest = (pad_starts[e] + rank).reshape(-1)
    n_blocks = -(-tk // bm) + N_EXPERTS
    n_rows = n_blocks * bm
    inv = jnp.full((n_rows,), -1, jnp.int32).at[dest].set(jnp.arange(tk, dtype=jnp.int32))
    tok, slot = inv // TOP_K, inv % TOP_K
    src_tok = jnp.where(inv >= 0, tok, 0)
    dst_row = jnp.where(inv >= 0, slot * t + tok, tk + (jnp.arange(n_rows, dtype=jnp.int32) % (2 * bm)))
    block_e = jnp.minimum(jnp.searchsorted(pad_ends, jnp.arange(n_blocks, dtype=jnp.int32) * bm, side='right'),
                          N_EXPERTS - 1).astype(jnp.int32)
    n_used = (pad_ends[-1] // bm).astype(jnp.int32).reshape(1)
    last_e = block_e[jnp.maximum(n_used[0] - 1, 0)]
    block_e = jnp.where(jnp.arange(n_blocks) < n_used[0], block_e, last_e)
    return block_e, n_used, src_tok, dst_row, wts, tk + 2 * bm


def _lane_row(pairs):
    row = jnp.zeros((128,), F32)
    for lane0, vals in pairs:
        row = row.at[lane0:lane0 + vals.shape[0]].set(vals.astype(F32))
    return row


def kernel(x, meta_tokens, hg_lb_logits, norm_mix_w, w_in, hg_norm_w, ml_b_i, ml_b_f, ml_norm_w,
           ssm_conv_w, ssm_conv_b, ssm_dt_bias, ssm_a_log, ssm_d, ssm_norm_w, w_out, norm_ffn_w,
           moe_w_group, moe_b_group, moe_w_router, moe_b_router, moe_w1, moe_w3, moe_w2, final_norm_w):
    batch, seq, d = x.shape
    depth = w_in.shape[0]
    p = LEAD_PAD + N_META + seq
    t = batch * p
    meta = jnp.broadcast_to(meta_tokens.astype(x.dtype)[None], (batch, N_META, d))
    h = jnp.concatenate([jnp.zeros((batch, LEAD_PAD, d), x.dtype), meta, x], axis=1).reshape(t, d)

    lb_w = jax.nn.softmax(hg_lb_logits.astype(F32), axis=0)
    lower_bounds = jnp.cumsum(lb_w, axis=0) - lb_w[0]

    contrib = wts = None
    for layer in range(depth):
        w = w_in[layer]
        o_mq, o_mv, o_mi, o_sz = HG_KEY * 2 + HG_WIDTH * 2, 2560, 3584, 3592
        o_sx = o_sz + SSM_WIDTH
        o_dt = o_sx + SSM_WIDTH + 2 * SSM_BC
        w_perm = jnp.concatenate([
            w[:, :o_mq], w[:, o_sz:o_sx], w[:, o_sx:o_dt], w[:, o_mv:o_mi], w[:, o_mq:o_mv],
            w[:, o_mi:o_sz], w[:, o_dt:], jnp.zeros((d, N_PROJ - C_SMALL - 2 * ML_HEADS - SSM_HEADS), w.dtype)],
            axis=1).astype(BF16)
        lb = lower_bounds[layer]
        lbf = jnp.maximum(lb, LB_FLOOR)
        hg_par = jnp.zeros((8, HG_KEY), F32).at[0].set(lbf).at[1].set(1.0 - lb).at[2].set(lbf - lb).at[3].set(hg_norm_w[layer])
        ml_par = jnp.zeros((8, 128), F32).at[0].set(_lane_row([(LANE_MI, ml_b_i[layer]), (LANE_MF, ml_b_f[layer])]))
        ss_par = jnp.zeros((8, 128), F32).at[0].set(_lane_row([(LANE_DT, ssm_dt_bias[layer])]))
        ss_par = ss_par.at[1].set(_lane_row([(LANE_DT, ssm_a_log[layer])]))
        dskip = jnp.repeat(ssm_d[layer].astype(F32), SSM_HEADDIM).reshape(1, SSM_WIDTH)
        w_r = jnp.concatenate([moe_w_group[layer],
                               moe_w_router[layer].transpose(1, 0, 2).reshape(d, N_EXPERTS),
                               jnp.zeros((d, 128 - N_GROUPS_MOE - N_EXPERTS), F32)], axis=1)
        r_hi = w_r.astype(BF16)
        r_lo = (w_r - r_hi.astype(F32)).astype(BF16)

        if layer == 0:
            (u,) = _norm_call(h, norm_mix_w[layer], write_h=False, u_dtype=BF16)
        else:
            h, u = _norm_call(h, norm_mix_w[layer], contrib, wts, write_h=True, u_dtype=BF16)
        proj = _inproj_call(u, w_perm)
        ya = _hgrn2_call(proj, hg_par, batch, p)
        yb = _mlstm_call(proj, ml_par, ml_norm_w[layer].reshape(1, ML_WIDTH), batch, p)
        yc = _ssd_call(proj, ssm_conv_w[layer], ssm_conv_b[layer].reshape(1, -1), ss_par, dskip,
                       ssm_norm_w[layer].reshape(1, SSM_WIDTH), batch, p)
        r_bias = _lane_row([(0, moe_b_group[layer]), (LANE_E0, moe_b_router[layer].reshape(-1))]).reshape(1, 128)
        h, u_ffn, rt, cnt = _outproj_call(ya, yb, yc, h, w_out[layer].astype(BF16), norm_ffn_w[layer],
                                          r_hi, r_lo, r_bias)
        block_e, n_used, src_tok, dst_row, wts, n_out_rows = _route_tables(rt, cnt, t)
        contrib = _moe_call(block_e, n_used, src_tok, dst_row, u_ffn, moe_w1, moe_w3, moe_w2, layer, n_out_rows)
    return _final_call(h, contrib, wts, final_norm_w, batch, p)
```

```python
import functools

import jax
import jax.numpy as jnp
import numpy as np
from jax import lax
from jax.experimental import pallas as pl
from jax.experimental.pallas import tpu as pltpu

F32 = jnp.float32
BF16 = jnp.bfloat16

D_MODEL = 2048
N_META = 16
CHUNK = 64
HG_CHUNK = 16
LEAD_PAD = CHUNK - N_META
EPS = 1e-6
NEG = -1e30
LB_FLOOR = 1e-30

HG_HEADS = 4
HG_KDIM = 128
HG_KEY = HG_HEADS * HG_KDIM
HG_WIDTH = HG_HEADS * 128

ML_HEADS = 4
ML_QK = 64
ML_V = 128
ML_QK_W = ML_HEADS * ML_QK
ML_WIDTH = ML_HEADS * ML_V
GATE_CAP = 15.0

SSM_HEADS = 16
SSM_HEADDIM = 64
SSM_WIDTH = SSM_HEADS * SSM_HEADDIM
SSM_STATE = 128
SSM_GROUPS = 4
SSM_HPG = SSM_HEADS // SSM_GROUPS
SSM_BC = SSM_GROUPS * SSM_STATE
CONV_W = 4

D_MIX = HG_WIDTH + ML_WIDTH + SSM_WIDTH

N_GROUPS_MOE = 4
EXPERTS_PER_GROUP = 8
N_EXPERTS = N_GROUPS_MOE * EXPERTS_PER_GROUP
TOP_K = 2
D_EXPERT = 512
MOE_BM = 128

C_HG = 0
C_SZ = 2048
C_SX = 3072
C_MV = 5120
C_MQ = 6144
C_SMALL = 6656
N_PROJ = 6912
LANE_MI = 0
LANE_MF = ML_HEADS
LANE_DT = 2 * ML_HEADS

VMEM_LIMIT = 56 * 1024 * 1024


def _cparams(sem):
    return pltpu.CompilerParams(dimension_semantics=sem, vmem_limit_bytes=VMEM_LIMIT)


def _row_tile(n, target, mult=16):
    best = None
    for t in range(mult, min(n, target) + 1, mult):
        if n % t == 0:
            best = t
    assert best is not None, (n, target, mult)
    return best


def _split3(x):
    hi = x.astype(BF16)
    r = x - hi.astype(F32)
    mid = r.astype(BF16)
    lo = (r - mid.astype(F32)).astype(BF16)
    return hi, mid, lo


def _dot(a, b):
    return jnp.dot(a, b, preferred_element_type=F32)


def _sel_dot(sel, x):
    hi, mid, lo = _split3(x)
    return _dot(sel, hi) + _dot(sel, mid) + _dot(sel, lo)


def _dot_sel(x, sel):
    hi, mid, lo = _split3(x)
    return _dot(hi, sel) + _dot(mid, sel) + _dot(lo, sel)


def _dot_nt(a, b):
    return lax.dot_general(a, b, (((1,), (1,)), ((), ())), preferred_element_type=F32)


def _dot_tn(a, b):
    return lax.dot_general(a, b, (((0,), (0,)), ((), ())), preferred_element_type=F32)


def _log_sigmoid(x):
    return jnp.minimum(x, 0.0) - jnp.log1p(jnp.exp(-jnp.abs(x)))


def _sigmoid(x):
    return 1.0 / (1.0 + jnp.exp(-x))


def _silu(x):
    return x * _sigmoid(x)


def _norm_kernel(*refs, combine, write_h):
    if combine:
        h_ref, c0_ref, c1_ref, wt_ref, nw_ref = refs[:5]
        outs = refs[5:]
        wt = wt_ref[...]
        h = h_ref[...] + wt[:, 0:1] * c0_ref[...] + wt[:, 1:2] * c1_ref[...]
    else:
        h_ref, nw_ref = refs[:2]
        outs = refs[2:]
        h = h_ref[...]
    if write_h:
        outs[0][...] = h
    ms = jnp.mean(h * h, axis=-1, keepdims=True)
    u_ref = outs[-1]
    u_ref[...] = (h * lax.rsqrt(ms + EPS) * nw_ref[...]).astype(u_ref.dtype)


def _norm_call(h, nw, contrib=None, wts=None, *, write_h, u_dtype, tm_target=264):
    t, d = h.shape
    tm = _row_tile(t, tm_target)
    combine = contrib is not None
    row = lambda i: (i, 0)
    in_specs = [pl.BlockSpec((tm, d), row)]
    args = [h]
    if combine:
        in_specs += [pl.BlockSpec((tm, d), row), pl.BlockSpec((tm, d), lambda i: (t // tm + i, 0)),
                     pl.BlockSpec((tm, 2), row)]
        args += [contrib, contrib, wts]
    in_specs.append(pl.BlockSpec((1, d), lambda i: (0, 0)))
    args.append(nw.reshape(1, d))
    out_shape, out_specs = [], []
    if write_h:
        out_shape.append(jax.ShapeDtypeStruct((t, d), F32))
        out_specs.append(pl.BlockSpec((tm, d), row))
    out_shape.append(jax.ShapeDtypeStruct((t, d), u_dtype))
    out_specs.append(pl.BlockSpec((tm, d), row))
    return pl.pallas_call(
        functools.partial(_norm_kernel, combine=combine, write_h=write_h),
        grid=(t // tm,), in_specs=in_specs, out_specs=out_specs, out_shape=out_shape,
        compiler_params=_cparams(("arbitrary",)), name="combine_norm",
    )(*args)


def _final_kernel(h_ref, c0_ref, c1_ref, wt_ref, nw_ref, o_ref):
    wt = wt_ref[...]
    h = h_ref[...] + wt[:, 0:1] * c0_ref[...] + wt[:, 1:2] * c1_ref[...]
    ms = jnp.mean(h * h, axis=-1, keepdims=True)
    o_ref[...] = h * lax.rsqrt(ms + EPS) * nw_ref[...]


def _final_call(h, contrib, wts, nw, batch, p):
    t, d = h.shape
    seq = p - CHUNK
    tm = _row_tile(seq, 256)
    n_out = seq // tm
    src = lambda b, i: (pl.multiple_of(b * p + CHUNK + i * tm, CHUNK), 0)
    src1 = lambda b, i: (pl.multiple_of(t + b * p + CHUNK + i * tm, CHUNK), 0)
    win = lambda w, fn: pl.BlockSpec((pl.Element(tm), pl.Element(w)), fn)
    return pl.pallas_call(
        _final_kernel, grid=(batch, n_out),
        in_specs=[win(d, src), win(d, src), win(d, src1), win(2, src),
                  pl.BlockSpec((1, d), lambda b, i: (0, 0))],
        out_specs=pl.BlockSpec((None, tm, d), lambda b, i: (b, i, 0)),
        out_shape=jax.ShapeDtypeStruct((batch, seq, d), F32),
        compiler_params=_cparams(("arbitrary", "arbitrary")), name="final_norm",
    )(h, contrib, contrib, wts, nw.reshape(1, d))


def _matmul_kernel(x_ref, w_ref, o_ref):
    o_ref[...] = _dot(x_ref[...], w_ref[...])


def _inproj_call(u, w, tm_target=1056, tn=768):
    t, d = u.shape
    n = w.shape[1]
    tm = _row_tile(t, tm_target)
    assert n % tn == 0
    return pl.pallas_call(
        _matmul_kernel, grid=(t // tm, n // tn),
        in_specs=[pl.BlockSpec((tm, d), lambda i, j: (i, 0)), pl.BlockSpec((d, tn), lambda i, j: (0, j))],
        out_specs=pl.BlockSpec((tm, tn), lambda i, j: (i, j)),
        out_shape=jax.ShapeDtypeStruct((t, n), F32),
        compiler_params=_cparams(("arbitrary", "arbitrary")), name="in_proj",
    )(u, w)


def _hgrn2_kernel(q_ref, f_ref, i_ref, g_ref, par_ref, o_ref, st_ref, *, rows):
    s = pl.program_id(1)

    @pl.when(s == 0)
    def _():
        st_ref[...] = jnp.zeros_like(st_ref)

    c = HG_CHUNK
    ones = jnp.ones((HG_KDIM, 128), BF16)
    rid = lax.broadcasted_iota(jnp.int32, (c, 128), 0)
    scale = HG_KDIM ** -0.5

    hc = c // 2

    def chunk(ci, carry):
        r0 = pl.multiple_of(ci * c, c)
        pad = (s * rows + r0 + rid) < LEAD_PAD

        def front(h):
            cols = slice(h * 128, (h + 1) * 128)
            a_lb = par_ref[0:1, cols]
            b_lb = par_ref[1:2, cols]
            c_lb = par_ref[2:3, cols]
            z = f_ref[pl.ds(r0, c), cols]
            sg = _sigmoid(z)
            f = a_lb + b_lb * sg
            log_f = jnp.where(pad, 0.0, jnp.log(f))
            k = jnp.where(pad, 0.0, b_lb * (1.0 - sg) - c_lb)
            q = q_ref[pl.ds(r0, c), cols] * scale
            v = i_ref[pl.ds(r0, c), cols]
            cum = log_f
            for sh in (1, 2, 4, 8):
                cum = cum + jnp.where(rid >= sh, pltpu.roll(cum, sh, axis=0), 0.0)
            parts = []
            for s_ in range(c):
                lo = 0 if s_ < hc else hc
                rel = jnp.where(rid[lo:] >= s_, cum[lo:] - cum[s_:s_ + 1, :], NEG)
                parts.append(q[lo:] * (k[s_:s_ + 1, :] * jnp.exp(rel)))
            sc = _dot(jnp.concatenate(parts, axis=0).astype(BF16), ones)
            st = st_ref[h]
            o_inter = _dot_nt((q * jnp.exp(cum)).astype(BF16), st.astype(BF16))
            last = cum[c - 1:c, :]
            kd = (k * jnp.exp(last - cum)).astype(BF16)
            st_ref[h] = st * jnp.exp(last) + _dot_tn(v.astype(BF16), kd)
            return sc, o_inter, v

        def back(h, sc, o_inter, v):
            cols = slice(h * 128, (h + 1) * 128)
            o_top = o_inter[:hc]
            o_bot = o_inter[hc:]
            for s_ in range(hc):
                o_top = o_top + sc[s_ * c:s_ * c + hc, :] * v[s_:s_ + 1, :]
                o_bot = o_bot + sc[s_ * c + hc:(s_ + 1) * c, :] * v[s_:s_ + 1, :]
            for s_ in range(hc, c):
                r_ = hc * c + (s_ - hc) * hc
                o_bot = o_bot + sc[r_:r_ + hc, :] * v[s_:s_ + 1, :]
            o = jnp.concatenate([o_top, o_bot], axis=0)
            ms = jnp.mean(o * o, axis=-1, keepdims=True)
            g = g_ref[pl.ds(r0, c), cols]
            o_ref[pl.ds(r0, c), cols] = (o * lax.rsqrt(ms + EPS) * par_ref[3:4, cols] * _silu(g)).astype(o_ref.dtype)

        pending = front(0)
        for h in range(1, HG_HEADS):
            nxt = front(h)
            back(h - 1, *pending)
            pending = nxt
        back(HG_HEADS - 1, *pending)
        return carry

    n_chunks = rows // c
    lax.fori_loop(0, n_chunks, chunk, 0, unroll=3 if n_chunks % 3 == 0 else 1)


def _hgrn2_call(proj, par, batch, p):
    t = proj.shape[0]
    rows = _row_tile(p, 528)
    nb = p // rows
    w = HG_KEY
    blk = lambda j: pl.BlockSpec((rows, w), lambda b, s, j=j: (b * nb + s, C_HG // w + j))
    return pl.pallas_call(
        functools.partial(_hgrn2_kernel, rows=rows), grid=(batch, nb),
        in_specs=[blk(0), blk(1), blk(2), blk(3), pl.BlockSpec((8, w), lambda b, s: (0, 0))],
        out_specs=pl.BlockSpec((rows, w), lambda b, s: (b * nb + s, 0)),
        out_shape=jax.ShapeDtypeStruct((t, HG_WIDTH), BF16),
        scratch_shapes=[pltpu.VMEM((HG_HEADS, 128, HG_KDIM), F32)],
        compiler_params=_cparams(("arbitrary", "arbitrary")), name="hgrn2",
    )(proj, proj, proj, proj, par)


def _mlstm_kernel(v_ref, o_ref, q_ref, k_ref, sm_ref, par_ref, nw_ref, tri_ref, sel_ref, dg_ref,
                  y_ref, c_ref, m_ref):
    s = pl.program_id(0)

    @pl.when(s == 0)
    def _():
        c_ref[...] = jnp.zeros_like(c_ref)
        m_ref[...] = jnp.zeros_like(m_ref)

    n = CHUNK
    rid = lax.broadcasted_iota(jnp.int32, (n, 128), 0)
    pad = (s * n + rid) < LEAD_PAD
    sel_i = sel_ref[0]
    sel_f = sel_ref[1]
    dg = dg_ref[...]
    w = ML_HEADS * n
    tt = lax.broadcasted_iota(jnp.int32, (n, w), 0)
    ss = lax.broadcasted_iota(jnp.int32, (n, w), 1) & (n - 1)
    causal = tt >= ss
    scale = ML_QK ** -0.5
    lane = lax.broadcasted_iota(jnp.int32, (n, 128), 1)
    one_col = jnp.where(lane == 0, 1.0, 0.0).astype(BF16)
    for b in range(v_ref.shape[0]):
        pre = sm_ref[b] + par_ref[0:1, :]
        cap = GATE_CAP * jnp.tanh(pre * (1.0 / GATE_CAP))
        log_i = jnp.where(pad, NEG, cap)
        log_f = jnp.where(pad, 0.0, _log_sigmoid(cap))
        cum = _sel_dot(tri_ref[...], log_f)
        cum_col = _dot_sel(cum, sel_f)
        cum_row = jnp.sum(cum_col * dg, axis=0, keepdims=True)
        li_row = jnp.sum(_dot_sel(log_i, sel_i) * dg, axis=0, keepdims=True)
        dmat = jnp.where(causal, cum_col - cum_row + li_row, NEG)
        hs = range(ML_HEADS)
        sh = [b * ML_HEADS + h for h in hs]
        d_h = [dmat[:, h * n:(h + 1) * n] for h in hs]
        cum_h = [cum[:, LANE_MF + h:LANE_MF + h + 1] for h in hs]
        li_h = [log_i[:, LANE_MI + h:LANE_MI + h + 1] for h in hs]
        m_st = [m_ref[sh[h]:sh[h] + 1, 0:1] for h in hs]
        q = [(q_ref[b, :, h * ML_QK:(h + 1) * ML_QK] * scale).astype(BF16) for h in hs]
        k = [k_ref[b, :, h * ML_QK:(h + 1) * ML_QK] for h in hs]
        v_aug = [jnp.concatenate([v_ref[b, :, h * ML_V:(h + 1) * ML_V].astype(BF16), one_col], axis=1) for h in hs]
        c_prev = [c_ref[sh[h]] for h in hs]
        inter = [cum_h[h] + m_st[h] for h in hs]
        m_t = [jnp.maximum(inter[h], jnp.max(d_h[h], axis=-1, keepdims=True)) for h in hs]
        qk = [_dot_nt(q[h], k[h].astype(BF16)) for h in hs]
        qc = [_dot(q[h], c_prev[h].astype(BF16)) for h in hs]
        pw = [(qk[h] * jnp.exp(d_h[h] - m_t[h])).astype(BF16) for h in hs]
        nd = [_dot(pw[h], v_aug[h]) + jnp.exp(inter[h] - m_t[h]) * qc[h] for h in hs]
        hh = [nd[h][:, :ML_V] / jnp.maximum(jnp.abs(nd[h][:, ML_V:ML_V + 1]), jnp.exp(-m_t[h])) for h in hs]
        tot = [cum_h[h][n - 1:n, :] for h in hs]
        to_end = [tot[h] - cum_h[h] + li_h[h] for h in hs]
        m_loc = [jnp.max(to_end[h], axis=0, keepdims=True) for h in hs]
        kw = [(k[h] * jnp.exp(to_end[h] - m_loc[h])).astype(BF16) for h in hs]
        c_loc = [_dot_tn(kw[h], v_aug[h]) for h in hs]
        m_new = [jnp.maximum(tot[h] + m_st[h], m_loc[h]) for h in hs]
        for h in hs:
            c_ref[sh[h]] = (jnp.exp(tot[h] + m_st[h] - m_new[h]) * c_prev[h]
                            + jnp.exp(m_loc[h] - m_new[h]) * c_loc[h])
            m_ref[sh[h]:sh[h] + 1, :] = jnp.broadcast_to(m_new[h], (1, 128))
        for h in hs:
            ms = jnp.mean(hh[h] * hh[h], axis=-1, keepdims=True)
            cols = slice(h * ML_V, (h + 1) * ML_V)
            y_ref[b, :, cols] = (hh[h] * lax.rsqrt(ms + EPS) * nw_ref[:, cols]
                                 * _sigmoid(o_ref[b, :, cols])).astype(y_ref.dtype)


def _lane_select(lane0, heads, width):
    m = np.zeros((128, heads * width), np.float32)
    for h in range(heads):
        m[lane0 + h, h * width:(h + 1) * width] = 1.0
    return m


def _diag_mask(n, heads):
    return np.tile(np.eye(n, dtype=np.float32), (1, heads))


def _mlstm_call(proj, par, nw, batch, p):
    t = proj.shape[0]
    n = CHUNK
    nc = p // n
    tri = jnp.asarray(np.tril(np.ones((n, n), np.float32)), BF16)
    sel = jnp.asarray(np.stack([_lane_select(LANE_MI, ML_HEADS, n), _lane_select(LANE_MF, ML_HEADS, n)]), BF16)
    dg = jnp.asarray(_diag_mask(n, ML_HEADS), F32)
    proj3 = proj.reshape(batch, p, proj.shape[1])
    blk = lambda w, off: pl.BlockSpec((batch, n, w), lambda s: (0, s, off // w))
    const = lambda shape: pl.BlockSpec(shape, lambda s: (0,) * len(shape))
    m_rows = -(-batch * ML_HEADS // 8) * 8
    y = pl.pallas_call(
        _mlstm_kernel, grid=(nc,),
        in_specs=[blk(ML_WIDTH, C_MV), blk(ML_WIDTH, C_MV + ML_WIDTH), blk(ML_QK_W, C_MQ),
                  blk(ML_QK_W, C_MQ + ML_QK_W), blk(128, C_SMALL),
                  const((8, 128)), const((1, ML_WIDTH)), const((n, n)), const((2, 128, ML_HEADS * n)),
                  const((n, ML_HEADS * n))],
        out_specs=pl.BlockSpec((batch, n, ML_WIDTH), lambda s: (0, s, 0)),
        out_shape=jax.ShapeDtypeStruct((batch, p, ML_WIDTH), BF16),
        scratch_shapes=[pltpu.VMEM((batch * ML_HEADS, ML_QK, 2 * ML_V), F32), pltpu.VMEM((m_rows, 128), F32)],
        compiler_params=_cparams(("arbitrary",)), name="mlstm",
    )(proj3, proj3, proj3, proj3, proj3, par, nw, tri, sel, dg)
    return y.reshape(t, ML_WIDTH)


def _ssd_kernel(z_ref, x_ref, b_ref, c_ref, sm_ref, cw_ref, cb_ref, par_ref, dsk_ref, nw_ref,
                tri_ref, sel_ref, dg_ref, y_ref, xs_ref, bs_ref, cs_ref, st_ref):
    s = pl.program_id(1)
    n = CHUNK
    tail = 8

    @pl.when(s == 0)
    def _():
        st_ref[...] = jnp.zeros_like(st_ref)
        xs_ref[0:tail, :] = jnp.zeros((tail, xs_ref.shape[1]), F32)
        bs_ref[0:tail, :] = jnp.zeros((tail, bs_ref.shape[1]), F32)
        cs_ref[0:tail, :] = jnp.zeros((tail, cs_ref.shape[1]), F32)

    def conv_silu(src_ref, scr_ref, c0, width, rowmask):
        scr_ref[tail:tail + n, :] = src_ref[...]
        acc = cb_ref[:, c0:c0 + width]
        for j in range(CONV_W):
            off = tail - (CONV_W - 1) + j
            acc = acc + cw_ref[j:j + 1, c0:c0 + width] * scr_ref[off:off + n, :]
        scr_ref[0:tail, :] = scr_ref[n:n + tail, :]
        return jnp.where(rowmask, 0.0, _silu(acc))

    def padmask(width):
        return (s * n + lax.broadcasted_iota(jnp.int32, (n, width), 0)) < LEAD_PAD

    x = conv_silu(x_ref, xs_ref, 0, SSM_WIDTH, padmask(SSM_WIDTH))
    bm = conv_silu(b_ref, bs_ref, SSM_WIDTH, SSM_BC, padmask(SSM_BC))
    cm = conv_silu(c_ref, cs_ref, SSM_WIDTH + SSM_BC, SSM_BC, padmask(SSM_BC))

    pre = sm_ref[...] + par_ref[0:1, :]
    dt = jnp.maximum(pre, 0.0) + jnp.log1p(jnp.exp(-jnp.abs(pre)))
    dt = jnp.where(padmask(128), 0.0, dt)
    da = dt * (-jnp.exp(par_ref[1:2, :]))
    cum = _sel_dot(tri_ref[...], da)
    sel = sel_ref[...]
    dt_col = _dot_sel(dt, sel)
    cum_col = _dot_sel(cum, sel)
    cum_row = jnp.sum(cum_col * dg_ref[...], axis=0, keepdims=True)
    w = SSM_WIDTH
    tt = lax.broadcasted_iota(jnp.int32, (n, w), 0)
    ss = lax.broadcasted_iota(jnp.int32, (n, w), 1) & (n - 1)
    decay = jnp.exp(jnp.where(tt >= ss, cum_col - cum_row, NEG))
    xdt = x * dt_col
    last = cum_col[n - 1:n, :]
    wend = (xdt * jnp.exp(last - cum_col)).astype(BF16)
    chunk_decay = jnp.exp(last)
    ecum = jnp.exp(cum_col)
    gw = SSM_HPG * SSM_HEADDIM
    rr = lax.broadcasted_iota(jnp.int32, (gw, gw), 0) // SSM_HEADDIM
    cc = lax.broadcasted_iota(jnp.int32, (gw, gw), 1) // SSM_HEADDIM
    blockdiag = rr == cc
    ys = []
    for g in range(SSM_GROUPS):
        gl = slice(g * gw, (g + 1) * gw)
        sl = slice(g * SSM_STATE, (g + 1) * SSM_STATE)
        cm_g = cm[:, sl].astype(BF16)
        bm_g = bm[:, sl].astype(BF16)
        cb = _dot_nt(cm_g, jnp.concatenate([bm_g] * SSM_HPG, axis=0))
        m = (cb * decay[:, gl]).astype(BF16)
        xdt_g = xdt[:, gl]
        bd = jnp.where(blockdiag, jnp.concatenate([xdt_g] * SSM_HPG, axis=0), 0.0).astype(BF16)
        st = st_ref[g]
        y_g = _dot(m, bd) + ecum[:, gl] * _dot(cm_g, st.astype(BF16))
        st_ref[g] = st * chunk_decay[:, gl] + _dot_tn(bm_g, wend[:, gl])
        ys.append(y_g)
    y = jnp.concatenate(ys, axis=1) + x * dsk_ref[...]
    y = y * _silu(z_ref[...])
    outs = []
    for g in range(SSM_GROUPS):
        gl = slice(g * gw, (g + 1) * gw)
        y_g = y[:, gl]
        ms = jnp.mean(y_g * y_g, axis=-1, keepdims=True)
        outs.append(y_g * lax.rsqrt(ms + EPS))
    y_ref[...] = (jnp.concatenate(outs, axis=1) * nw_ref[...]).astype(y_ref.dtype)


def _ssd_call(proj, cw, cb, par, dsk, nw, batch, p):
    t = proj.shape[0]
    n = CHUNK
    nc = p // n
    tri = jnp.asarray(np.tril(np.ones((n, n), np.float32)), BF16)
    sel = jnp.asarray(_lane_select(LANE_DT, SSM_HEADS, SSM_HEADDIM), BF16)
    dg = jnp.asarray(_diag_mask(n, SSM_HEADS), F32)
    blk = lambda w, off: pl.BlockSpec((n, w), lambda b, s: (b * nc + s, off // w))
    const = lambda shape: pl.BlockSpec(shape, lambda b, s: (0,) * len(shape))
    cch = SSM_WIDTH + 2 * SSM_BC
    return pl.pallas_call(
        _ssd_kernel, grid=(batch, nc),
        in_specs=[blk(SSM_WIDTH, C_SZ), blk(SSM_WIDTH, C_SX), blk(SSM_BC, C_SX + SSM_WIDTH),
                  blk(SSM_BC, C_SX + SSM_WIDTH + SSM_BC), blk(128, C_SMALL),
                  const((CONV_W, cch)), const((1, cch)), const((8, 128)), const((1, SSM_WIDTH)),
                  const((1, SSM_WIDTH)), const((n, n)), const((128, SSM_WIDTH)), const((n, SSM_WIDTH))],
        out_specs=pl.BlockSpec((n, SSM_WIDTH), lambda b, s: (b * nc + s, 0)),
        out_shape=jax.ShapeDtypeStruct((t, SSM_WIDTH), BF16),
        scratch_shapes=[pltpu.VMEM((n + 8, SSM_WIDTH), F32), pltpu.VMEM((n + 8, SSM_BC), F32),
                        pltpu.VMEM((n + 8, SSM_BC), F32),
                        pltpu.VMEM((SSM_GROUPS, SSM_STATE, SSM_HPG * SSM_HEADDIM), F32)],
        compiler_params=_cparams(("arbitrary", "arbitrary")), name="ssd",
    )(proj, proj, proj, proj, proj, cw, cb, par, dsk, nw, tri, sel, dg)


LANE_E0 = N_GROUPS_MOE
RT_E, RT_RANK, RT_W = 0, 2, 4


def _first_max(vals, lane):
    m = jnp.max(vals, axis=-1, keepdims=True)
    idx = jnp.min(jnp.where(vals == m, lane, 128), axis=-1, keepdims=True)
    return m, idx


def _outproj_kernel(ya_ref, yb_ref, yc_ref, h_ref, w_ref, nw_ref, rh_ref, rl_ref, rb_ref, tri_ref,
                    hm_ref, u_ref, rt_ref, cnt_ref):
    a0, a1 = HG_WIDTH, HG_WIDTH + ML_WIDTH
    h = h_ref[...]
    h = h + _dot(ya_ref[...], w_ref[0:a0, :])
    h = h + _dot(yb_ref[...], w_ref[a0:a1, :])
    h = h + _dot(yc_ref[...], w_ref[a1:, :])
    hm_ref[...] = h
    ms = jnp.mean(h * h, axis=-1, keepdims=True)
    u = h * lax.rsqrt(ms + EPS) * nw_ref[...]
    u_ref[...] = u
    u_hi = u.astype(BF16)
    u_lo = (u - u_hi.astype(F32)).astype(BF16)
    lg = _dot(u_hi, rh_ref[...]) + (_dot(u_lo, rh_ref[...]) + _dot(u_hi, rl_ref[...])) + rb_ref[...]

    tm = lg.shape[0]
    lane = lax.broadcasted_iota(jnp.int32, (tm, 128), 1)
    g_mask = lane < N_GROUPS_MOE
    g_max, g_sel = _first_max(jnp.where(g_mask, lg, NEG), lane)
    g_gate = 1.0 / jnp.sum(jnp.where(g_mask, jnp.exp(lg - g_max), 0.0), axis=-1, keepdims=True)
    lo = LANE_E0 + g_sel * EXPERTS_PER_GROUP
    e_vals = jnp.where((lane >= lo) & (lane < lo + EXPERTS_PER_GROUP), lg, NEG)
    v1, i1 = _first_max(e_vals, lane)
    v2, i2 = _first_max(jnp.where(lane == i1, NEG, e_vals), lane)
    a = jnp.exp(v2 - v1)
    w1 = g_gate / (1.0 + a)
    w2 = w1 * a
    @pl.when(pl.program_id(0) == 0)
    def _():
        cnt_ref[...] = jnp.zeros_like(cnt_ref)

    hit1 = lane == i1
    hit2 = lane == i2
    onehot = jnp.where(hit1 | hit2, 1.0, 0.0)
    before = _dot(tri_ref[...], onehot.astype(BF16)) + cnt_ref[0:1, :]
    r1 = jnp.sum(jnp.where(hit1, before, 0.0), axis=-1, keepdims=True)
    r2 = jnp.sum(jnp.where(hit2, before, 0.0), axis=-1, keepdims=True)
    cnt_ref[...] = cnt_ref[...] + jnp.sum(onehot, axis=0, keepdims=True)
    rec = jnp.zeros((tm, 128), F32)
    for ln, val in ((RT_E, (i1 - LANE_E0).astype(F32)), (RT_E + 1, (i2 - LANE_E0).astype(F32)),
                    (RT_RANK, r1), (RT_RANK + 1, r2), (RT_W, w1), (RT_W + 1, w2)):
        rec = jnp.where(lane == ln, val, rec)
    rt_ref[...] = rec


def _outproj_call(ya, yb, yc, h, w, nw, r_hi, r_lo, r_bias, tm_target=352):
    t, d = h.shape
    tm = _row_tile(t, tm_target)
    tri = jnp.asarray(np.tril(np.ones((tm, tm), np.float32), -1), BF16)
    row = lambda i: (i, 0)
    const = lambda shape: pl.BlockSpec(shape, lambda i: (0, 0))
    return pl.pallas_call(
        _outproj_kernel, grid=(t // tm,),
        in_specs=[pl.BlockSpec((tm, HG_WIDTH), row), pl.BlockSpec((tm, ML_WIDTH), row),
                  pl.BlockSpec((tm, SSM_WIDTH), row), pl.BlockSpec((tm, d), row),
                  const((D_MIX, d)), const((1, d)), const((d, 128)), const((d, 128)), const((1, 128)),
                  const((tm, tm))],
        out_specs=[pl.BlockSpec((tm, d), row), pl.BlockSpec((tm, d), row), pl.BlockSpec((tm, 128), row),
                   const((8, 128))],
        out_shape=[jax.ShapeDtypeStruct((t, d), F32), jax.ShapeDtypeStruct((t, d), F32),
                   jax.ShapeDtypeStruct((t, 128), F32), jax.ShapeDtypeStruct((8, 128), F32)],
        compiler_params=_cparams(("arbitrary",)), name="out_proj_router",
    )(ya, yb, yc, h, w, nw.reshape(1, d), r_hi, r_lo, r_bias, tri)


def _moe_kernel(be_ref, nu_ref, src0_ref, srcn_ref, dst_ref, u_hbm, w1_ref, w3_ref, w2_ref, o_hbm,
                xbuf, ybuf, w1b, w3b, w2b, gsem, ssem):
    i = pl.program_id(0)
    bm = MOE_BM
    n_used = nu_ref[0]
    slot = i % 2

    def gather(idx_ref, sl):
        def body(r, c):
            tok = idx_ref[0, r]
            pltpu.make_async_copy(u_hbm.at[pl.ds(tok, 1), :], xbuf.at[sl, pl.ds(r, 1), :], gsem.at[sl]).start()
            return c
        lax.fori_loop(0, bm, body, 0)

    def scatter(sl):
        def body(r, c):
            row = dst_ref[0, r]
            pltpu.make_async_copy(ybuf.at[sl, pl.ds(r, 1), :], o_hbm.at[pl.ds(row, 1), :], ssem.at[sl]).start()
            return c
        lax.fori_loop(0, bm, body, 0)

    def wait_gather(sl):
        pltpu.make_async_copy(u_hbm.at[pl.ds(0, bm), :], xbuf.at[sl], gsem.at[sl]).wait()

    def wait_scatter(sl):
        pltpu.make_async_copy(ybuf.at[sl], o_hbm.at[pl.ds(0, bm), :], ssem.at[sl]).wait()

    @pl.when(i == 0)
    def _():
        gather(src0_ref, 0)
        ybuf[...] = jnp.zeros_like(ybuf)
        n_real = o_hbm.shape[0] - 2 * bm
        for sl in range(2):
            fill = pltpu.make_async_copy(ybuf.at[sl], o_hbm.at[pl.ds(n_real + sl * bm, bm), :], ssem.at[sl])
            fill.start()
            fill.wait()

    @pl.when(i + 1 < n_used)
    def _():
        gather(srcn_ref, 1 - slot)

    @pl.when(i < n_used)
    def _():
        first = jnp.logical_or(i == 0, be_ref[i] != be_ref[jnp.maximum(i - 1, 0)])

        @pl.when(first)
        def _():
            w1b[...] = w1_ref[...].astype(BF16)
            w3b[...] = w3_ref[...].astype(BF16)
            w2b[...] = w2_ref[...].astype(BF16)

        wait_gather(slot)
        x = xbuf[slot].astype(BF16)
        hid = _silu(_dot(x, w1b[...])) * _dot(x, w3b[...])
        y = _dot(hid.astype(BF16), w2b[...])

        @pl.when(i >= 2)
        def _():
            wait_scatter(slot)

        ybuf[slot] = y
        scatter(slot)

    @pl.when(i == n_used - 1)
    def _():
        wait_scatter(slot)

        @pl.when(i >= 1)
        def _():
            wait_scatter(1 - slot)


def _moe_call(block_e, n_used, src_tok, dst_row, u, w1, w3, w2, layer, n_out_rows):
    t, d = u.shape
    f = w1.shape[-1]
    bm = MOE_BM
    nb = block_e.shape[0]
    wmap = lambda i, be, nu: (layer, be[i], 0, 0)
    src_tok = src_tok.reshape(nb, 1, bm)
    dst_row = dst_row.reshape(nb, 1, bm)
    idx = lambda fn: pl.BlockSpec((None, 1, bm), fn, memory_space=pltpu.SMEM)
    return pl.pallas_call(
        _moe_kernel,
        grid_spec=pltpu.PrefetchScalarGridSpec(
            num_scalar_prefetch=2, grid=(nb,),
            in_specs=[idx(lambda i, be, nu: (0, 0, 0)),
                      idx(lambda i, be, nu: (jnp.minimum(i + 1, nb - 1), 0, 0)),
                      idx(lambda i, be, nu: (i, 0, 0)),
                      pl.BlockSpec(memory_space=pl.ANY),
                      pl.BlockSpec((None, None, d, f), wmap), pl.BlockSpec((None, None, d, f), wmap),
                      pl.BlockSpec((None, None, f, d), wmap)],
            out_specs=pl.BlockSpec(memory_space=pl.ANY),
            scratch_shapes=[pltpu.VMEM((2, bm, d), F32), pltpu.VMEM((2, bm, d), F32),
                            pltpu.VMEM((d, f), BF16), pltpu.VMEM((d, f), BF16), pltpu.VMEM((f, d), BF16),
                            pltpu.SemaphoreType.DMA((2,)), pltpu.SemaphoreType.DMA((2,))]),
        out_shape=jax.ShapeDtypeStruct((n_out_rows, d), F32),
        compiler_params=_cparams(("arbitrary",)), name="moe_ffn",
    )(block_e, n_used, src_tok, src_tok, dst_row, u, w1, w3, w2)


def _route_tables(rt, cnt, t):
    bm = MOE_BM
    tk = t * TOP_K
    e = rt[:, RT_E:RT_E + TOP_K].astype(jnp.int32)
    rank = rt[:, RT_RANK:RT_RANK + TOP_K].astype(jnp.int32)
    wts = rt[:, RT_W:RT_W + TOP_K]
    counts = cnt[0, LANE_E0:LANE_E0 + N_EXPERTS].astype(jnp.int32)
    padded = (counts + bm - 1) // bm * bm
    pad_ends = jnp.cumsum(padded)
    pad_starts = pad_ends - padded
    dest = (pad_starts[e] + rank).reshape(-1)
    n_blocks = -(-tk // bm) + N_EXPERTS
    n_rows = n_blocks * bm
    inv = jnp.full((n_rows,), -1, jnp.int32).at[dest].set(jnp.arange(tk, dtype=jnp.int32))
    tok, slot = inv // TOP_K, inv % TOP_K
    src_tok = jnp.where(inv >= 0, tok, 0)
    dst_row = jnp.where(inv >= 0, slot * t + tok, tk + (jnp.arange(n_rows, dtype=jnp.int32) % (2 * bm)))
    block_row0 = jnp.arange(n_blocks, dtype=jnp.int32) * bm
    block_e = jnp.minimum(jnp.sum(pad_ends[None, :] <= block_row0[:, None], axis=1), N_EXPERTS - 1).astype(jnp.int32)
    n_used = (pad_ends[-1] // bm).astype(jnp.int32).reshape(1)
    last_e = block_e[jnp.maximum(n_used[0] - 1, 0)]
    block_e = jnp.where(jnp.arange(n_blocks) < n_used[0], block_e, last_e)
    return block_e, n_used, src_tok, dst_row, wts, tk + 2 * bm


def _lane_row(pairs):
    row = jnp.zeros((128,), F32)
    for lane0, vals in pairs:
        row = row.at[lane0:lane0 + vals.shape[0]].set(vals.astype(F32))
    return row


def kernel(x, meta_tokens, hg_lb_logits, norm_mix_w, w_in, hg_norm_w, ml_b_i, ml_b_f, ml_norm_w,
           ssm_conv_w, ssm_conv_b, ssm_dt_bias, ssm_a_log, ssm_d, ssm_norm_w, w_out, norm_ffn_w,
           moe_w_group, moe_b_group, moe_w_router, moe_b_router, moe_w1, moe_w3, moe_w2, final_norm_w):
    batch, seq, d = x.shape
    depth = w_in.shape[0]
    p = LEAD_PAD + N_META + seq
    t = batch * p
    meta = jnp.broadcast_to(meta_tokens.astype(x.dtype)[None], (batch, N_META, d))
    h = jnp.concatenate([jnp.zeros((batch, LEAD_PAD, d), x.dtype), meta, x], axis=1).reshape(t, d)

    lb_w = jax.nn.softmax(hg_lb_logits.astype(F32), axis=0)
    lower_bounds = jnp.cumsum(lb_w, axis=0) - lb_w[0]

    contrib = wts = None
    for layer in range(depth):
        w = w_in[layer]
        o_mq, o_mv, o_mi, o_sz = HG_KEY * 2 + HG_WIDTH * 2, 2560, 3584, 3592
        o_sx = o_sz + SSM_WIDTH
        o_dt = o_sx + SSM_WIDTH + 2 * SSM_BC
        w_perm = jnp.concatenate([
            w[:, :o_mq], w[:, o_sz:o_sx], w[:, o_sx:o_dt], w[:, o_mv:o_mi], w[:, o_mq:o_mv],
            w[:, o_mi:o_sz], w[:, o_dt:], jnp.zeros((d, N_PROJ - C_SMALL - 2 * ML_HEADS - SSM_HEADS), w.dtype)],
            axis=1).astype(BF16)
        lb = lower_bounds[layer]
        lbf = jnp.maximum(lb, LB_FLOOR)
        hg_par = jnp.zeros((8, HG_KEY), F32).at[0].set(lbf).at[1].set(1.0 - lb).at[2].set(lbf - lb).at[3].set(hg_norm_w[layer])
        ml_par = jnp.zeros((8, 128), F32).at[0].set(_lane_row([(LANE_MI, ml_b_i[layer]), (LANE_MF, ml_b_f[layer])]))
        ss_par = jnp.zeros((8, 128), F32).at[0].set(_lane_row([(LANE_DT, ssm_dt_bias[layer])]))
        ss_par = ss_par.at[1].set(_lane_row([(LANE_DT, ssm_a_log[layer])]))
        dskip = jnp.repeat(ssm_d[layer].astype(F32), SSM_HEADDIM).reshape(1, SSM_WIDTH)
        w_r = jnp.concatenate([moe_w_group[layer],
                               moe_w_router[layer].transpose(1, 0, 2).reshape(d, N_EXPERTS),
                               jnp.zeros((d, 128 - N_GROUPS_MOE - N_EXPERTS), F32)], axis=1)
        r_hi = w_r.astype(BF16)
        r_lo = (w_r - r_hi.astype(F32)).astype(BF16)

        if layer == 0:
            (u,) = _norm_call(h, norm_mix_w[layer], write_h=False, u_dtype=BF16)
        else:
            h, u = _norm_call(h, norm_mix_w[layer], contrib, wts, write_h=True, u_dtype=BF16)
        proj = _inproj_call(u, w_perm)
        ya = _hgrn2_call(proj, hg_par, batch, p)
        yb = _mlstm_call(proj, ml_par, ml_norm_w[layer].reshape(1, ML_WIDTH), batch, p)
        yc = _ssd_call(proj, ssm_conv_w[layer], ssm_conv_b[layer].reshape(1, -1), ss_par, dskip,
                       ssm_norm_w[layer].reshape(1, SSM_WIDTH), batch, p)
        r_bias = _lane_row([(0, moe_b_group[layer]), (LANE_E0, moe_b_router[layer].reshape(-1))]).reshape(1, 128)
        h, u_ffn, rt, cnt = _outproj_call(ya, yb, yc, h, w_out[layer].astype(BF16), norm_ffn_w[layer],
                                          r_hi, r_lo, r_bias)
        block_e, n_used, src_tok, dst_row, wts, n_out_rows = _route_tables(rt, cnt, t)
        contrib = _moe_call(block_e, n_used, src_tok, dst_row, u_ffn, moe_w1, moe_w3, moe_w2, layer, n_out_rows)
    return _final_call(h, contrib, wts, final_norm_w, batch, p)
```

```python
import functools

import jax
import jax.numpy as jnp
import numpy as np
from jax import lax
from jax.experimental import pallas as pl
from jax.experimental.pallas import tpu as pltpu

F32 = jnp.float32
BF16 = jnp.bfloat16

D_MODEL = 2048
N_META = 16
CHUNK = 64
HG_CHUNK = 16
LEAD_PAD = CHUNK - N_META
EPS = 1e-6
NEG = -1e30
LB_FLOOR = 1e-30

HG_HEADS = 4
HG_KDIM = 128
HG_KEY = HG_HEADS * HG_KDIM
HG_WIDTH = HG_HEADS * 128

ML_HEADS = 4
ML_QK = 64
ML_V = 128
ML_QK_W = ML_HEADS * ML_QK
ML_WIDTH = ML_HEADS * ML_V
GATE_CAP = 15.0

SSM_HEADS = 16
SSM_HEADDIM = 64
SSM_WIDTH = SSM_HEADS * SSM_HEADDIM
SSM_STATE = 128
SSM_GROUPS = 4
SSM_HPG = SSM_HEADS // SSM_GROUPS
SSM_BC = SSM_GROUPS * SSM_STATE
CONV_W = 4

D_MIX = HG_WIDTH + ML_WIDTH + SSM_WIDTH

N_GROUPS_MOE = 4
EXPERTS_PER_GROUP = 8
N_EXPERTS = N_GROUPS_MOE * EXPERTS_PER_GROUP
TOP_K = 2
D_EXPERT = 512
MOE_BM = 128

C_HG = 0
C_SZ = 2048
C_SX = 3072
C_MV = 5120
C_MQ = 6144
C_SMALL = 6656
N_PROJ = 6912
LANE_MI = 0
LANE_MF = ML_HEADS
LANE_DT = 2 * ML_HEADS

VMEM_LIMIT = 56 * 1024 * 1024


def _cparams(sem):
    return pltpu.CompilerParams(dimension_semantics=sem, vmem_limit_bytes=VMEM_LIMIT)


def _row_tile(n, target, mult=16):
    best = None
    for t in range(mult, min(n, target) + 1, mult):
        if n % t == 0:
            best = t
    assert best is not None, (n, target, mult)
    return best


def _split3(x):
    hi = x.astype(BF16)
    r = x - hi.astype(F32)
    mid = r.astype(BF16)
    lo = (r - mid.astype(F32)).astype(BF16)
    return hi, mid, lo


def _dot(a, b):
    return jnp.dot(a, b, preferred_element_type=F32)


def _sel_dot(sel, x):
    hi, mid, lo = _split3(x)
    return _dot(sel, hi) + _dot(sel, mid) + _dot(sel, lo)


def _dot_sel(x, sel):
    hi, mid, lo = _split3(x)
    return _dot(hi, sel) + _dot(mid, sel) + _dot(lo, sel)


def _dot_nt(a, b):
    return lax.dot_general(a, b, (((1,), (1,)), ((), ())), preferred_element_type=F32)


def _dot_tn(a, b):
    return lax.dot_general(a, b, (((0,), (0,)), ((), ())), preferred_element_type=F32)


def _log_sigmoid(x):
    return jnp.minimum(x, 0.0) - jnp.log1p(jnp.exp(-jnp.abs(x)))


def _sigmoid(x):
    return 1.0 / (1.0 + jnp.exp(-x))


def _silu(x):
    return x * _sigmoid(x)


def _norm_kernel(*refs, combine, write_h):
    if combine:
        h_ref, c0_ref, c1_ref, wt_ref, nw_ref = refs[:5]
        outs = refs[5:]
        wt = wt_ref[...]
        h = h_ref[...] + wt[:, 0:1] * c0_ref[...] + wt[:, 1:2] * c1_ref[...]
    else:
        h_ref, nw_ref = refs[:2]
        outs = refs[2:]
        h = h_ref[...]
    if write_h:
        outs[0][...] = h
    ms = jnp.mean(h * h, axis=-1, keepdims=True)
    u_ref = outs[-1]
    u_ref[...] = (h * lax.rsqrt(ms + EPS) * nw_ref[...]).astype(u_ref.dtype)


def _norm_call(h, nw, contrib=None, wts=None, *, write_h, u_dtype, tm_target=264):
    t, d = h.shape
    tm = _row_tile(t, tm_target)
    combine = contrib is not None
    row = lambda i: (i, 0)
    in_specs = [pl.BlockSpec((tm, d), row)]
    args = [h]
    if combine:
        in_specs += [pl.BlockSpec((tm, d), row), pl.BlockSpec((tm, d), lambda i: (t // tm + i, 0)),
                     pl.BlockSpec((tm, 2), row)]
        args += [contrib, contrib, wts]
    in_specs.append(pl.BlockSpec((1, d), lambda i: (0, 0)))
    args.append(nw.reshape(1, d))
    out_shape, out_specs = [], []
    if write_h:
        out_shape.append(jax.ShapeDtypeStruct((t, d), F32))
        out_specs.append(pl.BlockSpec((tm, d), row))
    out_shape.append(jax.ShapeDtypeStruct((t, d), u_dtype))
    out_specs.append(pl.BlockSpec((tm, d), row))
    return pl.pallas_call(
        functools.partial(_norm_kernel, combine=combine, write_h=write_h),
        grid=(t // tm,), in_specs=in_specs, out_specs=out_specs, out_shape=out_shape,
        compiler_params=_cparams(("arbitrary",)), name="combine_norm",
    )(*args)


def _final_kernel(h_ref, c0_ref, c1_ref, wt_ref, nw_ref, o_ref):
    wt = wt_ref[...]
    h = h_ref[...] + wt[:, 0:1] * c0_ref[...] + wt[:, 1:2] * c1_ref[...]
    ms = jnp.mean(h * h, axis=-1, keepdims=True)
    o_ref[...] = h * lax.rsqrt(ms + EPS) * nw_ref[...]


def _final_call(h, contrib, wts, nw, batch, p):
    t, d = h.shape
    seq = p - CHUNK
    tm = _row_tile(seq, 256)
    n_out = seq // tm
    src = lambda b, i: (pl.multiple_of(b * p + CHUNK + i * tm, CHUNK), 0)
    src1 = lambda b, i: (pl.multiple_of(t + b * p + CHUNK + i * tm, CHUNK), 0)
    win = lambda w, fn: pl.BlockSpec((pl.Element(tm), pl.Element(w)), fn)
    return pl.pallas_call(
        _final_kernel, grid=(batch, n_out),
        in_specs=[win(d, src), win(d, src), win(d, src1), win(2, src),
                  pl.BlockSpec((1, d), lambda b, i: (0, 0))],
        out_specs=pl.BlockSpec((None, tm, d), lambda b, i: (b, i, 0)),
        out_shape=jax.ShapeDtypeStruct((batch, seq, d), F32),
        compiler_params=_cparams(("arbitrary", "arbitrary")), name="final_norm",
    )(h, contrib, contrib, wts, nw.reshape(1, d))


def _matmul_kernel(x_ref, w_ref, o_ref):
    o_ref[...] = _dot(x_ref[...], w_ref[...])


def _inproj_call(u, w, tm_target=1056, tn=768):
    t, d = u.shape
    n = w.shape[1]
    tm = _row_tile(t, tm_target)
    assert n % tn == 0
    return pl.pallas_call(
        _matmul_kernel, grid=(t // tm, n // tn),
        in_specs=[pl.BlockSpec((tm, d), lambda i, j: (i, 0)), pl.BlockSpec((d, tn), lambda i, j: (0, j))],
        out_specs=pl.BlockSpec((tm, tn), lambda i, j: (i, j)),
        out_shape=jax.ShapeDtypeStruct((t, n), F32),
        compiler_params=_cparams(("arbitrary", "arbitrary")), name="in_proj",
    )(u, w)


def _hgrn2_kernel(q_ref, f_ref, i_ref, g_ref, par_ref, o_ref, st_ref, *, rows):
    s = pl.program_id(1)

    @pl.when(s == 0)
    def _():
        st_ref[...] = jnp.zeros_like(st_ref)

    c = HG_CHUNK
    ones = jnp.ones((HG_KDIM, 128), BF16)
    rid = lax.broadcasted_iota(jnp.int32, (c, 128), 0)
    scale = HG_KDIM ** -0.5

    hc = c // 2

    def chunk(ci, carry):
        r0 = pl.multiple_of(ci * c, c)
        pad = (s * rows + r0 + rid) < LEAD_PAD

        def front(h):
            cols = slice(h * 128, (h + 1) * 128)
            a_lb = par_ref[0:1, cols]
            b_lb = par_ref[1:2, cols]
            c_lb = par_ref[2:3, cols]
            z = f_ref[pl.ds(r0, c), cols]
            sg = _sigmoid(z)
            f = a_lb + b_lb * sg
            log_f = jnp.where(pad, 0.0, jnp.log(f))
            k = jnp.where(pad, 0.0, b_lb * (1.0 - sg) - c_lb)
            q = q_ref[pl.ds(r0, c), cols] * scale
            v = i_ref[pl.ds(r0, c), cols]
            cum = log_f
            for sh in (1, 2, 4, 8):
                cum = cum + jnp.where(rid >= sh, pltpu.roll(cum, sh, axis=0), 0.0)
            parts = []
            for s_ in range(c):
                lo = 0 if s_ < hc else hc
                rel = jnp.where(rid[lo:] >= s_, cum[lo:] - cum[s_:s_ + 1, :], NEG)
                parts.append(q[lo:] * (k[s_:s_ + 1, :] * jnp.exp(rel)))
            sc = _dot(jnp.concatenate(parts, axis=0).astype(BF16), ones)
            st = st_ref[h]
            o_inter = _dot_nt((q * jnp.exp(cum)).astype(BF16), st.astype(BF16))
            last = cum[c - 1:c, :]
            kd = (k * jnp.exp(last - cum)).astype(BF16)
            st_ref[h] = st * jnp.exp(last) + _dot_tn(v.astype(BF16), kd)
            return sc, o_inter, v

        def back(h, sc, o_inter, v):
            cols = slice(h * 128, (h + 1) * 128)
            o_top = o_inter[:hc]
            o_bot = o_inter[hc:]
            for s_ in range(hc):
                o_top = o_top + sc[s_ * c:s_ * c + hc, :] * v[s_:s_ + 1, :]
                o_bot = o_bot + sc[s_ * c + hc:(s_ + 1) * c, :] * v[s_:s_ + 1, :]
            for s_ in range(hc, c):
                r_ = hc * c + (s_ - hc) * hc
                o_bot = o_bot + sc[r_:r_ + hc, :] * v[s_:s_ + 1, :]
            o = jnp.concatenate([o_top, o_bot], axis=0)
            ms = jnp.mean(o * o, axis=-1, keepdims=True)
            g = g_ref[pl.ds(r0, c), cols]
            o_ref[pl.ds(r0, c), cols] = (o * lax.rsqrt(ms + EPS) * par_ref[3:4, cols] * _silu(g)).astype(o_ref.dtype)

        pending = front(0)
        for h in range(1, HG_HEADS):
            nxt = front(h)
            back(h - 1, *pending)
            pending = nxt
        back(HG_HEADS - 1, *pending)
        return carry

    n_chunks = rows // c
    lax.fori_loop(0, n_chunks, chunk, 0, unroll=3 if n_chunks % 3 == 0 else 1)


def _hgrn2_call(proj, par, batch, p):
    t = proj.shape[0]
    rows = _row_tile(p, 528)
    nb = p // rows
    w = HG_KEY
    blk = lambda j: pl.BlockSpec((rows, w), lambda b, s, j=j: (b * nb + s, C_HG // w + j))
    return pl.pallas_call(
        functools.partial(_hgrn2_kernel, rows=rows), grid=(batch, nb),
        in_specs=[blk(0), blk(1), blk(2), blk(3), pl.BlockSpec((8, w), lambda b, s: (0, 0))],
        out_specs=pl.BlockSpec((rows, w), lambda b, s: (b * nb + s, 0)),
        out_shape=jax.ShapeDtypeStruct((t, HG_WIDTH), BF16),
        scratch_shapes=[pltpu.VMEM((HG_HEADS, 128, HG_KDIM), F32)],
        compiler_params=_cparams(("arbitrary", "arbitrary")), name="hgrn2",
    )(proj, proj, proj, proj, par)


def _mlstm_kernel(v_ref, o_ref, q_ref, k_ref, sm_ref, par_ref, nw_ref, tri_ref, sel_ref, dg_ref,
                  y_ref, c_ref, m_ref):
    s = pl.program_id(0)

    @pl.when(s == 0)
    def _():
        c_ref[...] = jnp.zeros_like(c_ref)
        m_ref[...] = jnp.zeros_like(m_ref)

    n = CHUNK
    rid = lax.broadcasted_iota(jnp.int32, (n, 128), 0)
    pad = (s * n + rid) < LEAD_PAD
    sel_i = sel_ref[0]
    sel_f = sel_ref[1]
    dg = dg_ref[...]
    w = ML_HEADS * n
    tt = lax.broadcasted_iota(jnp.int32, (n, w), 0)
    ss = lax.broadcasted_iota(jnp.int32, (n, w), 1) & (n - 1)
    causal = tt >= ss
    scale = ML_QK ** -0.5
    lane = lax.broadcasted_iota(jnp.int32, (n, 128), 1)
    one_col = jnp.where(lane == 0, 1.0, 0.0).astype(BF16)
    for b in range(v_ref.shape[0]):
        pre = sm_ref[b] + par_ref[0:1, :]
        cap = GATE_CAP * jnp.tanh(pre * (1.0 / GATE_CAP))
        log_i = jnp.where(pad, NEG, cap)
        log_f = jnp.where(pad, 0.0, _log_sigmoid(cap))
        cum = _sel_dot(tri_ref[...], log_f)
        cum_col = _dot_sel(cum, sel_f)
        cum_row = jnp.sum(cum_col * dg, axis=0, keepdims=True)
        li_row = jnp.sum(_dot_sel(log_i, sel_i) * dg, axis=0, keepdims=True)
        dmat = jnp.where(causal, cum_col - cum_row + li_row, NEG)
        hs = range(ML_HEADS)
        sh = [b * ML_HEADS + h for h in hs]
        d_h = [dmat[:, h * n:(h + 1) * n] for h in hs]
        cum_h = [cum[:, LANE_MF + h:LANE_MF + h + 1] for h in hs]
        li_h = [log_i[:, LANE_MI + h:LANE_MI + h + 1] for h in hs]
        m_st = [m_ref[sh[h]:sh[h] + 1, 0:1] for h in hs]
        q = [(q_ref[b, :, h * ML_QK:(h + 1) * ML_QK] * scale).astype(BF16) for h in hs]
        k = [k_ref[b, :, h * ML_QK:(h + 1) * ML_QK] for h in hs]
        v_aug = [jnp.concatenate([v_ref[b, :, h * ML_V:(h + 1) * ML_V].astype(BF16), one_col], axis=1) for h in hs]
        c_prev = [c_ref[sh[h]] for h in hs]
        inter = [cum_h[h] + m_st[h] for h in hs]
        m_t = [jnp.maximum(inter[h], jnp.max(d_h[h], axis=-1, keepdims=True)) for h in hs]
        qk = [_dot_nt(q[h], k[h].astype(BF16)) for h in hs]
        qc = [_dot(q[h], c_prev[h].astype(BF16)) for h in hs]
        pw = [(qk[h] * jnp.exp(d_h[h] - m_t[h])).astype(BF16) for h in hs]
        nd = [_dot(pw[h], v_aug[h]) + jnp.exp(inter[h] - m_t[h]) * qc[h] for h in hs]
        hh = [nd[h][:, :ML_V] / jnp.maximum(jnp.abs(nd[h][:, ML_V:ML_V + 1]), jnp.exp(-m_t[h])) for h in hs]
        tot = [cum_h[h][n - 1:n, :] for h in hs]
        to_end = [tot[h] - cum_h[h] + li_h[h] for h in hs]
        m_loc = [jnp.max(to_end[h], axis=0, keepdims=True) for h in hs]
        kw = [(k[h] * jnp.exp(to_end[h] - m_loc[h])).astype(BF16) for h in hs]
        c_loc = [_dot_tn(kw[h], v_aug[h]) for h in hs]
        m_new = [jnp.maximum(tot[h] + m_st[h], m_loc[h]) for h in hs]
        for h in hs:
            c_ref[sh[h]] = (jnp.exp(tot[h] + m_st[h] - m_new[h]) * c_prev[h]
                            + jnp.exp(m_loc[h] - m_new[h]) * c_loc[h])
            m_ref[sh[h]:sh[h] + 1, :] = jnp.broadcast_to(m_new[h], (1, 128))
        for h in hs:
            ms = jnp.mean(hh[h] * hh[h], axis=-1, keepdims=True)
            cols = slice(h * ML_V, (h + 1) * ML_V)
            y_ref[b, :, cols] = (hh[h] * lax.rsqrt(ms + EPS) * nw_ref[:, cols]
                                 * _sigmoid(o_ref[b, :, cols])).astype(y_ref.dtype)


def _lane_select(lane0, heads, width):
    m = np.zeros((128, heads * width), np.float32)
    for h in range(heads):
        m[lane0 + h, h * width:(h + 1) * width] = 1.0
    return m


def _diag_mask(n, heads):
    return np.tile(np.eye(n, dtype=np.float32), (1, heads))


def _mlstm_call(proj, par, nw, batch, p):
    t = proj.shape[0]
    n = CHUNK
    nc = p // n
    tri = jnp.asarray(np.tril(np.ones((n, n), np.float32)), BF16)
    sel = jnp.asarray(np.stack([_lane_select(LANE_MI, ML_HEADS, n), _lane_select(LANE_MF, ML_HEADS, n)]), BF16)
    dg = jnp.asarray(_diag_mask(n, ML_HEADS), F32)
    proj3 = proj.reshape(batch, p, proj.shape[1])
    blk = lambda w, off: pl.BlockSpec((batch, n, w), lambda s: (0, s, off // w))
    const = lambda shape: pl.BlockSpec(shape, lambda s: (0,) * len(shape))
    m_rows = -(-batch * ML_HEADS // 8) * 8
    y = pl.pallas_call(
        _mlstm_kernel, grid=(nc,),
        in_specs=[blk(ML_WIDTH, C_MV), blk(ML_WIDTH, C_MV + ML_WIDTH), blk(ML_QK_W, C_MQ),
                  blk(ML_QK_W, C_MQ + ML_QK_W), blk(128, C_SMALL),
                  const((8, 128)), const((1, ML_WIDTH)), const((n, n)), const((2, 128, ML_HEADS * n)),
                  const((n, ML_HEADS * n))],
        out_specs=pl.BlockSpec((batch, n, ML_WIDTH), lambda s: (0, s, 0)),
        out_shape=jax.ShapeDtypeStruct((batch, p, ML_WIDTH), BF16),
        scratch_shapes=[pltpu.VMEM((batch * ML_HEADS, ML_QK, 2 * ML_V), F32), pltpu.VMEM((m_rows, 128), F32)],
        compiler_params=_cparams(("arbitrary",)), name="mlstm",
    )(proj3, proj3, proj3, proj3, proj3, par, nw, tri, sel, dg)
    return y.reshape(t, ML_WIDTH)


def _ssd_kernel(z_ref, x_ref, b_ref, c_ref, sm_ref, cw_ref, cb_ref, par_ref, dsk_ref, nw_ref,
                tri_ref, sel_ref, dg_ref, y_ref, xs_ref, bs_ref, cs_ref, st_ref):
    s = pl.program_id(1)
    n = CHUNK
    tail = 8

    @pl.when(s == 0)
    def _():
        st_ref[...] = jnp.zeros_like(st_ref)
        xs_ref[0:tail, :] = jnp.zeros((tail, xs_ref.shape[1]), F32)
        bs_ref[0:tail, :] = jnp.zeros((tail, bs_ref.shape[1]), F32)
        cs_ref[0:tail, :] = jnp.zeros((tail, cs_ref.shape[1]), F32)

    def conv_silu(src_ref, scr_ref, c0, width, rowmask):
        scr_ref[tail:tail + n, :] = src_ref[...]
        acc = cb_ref[:, c0:c0 + width]
        for j in range(CONV_W):
            off = tail - (CONV_W - 1) + j
            acc = acc + cw_ref[j:j + 1, c0:c0 + width] * scr_ref[off:off + n, :]
        scr_ref[0:tail, :] = scr_ref[n:n + tail, :]
        return jnp.where(rowmask, 0.0, _silu(acc))

    def padmask(width):
        return (s * n + lax.broadcasted_iota(jnp.int32, (n, width), 0)) < LEAD_PAD

    x = conv_silu(x_ref, xs_ref, 0, SSM_WIDTH, padmask(SSM_WIDTH))
    bm = conv_silu(b_ref, bs_ref, SSM_WIDTH, SSM_BC, padmask(SSM_BC))
    cm = conv_silu(c_ref, cs_ref, SSM_WIDTH + SSM_BC, SSM_BC, padmask(SSM_BC))

    pre = sm_ref[...] + par_ref[0:1, :]
    dt = jnp.maximum(pre, 0.0) + jnp.log1p(jnp.exp(-jnp.abs(pre)))
    dt = jnp.where(padmask(128), 0.0, dt)
    da = dt * (-jnp.exp(par_ref[1:2, :]))
    cum = _sel_dot(tri_ref[...], da)
    sel = sel_ref[...]
    dt_col = _dot_sel(dt, sel)
    cum_col = _dot_sel(cum, sel)
    cum_row = jnp.sum(cum_col * dg_ref[...], axis=0, keepdims=True)
    w = SSM_WIDTH
    tt = lax.broadcasted_iota(jnp.int32, (n, w), 0)
    ss = lax.broadcasted_iota(jnp.int32, (n, w), 1) & (n - 1)
    decay = jnp.exp(jnp.where(tt >= ss, cum_col - cum_row, NEG))
    xdt = x * dt_col
    last = cum_col[n - 1:n, :]
    wend = (xdt * jnp.exp(last - cum_col)).astype(BF16)
    chunk_decay = jnp.exp(last)
    ecum = jnp.exp(cum_col)
    gw = SSM_HPG * SSM_HEADDIM
    rr = lax.broadcasted_iota(jnp.int32, (gw, gw), 0) // SSM_HEADDIM
    cc = lax.broadcasted_iota(jnp.int32, (gw, gw), 1) // SSM_HEADDIM
    blockdiag = rr == cc
    ys = []
    for g in range(SSM_GROUPS):
        gl = slice(g * gw, (g + 1) * gw)
        sl = slice(g * SSM_STATE, (g + 1) * SSM_STATE)
        cm_g = cm[:, sl].astype(BF16)
        bm_g = bm[:, sl].astype(BF16)
        cb = _dot_nt(cm_g, jnp.concatenate([bm_g] * SSM_HPG, axis=0))
        m = (cb * decay[:, gl]).astype(BF16)
        xdt_g = xdt[:, gl]
        bd = jnp.where(blockdiag, jnp.concatenate([xdt_g] * SSM_HPG, axis=0), 0.0).astype(BF16)
        st = st_ref[g]
        y_g = _dot(m, bd) + ecum[:, gl] * _dot(cm_g, st.astype(BF16))
        st_ref[g] = st * chunk_decay[:, gl] + _dot_tn(bm_g, wend[:, gl])
        ys.append(y_g)
    y = jnp.concatenate(ys, axis=1) + x * dsk_ref[...]
    y = y * _silu(z_ref[...])
    outs = []
    for g in range(SSM_GROUPS):
        gl = slice(g * gw, (g + 1) * gw)
        y_g = y[:, gl]
        ms = jnp.mean(y_g * y_g, axis=-1, keepdims=True)
        outs.append(y_g * lax.rsqrt(ms + EPS))
    y_ref[...] = (jnp.concatenate(outs, axis=1) * nw_ref[...]).astype(y_ref.dtype)


def _ssd_call(proj, cw, cb, par, dsk, nw, batch, p):
    t = proj.shape[0]
    n = CHUNK
    nc = p // n
    tri = jnp.asarray(np.tril(np.ones((n, n), np.float32)), BF16)
    sel = jnp.asarray(_lane_select(LANE_DT, SSM_HEADS, SSM_HEADDIM), BF16)
    dg = jnp.asarray(_diag_mask(n, SSM_HEADS), F32)
    blk = lambda w, off: pl.BlockSpec((n, w), lambda b, s: (b * nc + s, off // w))
    const = lambda shape: pl.BlockSpec(shape, lambda b, s: (0,) * len(shape))
    cch = SSM_WIDTH + 2 * SSM_BC
    return pl.pallas_call(
        _ssd_kernel, grid=(batch, nc),
        in_specs=[blk(SSM_WIDTH, C_SZ), blk(SSM_WIDTH, C_SX), blk(SSM_BC, C_SX + SSM_WIDTH),
                  blk(SSM_BC, C_SX + SSM_WIDTH + SSM_BC), blk(128, C_SMALL),
                  const((CONV_W, cch)), const((1, cch)), const((8, 128)), const((1, SSM_WIDTH)),
                  const((1, SSM_WIDTH)), const((n, n)), const((128, SSM_WIDTH)), const((n, SSM_WIDTH))],
        out_specs=pl.BlockSpec((n, SSM_WIDTH), lambda b, s: (b * nc + s, 0)),
        out_shape=jax.ShapeDtypeStruct((t, SSM_WIDTH), BF16),
        scratch_shapes=[pltpu.VMEM((n + 8, SSM_WIDTH), F32), pltpu.VMEM((n + 8, SSM_BC), F32),
                        pltpu.VMEM((n + 8, SSM_BC), F32),
                        pltpu.VMEM((SSM_GROUPS, SSM_STATE, SSM_HPG * SSM_HEADDIM), F32)],
        compiler_params=_cparams(("arbitrary", "arbitrary")), name="ssd",
    )(proj, proj, proj, proj, proj, cw, cb, par, dsk, nw, tri, sel, dg)


LANE_E0 = N_GROUPS_MOE
RT_E, RT_RANK, RT_W = 0, 2, 4


def _first_max(vals, lane):
    m = jnp.max(vals, axis=-1, keepdims=True)
    idx = jnp.min(jnp.where(vals == m, lane, 128), axis=-1, keepdims=True)
    return m, idx


def _outproj_kernel(ya_ref, yb_ref, yc_ref, h_ref, w_ref, nw_ref, rh_ref, rl_ref, rb_ref, tri_ref,
                    hm_ref, u_ref, rt_ref, cnt_ref):
    a0, a1 = HG_WIDTH, HG_WIDTH + ML_WIDTH
    h = h_ref[...]
    h = h + _dot(ya_ref[...], w_ref[0:a0, :])
    h = h + _dot(yb_ref[...], w_ref[a0:a1, :])
    h = h + _dot(yc_ref[...], w_ref[a1:, :])
    hm_ref[...] = h
    ms = jnp.mean(h * h, axis=-1, keepdims=True)
    u = h * lax.rsqrt(ms + EPS) * nw_ref[...]
    u_ref[...] = u
    u_hi = u.astype(BF16)
    u_lo = (u - u_hi.astype(F32)).astype(BF16)
    lg = _dot(u_hi, rh_ref[...]) + (_dot(u_lo, rh_ref[...]) + _dot(u_hi, rl_ref[...])) + rb_ref[...]

    tm = lg.shape[0]
    lane = lax.broadcasted_iota(jnp.int32, (tm, 128), 1)
    g_mask = lane < N_GROUPS_MOE
    g_max, g_sel = _first_max(jnp.where(g_mask, lg, NEG), lane)
    g_gate = 1.0 / jnp.sum(jnp.where(g_mask, jnp.exp(lg - g_max), 0.0), axis=-1, keepdims=True)
    lo = LANE_E0 + g_sel * EXPERTS_PER_GROUP
    e_vals = jnp.where((lane >= lo) & (lane < lo + EXPERTS_PER_GROUP), lg, NEG)
    v1, i1 = _first_max(e_vals, lane)
    v2, i2 = _first_max(jnp.where(lane == i1, NEG, e_vals), lane)
    a = jnp.exp(v2 - v1)
    w1 = g_gate / (1.0 + a)
    w2 = w1 * a
    @pl.when(pl.program_id(0) == 0)
    def _():
        cnt_ref[...] = jnp.zeros_like(cnt_ref)

    hit1 = lane == i1
    hit2 = lane == i2
    onehot = jnp.where(hit1 | hit2, 1.0, 0.0)
    before = _dot(tri_ref[...], onehot.astype(BF16)) + cnt_ref[0:1, :]
    r1 = jnp.sum(jnp.where(hit1, before, 0.0), axis=-1, keepdims=True)
    r2 = jnp.sum(jnp.where(hit2, before, 0.0), axis=-1, keepdims=True)
    cnt_ref[...] = cnt_ref[...] + jnp.sum(onehot, axis=0, keepdims=True)
    rec = jnp.zeros((tm, 128), F32)
    for ln, val in ((RT_E, (i1 - LANE_E0).astype(F32)), (RT_E + 1, (i2 - LANE_E0).astype(F32)),
                    (RT_RANK, r1), (RT_RANK + 1, r2), (RT_W, w1), (RT_W + 1, w2)):
        rec = jnp.where(lane == ln, val, rec)
    rt_ref[...] = rec


def _outproj_call(ya, yb, yc, h, w, nw, r_hi, r_lo, r_bias, tm_target=352):
    t, d = h.shape
    tm = _row_tile(t, tm_target)
    tri = jnp.asarray(np.tril(np.ones((tm, tm), np.float32), -1), BF16)
    row = lambda i: (i, 0)
    const = lambda shape: pl.BlockSpec(shape, lambda i: (0, 0))
    return pl.pallas_call(
        _outproj_kernel, grid=(t // tm,),
        in_specs=[pl.BlockSpec((tm, HG_WIDTH), row), pl.BlockSpec((tm, ML_WIDTH), row),
                  pl.BlockSpec((tm, SSM_WIDTH), row), pl.BlockSpec((tm, d), row),
                  const((D_MIX, d)), const((1, d)), const((d, 128)), const((d, 128)), const((1, 128)),
                  const((tm, tm))],
        out_specs=[pl.BlockSpec((tm, d), row), pl.BlockSpec((tm, d), row), pl.BlockSpec((tm, 128), row),
                   const((8, 128))],
        out_shape=[jax.ShapeDtypeStruct((t, d), F32), jax.ShapeDtypeStruct((t, d), F32),
                   jax.ShapeDtypeStruct((t, 128), F32), jax.ShapeDtypeStruct((8, 128), F32)],
        compiler_params=_cparams(("arbitrary",)), name="out_proj_router",
    )(ya, yb, yc, h, w, nw.reshape(1, d), r_hi, r_lo, r_bias, tri)


def _moe_kernel(be_ref, nu_ref, src0_ref, srcn_ref, dst_ref, u_hbm, w1_ref, w3_ref, w2_ref, o_hbm,
                xbuf, ybuf, w1b, w3b, w2b, gsem, ssem):
    i = pl.program_id(0)
    bm = MOE_BM
    n_used = nu_ref[0]
    slot = i % 2

    def gather(idx_ref, sl):
        def body(g, c):
            for j in range(8):
                r = g * 8 + j
                tok = idx_ref[0, r]
                pltpu.make_async_copy(u_hbm.at[pl.ds(tok, 1), :], xbuf.at[sl, pl.ds(r, 1), :],
                                      gsem.at[sl]).start(priority=j % 2)
            return c
        lax.fori_loop(0, bm // 8, body, 0)

    def scatter(sl):
        def body(g, c):
            for j in range(8):
                r = g * 8 + j
                row = dst_ref[0, r]
                pltpu.make_async_copy(ybuf.at[sl, pl.ds(r, 1), :], o_hbm.at[pl.ds(row, 1), :],
                                      ssem.at[sl]).start(priority=j % 2)
            return c
        lax.fori_loop(0, bm // 8, body, 0)

    def wait_gather(sl):
        pltpu.make_async_copy(u_hbm.at[pl.ds(0, bm), :], xbuf.at[sl], gsem.at[sl]).wait()

    def wait_scatter(sl):
        pltpu.make_async_copy(ybuf.at[sl], o_hbm.at[pl.ds(0, bm), :], ssem.at[sl]).wait()

    @pl.when(i == 0)
    def _():
        gather(src0_ref, 0)
        ybuf[...] = jnp.zeros_like(ybuf)
        n_real = o_hbm.shape[0] - 2 * bm
        for sl in range(2):
            fill = pltpu.make_async_copy(ybuf.at[sl], o_hbm.at[pl.ds(n_real + sl * bm, bm), :], ssem.at[sl])
            fill.start()
            fill.wait()

    @pl.when(i + 1 < n_used)
    def _():
        gather(srcn_ref, 1 - slot)

    @pl.when(i < n_used)
    def _():
        first = jnp.logical_or(i == 0, be_ref[i] != be_ref[jnp.maximum(i - 1, 0)])

        @pl.when(first)
        def _():
            w1b[...] = w1_ref[...].astype(BF16)
            w3b[...] = w3_ref[...].astype(BF16)
            w2b[...] = w2_ref[...].astype(BF16)

        wait_gather(slot)
        x = xbuf[slot].astype(BF16)
        hid = _silu(_dot(x, w1b[...])) * _dot(x, w3b[...])
        y = _dot(hid.astype(BF16), w2b[...])

        @pl.when(i >= 2)
        def _():
            wait_scatter(slot)

        ybuf[slot] = y
        scatter(slot)

    @pl.when(i == n_used - 1)
    def _():
        wait_scatter(slot)

        @pl.when(i >= 1)
        def _():
            wait_scatter(1 - slot)


def _moe_call(block_e, n_used, src_tok, dst_row, u, w1, w3, w2, layer, n_out_rows):
    t, d = u.shape
    f = w1.shape[-1]
    bm = MOE_BM
    nb = block_e.shape[0]
    wmap = lambda i, be, nu: (layer, be[i], 0, 0)
    src_tok = src_tok.reshape(nb, 1, bm)
    dst_row = dst_row.reshape(nb, 1, bm)
    idx = lambda fn: pl.BlockSpec((None, 1, bm), fn, memory_space=pltpu.SMEM)
    return pl.pallas_call(
        _moe_kernel,
        grid_spec=pltpu.PrefetchScalarGridSpec(
            num_scalar_prefetch=2, grid=(nb,),
            in_specs=[idx(lambda i, be, nu: (0, 0, 0)),
                      idx(lambda i, be, nu: (jnp.minimum(i + 1, nb - 1), 0, 0)),
                      idx(lambda i, be, nu: (i, 0, 0)),
                      pl.BlockSpec(memory_space=pl.ANY),
                      pl.BlockSpec((None, None, d, f), wmap), pl.BlockSpec((None, None, d, f), wmap),
                      pl.BlockSpec((None, None, f, d), wmap)],
            out_specs=pl.BlockSpec(memory_space=pl.ANY),
            scratch_shapes=[pltpu.VMEM((2, bm, d), F32), pltpu.VMEM((2, bm, d), F32),
                            pltpu.VMEM((d, f), BF16), pltpu.VMEM((d, f), BF16), pltpu.VMEM((f, d), BF16),
                            pltpu.SemaphoreType.DMA((2,)), pltpu.SemaphoreType.DMA((2,))]),
        out_shape=jax.ShapeDtypeStruct((n_out_rows, d), F32),
        compiler_params=_cparams(("arbitrary",)), name="moe_ffn",
    )(block_e, n_used, src_tok, src_tok, dst_row, u, w1, w3, w2)


def _route_tables(rt, cnt, t):
    bm = MOE_BM
    tk = t * TOP_K
    e = rt[:, RT_E:RT_E + TOP_K].astype(jnp.int32)
    rank = rt[:, RT_RANK:RT_RANK + TOP_K].astype(jnp.int32)
    wts = rt[:, RT_W:RT_W + TOP_K]
    counts = cnt[0, LANE_E0:LANE_E0 + N_EXPERTS].astype(jnp.int32)
    padded = (counts + bm - 1) // bm * bm
    pad_ends = jnp.cumsum(padded)
    pad_starts = pad_ends - padded
    dest = (pad_starts[e] + rank).reshape(-1)
    n_blocks = -(-tk // bm) + N_EXPERTS
    n_rows = n_blocks * bm
    inv = jnp.full((n_rows,), -1, jnp.int32).at[dest].set(jnp.arange(tk, dtype=jnp.int32))
    tok, slot = inv // TOP_K, inv % TOP_K
    src_tok = jnp.where(inv >= 0, tok, 0)
    dst_row = jnp.where(inv >= 0, slot * t + tok, tk + (jnp.arange(n_rows, dtype=jnp.int32) % (2 * bm)))
    block_row0 = jnp.arange(n_blocks, dtype=jnp.int32) * bm
    block_e = jnp.minimum(jnp.sum(pad_ends[None, :] <= block_row0[:, None], axis=1), N_EXPERTS - 1).astype(jnp.int32)
    n_used = (pad_ends[-1] // bm).astype(jnp.int32).reshape(1)
    last_e = block_e[jnp.maximum(n_used[0] - 1, 0)]
    block_e = jnp.where(jnp.arange(n_blocks) < n_used[0], block_e, last_e)
    return block_e, n_used, src_tok, dst_row, wts, tk + 2 * bm


def _lane_row(pairs):
    row = jnp.zeros((128,), F32)
    for lane0, vals in pairs:
        row = row.at[lane0:lane0 + vals.shape[0]].set(vals.astype(F32))
    return row


def kernel(x, meta_tokens, hg_lb_logits, norm_mix_w, w_in, hg_norm_w, ml_b_i, ml_b_f, ml_norm_w,
           ssm_conv_w, ssm_conv_b, ssm_dt_bias, ssm_a_log, ssm_d, ssm_norm_w, w_out, norm_ffn_w,
           moe_w_group, moe_b_group, moe_w_router, moe_b_router, moe_w1, moe_w3, moe_w2, final_norm_w):
    batch, seq, d = x.shape
    depth = w_in.shape[0]
    p = LEAD_PAD + N_META + seq
    t = batch * p
    meta = jnp.broadcast_to(meta_tokens.astype(x.dtype)[None], (batch, N_META, d))
    h = jnp.concatenate([jnp.zeros((batch, LEAD_PAD, d), x.dtype), meta, x], axis=1).reshape(t, d)

    lb_w = jax.nn.softmax(hg_lb_logits.astype(F32), axis=0)
    lower_bounds = jnp.cumsum(lb_w, axis=0) - lb_w[0]

    contrib = wts = None
    for layer in range(depth):
        w = w_in[layer]
        o_mq, o_mv, o_mi, o_sz = HG_KEY * 2 + HG_WIDTH * 2, 2560, 3584, 3592
        o_sx = o_sz + SSM_WIDTH
        o_dt = o_sx + SSM_WIDTH + 2 * SSM_BC
        w_perm = jnp.concatenate([
            w[:, :o_mq], w[:, o_sz:o_sx], w[:, o_sx:o_dt], w[:, o_mv:o_mi], w[:, o_mq:o_mv],
            w[:, o_mi:o_sz], w[:, o_dt:], jnp.zeros((d, N_PROJ - C_SMALL - 2 * ML_HEADS - SSM_HEADS), w.dtype)],
            axis=1).astype(BF16)
        lb = lower_bounds[layer]
        lbf = jnp.maximum(lb, LB_FLOOR)
        hg_par = jnp.zeros((8, HG_KEY), F32).at[0].set(lbf).at[1].set(1.0 - lb).at[2].set(lbf - lb).at[3].set(hg_norm_w[layer])
        ml_par = jnp.zeros((8, 128), F32).at[0].set(_lane_row([(LANE_MI, ml_b_i[layer]), (LANE_MF, ml_b_f[layer])]))
        ss_par = jnp.zeros((8, 128), F32).at[0].set(_lane_row([(LANE_DT, ssm_dt_bias[layer])]))
        ss_par = ss_par.at[1].set(_lane_row([(LANE_DT, ssm_a_log[layer])]))
        dskip = jnp.repeat(ssm_d[layer].astype(F32), SSM_HEADDIM).reshape(1, SSM_WIDTH)
        w_r = jnp.concatenate([moe_w_group[layer],
                               moe_w_router[layer].transpose(1, 0, 2).reshape(d, N_EXPERTS),
                               jnp.zeros((d, 128 - N_GROUPS_MOE - N_EXPERTS), F32)], axis=1)
        r_hi = w_r.astype(BF16)
        r_lo = (w_r - r_hi.astype(F32)).astype(BF16)

        if layer == 0:
            (u,) = _norm_call(h, norm_mix_w[layer], write_h=False, u_dtype=BF16)
        else:
            h, u = _norm_call(h, norm_mix_w[layer], contrib, wts, write_h=True, u_dtype=BF16)
        proj = _inproj_call(u, w_perm)
        ya = _hgrn2_call(proj, hg_par, batch, p)
        yb = _mlstm_call(proj, ml_par, ml_norm_w[layer].reshape(1, ML_WIDTH), batch, p)
        yc = _ssd_call(proj, ssm_conv_w[layer], ssm_conv_b[layer].reshape(1, -1), ss_par, dskip,
                       ssm_norm_w[layer].reshape(1, SSM_WIDTH), batch, p)
        r_bias = _lane_row([(0, moe_b_group[layer]), (LANE_E0, moe_b_router[layer].reshape(-1))]).reshape(1, 128)
        h, u_ffn, rt, cnt = _outproj_call(ya, yb, yc, h, w_out[layer].astype(BF16), norm_ffn_w[layer],
                                          r_hi, r_lo, r_bias)
        block_e, n_used, src_tok, dst_row, wts, n_out_rows = _route_tables(rt, cnt, t)
        contrib = _moe_call(block_e, n_used, src_tok, dst_row, u_ffn, moe_w1, moe_w3, moe_w2, layer, n_out_rows)
    return _final_call(h, contrib, wts, final_norm_w, batch, p)
```

```python
import functools

import jax
import jax.numpy as jnp
import numpy as np
from jax import lax
from jax.experimental import pallas as pl
from jax.experimental.pallas import tpu as pltpu

F32 = jnp.float32
BF16 = jnp.bfloat16

D_MODEL = 2048
N_META = 16
CHUNK = 64
HG_CHUNK = 16
LEAD_PAD = CHUNK - N_META
EPS = 1e-6
NEG = -1e30
LB_FLOOR = 1e-30

HG_HEADS = 4
HG_KDIM = 128
HG_KEY = HG_HEADS * HG_KDIM
HG_WIDTH = HG_HEADS * 128

ML_HEADS = 4
ML_QK = 64
ML_V = 128
ML_QK_W = ML_HEADS * ML_QK
ML_WIDTH = ML_HEADS * ML_V
GATE_CAP = 15.0

SSM_HEADS = 16
SSM_HEADDIM = 64
SSM_WIDTH = SSM_HEADS * SSM_HEADDIM
SSM_STATE = 128
SSM_GROUPS = 4
SSM_HPG = SSM_HEADS // SSM_GROUPS
SSM_BC = SSM_GROUPS * SSM_STATE
CONV_W = 4

D_MIX = HG_WIDTH + ML_WIDTH + SSM_WIDTH

N_GROUPS_MOE = 4
EXPERTS_PER_GROUP = 8
N_EXPERTS = N_GROUPS_MOE * EXPERTS_PER_GROUP
TOP_K = 2
D_EXPERT = 512
MOE_BM = 128

C_HG = 0
C_SZ = 2048
C_SX = 3072
C_MV = 5120
C_MQ = 6144
C_SMALL = 6656
N_PROJ = 6912
LANE_MI = 0
LANE_MF = ML_HEADS
LANE_DT = 2 * ML_HEADS

VMEM_LIMIT = 56 * 1024 * 1024


def _cparams(sem):
    return pltpu.CompilerParams(dimension_semantics=sem, vmem_limit_bytes=VMEM_LIMIT)


def _row_tile(n, target, mult=16):
    best = None
    for t in range(mult, min(n, target) + 1, mult):
        if n % t == 0:
            best = t
    assert best is not None, (n, target, mult)
    return best


def _split3(x):
    hi = x.astype(BF16)
    r = x - hi.astype(F32)
    mid = r.astype(BF16)
    lo = (r - mid.astype(F32)).astype(BF16)
    return hi, mid, lo


def _dot(a, b):
    return jnp.dot(a, b, preferred_element_type=F32)


def _sel_dot(sel, x):
    hi, mid, lo = _split3(x)
    return _dot(sel, hi) + _dot(sel, mid) + _dot(sel, lo)


def _dot_sel(x, sel):
    hi, mid, lo = _split3(x)
    return _dot(hi, sel) + _dot(mid, sel) + _dot(lo, sel)


def _dot_nt(a, b):
    return lax.dot_general(a, b, (((1,), (1,)), ((), ())), preferred_element_type=F32)


def _dot_tn(a, b):
    return lax.dot_general(a, b, (((0,), (0,)), ((), ())), preferred_element_type=F32)


def _log_sigmoid(x):
    return jnp.minimum(x, 0.0) - jnp.log1p(jnp.exp(-jnp.abs(x)))


def _sigmoid(x):
    return 1.0 / (1.0 + jnp.exp(-x))


def _silu(x):
    return x * _sigmoid(x)


def _norm_kernel(*refs, combine, write_h):
    if combine:
        h_ref, c0_ref, c1_ref, wt_ref, nw_ref = refs[:5]
        outs = refs[5:]
        wt = wt_ref[...]
        h = h_ref[...] + wt[:, 0:1] * c0_ref[...] + wt[:, 1:2] * c1_ref[...]
    else:
        h_ref, nw_ref = refs[:2]
        outs = refs[2:]
        h = h_ref[...]
    if write_h:
        outs[0][...] = h
    ms = jnp.mean(h * h, axis=-1, keepdims=True)
    u_ref = outs[-1]
    u_ref[...] = (h * lax.rsqrt(ms + EPS) * nw_ref[...]).astype(u_ref.dtype)


def _norm_call(h, nw, contrib=None, wts=None, *, write_h, u_dtype, tm_target=264):
    t, d = h.shape
    tm = _row_tile(t, tm_target)
    combine = contrib is not None
    row = lambda i: (i, 0)
    in_specs = [pl.BlockSpec((tm, d), row)]
    args = [h]
    if combine:
        in_specs += [pl.BlockSpec((tm, d), row), pl.BlockSpec((tm, d), lambda i: (t // tm + i, 0)),
                     pl.BlockSpec((tm, 2), row)]
        args += [contrib, contrib, wts]
    in_specs.append(pl.BlockSpec((1, d), lambda i: (0, 0)))
    args.append(nw.reshape(1, d))
    out_shape, out_specs = [], []
    if write_h:
        out_shape.append(jax.ShapeDtypeStruct((t, d), F32))
        out_specs.append(pl.BlockSpec((tm, d), row))
    out_shape.append(jax.ShapeDtypeStruct((t, d), u_dtype))
    out_specs.append(pl.BlockSpec((tm, d), row))
    return pl.pallas_call(
        functools.partial(_norm_kernel, combine=combine, write_h=write_h),
        grid=(t // tm,), in_specs=in_specs, out_specs=out_specs, out_shape=out_shape,
        compiler_params=_cparams(("arbitrary",)), name="combine_norm",
    )(*args)


def _final_kernel(h_ref, c0_ref, c1_ref, wt_ref, nw_ref, o_ref):
    wt = wt_ref[...]
    h = h_ref[...] + wt[:, 0:1] * c0_ref[...] + wt[:, 1:2] * c1_ref[...]
    ms = jnp.mean(h * h, axis=-1, keepdims=True)
    o_ref[...] = h * lax.rsqrt(ms + EPS) * nw_ref[...]


def _final_call(h, contrib, wts, nw, batch, p):
    t, d = h.shape
    seq = p - CHUNK
    tm = _row_tile(seq, 256)
    n_out = seq // tm
    src = lambda b, i: (pl.multiple_of(b * p + CHUNK + i * tm, CHUNK), 0)
    src1 = lambda b, i: (pl.multiple_of(t + b * p + CHUNK + i * tm, CHUNK), 0)
    win = lambda w, fn: pl.BlockSpec((pl.Element(tm), pl.Element(w)), fn)
    return pl.pallas_call(
        _final_kernel, grid=(batch, n_out),
        in_specs=[win(d, src), win(d, src), win(d, src1), win(2, src),
                  pl.BlockSpec((1, d), lambda b, i: (0, 0))],
        out_specs=pl.BlockSpec((None, tm, d), lambda b, i: (b, i, 0)),
        out_shape=jax.ShapeDtypeStruct((batch, seq, d), F32),
        compiler_params=_cparams(("arbitrary", "arbitrary")), name="final_norm",
    )(h, contrib, contrib, wts, nw.reshape(1, d))


def _matmul_kernel(x_ref, w_ref, o_ref):
    o_ref[...] = _dot(x_ref[...], w_ref[...])


def _inproj_call(u, w, tm_target=1056, tn=768):
    t, d = u.shape
    n = w.shape[1]
    tm = _row_tile(t, tm_target)
    assert n % tn == 0
    return pl.pallas_call(
        _matmul_kernel, grid=(t // tm, n // tn),
        in_specs=[pl.BlockSpec((tm, d), lambda i, j: (i, 0)), pl.BlockSpec((d, tn), lambda i, j: (0, j))],
        out_specs=pl.BlockSpec((tm, tn), lambda i, j: (i, j)),
        out_shape=jax.ShapeDtypeStruct((t, n), F32),
        compiler_params=_cparams(("arbitrary", "arbitrary")), name="in_proj",
    )(u, w)


def _hgrn2_kernel(q_ref, f_ref, i_ref, g_ref, par_ref, o_ref, st_ref, *, rows):
    s = pl.program_id(1)

    @pl.when(s == 0)
    def _():
        st_ref[...] = jnp.zeros_like(st_ref)

    c = HG_CHUNK
    ones = jnp.ones((HG_KDIM, 128), BF16)
    rid = lax.broadcasted_iota(jnp.int32, (c, 128), 0)
    scale = HG_KDIM ** -0.5

    hc = c // 2

    def chunk(ci, carry):
        r0 = pl.multiple_of(ci * c, c)
        pad = (s * rows + r0 + rid) < LEAD_PAD

        def front(h):
            cols = slice(h * 128, (h + 1) * 128)
            a_lb = par_ref[0:1, cols]
            b_lb = par_ref[1:2, cols]
            c_lb = par_ref[2:3, cols]
            z = f_ref[pl.ds(r0, c), cols]
            sg = _sigmoid(z)
            f = a_lb + b_lb * sg
            log_f = jnp.where(pad, 0.0, jnp.log(f))
            k = jnp.where(pad, 0.0, b_lb * (1.0 - sg) - c_lb)
            q = q_ref[pl.ds(r0, c), cols] * scale
            v = i_ref[pl.ds(r0, c), cols]
            cum = log_f
            for sh in (1, 2, 4, 8):
                cum = cum + jnp.where(rid >= sh, pltpu.roll(cum, sh, axis=0), 0.0)
            parts = []
            for s_ in range(c):
                lo = 0 if s_ < hc else hc
                rel = jnp.where(rid[lo:] >= s_, cum[lo:] - cum[s_:s_ + 1, :], NEG)
                parts.append(q[lo:] * (k[s_:s_ + 1, :] * jnp.exp(rel)))
            sc = _dot(jnp.concatenate(parts, axis=0).astype(BF16), ones)
            st = st_ref[h]
            o_inter = _dot_nt((q * jnp.exp(cum)).astype(BF16), st.astype(BF16))
            last = cum[c - 1:c, :]
            kd = (k * jnp.exp(last - cum)).astype(BF16)
            st_ref[h] = st * jnp.exp(last) + _dot_tn(v.astype(BF16), kd)
            return sc, o_inter, v

        def back(h, sc, o_inter, v):
            cols = slice(h * 128, (h + 1) * 128)
            o_top = o_inter[:hc]
            o_bot = o_inter[hc:]
            for s_ in range(hc):
                o_top = o_top + sc[s_ * c:s_ * c + hc, :] * v[s_:s_ + 1, :]
                o_bot = o_bot + sc[s_ * c + hc:(s_ + 1) * c, :] * v[s_:s_ + 1, :]
            for s_ in range(hc, c):
                r_ = hc * c + (s_ - hc) * hc
                o_bot = o_bot + sc[r_:r_ + hc, :] * v[s_:s_ + 1, :]
            o = jnp.concatenate([o_top, o_bot], axis=0)
            ms = jnp.mean(o * o, axis=-1, keepdims=True)
            g = g_ref[pl.ds(r0, c), cols]
            o_ref[pl.ds(r0, c), cols] = (o * lax.rsqrt(ms + EPS) * par_ref[3:4, cols] * _silu(g)).astype(o_ref.dtype)

        pending = front(0)
        for h in range(1, HG_HEADS):
            nxt = front(h)
            back(h - 1, *pending)
            pending = nxt
        back(HG_HEADS - 1, *pending)
        return carry

    n_chunks = rows // c
    lax.fori_loop(0, n_chunks, chunk, 0, unroll=3 if n_chunks % 3 == 0 else 1)


def _hgrn2_call(proj, par, batch, p):
    t = proj.shape[0]
    rows = _row_tile(p, 528)
    nb = p // rows
    w = HG_KEY
    blk = lambda j: pl.BlockSpec((rows, w), lambda b, s, j=j: (b * nb + s, C_HG // w + j))
    return pl.pallas_call(
        functools.partial(_hgrn2_kernel, rows=rows), grid=(batch, nb),
        in_specs=[blk(0), blk(1), blk(2), blk(3), pl.BlockSpec((8, w), lambda b, s: (0, 0))],
        out_specs=pl.BlockSpec((rows, w), lambda b, s: (b * nb + s, 0)),
        out_shape=jax.ShapeDtypeStruct((t, HG_WIDTH), BF16),
        scratch_shapes=[pltpu.VMEM((HG_HEADS, 128, HG_KDIM), F32)],
        compiler_params=_cparams(("arbitrary", "arbitrary")), name="hgrn2",
    )(proj, proj, proj, proj, par)


def _mlstm_kernel(v_ref, o_ref, q_ref, k_ref, sm_ref, par_ref, nw_ref, tri_ref, sel_ref, dg_ref,
                  y_ref, c_ref, m_ref):
    s = pl.program_id(0)

    @pl.when(s == 0)
    def _():
        c_ref[...] = jnp.zeros_like(c_ref)
        m_ref[...] = jnp.zeros_like(m_ref)

    n = CHUNK
    rid = lax.broadcasted_iota(jnp.int32, (n, 128), 0)
    pad = (s * n + rid) < LEAD_PAD
    sel_i = sel_ref[0]
    sel_f = sel_ref[1]
    dg = dg_ref[...]
    w = ML_HEADS * n
    tt = lax.broadcasted_iota(jnp.int32, (n, w), 0)
    ss = lax.broadcasted_iota(jnp.int32, (n, w), 1) & (n - 1)
    causal = tt >= ss
    scale = ML_QK ** -0.5
    lane = lax.broadcasted_iota(jnp.int32, (n, 128), 1)
    one_col = jnp.where(lane == 0, 1.0, 0.0).astype(BF16)
    for b in range(v_ref.shape[0]):
        pre = sm_ref[b] + par_ref[0:1, :]
        cap = GATE_CAP * jnp.tanh(pre * (1.0 / GATE_CAP))
        log_i = jnp.where(pad, NEG, cap)
        log_f = jnp.where(pad, 0.0, _log_sigmoid(cap))
        cum = _sel_dot(tri_ref[...], log_f)
        cum_col = _dot_sel(cum, sel_f)
        cum_row = jnp.sum(cum_col * dg, axis=0, keepdims=True)
        li_row = jnp.sum(_dot_sel(log_i, sel_i) * dg, axis=0, keepdims=True)
        dmat = jnp.where(causal, cum_col - cum_row + li_row, NEG)
        hs = range(ML_HEADS)
        sh = [b * ML_HEADS + h for h in hs]
        d_h = [dmat[:, h * n:(h + 1) * n] for h in hs]
        cum_h = [cum[:, LANE_MF + h:LANE_MF + h + 1] for h in hs]
        li_h = [log_i[:, LANE_MI + h:LANE_MI + h + 1] for h in hs]
        m_st = [m_ref[sh[h]:sh[h] + 1, 0:1] for h in hs]
        q = [(q_ref[b, :, h * ML_QK:(h + 1) * ML_QK] * scale).astype(BF16) for h in hs]
        k = [k_ref[b, :, h * ML_QK:(h + 1) * ML_QK] for h in hs]
        v_aug = [jnp.concatenate([v_ref[b, :, h * ML_V:(h + 1) * ML_V].astype(BF16), one_col], axis=1) for h in hs]
        c_prev = [c_ref[sh[h]] for h in hs]
        inter = [cum_h[h] + m_st[h] for h in hs]
        m_t = [jnp.maximum(inter[h], jnp.max(d_h[h], axis=-1, keepdims=True)) for h in hs]
        qk = [_dot_nt(q[h], k[h].astype(BF16)) for h in hs]
        qc = [_dot(q[h], c_prev[h].astype(BF16)) for h in hs]
        pw = [(qk[h] * jnp.exp(d_h[h] - m_t[h])).astype(BF16) for h in hs]
        nd = [_dot(pw[h], v_aug[h]) + jnp.exp(inter[h] - m_t[h]) * qc[h] for h in hs]
        hh = [nd[h][:, :ML_V] / jnp.maximum(jnp.abs(nd[h][:, ML_V:ML_V + 1]), jnp.exp(-m_t[h])) for h in hs]
        tot = [cum_h[h][n - 1:n, :] for h in hs]
        to_end = [tot[h] - cum_h[h] + li_h[h] for h in hs]
        m_loc = [jnp.max(to_end[h], axis=0, keepdims=True) for h in hs]
        kw = [(k[h] * jnp.exp(to_end[h] - m_loc[h])).astype(BF16) for h in hs]
        c_loc = [_dot_tn(kw[h], v_aug[h]) for h in hs]
        m_new = [jnp.maximum(tot[h] + m_st[h], m_loc[h]) for h in hs]
        for h in hs:
            c_ref[sh[h]] = (jnp.exp(tot[h] + m_st[h] - m_new[h]) * c_prev[h]
                            + jnp.exp(m_loc[h] - m_new[h]) * c_loc[h])
            m_ref[sh[h]:sh[h] + 1, :] = jnp.broadcast_to(m_new[h], (1, 128))
        for h in hs:
            ms = jnp.mean(hh[h] * hh[h], axis=-1, keepdims=True)
            cols = slice(h * ML_V, (h + 1) * ML_V)
            y_ref[b, :, cols] = (hh[h] * lax.rsqrt(ms + EPS) * nw_ref[:, cols]
                                 * _sigmoid(o_ref[b, :, cols])).astype(y_ref.dtype)


def _lane_select(lane0, heads, width):
    m = np.zeros((128, heads * width), np.float32)
    for h in range(heads):
        m[lane0 + h, h * width:(h + 1) * width] = 1.0
    return m


def _diag_mask(n, heads):
    return np.tile(np.eye(n, dtype=np.float32), (1, heads))


def _mlstm_call(proj, par, nw, batch, p):
    t = proj.shape[0]
    n = CHUNK
    nc = p // n
    tri = jnp.asarray(np.tril(np.ones((n, n), np.float32)), BF16)
    sel = jnp.asarray(np.stack([_lane_select(LANE_MI, ML_HEADS, n), _lane_select(LANE_MF, ML_HEADS, n)]), BF16)
    dg = jnp.asarray(_diag_mask(n, ML_HEADS), F32)
    proj3 = proj.reshape(batch, p, proj.shape[1])
    blk = lambda w, off: pl.BlockSpec((batch, n, w), lambda s: (0, s, off // w))
    const = lambda shape: pl.BlockSpec(shape, lambda s: (0,) * len(shape))
    m_rows = -(-batch * ML_HEADS // 8) * 8
    y = pl.pallas_call(
        _mlstm_kernel, grid=(nc,),
        in_specs=[blk(ML_WIDTH, C_MV), blk(ML_WIDTH, C_MV + ML_WIDTH), blk(ML_QK_W, C_MQ),
                  blk(ML_QK_W, C_MQ + ML_QK_W), blk(128, C_SMALL),
                  const((8, 128)), const((1, ML_WIDTH)), const((n, n)), const((2, 128, ML_HEADS * n)),
                  const((n, ML_HEADS * n))],
        out_specs=pl.BlockSpec((batch, n, ML_WIDTH), lambda s: (0, s, 0)),
        out_shape=jax.ShapeDtypeStruct((batch, p, ML_WIDTH), BF16),
        scratch_shapes=[pltpu.VMEM((batch * ML_HEADS, ML_QK, 2 * ML_V), F32), pltpu.VMEM((m_rows, 128), F32)],
        compiler_params=_cparams(("arbitrary",)), name="mlstm",
    )(proj3, proj3, proj3, proj3, proj3, par, nw, tri, sel, dg)
    return y.reshape(t, ML_WIDTH)


def _ssd_kernel(z_ref, x_ref, b_ref, c_ref, sm_ref, cw_ref, cb_ref, par_ref, dsk_ref, nw_ref,
                tri_ref, sel_ref, dg_ref, y_ref, xs_ref, bs_ref, cs_ref, st_ref):
    s = pl.program_id(1)
    n = CHUNK
    tail = 8

    @pl.when(s == 0)
    def _():
        st_ref[...] = jnp.zeros_like(st_ref)
        xs_ref[0:tail, :] = jnp.zeros((tail, xs_ref.shape[1]), F32)
        bs_ref[0:tail, :] = jnp.zeros((tail, bs_ref.shape[1]), F32)
        cs_ref[0:tail, :] = jnp.zeros((tail, cs_ref.shape[1]), F32)

    def conv_silu(src_ref, scr_ref, c0, width, rowmask):
        scr_ref[tail:tail + n, :] = src_ref[...]
        acc = cb_ref[:, c0:c0 + width]
        for j in range(CONV_W):
            off = tail - (CONV_W - 1) + j
            acc = acc + cw_ref[j:j + 1, c0:c0 + width] * scr_ref[off:off + n, :]
        scr_ref[0:tail, :] = scr_ref[n:n + tail, :]
        return jnp.where(rowmask, 0.0, _silu(acc))

    def padmask(width):
        return (s * n + lax.broadcasted_iota(jnp.int32, (n, width), 0)) < LEAD_PAD

    x = conv_silu(x_ref, xs_ref, 0, SSM_WIDTH, padmask(SSM_WIDTH))
    bm = conv_silu(b_ref, bs_ref, SSM_WIDTH, SSM_BC, padmask(SSM_BC))
    cm = conv_silu(c_ref, cs_ref, SSM_WIDTH + SSM_BC, SSM_BC, padmask(SSM_BC))

    pre = sm_ref[...] + par_ref[0:1, :]
    dt = jnp.maximum(pre, 0.0) + jnp.log1p(jnp.exp(-jnp.abs(pre)))
    dt = jnp.where(padmask(128), 0.0, dt)
    da = dt * (-jnp.exp(par_ref[1:2, :]))
    cum = _sel_dot(tri_ref[...], da)
    sel = sel_ref[...]
    dt_col = _dot_sel(dt, sel)
    cum_col = _dot_sel(cum, sel)
    cum_row = jnp.sum(cum_col * dg_ref[...], axis=0, keepdims=True)
    w = SSM_WIDTH
    tt = lax.broadcasted_iota(jnp.int32, (n, w), 0)
    ss = lax.broadcasted_iota(jnp.int32, (n, w), 1) & (n - 1)
    decay = jnp.exp(jnp.where(tt >= ss, cum_col - cum_row, NEG))
    xdt = x * dt_col
    last = cum_col[n - 1:n, :]
    wend = (xdt * jnp.exp(last - cum_col)).astype(BF16)
    chunk_decay = jnp.exp(last)
    ecum = jnp.exp(cum_col)
    gw = SSM_HPG * SSM_HEADDIM
    rr = lax.broadcasted_iota(jnp.int32, (gw, gw), 0) // SSM_HEADDIM
    cc = lax.broadcasted_iota(jnp.int32, (gw, gw), 1) // SSM_HEADDIM
    blockdiag = rr == cc
    ys = []
    for g in range(SSM_GROUPS):
        gl = slice(g * gw, (g + 1) * gw)
        sl = slice(g * SSM_STATE, (g + 1) * SSM_STATE)
        cm_g = cm[:, sl].astype(BF16)
        bm_g = bm[:, sl].astype(BF16)
        cb = _dot_nt(cm_g, jnp.concatenate([bm_g] * SSM_HPG, axis=0))
        m = (cb * decay[:, gl]).astype(BF16)
        xdt_g = xdt[:, gl]
        bd = jnp.where(blockdiag, jnp.concatenate([xdt_g] * SSM_HPG, axis=0), 0.0).astype(BF16)
        st = st_ref[g]
        y_g = _dot(m, bd) + ecum[:, gl] * _dot(cm_g, st.astype(BF16))
        st_ref[g] = st * chunk_decay[:, gl] + _dot_tn(bm_g, wend[:, gl])
        ys.append(y_g)
    y = jnp.concatenate(ys, axis=1) + x * dsk_ref[...]
    y = y * _silu(z_ref[...])
    outs = []
    for g in range(SSM_GROUPS):
        gl = slice(g * gw, (g + 1) * gw)
        y_g = y[:, gl]
        ms = jnp.mean(y_g * y_g, axis=-1, keepdims=True)
        outs.append(y_g * lax.rsqrt(ms + EPS))
    y_ref[...] = (jnp.concatenate(outs, axis=1) * nw_ref[...]).astype(y_ref.dtype)


def _ssd_call(proj, cw, cb, par, dsk, nw, batch, p):
    t = proj.shape[0]
    n = CHUNK
    nc = p // n
    tri = jnp.asarray(np.tril(np.ones((n, n), np.float32)), BF16)
    sel = jnp.asarray(_lane_select(LANE_DT, SSM_HEADS, SSM_HEADDIM), BF16)
    dg = jnp.asarray(_diag_mask(n, SSM_HEADS), F32)
    blk = lambda w, off: pl.BlockSpec((n, w), lambda b, s: (b * nc + s, off // w))
    const = lambda shape: pl.BlockSpec(shape, lambda b, s: (0,) * len(shape))
    cch = SSM_WIDTH + 2 * SSM_BC
    return pl.pallas_call(
        _ssd_kernel, grid=(batch, nc),
        in_specs=[blk(SSM_WIDTH, C_SZ), blk(SSM_WIDTH, C_SX), blk(SSM_BC, C_SX + SSM_WIDTH),
                  blk(SSM_BC, C_SX + SSM_WIDTH + SSM_BC), blk(128, C_SMALL),
                  const((CONV_W, cch)), const((1, cch)), const((8, 128)), const((1, SSM_WIDTH)),
                  const((1, SSM_WIDTH)), const((n, n)), const((128, SSM_WIDTH)), const((n, SSM_WIDTH))],
        out_specs=pl.BlockSpec((n, SSM_WIDTH), lambda b, s: (b * nc + s, 0)),
        out_shape=jax.ShapeDtypeStruct((t, SSM_WIDTH), BF16),
        scratch_shapes=[pltpu.VMEM((n + 8, SSM_WIDTH), F32), pltpu.VMEM((n + 8, SSM_BC), F32),
                        pltpu.VMEM((n + 8, SSM_BC), F32),
                        pltpu.VMEM((SSM_GROUPS, SSM_STATE, SSM_HPG * SSM_HEADDIM), F32)],
        compiler_params=_cparams(("arbitrary", "arbitrary")), name="ssd",
    )(proj, proj, proj, proj, proj, cw, cb, par, dsk, nw, tri, sel, dg)


LANE_E0 = N_GROUPS_MOE
RT_E, RT_RANK, RT_W = 0, 2, 4


def _first_max(vals, lane):
    m = jnp.max(vals, axis=-1, keepdims=True)
    idx = jnp.min(jnp.where(vals == m, lane, 128), axis=-1, keepdims=True)
    return m, idx


def _outproj_kernel(ya_ref, yb_ref, yc_ref, h_ref, w_ref, nw_ref, rh_ref, rl_ref, rb_ref, tri_ref,
                    hm_ref, u_ref, rt_ref, cnt_ref):
    a0, a1 = HG_WIDTH, HG_WIDTH + ML_WIDTH
    h = h_ref[...]
    h = h + _dot(ya_ref[...], w_ref[0:a0, :])
    h = h + _dot(yb_ref[...], w_ref[a0:a1, :])
    h = h + _dot(yc_ref[...], w_ref[a1:, :])
    hm_ref[...] = h
    ms = jnp.mean(h * h, axis=-1, keepdims=True)
    u = h * lax.rsqrt(ms + EPS) * nw_ref[...]
    u_ref[...] = u
    u_hi = u.astype(BF16)
    u_lo = (u - u_hi.astype(F32)).astype(BF16)
    lg = _dot(u_hi, rh_ref[...]) + (_dot(u_lo, rh_ref[...]) + _dot(u_hi, rl_ref[...])) + rb_ref[...]

    tm = lg.shape[0]
    lane = lax.broadcasted_iota(jnp.int32, (tm, 128), 1)
    g_mask = lane < N_GROUPS_MOE
    g_max, g_sel = _first_max(jnp.where(g_mask, lg, NEG), lane)
    g_gate = 1.0 / jnp.sum(jnp.where(g_mask, jnp.exp(lg - g_max), 0.0), axis=-1, keepdims=True)
    lo = LANE_E0 + g_sel * EXPERTS_PER_GROUP
    e_vals = jnp.where((lane >= lo) & (lane < lo + EXPERTS_PER_GROUP), lg, NEG)
    v1, i1 = _first_max(e_vals, lane)
    v2, i2 = _first_max(jnp.where(lane == i1, NEG, e_vals), lane)
    a = jnp.exp(v2 - v1)
    w1 = g_gate / (1.0 + a)
    w2 = w1 * a
    @pl.when(pl.program_id(0) == 0)
    def _():
        cnt_ref[...] = jnp.zeros_like(cnt_ref)

    hit1 = lane == i1
    hit2 = lane == i2
    onehot = jnp.where(hit1 | hit2, 1.0, 0.0)
    before = _dot(tri_ref[...], onehot.astype(BF16)) + cnt_ref[0:1, :]
    r1 = jnp.sum(jnp.where(hit1, before, 0.0), axis=-1, keepdims=True)
    r2 = jnp.sum(jnp.where(hit2, before, 0.0), axis=-1, keepdims=True)
    cnt_ref[...] = cnt_ref[...] + jnp.sum(onehot, axis=0, keepdims=True)
    rec = jnp.zeros((tm, 128), F32)
    for ln, val in ((RT_E, (i1 - LANE_E0).astype(F32)), (RT_E + 1, (i2 - LANE_E0).astype(F32)),
                    (RT_RANK, r1), (RT_RANK + 1, r2), (RT_W, w1), (RT_W + 1, w2)):
        rec = jnp.where(lane == ln, val, rec)
    rt_ref[...] = rec


def _outproj_call(ya, yb, yc, h, w, nw, r_hi, r_lo, r_bias, tm_target=352):
    t, d = h.shape
    tm = _row_tile(t, tm_target)
    tri = jnp.asarray(np.tril(np.ones((tm, tm), np.float32), -1), BF16)
    row = lambda i: (i, 0)
    const = lambda shape: pl.BlockSpec(shape, lambda i: (0, 0))
    return pl.pallas_call(
        _outproj_kernel, grid=(t // tm,),
        in_specs=[pl.BlockSpec((tm, HG_WIDTH), row), pl.BlockSpec((tm, ML_WIDTH), row),
                  pl.BlockSpec((tm, SSM_WIDTH), row), pl.BlockSpec((tm, d), row),
                  const((D_MIX, d)), const((1, d)), const((d, 128)), const((d, 128)), const((1, 128)),
                  const((tm, tm))],
        out_specs=[pl.BlockSpec((tm, d), row), pl.BlockSpec((tm, d), row), pl.BlockSpec((tm, 128), row),
                   const((8, 128))],
        out_shape=[jax.ShapeDtypeStruct((t, d), F32), jax.ShapeDtypeStruct((t, d), F32),
                   jax.ShapeDtypeStruct((t, 128), F32), jax.ShapeDtypeStruct((8, 128), F32)],
        compiler_params=_cparams(("arbitrary",)), name="out_proj_router",
    )(ya, yb, yc, h, w, nw.reshape(1, d), r_hi, r_lo, r_bias, tri)


def _moe_kernel(be_ref, nu_ref, src0_ref, srcn_ref, dstp_ref, dst_ref, u_hbm, w1_ref, w3_ref, w2_ref, o_hbm,
                xbuf, ybuf, w1b, w3b, w2b, gsem, ssem):
    i = pl.program_id(0)
    bm = MOE_BM
    n_used = nu_ref[0]
    slot = i % 2

    def gather(idx_ref, sl):
        def body(g, c):
            for j in range(8):
                r = g * 8 + j
                tok = idx_ref[0, r]
                pltpu.make_async_copy(u_hbm.at[pl.ds(tok, 1), :], xbuf.at[sl, pl.ds(r, 1), :],
                                      gsem.at[sl]).start(priority=j % 2)
            return c
        lax.fori_loop(0, bm // 8, body, 0)

    def scatter(sl):
        def body(g, c):
            for j in range(8):
                r = g * 8 + j
                row = dst_ref[0, r]
                pltpu.make_async_copy(ybuf.at[sl, pl.ds(r, 1), :], o_hbm.at[pl.ds(row, 1), :],
                                      ssem.at[sl]).start(priority=j % 2)
            return c
        lax.fori_loop(0, bm // 8, body, 0)

    def wait_gather(sl):
        pltpu.make_async_copy(u_hbm.at[pl.ds(0, bm), :], xbuf.at[sl], gsem.at[sl]).wait()

    def wait_scatter(sl):
        pltpu.make_async_copy(ybuf.at[sl], o_hbm.at[pl.ds(0, bm), :], ssem.at[sl]).wait()

    other = 1 - slot
    d = xbuf.shape[-1]
    n_real = o_hbm.shape[0] - 2 * bm

    @pl.when(i == 0)
    def _():
        gather(src0_ref, 0)
        ybuf[...] = jnp.zeros_like(ybuf)
        pltpu.make_async_copy(ybuf.at[0], o_hbm.at[pl.ds(n_real, bm), :], ssem.at[0]).start()

    @pl.when(i < n_used)
    def _():
        first = jnp.logical_or(i == 0, be_ref[i] != be_ref[jnp.maximum(i - 1, 0)])

        @pl.when(first)
        def _():
            w1b[...] = w1_ref[...].astype(BF16)
            w3b[...] = w3_ref[...].astype(BF16)
            w2b[...] = w2_ref[...].astype(BF16)

        wait_gather(slot)

        def issue(part, parts=4):
            for r in range(part * bm // parts, (part + 1) * bm // parts):
                pltpu.make_async_copy(u_hbm.at[pl.ds(srcn_ref[0, r], 1), :], xbuf.at[other, pl.ds(r, 1), :],
                                      gsem.at[other]).start(priority=r % 2)
                pltpu.make_async_copy(ybuf.at[other, pl.ds(r, 1), :], o_hbm.at[pl.ds(dstp_ref[0, r], 1), :],
                                      ssem.at[other]).start(priority=(r + 1) % 2)

        x = xbuf[slot].astype(BF16)
        h1 = _dot(x, w1b[...])
        issue(0)
        h3 = _dot(x, w3b[...])
        issue(1)
        hid = (_silu(h1) * h3).astype(BF16)
        wait_scatter(slot)
        ybuf[slot, :, :d // 2] = _dot(hid, w2b[:, :d // 2])
        issue(2)
        ybuf[slot, :, d // 2:] = _dot(hid, w2b[:, d // 2:])
        issue(3)

    @pl.when(i == n_used - 1)
    def _():
        scatter(slot)
        wait_scatter(slot)
        wait_scatter(other)
        wait_gather(other)


def _moe_call(block_e, n_used, src_tok, dst_row, u, w1, w3, w2, layer, n_out_rows):
    t, d = u.shape
    f = w1.shape[-1]
    bm = MOE_BM
    nb = block_e.shape[0]
    wmap = lambda i, be, nu: (layer, be[i], 0, 0)
    src_tok = src_tok.reshape(nb, 1, bm)
    lead = (n_out_rows - bm + jnp.arange(bm, dtype=jnp.int32)).reshape(1, 1, bm)
    dst_row = jnp.concatenate([lead, dst_row.reshape(nb, 1, bm)], axis=0)
    idx = lambda fn: pl.BlockSpec((None, 1, bm), fn, memory_space=pltpu.SMEM)
    return pl.pallas_call(
        _moe_kernel,
        grid_spec=pltpu.PrefetchScalarGridSpec(
            num_scalar_prefetch=2, grid=(nb,),
            in_specs=[idx(lambda i, be, nu: (0, 0, 0)),
                      idx(lambda i, be, nu: (jnp.minimum(i + 1, nb - 1), 0, 0)),
                      idx(lambda i, be, nu: (i, 0, 0)),
                      idx(lambda i, be, nu: (i + 1, 0, 0)),
                      pl.BlockSpec(memory_space=pl.ANY),
                      pl.BlockSpec((None, None, d, f), wmap), pl.BlockSpec((None, None, d, f), wmap),
                      pl.BlockSpec((None, None, f, d), wmap)],
            out_specs=pl.BlockSpec(memory_space=pl.ANY),
            scratch_shapes=[pltpu.VMEM((2, bm, d), F32), pltpu.VMEM((2, bm, d), F32),
                            pltpu.VMEM((d, f), BF16), pltpu.VMEM((d, f), BF16), pltpu.VMEM((f, d), BF16),
                            pltpu.SemaphoreType.DMA((2,)), pltpu.SemaphoreType.DMA((2,))]),
        out_shape=jax.ShapeDtypeStruct((n_out_rows, d), F32),
        compiler_params=_cparams(("arbitrary",)), name="moe_ffn",
    )(block_e, n_used, src_tok, src_tok, dst_row, dst_row, u, w1, w3, w2)


def _route_tables(rt, cnt, t):
    bm = MOE_BM
    tk = t * TOP_K
    e = rt[:, RT_E:RT_E + TOP_K].astype(jnp.int32)
    rank = rt[:, RT_RANK:RT_RANK + TOP_K].astype(jnp.int32)
    wts = rt[:, RT_W:RT_W + TOP_K]
    counts = cnt[0, LANE_E0:LANE_E0 + N_EXPERTS].astype(jnp.int32)
    padded = (counts + bm - 1) // bm * bm
    pad_ends = jnp.cumsum(padded)
    pad_starts = pad_ends - padded
    dest = (pad_starts[e] + rank).reshape(-1)
    n_blocks = -(-tk // bm) + N_EXPERTS
    n_rows = n_blocks * bm
    inv = jnp.full((n_rows,), -1, jnp.int32).at[dest].set(jnp.arange(tk, dtype=jnp.int32))
    tok, slot = inv // TOP_K, inv % TOP_K
    src_tok = jnp.where(inv >= 0, tok, 0)
    dst_row = jnp.where(inv >= 0, slot * t + tok, tk + (jnp.arange(n_rows, dtype=jnp.int32) % (2 * bm)))
    block_row0 = jnp.arange(n_blocks, dtype=jnp.int32) * bm
    block_e = jnp.minimum(jnp.sum(pad_ends[None, :] <= block_row0[:, None], axis=1), N_EXPERTS - 1).astype(jnp.int32)
    n_used = (pad_ends[-1] // bm).astype(jnp.int32).reshape(1)
    last_e = block_e[jnp.maximum(n_used[0] - 1, 0)]
    block_e = jnp.where(jnp.arange(n_blocks) < n_used[0], block_e, last_e)
    return block_e, n_used, src_tok, dst_row, wts, tk + 2 * bm


def _lane_row(pairs):
    row = jnp.zeros((128,), F32)
    for lane0, vals in pairs:
        row = row.at[lane0:lane0 + vals.shape[0]].set(vals.astype(F32))
    return row


def kernel(x, meta_tokens, hg_lb_logits, norm_mix_w, w_in, hg_norm_w, ml_b_i, ml_b_f, ml_norm_w,
           ssm_conv_w, ssm_conv_b, ssm_dt_bias, ssm_a_log, ssm_d, ssm_norm_w, w_out, norm_ffn_w,
           moe_w_group, moe_b_group, moe_w_router, moe_b_router, moe_w1, moe_w3, moe_w2, final_norm_w):
    batch, seq, d = x.shape
    depth = w_in.shape[0]
    p = LEAD_PAD + N_META + seq
    t = batch * p
    meta = jnp.broadcast_to(meta_tokens.astype(x.dtype)[None], (batch, N_META, d))
    h = jnp.concatenate([jnp.zeros((batch, LEAD_PAD, d), x.dtype), meta, x], axis=1).reshape(t, d)

    lb_w = jax.nn.softmax(hg_lb_logits.astype(F32), axis=0)
    lower_bounds = jnp.cumsum(lb_w, axis=0) - lb_w[0]

    contrib = wts = None
    for layer in range(depth):
        w = w_in[layer]
        o_mq, o_mv, o_mi, o_sz = HG_KEY * 2 + HG_WIDTH * 2, 2560, 3584, 3592
        o_sx = o_sz + SSM_WIDTH
        o_dt = o_sx + SSM_WIDTH + 2 * SSM_BC
        w_perm = jnp.concatenate([
            w[:, :o_mq], w[:, o_sz:o_sx], w[:, o_sx:o_dt], w[:, o_mv:o_mi], w[:, o_mq:o_mv],
            w[:, o_mi:o_sz], w[:, o_dt:], jnp.zeros((d, N_PROJ - C_SMALL - 2 * ML_HEADS - SSM_HEADS), w.dtype)],
            axis=1).astype(BF16)
        lb = lower_bounds[layer]
        lbf = jnp.maximum(lb, LB_FLOOR)
        hg_par = jnp.zeros((8, HG_KEY), F32).at[0].set(lbf).at[1].set(1.0 - lb).at[2].set(lbf - lb).at[3].set(hg_norm_w[layer])
        ml_par = jnp.zeros((8, 128), F32).at[0].set(_lane_row([(LANE_MI, ml_b_i[layer]), (LANE_MF, ml_b_f[layer])]))
        ss_par = jnp.zeros((8, 128), F32).at[0].set(_lane_row([(LANE_DT, ssm_dt_bias[layer])]))
        ss_par = ss_par.at[1].set(_lane_row([(LANE_DT, ssm_a_log[layer])]))
        dskip = jnp.repeat(ssm_d[layer].astype(F32), SSM_HEADDIM).reshape(1, SSM_WIDTH)
        w_r = jnp.concatenate([moe_w_group[layer],
                               moe_w_router[layer].transpose(1, 0, 2).reshape(d, N_EXPERTS),
                               jnp.zeros((d, 128 - N_GROUPS_MOE - N_EXPERTS), F32)], axis=1)
        r_hi = w_r.astype(BF16)
        r_lo = (w_r - r_hi.astype(F32)).astype(BF16)

        if layer == 0:
            (u,) = _norm_call(h, norm_mix_w[layer], write_h=False, u_dtype=BF16)
        else:
            h, u = _norm_call(h, norm_mix_w[layer], contrib, wts, write_h=True, u_dtype=BF16)
        proj = _inproj_call(u, w_perm)
        ya = _hgrn2_call(proj, hg_par, batch, p)
        yb = _mlstm_call(proj, ml_par, ml_norm_w[layer].reshape(1, ML_WIDTH), batch, p)
        yc = _ssd_call(proj, ssm_conv_w[layer], ssm_conv_b[layer].reshape(1, -1), ss_par, dskip,
                       ssm_norm_w[layer].reshape(1, SSM_WIDTH), batch, p)
        r_bias = _lane_row([(0, moe_b_group[layer]), (LANE_E0, moe_b_router[layer].reshape(-1))]).reshape(1, 128)
        h, u_ffn, rt, cnt = _outproj_call(ya, yb, yc, h, w_out[layer].astype(BF16), norm_ffn_w[layer],
                                          r_hi, r_lo, r_bias)
        block_e, n_used, src_tok, dst_row, wts, n_out_rows = _route_tables(rt, cnt, t)
        contrib = _moe_call(block_e, n_used, src_tok, dst_row, u_ffn, moe_w1, moe_w3, moe_w2, layer, n_out_rows)
    return _final_call(h, contrib, wts, final_norm_w, batch, p)
```

```python
import functools

import jax
import jax.numpy as jnp
import numpy as np
from jax import lax
from jax.experimental import pallas as pl
from jax.experimental.pallas import tpu as pltpu

F32 = jnp.float32
BF16 = jnp.bfloat16

D_MODEL = 2048
N_META = 16
CHUNK = 64
HG_CHUNK = 16
LEAD_PAD = CHUNK - N_META
EPS = 1e-6
NEG = -1e30
LB_FLOOR = 1e-30

HG_HEADS = 4
HG_KDIM = 128
HG_KEY = HG_HEADS * HG_KDIM
HG_WIDTH = HG_HEADS * 128

ML_HEADS = 4
ML_QK = 64
ML_V = 128
ML_QK_W = ML_HEADS * ML_QK
ML_WIDTH = ML_HEADS * ML_V
GATE_CAP = 15.0

SSM_HEADS = 16
SSM_HEADDIM = 64
SSM_WIDTH = SSM_HEADS * SSM_HEADDIM
SSM_STATE = 128
SSM_GROUPS = 4
SSM_HPG = SSM_HEADS // SSM_GROUPS
SSM_BC = SSM_GROUPS * SSM_STATE
CONV_W = 4

D_MIX = HG_WIDTH + ML_WIDTH + SSM_WIDTH

N_GROUPS_MOE = 4
EXPERTS_PER_GROUP = 8
N_EXPERTS = N_GROUPS_MOE * EXPERTS_PER_GROUP
TOP_K = 2
D_EXPERT = 512
MOE_BM = 128

C_HG = 0
C_SZ = 2048
C_SX = 3072
C_MV = 5120
C_MQ = 6144
C_SMALL = 6656
N_PROJ = 6912
LANE_MI = 0
LANE_MF = ML_HEADS
LANE_DT = 2 * ML_HEADS

VMEM_LIMIT = 56 * 1024 * 1024


def _cparams(sem):
    return pltpu.CompilerParams(dimension_semantics=sem, vmem_limit_bytes=VMEM_LIMIT)


def _row_tile(n, target, mult=16):
    best = None
    for t in range(mult, min(n, target) + 1, mult):
        if n % t == 0:
            best = t
    assert best is not None, (n, target, mult)
    return best


def _split3(x):
    hi = x.astype(BF16)
    r = x - hi.astype(F32)
    mid = r.astype(BF16)
    lo = (r - mid.astype(F32)).astype(BF16)
    return hi, mid, lo


def _dot(a, b):
    return jnp.dot(a, b, preferred_element_type=F32)


def _sel_dot(sel, x):
    hi, mid, lo = _split3(x)
    return _dot(sel, hi) + _dot(sel, mid) + _dot(sel, lo)


def _dot_sel(x, sel):
    hi, mid, lo = _split3(x)
    return _dot(hi, sel) + _dot(mid, sel) + _dot(lo, sel)


def _dot_nt(a, b):
    return lax.dot_general(a, b, (((1,), (1,)), ((), ())), preferred_element_type=F32)


def _dot_tn(a, b):
    return lax.dot_general(a, b, (((0,), (0,)), ((), ())), preferred_element_type=F32)


def _log_sigmoid(x):
    return jnp.minimum(x, 0.0) - jnp.log1p(jnp.exp(-jnp.abs(x)))


def _sigmoid(x):
    return 1.0 / (1.0 + jnp.exp(-x))


def _silu(x):
    return x * _sigmoid(x)


SLAB = 8
U32 = jnp.uint32
HI_MASK = 0xFFFF0000


def _bf16_bits(x):
    return lax.bitcast_convert_type(x.astype(BF16).astype(F32), U32)


def _pack_slab(ref, val, lead=(), j0=0):
    rows = val.shape[0]
    for jj in range(val.shape[1] // 256):
        lo = _bf16_bits(val[:, jj * 256:jj * 256 + 128])
        hi = _bf16_bits(val[:, jj * 256 + 128:(jj + 1) * 256])
        ref[(*lead, pl.ds(j0 + jj, rows, stride=SLAB), slice(None))] = (lo >> 16) | (hi & U32(HI_MASK))


def _unpack_slab(ref, j, rows, lead=()):
    w = ref[(*lead, pl.ds(j, rows, stride=SLAB), slice(None))]
    return (lax.bitcast_convert_type(w << 16, F32), lax.bitcast_convert_type(w & U32(HI_MASK), F32))


def _combine_rows(h_ref, c0_ref, c1_ref, wt_ref):
    rows = h_ref.shape[0]
    wt = wt_ref[...]
    w0, w1 = wt[:, 0:1], wt[:, 1:2]
    pieces = []
    for j in range(SLAB):
        a0, b0 = _unpack_slab(c0_ref, j, rows)
        a1, b1 = _unpack_slab(c1_ref, j, rows)
        pieces += [w0 * a0 + w1 * a1, w0 * b0 + w1 * b1]
    return h_ref[...] + jnp.concatenate(pieces, axis=1)


def _norm_kernel(*refs, combine, write_h):
    if combine:
        h_ref, c0_ref, c1_ref, wt_ref, nw_ref = refs[:5]
        outs = refs[5:]
        h = _combine_rows(h_ref, c0_ref, c1_ref, wt_ref)
    else:
        h_ref, nw_ref = refs[:2]
        outs = refs[2:]
        h = h_ref[...]
    if write_h:
        outs[0][...] = h
    ms = jnp.mean(h * h, axis=-1, keepdims=True)
    u_ref = outs[-1]
    u_ref[...] = (h * lax.rsqrt(ms + EPS) * nw_ref[...]).astype(u_ref.dtype)


def _norm_call(h, nw, contrib=None, wts=None, *, write_h, u_dtype, tm_target=264):
    t, d = h.shape
    tm = _row_tile(t, tm_target)
    combine = contrib is not None
    row = lambda i: (i, 0)
    in_specs = [pl.BlockSpec((tm, d), row)]
    args = [h]
    if combine:
        in_specs += [pl.BlockSpec((tm * SLAB, 128), row), pl.BlockSpec((tm * SLAB, 128), lambda i: (t // tm + i, 0)),
                     pl.BlockSpec((tm, 2), row)]
        args += [contrib, contrib, wts]
    in_specs.append(pl.BlockSpec((1, d), lambda i: (0, 0)))
    args.append(nw.reshape(1, d))
    out_shape, out_specs = [], []
    if write_h:
        out_shape.append(jax.ShapeDtypeStruct((t, d), F32))
        out_specs.append(pl.BlockSpec((tm, d), row))
    out_shape.append(jax.ShapeDtypeStruct((t, d), u_dtype))
    out_specs.append(pl.BlockSpec((tm, d), row))
    return pl.pallas_call(
        functools.partial(_norm_kernel, combine=combine, write_h=write_h),
        grid=(t // tm,), in_specs=in_specs, out_specs=out_specs, out_shape=out_shape,
        compiler_params=_cparams(("arbitrary",)), name="combine_norm",
    )(*args)


def _final_kernel(h_ref, c0_ref, c1_ref, wt_ref, nw_ref, o_ref):
    h = _combine_rows(h_ref, c0_ref, c1_ref, wt_ref)
    ms = jnp.mean(h * h, axis=-1, keepdims=True)
    o_ref[...] = h * lax.rsqrt(ms + EPS) * nw_ref[...]


def _final_call(h, contrib, wts, nw, batch, p):
    t, d = h.shape
    seq = p - CHUNK
    tm = _row_tile(seq, 256)
    n_out = seq // tm
    row0 = lambda b, i: b * p + CHUNK + i * tm
    src = lambda b, i: (pl.multiple_of(row0(b, i), CHUNK), 0)
    slab0 = lambda b, i: (pl.multiple_of(row0(b, i) * SLAB, CHUNK), 0)
    slab1 = lambda b, i: (pl.multiple_of((t + row0(b, i)) * SLAB, CHUNK), 0)
    win = lambda r, w, fn: pl.BlockSpec((pl.Element(r), pl.Element(w)), fn)
    return pl.pallas_call(
        _final_kernel, grid=(batch, n_out),
        in_specs=[win(tm, d, src), win(tm * SLAB, 128, slab0), win(tm * SLAB, 128, slab1), win(tm, 2, src),
                  pl.BlockSpec((1, d), lambda b, i: (0, 0))],
        out_specs=pl.BlockSpec((None, tm, d), lambda b, i: (b, i, 0)),
        out_shape=jax.ShapeDtypeStruct((batch, seq, d), F32),
        compiler_params=_cparams(("arbitrary", "arbitrary")), name="final_norm",
    )(h, contrib, contrib, wts, nw.reshape(1, d))


def _matmul_kernel(x_ref, w_ref, o_ref):
    o_ref[...] = _dot(x_ref[...], w_ref[...])


def _inproj_call(u, w, tm_target=1056, tn=768):
    t, d = u.shape
    n = w.shape[1]
    tm = _row_tile(t, tm_target)
    assert n % tn == 0
    return pl.pallas_call(
        _matmul_kernel, grid=(t // tm, n // tn),
        in_specs=[pl.BlockSpec((tm, d), lambda i, j: (i, 0)), pl.BlockSpec((d, tn), lambda i, j: (0, j))],
        out_specs=pl.BlockSpec((tm, tn), lambda i, j: (i, j)),
        out_shape=jax.ShapeDtypeStruct((t, n), F32),
        compiler_params=_cparams(("arbitrary", "arbitrary")), name="in_proj",
    )(u, w)


def _hgrn2_kernel(q_ref, f_ref, i_ref, g_ref, par_ref, o_ref, st_ref, *, rows):
    s = pl.program_id(1)

    @pl.when(s == 0)
    def _():
        st_ref[...] = jnp.zeros_like(st_ref)

    c = HG_CHUNK
    ones = jnp.ones((HG_KDIM, 128), BF16)
    rid = lax.broadcasted_iota(jnp.int32, (c, 128), 0)
    scale = HG_KDIM ** -0.5

    hc = c // 2

    def chunk(ci, carry):
        r0 = pl.multiple_of(ci * c, c)
        pad = (s * rows + r0 + rid) < LEAD_PAD

        def front(h):
            cols = slice(h * 128, (h + 1) * 128)
            a_lb = par_ref[0:1, cols]
            b_lb = par_ref[1:2, cols]
            c_lb = par_ref[2:3, cols]
            z = f_ref[pl.ds(r0, c), cols]
            sg = _sigmoid(z)
            f = a_lb + b_lb * sg
            log_f = jnp.where(pad, 0.0, jnp.log(f))
            k = jnp.where(pad, 0.0, b_lb * (1.0 - sg) - c_lb)
            q = q_ref[pl.ds(r0, c), cols] * scale
            v = i_ref[pl.ds(r0, c), cols]
            cum = log_f
            for sh in (1, 2, 4, 8):
                cum = cum + jnp.where(rid >= sh, pltpu.roll(cum, sh, axis=0), 0.0)
            parts = []
            for s_ in range(c):
                lo = 0 if s_ < hc else hc
                rel = jnp.where(rid[lo:] >= s_, cum[lo:] - cum[s_:s_ + 1, :], NEG)
                parts.append(q[lo:] * (k[s_:s_ + 1, :] * jnp.exp(rel)))
            sc = _dot(jnp.concatenate(parts, axis=0).astype(BF16), ones)
            st = st_ref[h]
            o_inter = _dot_nt((q * jnp.exp(cum)).astype(BF16), st.astype(BF16))
            last = cum[c - 1:c, :]
            kd = (k * jnp.exp(last - cum)).astype(BF16)
            st_ref[h] = st * jnp.exp(last) + _dot_tn(v.astype(BF16), kd)
            return sc, o_inter, v

        def back(h, sc, o_inter, v):
            cols = slice(h * 128, (h + 1) * 128)
            o_top = o_inter[:hc]
            o_bot = o_inter[hc:]
            for s_ in range(hc):
                o_top = o_top + sc[s_ * c:s_ * c + hc, :] * v[s_:s_ + 1, :]
                o_bot = o_bot + sc[s_ * c + hc:(s_ + 1) * c, :] * v[s_:s_ + 1, :]
            for s_ in range(hc, c):
                r_ = hc * c + (s_ - hc) * hc
                o_bot = o_bot + sc[r_:r_ + hc, :] * v[s_:s_ + 1, :]
            o = jnp.concatenate([o_top, o_bot], axis=0)
            ms = jnp.mean(o * o, axis=-1, keepdims=True)
            g = g_ref[pl.ds(r0, c), cols]
            o_ref[pl.ds(r0, c), cols] = (o * lax.rsqrt(ms + EPS) * par_ref[3:4, cols] * _silu(g)).astype(o_ref.dtype)

        pending = front(0)
        for h in range(1, HG_HEADS):
            nxt = front(h)
            back(h - 1, *pending)
            pending = nxt
        back(HG_HEADS - 1, *pending)
        return carry

    n_chunks = rows // c
    lax.fori_loop(0, n_chunks, chunk, 0, unroll=3 if n_chunks % 3 == 0 else 1)


def _hgrn2_call(proj, par, batch, p):
    t = proj.shape[0]
    rows = _row_tile(p, 528)
    nb = p // rows
    w = HG_KEY
    blk = lambda j: pl.BlockSpec((rows, w), lambda b, s, j=j: (b * nb + s, C_HG // w + j))
    return pl.pallas_call(
        functools.partial(_hgrn2_kernel, rows=rows), grid=(batch, nb),
        in_specs=[blk(0), blk(1), blk(2), blk(3), pl.BlockSpec((8, w), lambda b, s: (0, 0))],
        out_specs=pl.BlockSpec((rows, w), lambda b, s: (b * nb + s, 0)),
        out_shape=jax.ShapeDtypeStruct((t, HG_WIDTH), BF16),
        scratch_shapes=[pltpu.VMEM((HG_HEADS, 128, HG_KDIM), F32)],
        compiler_params=_cparams(("arbitrary", "arbitrary")), name="hgrn2",
    )(proj, proj, proj, proj, par)


def _mlstm_kernel(v_ref, o_ref, q_ref, k_ref, sm_ref, par_ref, nw_ref, tri_ref, sel_ref, dg_ref,
                  y_ref, c_ref, m_ref):
    s = pl.program_id(0)

    @pl.when(s == 0)
    def _():
        c_ref[...] = jnp.zeros_like(c_ref)
        m_ref[...] = jnp.zeros_like(m_ref)

    n = CHUNK
    rid = lax.broadcasted_iota(jnp.int32, (n, 128), 0)
    pad = (s * n + rid) < LEAD_PAD
    sel_i = sel_ref[0]
    sel_f = sel_ref[1]
    dg = dg_ref[...]
    w = ML_HEADS * n
    tt = lax.broadcasted_iota(jnp.int32, (n, w), 0)
    ss = lax.broadcasted_iota(jnp.int32, (n, w), 1) & (n - 1)
    causal = tt >= ss
    scale = ML_QK ** -0.5
    lane = lax.broadcasted_iota(jnp.int32, (n, 128), 1)
    one_col = jnp.where(lane == 0, 1.0, 0.0).astype(BF16)
    for b in range(v_ref.shape[0]):
        pre = sm_ref[b] + par_ref[0:1, :]
        cap = GATE_CAP * jnp.tanh(pre * (1.0 / GATE_CAP))
        log_i = jnp.where(pad, NEG, cap)
        log_f = jnp.where(pad, 0.0, _log_sigmoid(cap))
        cum = _sel_dot(tri_ref[...], log_f)
        cum_col = _dot_sel(cum, sel_f)
        cum_row = jnp.sum(cum_col * dg, axis=0, keepdims=True)
        li_row = jnp.sum(_dot_sel(log_i, sel_i) * dg, axis=0, keepdims=True)
        dmat = jnp.where(causal, cum_col - cum_row + li_row, NEG)
        hs = range(ML_HEADS)
        sh = [b * ML_HEADS + h for h in hs]
        d_h = [dmat[:, h * n:(h + 1) * n] for h in hs]
        cum_h = [cum[:, LANE_MF + h:LANE_MF + h + 1] for h in hs]
        li_h = [log_i[:, LANE_MI + h:LANE_MI + h + 1] for h in hs]
        m_st = [m_ref[sh[h]:sh[h] + 1, 0:1] for h in hs]
        q = [(q_ref[b, :, h * ML_QK:(h + 1) * ML_QK] * scale).astype(BF16) for h in hs]
        k = [k_ref[b, :, h * ML_QK:(h + 1) * ML_QK] for h in hs]
        v_aug = [jnp.concatenate([v_ref[b, :, h * ML_V:(h + 1) * ML_V].astype(BF16), one_col], axis=1) for h in hs]
        c_prev = [c_ref[sh[h]] for h in hs]
        inter = [cum_h[h] + m_st[h] for h in hs]
        m_t = [jnp.maximum(inter[h], jnp.max(d_h[h], axis=-1, keepdims=True)) for h in hs]
        qk = [_dot_nt(q[h], k[h].astype(BF16)) for h in hs]
        qc = [_dot(q[h], c_prev[h].astype(BF16)) for h in hs]
        pw = [(qk[h] * jnp.exp(d_h[h] - m_t[h])).astype(BF16) for h in hs]
        nd = [_dot(pw[h], v_aug[h]) + jnp.exp(inter[h] - m_t[h]) * qc[h] for h in hs]
        hh = [nd[h][:, :ML_V] / jnp.maximum(jnp.abs(nd[h][:, ML_V:ML_V + 1]), jnp.exp(-m_t[h])) for h in hs]
        tot = [cum_h[h][n - 1:n, :] for h in hs]
        to_end = [tot[h] - cum_h[h] + li_h[h] for h in hs]
        m_loc = [jnp.max(to_end[h], axis=0, keepdims=True) for h in hs]
        kw = [(k[h] * jnp.exp(to_end[h] - m_loc[h])).astype(BF16) for h in hs]
        c_loc = [_dot_tn(kw[h], v_aug[h]) for h in hs]
        m_new = [jnp.maximum(tot[h] + m_st[h], m_loc[h]) for h in hs]
        for h in hs:
            c_ref[sh[h]] = (jnp.exp(tot[h] + m_st[h] - m_new[h]) * c_prev[h]
                            + jnp.exp(m_loc[h] - m_new[h]) * c_loc[h])
            m_ref[sh[h]:sh[h] + 1, :] = jnp.broadcast_to(m_new[h], (1, 128))
        for h in hs:
            ms = jnp.mean(hh[h] * hh[h], axis=-1, keepdims=True)
            cols = slice(h * ML_V, (h + 1) * ML_V)
            y_ref[b, :, cols] = (hh[h] * lax.rsqrt(ms + EPS) * nw_ref[:, cols]
                                 * _sigmoid(o_ref[b, :, cols])).astype(y_ref.dtype)


def _lane_select(lane0, heads, width):
    m = np.zeros((128, heads * width), np.float32)
    for h in range(heads):
        m[lane0 + h, h * width:(h + 1) * width] = 1.0
    return m


def _diag_mask(n, heads):
    return np.tile(np.eye(n, dtype=np.float32), (1, heads))


def _mlstm_call(proj, par, nw, batch, p):
    t = proj.shape[0]
    n = CHUNK
    nc = p // n
    tri = jnp.asarray(np.tril(np.ones((n, n), np.float32)), BF16)
    sel = jnp.asarray(np.stack([_lane_select(LANE_MI, ML_HEADS, n), _lane_select(LANE_MF, ML_HEADS, n)]), BF16)
    dg = jnp.asarray(_diag_mask(n, ML_HEADS), F32)
    proj3 = proj.reshape(batch, p, proj.shape[1])
    blk = lambda w, off: pl.BlockSpec((batch, n, w), lambda s: (0, s, off // w))
    const = lambda shape: pl.BlockSpec(shape, lambda s: (0,) * len(shape))
    m_rows = -(-batch * ML_HEADS // 8) * 8
    y = pl.pallas_call(
        _mlstm_kernel, grid=(nc,),
        in_specs=[blk(ML_WIDTH, C_MV), blk(ML_WIDTH, C_MV + ML_WIDTH), blk(ML_QK_W, C_MQ),
                  blk(ML_QK_W, C_MQ + ML_QK_W), blk(128, C_SMALL),
                  const((8, 128)), const((1, ML_WIDTH)), const((n, n)), const((2, 128, ML_HEADS * n)),
                  const((n, ML_HEADS * n))],
        out_specs=pl.BlockSpec((batch, n, ML_WIDTH), lambda s: (0, s, 0)),
        out_shape=jax.ShapeDtypeStruct((batch, p, ML_WIDTH), BF16),
        scratch_shapes=[pltpu.VMEM((batch * ML_HEADS, ML_QK, 2 * ML_V), F32), pltpu.VMEM((m_rows, 128), F32)],
        compiler_params=_cparams(("arbitrary",)), name="mlstm",
    )(proj3, proj3, proj3, proj3, proj3, par, nw, tri, sel, dg)
    return y.reshape(t, ML_WIDTH)


def _ssd_kernel(z_ref, x_ref, b_ref, c_ref, sm_ref, cw_ref, cb_ref, par_ref, dsk_ref, nw_ref,
                tri_ref, sel_ref, dg_ref, y_ref, xs_ref, bs_ref, cs_ref, st_ref):
    s = pl.program_id(1)
    n = CHUNK
    tail = 8

    @pl.when(s == 0)
    def _():
        st_ref[...] = jnp.zeros_like(st_ref)
        xs_ref[0:tail, :] = jnp.zeros((tail, xs_ref.shape[1]), F32)
        bs_ref[0:tail, :] = jnp.zeros((tail, bs_ref.shape[1]), F32)
        cs_ref[0:tail, :] = jnp.zeros((tail, cs_ref.shape[1]), F32)

    def conv_silu(src_ref, scr_ref, c0, width, rowmask):
        scr_ref[tail:tail + n, :] = src_ref[...]
        acc = cb_ref[:, c0:c0 + width]
        for j in range(CONV_W):
            off = tail - (CONV_W - 1) + j
            acc = acc + cw_ref[j:j + 1, c0:c0 + width] * scr_ref[off:off + n, :]
        scr_ref[0:tail, :] = scr_ref[n:n + tail, :]
        return jnp.where(rowmask, 0.0, _silu(acc))

    def padmask(width):
        return (s * n + lax.broadcasted_iota(jnp.int32, (n, width), 0)) < LEAD_PAD

    x = conv_silu(x_ref, xs_ref, 0, SSM_WIDTH, padmask(SSM_WIDTH))
    bm = conv_silu(b_ref, bs_ref, SSM_WIDTH, SSM_BC, padmask(SSM_BC))
    cm = conv_silu(c_ref, cs_ref, SSM_WIDTH + SSM_BC, SSM_BC, padmask(SSM_BC))

    pre = sm_ref[...] + par_ref[0:1, :]
    dt = jnp.maximum(pre, 0.0) + jnp.log1p(jnp.exp(-jnp.abs(pre)))
    dt = jnp.where(padmask(128), 0.0, dt)
    da = dt * (-jnp.exp(par_ref[1:2, :]))
    cum = _sel_dot(tri_ref[...], da)
    sel = sel_ref[...]
    dt_col = _dot_sel(dt, sel)
    cum_col = _dot_sel(cum, sel)
    cum_row = jnp.sum(cum_col * dg_ref[...], axis=0, keepdims=True)
    w = SSM_WIDTH
    tt = lax.broadcasted_iota(jnp.int32, (n, w), 0)
    ss = lax.broadcasted_iota(jnp.int32, (n, w), 1) & (n - 1)
    decay = jnp.exp(jnp.where(tt >= ss, cum_col - cum_row, NEG))
    xdt = x * dt_col
    last = cum_col[n - 1:n, :]
    wend = (xdt * jnp.exp(last - cum_col)).astype(BF16)
    chunk_decay = jnp.exp(last)
    ecum = jnp.exp(cum_col)
    gw = SSM_HPG * SSM_HEADDIM
    rr = lax.broadcasted_iota(jnp.int32, (gw, gw), 0) // SSM_HEADDIM
    cc = lax.broadcasted_iota(jnp.int32, (gw, gw), 1) // SSM_HEADDIM
    blockdiag = rr == cc
    ys = []
    for g in range(SSM_GROUPS):
        gl = slice(g * gw, (g + 1) * gw)
        sl = slice(g * SSM_STATE, (g + 1) * SSM_STATE)
        cm_g = cm[:, sl].astype(BF16)
        bm_g = bm[:, sl].astype(BF16)
        cb = _dot_nt(cm_g, jnp.concatenate([bm_g] * SSM_HPG, axis=0))
        m = (cb * decay[:, gl]).astype(BF16)
        xdt_g = xdt[:, gl]
        bd = jnp.where(blockdiag, jnp.concatenate([xdt_g] * SSM_HPG, axis=0), 0.0).astype(BF16)
        st = st_ref[g]
        y_g = _dot(m, bd) + ecum[:, gl] * _dot(cm_g, st.astype(BF16))
        st_ref[g] = st * chunk_decay[:, gl] + _dot_tn(bm_g, wend[:, gl])
        ys.append(y_g)
    y = jnp.concatenate(ys, axis=1) + x * dsk_ref[...]
    y = y * _silu(z_ref[...])
    outs = []
    for g in range(SSM_GROUPS):
        gl = slice(g * gw, (g + 1) * gw)
        y_g = y[:, gl]
        ms = jnp.mean(y_g * y_g, axis=-1, keepdims=True)
        outs.append(y_g * lax.rsqrt(ms + EPS))
    y_ref[...] = (jnp.concatenate(outs, axis=1) * nw_ref[...]).astype(y_ref.dtype)


def _ssd_call(proj, cw, cb, par, dsk, nw, batch, p):
    t = proj.shape[0]
    n = CHUNK
    nc = p // n
    tri = jnp.asarray(np.tril(np.ones((n, n), np.float32)), BF16)
    sel = jnp.asarray(_lane_select(LANE_DT, SSM_HEADS, SSM_HEADDIM), BF16)
    dg = jnp.asarray(_diag_mask(n, SSM_HEADS), F32)
    blk = lambda w, off: pl.BlockSpec((n, w), lambda b, s: (b * nc + s, off // w))
    const = lambda shape: pl.BlockSpec(shape, lambda b, s: (0,) * len(shape))
    cch = SSM_WIDTH + 2 * SSM_BC
    return pl.pallas_call(
        _ssd_kernel, grid=(batch, nc),
        in_specs=[blk(SSM_WIDTH, C_SZ), blk(SSM_WIDTH, C_SX), blk(SSM_BC, C_SX + SSM_WIDTH),
                  blk(SSM_BC, C_SX + SSM_WIDTH + SSM_BC), blk(128, C_SMALL),
                  const((CONV_W, cch)), const((1, cch)), const((8, 128)), const((1, SSM_WIDTH)),
                  const((1, SSM_WIDTH)), const((n, n)), const((128, SSM_WIDTH)), const((n, SSM_WIDTH))],
        out_specs=pl.BlockSpec((n, SSM_WIDTH), lambda b, s: (b * nc + s, 0)),
        out_shape=jax.ShapeDtypeStruct((t, SSM_WIDTH), BF16),
        scratch_shapes=[pltpu.VMEM((n + 8, SSM_WIDTH), F32), pltpu.VMEM((n + 8, SSM_BC), F32),
                        pltpu.VMEM((n + 8, SSM_BC), F32),
                        pltpu.VMEM((SSM_GROUPS, SSM_STATE, SSM_HPG * SSM_HEADDIM), F32)],
        compiler_params=_cparams(("arbitrary", "arbitrary")), name="ssd",
    )(proj, proj, proj, proj, proj, cw, cb, par, dsk, nw, tri, sel, dg)


LANE_E0 = N_GROUPS_MOE
RT_E, RT_RANK, RT_W = 0, 2, 4


def _first_max(vals, lane):
    m = jnp.max(vals, axis=-1, keepdims=True)
    idx = jnp.min(jnp.where(vals == m, lane, 128), axis=-1, keepdims=True)
    return m, idx


def _outproj_kernel(ya_ref, yb_ref, yc_ref, h_ref, w_ref, nw_ref, rh_ref, rl_ref, rb_ref, tri_ref,
                    hm_ref, u_ref, rt_ref, cnt_ref):
    a0, a1 = HG_WIDTH, HG_WIDTH + ML_WIDTH
    h = h_ref[...]
    h = h + _dot(ya_ref[...], w_ref[0:a0, :])
    h = h + _dot(yb_ref[...], w_ref[a0:a1, :])
    h = h + _dot(yc_ref[...], w_ref[a1:, :])
    hm_ref[...] = h
    ms = jnp.mean(h * h, axis=-1, keepdims=True)
    u = h * lax.rsqrt(ms + EPS) * nw_ref[...]
    _pack_slab(u_ref, u)
    u_hi = u.astype(BF16)
    u_lo = (u - u_hi.astype(F32)).astype(BF16)
    lg = _dot(u_hi, rh_ref[...]) + (_dot(u_lo, rh_ref[...]) + _dot(u_hi, rl_ref[...])) + rb_ref[...]

    tm = lg.shape[0]
    lane = lax.broadcasted_iota(jnp.int32, (tm, 128), 1)
    g_mask = lane < N_GROUPS_MOE
    g_max, g_sel = _first_max(jnp.where(g_mask, lg, NEG), lane)
    g_gate = 1.0 / jnp.sum(jnp.where(g_mask, jnp.exp(lg - g_max), 0.0), axis=-1, keepdims=True)
    lo = LANE_E0 + g_sel * EXPERTS_PER_GROUP
    e_vals = jnp.where((lane >= lo) & (lane < lo + EXPERTS_PER_GROUP), lg, NEG)
    v1, i1 = _first_max(e_vals, lane)
    v2, i2 = _first_max(jnp.where(lane == i1, NEG, e_vals), lane)
    a = jnp.exp(v2 - v1)
    w1 = g_gate / (1.0 + a)
    w2 = w1 * a
    @pl.when(pl.program_id(0) == 0)
    def _():
        cnt_ref[...] = jnp.zeros_like(cnt_ref)

    hit1 = lane == i1
    hit2 = lane == i2
    onehot = jnp.where(hit1 | hit2, 1.0, 0.0)
    before = _dot(tri_ref[...], onehot.astype(BF16)) + cnt_ref[0:1, :]
    r1 = jnp.sum(jnp.where(hit1, before, 0.0), axis=-1, keepdims=True)
    r2 = jnp.sum(jnp.where(hit2, before, 0.0), axis=-1, keepdims=True)
    cnt_ref[...] = cnt_ref[...] + jnp.sum(onehot, axis=0, keepdims=True)
    rec = jnp.zeros((tm, 128), F32)
    for ln, val in ((RT_E, (i1 - LANE_E0).astype(F32)), (RT_E + 1, (i2 - LANE_E0).astype(F32)),
                    (RT_RANK, r1), (RT_RANK + 1, r2), (RT_W, w1), (RT_W + 1, w2)):
        rec = jnp.where(lane == ln, val, rec)
    rt_ref[...] = rec


def _outproj_call(ya, yb, yc, h, w, nw, r_hi, r_lo, r_bias, tm_target=352):
    t, d = h.shape
    tm = _row_tile(t, tm_target)
    tri = jnp.asarray(np.tril(np.ones((tm, tm), np.float32), -1), BF16)
    row = lambda i: (i, 0)
    const = lambda shape: pl.BlockSpec(shape, lambda i: (0, 0))
    return pl.pallas_call(
        _outproj_kernel, grid=(t // tm,),
        in_specs=[pl.BlockSpec((tm, HG_WIDTH), row), pl.BlockSpec((tm, ML_WIDTH), row),
                  pl.BlockSpec((tm, SSM_WIDTH), row), pl.BlockSpec((tm, d), row),
                  const((D_MIX, d)), const((1, d)), const((d, 128)), const((d, 128)), const((1, 128)),
                  const((tm, tm))],
        out_specs=[pl.BlockSpec((tm, d), row), pl.BlockSpec((tm * SLAB, 128), row), pl.BlockSpec((tm, 128), row),
                   const((8, 128))],
        out_shape=[jax.ShapeDtypeStruct((t, d), F32), jax.ShapeDtypeStruct((t * SLAB, 128), U32),
                   jax.ShapeDtypeStruct((t, 128), F32), jax.ShapeDtypeStruct((8, 128), F32)],
        compiler_params=_cparams(("arbitrary",)), name="out_proj_router",
    )(ya, yb, yc, h, w, nw.reshape(1, d), r_hi, r_lo, r_bias, tri)


def _moe_kernel(be_ref, nu_ref, src0_ref, srcn_ref, dstp_ref, dst_ref, u_hbm, w1_ref, w3_ref, w2_ref, o_hbm,
                xbuf, ybuf, w1b, w3b, w2b, gsem, ssem):
    i = pl.program_id(0)
    bm = MOE_BM
    n_used = nu_ref[0]
    slot = i % 2

    def slab(idx):
        return pl.ds(pl.multiple_of(idx * SLAB, SLAB), SLAB)

    def row_in(tok, r, sl):
        return pltpu.make_async_copy(u_hbm.at[slab(tok), :], xbuf.at[sl, slab(r), :], gsem.at[sl])

    def row_out(r, row, sl):
        return pltpu.make_async_copy(ybuf.at[sl, slab(r), :], o_hbm.at[slab(row), :], ssem.at[sl])

    def gather(idx_ref, sl):
        def body(g, c):
            for j in range(8):
                r = g * 8 + j
                row_in(idx_ref[0, r], r, sl).start(priority=j % 2)
            return c
        lax.fori_loop(0, bm // 8, body, 0)

    def scatter(sl):
        def body(g, c):
            for j in range(8):
                r = g * 8 + j
                row_out(r, dst_ref[0, r], sl).start(priority=j % 2)
            return c
        lax.fori_loop(0, bm // 8, body, 0)

    def wait_gather(sl):
        pltpu.make_async_copy(u_hbm.at[pl.ds(0, bm * SLAB), :], xbuf.at[sl], gsem.at[sl]).wait()

    def wait_scatter(sl):
        pltpu.make_async_copy(ybuf.at[sl], o_hbm.at[pl.ds(0, bm * SLAB), :], ssem.at[sl]).wait()

    other = 1 - slot
    n_real = o_hbm.shape[0] // SLAB - 2 * bm

    @pl.when(i == 0)
    def _():
        gather(src0_ref, 0)
        ybuf[...] = jnp.zeros_like(ybuf)
        pltpu.make_async_copy(ybuf.at[0], o_hbm.at[pl.ds(n_real * SLAB, bm * SLAB), :], ssem.at[0]).start()

    @pl.when(i < n_used)
    def _():
        first = jnp.logical_or(i == 0, be_ref[i] != be_ref[jnp.maximum(i - 1, 0)])

        @pl.when(first)
        def _():
            w1b[...] = w1_ref[...].astype(BF16)
            w3b[...] = w3_ref[...].astype(BF16)
            w2b[...] = w2_ref[...].astype(BF16)

        wait_gather(slot)

        def issue(part, parts=4):
            for r in range(part * bm // parts, (part + 1) * bm // parts):
                row_in(srcn_ref[0, r], r, other).start(priority=r % 2)
                row_out(r, dstp_ref[0, r], other).start(priority=(r + 1) % 2)

        f = w1b.shape[1]
        h1 = jnp.zeros((bm, f), F32)
        h3 = jnp.zeros((bm, f), F32)
        for j in range(SLAB):
            lo, hi = _unpack_slab(xbuf, j, bm, lead=(slot,))
            xk = jnp.concatenate([lo, hi], axis=1).astype(BF16)
            h1 = h1 + _dot(xk, w1b[j * 256:(j + 1) * 256, :])
            h3 = h3 + _dot(xk, w3b[j * 256:(j + 1) * 256, :])
            if j % 4 == 3:
                issue(j // 4)
        hid = (_silu(h1) * h3).astype(BF16)
        wait_scatter(slot)
        half = SLAB // 2
        _pack_slab(ybuf, _dot(hid, w2b[:, :half * 256]), lead=(slot,))
        issue(2)
        _pack_slab(ybuf, _dot(hid, w2b[:, half * 256:]), lead=(slot,), j0=half)
        issue(3)

    @pl.when(i == n_used - 1)
    def _():
        scatter(slot)
        wait_scatter(slot)
        wait_scatter(other)
        wait_gather(other)


def _moe_call(block_e, n_used, src_tok, dst_row, u, w1, w3, w2, layer, n_out_rows):
    d, f = w1.shape[-2:]
    assert d == SLAB * 256
    bm = MOE_BM
    nb = block_e.shape[0]
    wmap = lambda i, be, nu: (layer, be[i], 0, 0)
    src_tok = src_tok.reshape(nb, 1, bm)
    lead = (n_out_rows - bm + jnp.arange(bm, dtype=jnp.int32)).reshape(1, 1, bm)
    dst_row = jnp.concatenate([lead, dst_row.reshape(nb, 1, bm)], axis=0)
    idx = lambda fn: pl.BlockSpec((None, 1, bm), fn, memory_space=pltpu.SMEM)
    return pl.pallas_call(
        _moe_kernel,
        grid_spec=pltpu.PrefetchScalarGridSpec(
            num_scalar_prefetch=2, grid=(nb,),
            in_specs=[idx(lambda i, be, nu: (0, 0, 0)),
                      idx(lambda i, be, nu: (jnp.minimum(i + 1, nb - 1), 0, 0)),
                      idx(lambda i, be, nu: (i, 0, 0)),
                      idx(lambda i, be, nu: (i + 1, 0, 0)),
                      pl.BlockSpec(memory_space=pl.ANY),
                      pl.BlockSpec((None, None, d, f), wmap), pl.BlockSpec((None, None, d, f), wmap),
                      pl.BlockSpec((None, None, f, d), wmap)],
            out_specs=pl.BlockSpec(memory_space=pl.ANY),
            scratch_shapes=[pltpu.VMEM((2, bm * SLAB, 128), U32), pltpu.VMEM((2, bm * SLAB, 128), U32),
                            pltpu.VMEM((d, f), BF16), pltpu.VMEM((d, f), BF16), pltpu.VMEM((f, d), BF16),
                            pltpu.SemaphoreType.DMA((2,)), pltpu.SemaphoreType.DMA((2,))]),
        out_shape=jax.ShapeDtypeStruct((n_out_rows * SLAB, 128), U32),
        compiler_params=_cparams(("arbitrary",)), name="moe_ffn",
    )(block_e, n_used, src_tok, src_tok, dst_row, dst_row, u, w1, w3, w2)


def _route_tables(rt, cnt, t):
    bm = MOE_BM
    tk = t * TOP_K
    e = rt[:, RT_E:RT_E + TOP_K].astype(jnp.int32)
    rank = rt[:, RT_RANK:RT_RANK + TOP_K].astype(jnp.int32)
    wts = rt[:, RT_W:RT_W + TOP_K]
    counts = cnt[0, LANE_E0:LANE_E0 + N_EXPERTS].astype(jnp.int32)
    padded = (counts + bm - 1) // bm * bm
    pad_ends = jnp.cumsum(padded)
    pad_starts = pad_ends - padded
    dest = (pad_starts[e] + rank).reshape(-1)
    n_blocks = -(-tk // bm) + N_EXPERTS
    n_rows = n_blocks * bm
    inv = jnp.full((n_rows,), -1, jnp.int32).at[dest].set(jnp.arange(tk, dtype=jnp.int32))
    tok, slot = inv // TOP_K, inv % TOP_K
    src_tok = jnp.where(inv >= 0, tok, 0)
    dst_row = jnp.where(inv >= 0, slot * t + tok, tk + (jnp.arange(n_rows, dtype=jnp.int32) % (2 * bm)))
    block_row0 = jnp.arange(n_blocks, dtype=jnp.int32) * bm
    block_e = jnp.minimum(jnp.sum(pad_ends[None, :] <= block_row0[:, None], axis=1), N_EXPERTS - 1).astype(jnp.int32)
    n_used = (pad_ends[-1] // bm).astype(jnp.int32).reshape(1)
    last_e = block_e[jnp.maximum(n_used[0] - 1, 0)]
    block_e = jnp.where(jnp.arange(n_blocks) < n_used[0], block_e, last_e)
    return block_e, n_used, src_tok, dst_row, wts, tk + 2 * bm


def _lane_row(pairs):
    row = jnp.zeros((128,), F32)
    for lane0, vals in pairs:
        row = row.at[lane0:lane0 + vals.shape[0]].set(vals.astype(F32))
    return row


def kernel(x, meta_tokens, hg_lb_logits, norm_mix_w, w_in, hg_norm_w, ml_b_i, ml_b_f, ml_norm_w,
           ssm_conv_w, ssm_conv_b, ssm_dt_bias, ssm_a_log, ssm_d, ssm_norm_w, w_out, norm_ffn_w,
           moe_w_group, moe_b_group, moe_w_router, moe_b_router, moe_w1, moe_w3, moe_w2, final_norm_w):
    batch, seq, d = x.shape
    depth = w_in.shape[0]
    p = LEAD_PAD + N_META + seq
    t = batch * p
    meta = jnp.broadcast_to(meta_tokens.astype(x.dtype)[None], (batch, N_META, d))
    h = jnp.concatenate([jnp.zeros((batch, LEAD_PAD, d), x.dtype), meta, x], axis=1).reshape(t, d)

    lb_w = jax.nn.softmax(hg_lb_logits.astype(F32), axis=0)
    lower_bounds = jnp.cumsum(lb_w, axis=0) - lb_w[0]

    contrib = wts = None
    for layer in range(depth):
        w = w_in[layer]
        o_mq, o_mv, o_mi, o_sz = HG_KEY * 2 + HG_WIDTH * 2, 2560, 3584, 3592
        o_sx = o_sz + SSM_WIDTH
        o_dt = o_sx + SSM_WIDTH + 2 * SSM_BC
        w_perm = jnp.concatenate([
            w[:, :o_mq], w[:, o_sz:o_sx], w[:, o_sx:o_dt], w[:, o_mv:o_mi], w[:, o_mq:o_mv],
            w[:, o_mi:o_sz], w[:, o_dt:], jnp.zeros((d, N_PROJ - C_SMALL - 2 * ML_HEADS - SSM_HEADS), w.dtype)],
            axis=1).astype(BF16)
        lb = lower_bounds[layer]
        lbf = jnp.maximum(lb, LB_FLOOR)
        hg_par = jnp.zeros((8, HG_KEY), F32).at[0].set(lbf).at[1].set(1.0 - lb).at[2].set(lbf - lb).at[3].set(hg_norm_w[layer])
        ml_par = jnp.zeros((8, 128), F32).at[0].set(_lane_row([(LANE_MI, ml_b_i[layer]), (LANE_MF, ml_b_f[layer])]))
        ss_par = jnp.zeros((8, 128), F32).at[0].set(_lane_row([(LANE_DT, ssm_dt_bias[layer])]))
        ss_par = ss_par.at[1].set(_lane_row([(LANE_DT, ssm_a_log[layer])]))
        dskip = jnp.repeat(ssm_d[layer].astype(F32), SSM_HEADDIM).reshape(1, SSM_WIDTH)
        w_r = jnp.concatenate([moe_w_group[layer],
                               moe_w_router[layer].transpose(1, 0, 2).reshape(d, N_EXPERTS),
                               jnp.zeros((d, 128 - N_GROUPS_MOE - N_EXPERTS), F32)], axis=1)
        r_hi = w_r.astype(BF16)
        r_lo = (w_r - r_hi.astype(F32)).astype(BF16)

        if layer == 0:
            (u,) = _norm_call(h, norm_mix_w[layer], write_h=False, u_dtype=BF16)
        else:
            h, u = _norm_call(h, norm_mix_w[layer], contrib, wts, write_h=True, u_dtype=BF16)
        proj = _inproj_call(u, w_perm)
        ya = _hgrn2_call(proj, hg_par, batch, p)
        yb = _mlstm_call(proj, ml_par, ml_norm_w[layer].reshape(1, ML_WIDTH), batch, p)
        yc = _ssd_call(proj, ssm_conv_w[layer], ssm_conv_b[layer].reshape(1, -1), ss_par, dskip,
                       ssm_norm_w[layer].reshape(1, SSM_WIDTH), batch, p)
        r_bias = _lane_row([(0, moe_b_group[layer]), (LANE_E0, moe_b_router[layer].reshape(-1))]).reshape(1, 128)
        h, u_ffn, rt, cnt = _outproj_call(ya, yb, yc, h, w_out[layer].astype(BF16), norm_ffn_w[layer],
                                          r_hi, r_lo, r_bias)
        block_e, n_used, src_tok, dst_row, wts, n_out_rows = _route_tables(rt, cnt, t)
        contrib = _moe_call(block_e, n_used, src_tok, dst_row, u_ffn, moe_w1, moe_w3, moe_w2, layer, n_out_rows)
    return _final_call(h, contrib, wts, final_norm_w, batch, p)
```

```python
import functools

import jax
import jax.numpy as jnp
import numpy as np
from jax import lax
from jax.experimental import pallas as pl
from jax.experimental.pallas import tpu as pltpu

F32 = jnp.float32
BF16 = jnp.bfloat16

D_MODEL = 2048
N_META = 16
CHUNK = 64
HG_CHUNK = 16
LEAD_PAD = CHUNK - N_META
EPS = 1e-6
NEG = -1e30
LB_FLOOR = 1e-30

HG_HEADS = 4
HG_KDIM = 128
HG_KEY = HG_HEADS * HG_KDIM
HG_WIDTH = HG_HEADS * 128

ML_HEADS = 4
ML_QK = 64
ML_V = 128
ML_QK_W = ML_HEADS * ML_QK
ML_WIDTH = ML_HEADS * ML_V
GATE_CAP = 15.0

SSM_HEADS = 16
SSM_HEADDIM = 64
SSM_WIDTH = SSM_HEADS * SSM_HEADDIM
SSM_STATE = 128
SSM_GROUPS = 4
SSM_HPG = SSM_HEADS // SSM_GROUPS
SSM_BC = SSM_GROUPS * SSM_STATE
CONV_W = 4

D_MIX = HG_WIDTH + ML_WIDTH + SSM_WIDTH

N_GROUPS_MOE = 4
EXPERTS_PER_GROUP = 8
N_EXPERTS = N_GROUPS_MOE * EXPERTS_PER_GROUP
TOP_K = 2
D_EXPERT = 512
MOE_BM = 128

C_HG = 0
C_SZ = 2048
C_SX = 3072
C_MV = 5120
C_MQ = 6144
C_SMALL = 6656
N_PROJ = 6912
LANE_MI = 0
LANE_MF = ML_HEADS
LANE_DT = 2 * ML_HEADS

VMEM_LIMIT = 56 * 1024 * 1024


def _cparams(sem):
    return pltpu.CompilerParams(dimension_semantics=sem, vmem_limit_bytes=VMEM_LIMIT)


def _row_tile(n, target, mult=16):
    best = None
    for t in range(mult, min(n, target) + 1, mult):
        if n % t == 0:
            best = t
    assert best is not None, (n, target, mult)
    return best


def _split3(x):
    hi = x.astype(BF16)
    r = x - hi.astype(F32)
    mid = r.astype(BF16)
    lo = (r - mid.astype(F32)).astype(BF16)
    return hi, mid, lo


def _dot(a, b):
    return jnp.dot(a, b, preferred_element_type=F32)


def _sel_dot(sel, x):
    hi, mid, lo = _split3(x)
    return _dot(sel, hi) + _dot(sel, mid) + _dot(sel, lo)


def _dot_sel(x, sel):
    hi, mid, lo = _split3(x)
    return _dot(hi, sel) + _dot(mid, sel) + _dot(lo, sel)


def _dot_nt(a, b):
    return lax.dot_general(a, b, (((1,), (1,)), ((), ())), preferred_element_type=F32)


def _dot_tn(a, b):
    return lax.dot_general(a, b, (((0,), (0,)), ((), ())), preferred_element_type=F32)


def _log_sigmoid(x):
    return jnp.minimum(x, 0.0) - jnp.log1p(jnp.exp(-jnp.abs(x)))


def _sigmoid(x):
    return 1.0 / (1.0 + jnp.exp(-x))


def _silu(x):
    return x * _sigmoid(x)


SLAB = 8
U32 = jnp.uint32
HI_MASK = 0xFFFF0000


def _bf16_bits(x):
    return lax.bitcast_convert_type(x.astype(BF16).astype(F32), U32)


def _pack_slab(ref, val, lead=(), j0=0):
    rows = val.shape[0]
    for jj in range(val.shape[1] // 256):
        lo = _bf16_bits(val[:, jj * 256:jj * 256 + 128])
        hi = _bf16_bits(val[:, jj * 256 + 128:(jj + 1) * 256])
        ref[(*lead, pl.ds(j0 + jj, rows, stride=SLAB), slice(None))] = (lo >> 16) | (hi & U32(HI_MASK))


def _unpack_slab(ref, j, rows, lead=()):
    w = ref[(*lead, pl.ds(j, rows, stride=SLAB), slice(None))]
    return (lax.bitcast_convert_type(w << 16, F32), lax.bitcast_convert_type(w & U32(HI_MASK), F32))


def _combine_rows(h_ref, c0_ref, c1_ref, wt_ref):
    rows = h_ref.shape[0]
    wt = wt_ref[...]
    w0, w1 = wt[:, 0:1], wt[:, 1:2]
    pieces = []
    for j in range(SLAB):
        a0, b0 = _unpack_slab(c0_ref, j, rows)
        a1, b1 = _unpack_slab(c1_ref, j, rows)
        pieces += [w0 * a0 + w1 * a1, w0 * b0 + w1 * b1]
    return h_ref[...] + jnp.concatenate(pieces, axis=1)


def _norm_kernel(*refs, combine, write_h):
    if combine:
        h_ref, c0_ref, c1_ref, wt_ref, nw_ref = refs[:5]
        outs = refs[5:]
        h = _combine_rows(h_ref, c0_ref, c1_ref, wt_ref)
    else:
        h_ref, nw_ref = refs[:2]
        outs = refs[2:]
        h = h_ref[...]
    if write_h:
        outs[0][...] = h
    ms = jnp.mean(h * h, axis=-1, keepdims=True)
    u_ref = outs[-1]
    u_ref[...] = (h * lax.rsqrt(ms + EPS) * nw_ref[...]).astype(u_ref.dtype)


def _norm_call(h, nw, contrib=None, wts=None, *, write_h, u_dtype, tm_target=264):
    t, d = h.shape
    tm = _row_tile(t, tm_target)
    combine = contrib is not None
    row = lambda i: (i, 0)
    in_specs = [pl.BlockSpec((tm, d), row)]
    args = [h]
    if combine:
        in_specs += [pl.BlockSpec((tm * SLAB, 128), row), pl.BlockSpec((tm * SLAB, 128), lambda i: (t // tm + i, 0)),
                     pl.BlockSpec((tm, 2), row)]
        args += [contrib, contrib, wts]
    in_specs.append(pl.BlockSpec((1, d), lambda i: (0, 0)))
    args.append(nw.reshape(1, d))
    out_shape, out_specs = [], []
    if write_h:
        out_shape.append(jax.ShapeDtypeStruct((t, d), F32))
        out_specs.append(pl.BlockSpec((tm, d), row))
    out_shape.append(jax.ShapeDtypeStruct((t, d), u_dtype))
    out_specs.append(pl.BlockSpec((tm, d), row))
    return pl.pallas_call(
        functools.partial(_norm_kernel, combine=combine, write_h=write_h),
        grid=(t // tm,), in_specs=in_specs, out_specs=out_specs, out_shape=out_shape,
        compiler_params=_cparams(("arbitrary",)), name="combine_norm",
    )(*args)


def _final_kernel(h_ref, c0_ref, c1_ref, wt_ref, nw_ref, o_ref):
    h = _combine_rows(h_ref, c0_ref, c1_ref, wt_ref)
    ms = jnp.mean(h * h, axis=-1, keepdims=True)
    o_ref[...] = h * lax.rsqrt(ms + EPS) * nw_ref[...]


def _final_call(h, contrib, wts, nw, batch, p):
    t, d = h.shape
    seq = p - CHUNK
    tm = _row_tile(seq, 256)
    n_out = seq // tm
    row0 = lambda b, i: b * p + CHUNK + i * tm
    src = lambda b, i: (pl.multiple_of(row0(b, i), CHUNK), 0)
    slab0 = lambda b, i: (pl.multiple_of(row0(b, i) * SLAB, CHUNK), 0)
    slab1 = lambda b, i: (pl.multiple_of((t + row0(b, i)) * SLAB, CHUNK), 0)
    win = lambda r, w, fn: pl.BlockSpec((pl.Element(r), pl.Element(w)), fn)
    return pl.pallas_call(
        _final_kernel, grid=(batch, n_out),
        in_specs=[win(tm, d, src), win(tm * SLAB, 128, slab0), win(tm * SLAB, 128, slab1), win(tm, 2, src),
                  pl.BlockSpec((1, d), lambda b, i: (0, 0))],
        out_specs=pl.BlockSpec((None, tm, d), lambda b, i: (b, i, 0)),
        out_shape=jax.ShapeDtypeStruct((batch, seq, d), F32),
        compiler_params=_cparams(("arbitrary", "arbitrary")), name="final_norm",
    )(h, contrib, contrib, wts, nw.reshape(1, d))


def _matmul_kernel(x_ref, w_ref, o_ref):
    o_ref[...] = _dot(x_ref[...], w_ref[...])


def _inproj_call(u, w, tm_target=1056, tn=768):
    t, d = u.shape
    n = w.shape[1]
    tm = _row_tile(t, tm_target)
    assert n % tn == 0
    return pl.pallas_call(
        _matmul_kernel, grid=(t // tm, n // tn),
        in_specs=[pl.BlockSpec((tm, d), lambda i, j: (i, 0)), pl.BlockSpec((d, tn), lambda i, j: (0, j))],
        out_specs=pl.BlockSpec((tm, tn), lambda i, j: (i, j)),
        out_shape=jax.ShapeDtypeStruct((t, n), F32),
        compiler_params=_cparams(("arbitrary", "arbitrary")), name="in_proj",
    )(u, w)


def _hgrn2_kernel(q_ref, f_ref, i_ref, g_ref, par_ref, o_ref, st_ref, *, rows):
    s = pl.program_id(1)

    @pl.when(s == 0)
    def _():
        st_ref[...] = jnp.zeros_like(st_ref)

    c = HG_CHUNK
    ones = jnp.ones((HG_KDIM, 128), BF16)
    rid = lax.broadcasted_iota(jnp.int32, (c, 128), 0)
    scale = HG_KDIM ** -0.5

    hc = c // 2

    def chunk(ci, carry):
        r0 = pl.multiple_of(ci * c, c)
        pad = (s * rows + r0 + rid) < LEAD_PAD

        def front(h):
            cols = slice(h * 128, (h + 1) * 128)
            a_lb = par_ref[0:1, cols]
            b_lb = par_ref[1:2, cols]
            c_lb = par_ref[2:3, cols]
            z = f_ref[pl.ds(r0, c), cols]
            sg = _sigmoid(z)
            f = a_lb + b_lb * sg
            log_f = jnp.where(pad, 0.0, jnp.log(f))
            k = jnp.where(pad, 0.0, b_lb * (1.0 - sg) - c_lb)
            q = q_ref[pl.ds(r0, c), cols] * scale
            v = i_ref[pl.ds(r0, c), cols]
            cum = log_f
            for sh in (1, 2, 4, 8):
                cum = cum + jnp.where(rid >= sh, pltpu.roll(cum, sh, axis=0), 0.0)
            parts = []
            for s_ in range(c):
                lo = 0 if s_ < hc else hc
                rel = jnp.where(rid[lo:] >= s_, cum[lo:] - cum[s_:s_ + 1, :], NEG)
                parts.append(q[lo:] * (k[s_:s_ + 1, :] * jnp.exp(rel)))
            sc = _dot(jnp.concatenate(parts, axis=0).astype(BF16), ones)
            st = st_ref[h]
            o_inter = _dot_nt((q * jnp.exp(cum)).astype(BF16), st.astype(BF16))
            last = cum[c - 1:c, :]
            kd = (k * jnp.exp(last - cum)).astype(BF16)
            st_ref[h] = st * jnp.exp(last) + _dot_tn(v.astype(BF16), kd)
            return sc, o_inter, v

        def back(h, sc, o_inter, v):
            cols = slice(h * 128, (h + 1) * 128)
            o_top = o_inter[:hc]
            o_bot = o_inter[hc:]
            for s_ in range(hc):
                o_top = o_top + sc[s_ * c:s_ * c + hc, :] * v[s_:s_ + 1, :]
                o_bot = o_bot + sc[s_ * c + hc:(s_ + 1) * c, :] * v[s_:s_ + 1, :]
            for s_ in range(hc, c):
                r_ = hc * c + (s_ - hc) * hc
                o_bot = o_bot + sc[r_:r_ + hc, :] * v[s_:s_ + 1, :]
            o = jnp.concatenate([o_top, o_bot], axis=0)
            ms = jnp.mean(o * o, axis=-1, keepdims=True)
            g = g_ref[pl.ds(r0, c), cols]
            o_ref[pl.ds(r0, c), cols] = (o * lax.rsqrt(ms + EPS) * par_ref[3:4, cols] * _silu(g)).astype(o_ref.dtype)

        pending = front(0)
        for h in range(1, HG_HEADS):
            nxt = front(h)
            back(h - 1, *pending)
            pending = nxt
        back(HG_HEADS - 1, *pending)
        return carry

    n_chunks = rows // c
    lax.fori_loop(0, n_chunks, chunk, 0, unroll=3 if n_chunks % 3 == 0 else 1)


def _hgrn2_call(proj, par, batch, p):
    t = proj.shape[0]
    rows = _row_tile(p, 528)
    nb = p // rows
    w = HG_KEY
    blk = lambda j: pl.BlockSpec((rows, w), lambda b, s, j=j: (b * nb + s, C_HG // w + j))
    return pl.pallas_call(
        functools.partial(_hgrn2_kernel, rows=rows), grid=(batch, nb),
        in_specs=[blk(0), blk(1), blk(2), blk(3), pl.BlockSpec((8, w), lambda b, s: (0, 0))],
        out_specs=pl.BlockSpec((rows, w), lambda b, s: (b * nb + s, 0)),
        out_shape=jax.ShapeDtypeStruct((t, HG_WIDTH), BF16),
        scratch_shapes=[pltpu.VMEM((HG_HEADS, 128, HG_KDIM), F32)],
        compiler_params=_cparams(("arbitrary", "arbitrary")), name="hgrn2",
    )(proj, proj, proj, proj, par)


def _mlstm_kernel(v_ref, o_ref, q_ref, k_ref, sm_ref, par_ref, nw_ref, tri_ref, sel_ref, dg_ref,
                  y_ref, c_ref, m_ref):
    s = pl.program_id(0)

    @pl.when(s == 0)
    def _():
        c_ref[...] = jnp.zeros_like(c_ref)
        m_ref[...] = jnp.zeros_like(m_ref)

    n = CHUNK
    rid = lax.broadcasted_iota(jnp.int32, (n, 128), 0)
    pad = (s * n + rid) < LEAD_PAD
    sel_i = sel_ref[0]
    sel_f = sel_ref[1]
    dg = dg_ref[...]
    w = ML_HEADS * n
    tt = lax.broadcasted_iota(jnp.int32, (n, w), 0)
    ss = lax.broadcasted_iota(jnp.int32, (n, w), 1) & (n - 1)
    causal = tt >= ss
    scale = ML_QK ** -0.5
    lane = lax.broadcasted_iota(jnp.int32, (n, 128), 1)
    one_col = jnp.where(lane == 0, 1.0, 0.0).astype(BF16)
    for b in range(v_ref.shape[0]):
        pre = sm_ref[b] + par_ref[0:1, :]
        cap = GATE_CAP * jnp.tanh(pre * (1.0 / GATE_CAP))
        log_i = jnp.where(pad, NEG, cap)
        log_f = jnp.where(pad, 0.0, _log_sigmoid(cap))
        cum = _sel_dot(tri_ref[...], log_f)
        cum_col = _dot_sel(cum, sel_f)
        cum_row = jnp.sum(cum_col * dg, axis=0, keepdims=True)
        li_row = jnp.sum(_dot_sel(log_i, sel_i) * dg, axis=0, keepdims=True)
        dmat = jnp.where(causal, cum_col - cum_row + li_row, NEG)
        hs = range(ML_HEADS)
        sh = [b * ML_HEADS + h for h in hs]
        d_h = [dmat[:, h * n:(h + 1) * n] for h in hs]
        cum_h = [cum[:, LANE_MF + h:LANE_MF + h + 1] for h in hs]
        li_h = [log_i[:, LANE_MI + h:LANE_MI + h + 1] for h in hs]
        m_st = [m_ref[sh[h]:sh[h] + 1, 0:1] for h in hs]
        q = [(q_ref[b, :, h * ML_QK:(h + 1) * ML_QK] * scale).astype(BF16) for h in hs]
        k = [k_ref[b, :, h * ML_QK:(h + 1) * ML_QK] for h in hs]
        v_aug = [jnp.concatenate([v_ref[b, :, h * ML_V:(h + 1) * ML_V].astype(BF16), one_col], axis=1) for h in hs]
        c_prev = [c_ref[sh[h]] for h in hs]
        inter = [cum_h[h] + m_st[h] for h in hs]
        m_t = [jnp.maximum(inter[h], jnp.max(d_h[h], axis=-1, keepdims=True)) for h in hs]
        qk = [_dot_nt(q[h], k[h].astype(BF16)) for h in hs]
        qc = [_dot(q[h], c_prev[h].astype(BF16)) for h in hs]
        pw = [(qk[h] * jnp.exp(d_h[h] - m_t[h])).astype(BF16) for h in hs]
        nd = [_dot(pw[h], v_aug[h]) + jnp.exp(inter[h] - m_t[h]) * qc[h] for h in hs]
        hh = [nd[h][:, :ML_V] / jnp.maximum(jnp.abs(nd[h][:, ML_V:ML_V + 1]), jnp.exp(-m_t[h])) for h in hs]
        tot = [cum_h[h][n - 1:n, :] for h in hs]
        to_end = [tot[h] - cum_h[h] + li_h[h] for h in hs]
        m_loc = [jnp.max(to_end[h], axis=0, keepdims=True) for h in hs]
        kw = [(k[h] * jnp.exp(to_end[h] - m_loc[h])).astype(BF16) for h in hs]
        c_loc = [_dot_tn(kw[h], v_aug[h]) for h in hs]
        m_new = [jnp.maximum(tot[h] + m_st[h], m_loc[h]) for h in hs]
        for h in hs:
            c_ref[sh[h]] = (jnp.exp(tot[h] + m_st[h] - m_new[h]) * c_prev[h]
                            + jnp.exp(m_loc[h] - m_new[h]) * c_loc[h])
            m_ref[sh[h]:sh[h] + 1, :] = jnp.broadcast_to(m_new[h], (1, 128))
        for h in hs:
            ms = jnp.mean(hh[h] * hh[h], axis=-1, keepdims=True)
            cols = slice(h * ML_V, (h + 1) * ML_V)
            y_ref[b, :, cols] = (hh[h] * lax.rsqrt(ms + EPS) * nw_ref[:, cols]
                                 * _sigmoid(o_ref[b, :, cols])).astype(y_ref.dtype)


def _lane_select(lane0, heads, width):
    m = np.zeros((128, heads * width), np.float32)
    for h in range(heads):
        m[lane0 + h, h * width:(h + 1) * width] = 1.0
    return m


def _diag_mask(n, heads):
    return np.tile(np.eye(n, dtype=np.float32), (1, heads))


def _mlstm_call(proj, par, nw, batch, p):
    t = proj.shape[0]
    n = CHUNK
    nc = p // n
    tri = jnp.asarray(np.tril(np.ones((n, n), np.float32)), BF16)
    sel = jnp.asarray(np.stack([_lane_select(LANE_MI, ML_HEADS, n), _lane_select(LANE_MF, ML_HEADS, n)]), BF16)
    dg = jnp.asarray(_diag_mask(n, ML_HEADS), F32)
    proj3 = proj.reshape(batch, p, proj.shape[1])
    blk = lambda w, off: pl.BlockSpec((batch, n, w), lambda s: (0, s, off // w))
    const = lambda shape: pl.BlockSpec(shape, lambda s: (0,) * len(shape))
    m_rows = -(-batch * ML_HEADS // 8) * 8
    y = pl.pallas_call(
        _mlstm_kernel, grid=(nc,),
        in_specs=[blk(ML_WIDTH, C_MV), blk(ML_WIDTH, C_MV + ML_WIDTH), blk(ML_QK_W, C_MQ),
                  blk(ML_QK_W, C_MQ + ML_QK_W), blk(128, C_SMALL),
                  const((8, 128)), const((1, ML_WIDTH)), const((n, n)), const((2, 128, ML_HEADS * n)),
                  const((n, ML_HEADS * n))],
        out_specs=pl.BlockSpec((batch, n, ML_WIDTH), lambda s: (0, s, 0)),
        out_shape=jax.ShapeDtypeStruct((batch, p, ML_WIDTH), BF16),
        scratch_shapes=[pltpu.VMEM((batch * ML_HEADS, ML_QK, 2 * ML_V), F32), pltpu.VMEM((m_rows, 128), F32)],
        compiler_params=_cparams(("arbitrary",)), name="mlstm",
    )(proj3, proj3, proj3, proj3, proj3, par, nw, tri, sel, dg)
    return y.reshape(t, ML_WIDTH)


def _ssd_kernel(z_ref, x_ref, b_ref, c_ref, sm_ref, cw_ref, cb_ref, par_ref, dsk_ref, nw_ref,
                tri_ref, sel_ref, dg_ref, y_ref, xs_ref, bs_ref, cs_ref, st_ref):
    s = pl.program_id(1)
    n = CHUNK
    tail = 8

    @pl.when(s == 0)
    def _():
        st_ref[...] = jnp.zeros_like(st_ref)
        xs_ref[0:tail, :] = jnp.zeros((tail, xs_ref.shape[1]), F32)
        bs_ref[0:tail, :] = jnp.zeros((tail, bs_ref.shape[1]), F32)
        cs_ref[0:tail, :] = jnp.zeros((tail, cs_ref.shape[1]), F32)

    def conv_silu(src_ref, scr_ref, c0, width, rowmask):
        scr_ref[tail:tail + n, :] = src_ref[...]
        acc = cb_ref[:, c0:c0 + width]
        for j in range(CONV_W):
            off = tail - (CONV_W - 1) + j
            acc = acc + cw_ref[j:j + 1, c0:c0 + width] * scr_ref[off:off + n, :]
        scr_ref[0:tail, :] = scr_ref[n:n + tail, :]
        return jnp.where(rowmask, 0.0, _silu(acc))

    def padmask(width):
        return (s * n + lax.broadcasted_iota(jnp.int32, (n, width), 0)) < LEAD_PAD

    x = conv_silu(x_ref, xs_ref, 0, SSM_WIDTH, padmask(SSM_WIDTH))
    bm = conv_silu(b_ref, bs_ref, SSM_WIDTH, SSM_BC, padmask(SSM_BC))
    cm = conv_silu(c_ref, cs_ref, SSM_WIDTH + SSM_BC, SSM_BC, padmask(SSM_BC))

    pre = sm_ref[...] + par_ref[0:1, :]
    dt = jnp.maximum(pre, 0.0) + jnp.log1p(jnp.exp(-jnp.abs(pre)))
    dt = jnp.where(padmask(128), 0.0, dt)
    da = dt * (-jnp.exp(par_ref[1:2, :]))
    cum = _sel_dot(tri_ref[...], da)
    sel = sel_ref[...]
    dt_col = _dot_sel(dt, sel)
    cum_col = _dot_sel(cum, sel)
    cum_row = jnp.sum(cum_col * dg_ref[...], axis=0, keepdims=True)
    w = SSM_WIDTH
    tt = lax.broadcasted_iota(jnp.int32, (n, w), 0)
    ss = lax.broadcasted_iota(jnp.int32, (n, w), 1) & (n - 1)
    decay = jnp.exp(jnp.where(tt >= ss, cum_col - cum_row, NEG))
    xdt = x * dt_col
    last = cum_col[n - 1:n, :]
    wend = (xdt * jnp.exp(last - cum_col)).astype(BF16)
    chunk_decay = jnp.exp(last)
    ecum = jnp.exp(cum_col)
    gw = SSM_HPG * SSM_HEADDIM
    rr = lax.broadcasted_iota(jnp.int32, (gw, gw), 0) // SSM_HEADDIM
    cc = lax.broadcasted_iota(jnp.int32, (gw, gw), 1) // SSM_HEADDIM
    blockdiag = rr == cc
    ys = []
    for g in range(SSM_GROUPS):
        gl = slice(g * gw, (g + 1) * gw)
        sl = slice(g * SSM_STATE, (g + 1) * SSM_STATE)
        cm_g = cm[:, sl].astype(BF16)
        bm_g = bm[:, sl].astype(BF16)
        cb = _dot_nt(cm_g, jnp.concatenate([bm_g] * SSM_HPG, axis=0))
        m = (cb * decay[:, gl]).astype(BF16)
        xdt_g = xdt[:, gl]
        bd = jnp.where(blockdiag, jnp.concatenate([xdt_g] * SSM_HPG, axis=0), 0.0).astype(BF16)
        st = st_ref[g]
        y_g = _dot(m, bd) + ecum[:, gl] * _dot(cm_g, st.astype(BF16))
        st_ref[g] = st * chunk_decay[:, gl] + _dot_tn(bm_g, wend[:, gl])
        ys.append(y_g)
    y = jnp.concatenate(ys, axis=1) + x * dsk_ref[...]
    y = y * _silu(z_ref[...])
    outs = []
    for g in range(SSM_GROUPS):
        gl = slice(g * gw, (g + 1) * gw)
        y_g = y[:, gl]
        ms = jnp.mean(y_g * y_g, axis=-1, keepdims=True)
        outs.append(y_g * lax.rsqrt(ms + EPS))
    y_ref[...] = (jnp.concatenate(outs, axis=1) * nw_ref[...]).astype(y_ref.dtype)


def _ssd_call(proj, cw, cb, par, dsk, nw, batch, p):
    t = proj.shape[0]
    n = CHUNK
    nc = p // n
    tri = jnp.asarray(np.tril(np.ones((n, n), np.float32)), BF16)
    sel = jnp.asarray(_lane_select(LANE_DT, SSM_HEADS, SSM_HEADDIM), BF16)
    dg = jnp.asarray(_diag_mask(n, SSM_HEADS), F32)
    blk = lambda w, off: pl.BlockSpec((n, w), lambda b, s: (b * nc + s, off // w))
    const = lambda shape: pl.BlockSpec(shape, lambda b, s: (0,) * len(shape))
    cch = SSM_WIDTH + 2 * SSM_BC
    return pl.pallas_call(
        _ssd_kernel, grid=(batch, nc),
        in_specs=[blk(SSM_WIDTH, C_SZ), blk(SSM_WIDTH, C_SX), blk(SSM_BC, C_SX + SSM_WIDTH),
                  blk(SSM_BC, C_SX + SSM_WIDTH + SSM_BC), blk(128, C_SMALL),
                  const((CONV_W, cch)), const((1, cch)), const((8, 128)), const((1, SSM_WIDTH)),
                  const((1, SSM_WIDTH)), const((n, n)), const((128, SSM_WIDTH)), const((n, SSM_WIDTH))],
        out_specs=pl.BlockSpec((n, SSM_WIDTH), lambda b, s: (b * nc + s, 0)),
        out_shape=jax.ShapeDtypeStruct((t, SSM_WIDTH), BF16),
        scratch_shapes=[pltpu.VMEM((n + 8, SSM_WIDTH), F32), pltpu.VMEM((n + 8, SSM_BC), F32),
                        pltpu.VMEM((n + 8, SSM_BC), F32),
                        pltpu.VMEM((SSM_GROUPS, SSM_STATE, SSM_HPG * SSM_HEADDIM), F32)],
        compiler_params=_cparams(("arbitrary", "arbitrary")), name="ssd",
    )(proj, proj, proj, proj, proj, cw, cb, par, dsk, nw, tri, sel, dg)


LANE_E0 = N_GROUPS_MOE
RT_E, RT_RANK, RT_W = 0, 2, 4


def _first_max(vals, lane):
    m = jnp.max(vals, axis=-1, keepdims=True)
    idx = jnp.min(jnp.where(vals == m, lane, 128), axis=-1, keepdims=True)
    return m, idx


def _outproj_kernel(ya_ref, yb_ref, yc_ref, h_ref, w_ref, nw_ref, rh_ref, rl_ref, rb_ref, tri_ref,
                    hm_ref, u_ref, rt_ref, cnt_ref):
    a0, a1 = HG_WIDTH, HG_WIDTH + ML_WIDTH
    h = h_ref[...]
    h = h + _dot(ya_ref[...], w_ref[0:a0, :])
    h = h + _dot(yb_ref[...], w_ref[a0:a1, :])
    h = h + _dot(yc_ref[...], w_ref[a1:, :])
    hm_ref[...] = h
    ms = jnp.mean(h * h, axis=-1, keepdims=True)
    u = h * lax.rsqrt(ms + EPS) * nw_ref[...]
    _pack_slab(u_ref, u)
    u_hi = u.astype(BF16)
    u_lo = (u - u_hi.astype(F32)).astype(BF16)
    lg = _dot(u_hi, rh_ref[...]) + (_dot(u_lo, rh_ref[...]) + _dot(u_hi, rl_ref[...])) + rb_ref[...]

    tm = lg.shape[0]
    lane = lax.broadcasted_iota(jnp.int32, (tm, 128), 1)
    g_mask = lane < N_GROUPS_MOE
    g_max, g_sel = _first_max(jnp.where(g_mask, lg, NEG), lane)
    g_gate = 1.0 / jnp.sum(jnp.where(g_mask, jnp.exp(lg - g_max), 0.0), axis=-1, keepdims=True)
    lo = LANE_E0 + g_sel * EXPERTS_PER_GROUP
    e_vals = jnp.where((lane >= lo) & (lane < lo + EXPERTS_PER_GROUP), lg, NEG)
    v1, i1 = _first_max(e_vals, lane)
    v2, i2 = _first_max(jnp.where(lane == i1, NEG, e_vals), lane)
    a = jnp.exp(v2 - v1)
    w1 = g_gate / (1.0 + a)
    w2 = w1 * a
    @pl.when(pl.program_id(0) == 0)
    def _():
        cnt_ref[...] = jnp.zeros_like(cnt_ref)

    hit1 = lane == i1
    hit2 = lane == i2
    onehot = jnp.where(hit1 | hit2, 1.0, 0.0)
    before = _dot(tri_ref[...], onehot.astype(BF16)) + cnt_ref[0:1, :]
    r1 = jnp.sum(jnp.where(hit1, before, 0.0), axis=-1, keepdims=True)
    r2 = jnp.sum(jnp.where(hit2, before, 0.0), axis=-1, keepdims=True)
    cnt_ref[...] = cnt_ref[...] + jnp.sum(onehot, axis=0, keepdims=True)
    rec = jnp.zeros((tm, 128), F32)
    for ln, val in ((RT_E, (i1 - LANE_E0).astype(F32)), (RT_E + 1, (i2 - LANE_E0).astype(F32)),
                    (RT_RANK, r1), (RT_RANK + 1, r2), (RT_W, w1), (RT_W + 1, w2)):
        rec = jnp.where(lane == ln, val, rec)
    rt_ref[...] = rec


def _outproj_call(ya, yb, yc, h, w, nw, r_hi, r_lo, r_bias, tm_target=352):
    t, d = h.shape
    tm = _row_tile(t, tm_target)
    tri = jnp.asarray(np.tril(np.ones((tm, tm), np.float32), -1), BF16)
    row = lambda i: (i, 0)
    const = lambda shape: pl.BlockSpec(shape, lambda i: (0, 0))
    return pl.pallas_call(
        _outproj_kernel, grid=(t // tm,),
        in_specs=[pl.BlockSpec((tm, HG_WIDTH), row), pl.BlockSpec((tm, ML_WIDTH), row),
                  pl.BlockSpec((tm, SSM_WIDTH), row), pl.BlockSpec((tm, d), row),
                  const((D_MIX, d)), const((1, d)), const((d, 128)), const((d, 128)), const((1, 128)),
                  const((tm, tm))],
        out_specs=[pl.BlockSpec((tm, d), row), pl.BlockSpec((tm * SLAB, 128), row), pl.BlockSpec((tm, 128), row),
                   const((8, 128))],
        out_shape=[jax.ShapeDtypeStruct((t, d), F32), jax.ShapeDtypeStruct((t * SLAB, 128), U32),
                   jax.ShapeDtypeStruct((t, 128), F32), jax.ShapeDtypeStruct((8, 128), F32)],
        compiler_params=_cparams(("arbitrary",)), name="out_proj_router",
    )(ya, yb, yc, h, w, nw.reshape(1, d), r_hi, r_lo, r_bias, tri)


def _moe_kernel(be_ref, nu_ref, src0_ref, src1_ref, srcn_ref, dstp_ref, dst_ref, u_hbm, w1_ref, w3_ref, w2_ref,
                o_hbm, xbuf, ybuf, w1b, w3b, w2b, gsem, ssem):
    i = pl.program_id(0)
    bm = MOE_BM
    n_used = nu_ref[0]
    slot = i % 2
    xs = i % 3

    def slab(idx):
        return pl.ds(pl.multiple_of(idx * SLAB, SLAB), SLAB)

    def row_in(tok, r, sl):
        return pltpu.make_async_copy(u_hbm.at[slab(tok), :], xbuf.at[sl, slab(r), :], gsem.at[sl])

    def row_out(r, row, sl):
        return pltpu.make_async_copy(ybuf.at[sl, slab(r), :], o_hbm.at[slab(row), :], ssem.at[sl])

    def gather(idx_ref, sl):
        def body(g, c):
            for j in range(8):
                r = g * 8 + j
                row_in(idx_ref[0, r], r, sl).start(priority=j % 2)
            return c
        lax.fori_loop(0, bm // 8, body, 0)

    def scatter(sl):
        def body(g, c):
            for j in range(8):
                r = g * 8 + j
                row_out(r, dst_ref[0, r], sl).start(priority=j % 2)
            return c
        lax.fori_loop(0, bm // 8, body, 0)

    def wait_gather(sl):
        pltpu.make_async_copy(u_hbm.at[pl.ds(0, bm * SLAB), :], xbuf.at[sl], gsem.at[sl]).wait()

    def wait_scatter(sl):
        pltpu.make_async_copy(ybuf.at[sl], o_hbm.at[pl.ds(0, bm * SLAB), :], ssem.at[sl]).wait()

    other = 1 - slot
    n_real = o_hbm.shape[0] // SLAB - 2 * bm

    @pl.when(i == 0)
    def _():
        gather(src0_ref, 0)
        gather(src1_ref, 1)
        ybuf[...] = jnp.zeros_like(ybuf)
        pltpu.make_async_copy(ybuf.at[0], o_hbm.at[pl.ds(n_real * SLAB, bm * SLAB), :], ssem.at[0]).start()

    @pl.when(i < n_used)
    def _():
        first = jnp.logical_or(i == 0, be_ref[i] != be_ref[jnp.maximum(i - 1, 0)])

        @pl.when(first)
        def _():
            w1b[...] = w1_ref[...].astype(BF16)
            w3b[...] = w3_ref[...].astype(BF16)
            w2b[...] = w2_ref[...].astype(BF16)

        wait_gather(xs)

        xn = (i + 2) % 3

        def issue(part, parts=4):
            for r in range(part * bm // parts, (part + 1) * bm // parts):
                row_in(srcn_ref[0, r], r, xn).start(priority=r % 2)
                row_out(r, dstp_ref[0, r], other).start(priority=(r + 1) % 2)

        f = w1b.shape[1]
        h1 = jnp.zeros((bm, f), F32)
        h3 = jnp.zeros((bm, f), F32)
        for j in range(SLAB):
            lo, hi = _unpack_slab(xbuf, j, bm, lead=(xs,))
            xk = jnp.concatenate([lo, hi], axis=1).astype(BF16)
            h1 = h1 + _dot(xk, w1b[j * 256:(j + 1) * 256, :])
            h3 = h3 + _dot(xk, w3b[j * 256:(j + 1) * 256, :])
            if j % 4 == 3:
                issue(j // 4)
        hid = (_silu(h1) * h3).astype(BF16)
        wait_scatter(slot)
        half = SLAB // 2
        _pack_slab(ybuf, _dot(hid, w2b[:, :half * 256]), lead=(slot,))
        issue(2)
        _pack_slab(ybuf, _dot(hid, w2b[:, half * 256:]), lead=(slot,), j0=half)
        issue(3)

    @pl.when(i == n_used - 1)
    def _():
        scatter(slot)
        wait_scatter(slot)
        wait_scatter(other)
        wait_gather((i + 1) % 3)
        wait_gather((i + 2) % 3)


def _moe_call(block_e, n_used, src_tok, dst_row, u, w1, w3, w2, layer, n_out_rows):
    d, f = w1.shape[-2:]
    assert d == SLAB * 256
    bm = MOE_BM
    nb = block_e.shape[0]
    wmap = lambda i, be, nu: (layer, be[i], 0, 0)
    src_tok = src_tok.reshape(nb, 1, bm)
    lead = (n_out_rows - bm + jnp.arange(bm, dtype=jnp.int32)).reshape(1, 1, bm)
    dst_row = jnp.concatenate([lead, dst_row.reshape(nb, 1, bm)], axis=0)
    idx = lambda fn: pl.BlockSpec((None, 1, bm), fn, memory_space=pltpu.SMEM)
    return pl.pallas_call(
        _moe_kernel,
        grid_spec=pltpu.PrefetchScalarGridSpec(
            num_scalar_prefetch=2, grid=(nb,),
            in_specs=[idx(lambda i, be, nu: (0, 0, 0)),
                      idx(lambda i, be, nu: (1, 0, 0)),
                      idx(lambda i, be, nu: (jnp.minimum(i + 2, nb - 1), 0, 0)),
                      idx(lambda i, be, nu: (i, 0, 0)),
                      idx(lambda i, be, nu: (i + 1, 0, 0)),
                      pl.BlockSpec(memory_space=pl.ANY),
                      pl.BlockSpec((None, None, d, f), wmap), pl.BlockSpec((None, None, d, f), wmap),
                      pl.BlockSpec((None, None, f, d), wmap)],
            out_specs=pl.BlockSpec(memory_space=pl.ANY),
            scratch_shapes=[pltpu.VMEM((3, bm * SLAB, 128), U32), pltpu.VMEM((2, bm * SLAB, 128), U32),
                            pltpu.VMEM((d, f), BF16), pltpu.VMEM((d, f), BF16), pltpu.VMEM((f, d), BF16),
                            pltpu.SemaphoreType.DMA((3,)), pltpu.SemaphoreType.DMA((2,))]),
        out_shape=jax.ShapeDtypeStruct((n_out_rows * SLAB, 128), U32),
        compiler_params=_cparams(("arbitrary",)), name="moe_ffn",
    )(block_e, n_used, src_tok, src_tok, src_tok, dst_row, dst_row, u, w1, w3, w2)


def _route_tables(rt, cnt, t):
    bm = MOE_BM
    tk = t * TOP_K
    e = rt[:, RT_E:RT_E + TOP_K].astype(jnp.int32)
    rank = rt[:, RT_RANK:RT_RANK + TOP_K].astype(jnp.int32)
    wts = rt[:, RT_W:RT_W + TOP_K]
    counts = cnt[0, LANE_E0:LANE_E0 + N_EXPERTS].astype(jnp.int32)
    padded = (counts + bm - 1) // bm * bm
    pad_ends = jnp.cumsum(padded)
    pad_starts = pad_ends - padded
    dest = (pad_starts[e] + rank).reshape(-1)
    n_blocks = -(-tk // bm) + N_EXPERTS
    n_rows = n_blocks * bm
    inv = jnp.full((n_rows,), -1, jnp.int32).at[dest].set(jnp.arange(tk, dtype=jnp.int32))
    tok, slot = inv // TOP_K, inv % TOP_K
    src_tok = jnp.where(inv >= 0, tok, 0)
    dst_row = jnp.where(inv >= 0, slot * t + tok, tk + (jnp.arange(n_rows, dtype=jnp.int32) % (2 * bm)))
    block_row0 = jnp.arange(n_blocks, dtype=jnp.int32) * bm
    block_e = jnp.minimum(jnp.sum(pad_ends[None, :] <= block_row0[:, None], axis=1), N_EXPERTS - 1).astype(jnp.int32)
    n_used = (pad_ends[-1] // bm).astype(jnp.int32).reshape(1)
    last_e = block_e[jnp.maximum(n_used[0] - 1, 0)]
    block_e = jnp.where(jnp.arange(n_blocks) < n_used[0], block_e, last_e)
    return block_e, n_used, src_tok, dst_row, wts, tk + 2 * bm


def _lane_row(pairs):
    row = jnp.zeros((128,), F32)
    for lane0, vals in pairs:
        row = row.at[lane0:lane0 + vals.shape[0]].set(vals.astype(F32))
    return row


def kernel(x, meta_tokens, hg_lb_logits, norm_mix_w, w_in, hg_norm_w, ml_b_i, ml_b_f, ml_norm_w,
           ssm_conv_w, ssm_conv_b, ssm_dt_bias, ssm_a_log, ssm_d, ssm_norm_w, w_out, norm_ffn_w,
           moe_w_group, moe_b_group, moe_w_router, moe_b_router, moe_w1, moe_w3, moe_w2, final_norm_w):
    batch, seq, d = x.shape
    depth = w_in.shape[0]
    p = LEAD_PAD + N_META + seq
    t = batch * p
    meta = jnp.broadcast_to(meta_tokens.astype(x.dtype)[None], (batch, N_META, d))
    h = jnp.concatenate([jnp.zeros((batch, LEAD_PAD, d), x.dtype), meta, x], axis=1).reshape(t, d)

    lb_w = jax.nn.softmax(hg_lb_logits.astype(F32), axis=0)
    lower_bounds = jnp.cumsum(lb_w, axis=0) - lb_w[0]

    contrib = wts = None
    for layer in range(depth):
        w = w_in[layer]
        o_mq, o_mv, o_mi, o_sz = HG_KEY * 2 + HG_WIDTH * 2, 2560, 3584, 3592
        o_sx = o_sz + SSM_WIDTH
        o_dt = o_sx + SSM_WIDTH + 2 * SSM_BC
        w_perm = jnp.concatenate([
            w[:, :o_mq], w[:, o_sz:o_sx], w[:, o_sx:o_dt], w[:, o_mv:o_mi], w[:, o_mq:o_mv],
            w[:, o_mi:o_sz], w[:, o_dt:], jnp.zeros((d, N_PROJ - C_SMALL - 2 * ML_HEADS - SSM_HEADS), w.dtype)],
            axis=1).astype(BF16)
        lb = lower_bounds[layer]
        lbf = jnp.maximum(lb, LB_FLOOR)
        hg_par = jnp.zeros((8, HG_KEY), F32).at[0].set(lbf).at[1].set(1.0 - lb).at[2].set(lbf - lb).at[3].set(hg_norm_w[layer])
        ml_par = jnp.zeros((8, 128), F32).at[0].set(_lane_row([(LANE_MI, ml_b_i[layer]), (LANE_MF, ml_b_f[layer])]))
        ss_par = jnp.zeros((8, 128), F32).at[0].set(_lane_row([(LANE_DT, ssm_dt_bias[layer])]))
        ss_par = ss_par.at[1].set(_lane_row([(LANE_DT, ssm_a_log[layer])]))
        dskip = jnp.repeat(ssm_d[layer].astype(F32), SSM_HEADDIM).reshape(1, SSM_WIDTH)
        w_r = jnp.concatenate([moe_w_group[layer],
                               moe_w_router[layer].transpose(1, 0, 2).reshape(d, N_EXPERTS),
                               jnp.zeros((d, 128 - N_GROUPS_MOE - N_EXPERTS), F32)], axis=1)
        r_hi = w_r.astype(BF16)
        r_lo = (w_r - r_hi.astype(F32)).astype(BF16)

        if layer == 0:
            (u,) = _norm_call(h, norm_mix_w[layer], write_h=False, u_dtype=BF16)
        else:
            h, u = _norm_call(h, norm_mix_w[layer], contrib, wts, write_h=True, u_dtype=BF16)
        proj = _inproj_call(u, w_perm)
        ya = _hgrn2_call(proj, hg_par, batch, p)
        yb = _mlstm_call(proj, ml_par, ml_norm_w[layer].reshape(1, ML_WIDTH), batch, p)
        yc = _ssd_call(proj, ssm_conv_w[layer], ssm_conv_b[layer].reshape(1, -1), ss_par, dskip,
                       ssm_norm_w[layer].reshape(1, SSM_WIDTH), batch, p)
        r_bias = _lane_row([(0, moe_b_group[layer]), (LANE_E0, moe_b_router[layer].reshape(-1))]).reshape(1, 128)
        h, u_ffn, rt, cnt = _outproj_call(ya, yb, yc, h, w_out[layer].astype(BF16), norm_ffn_w[layer],
                                          r_hi, r_lo, r_bias)
        block_e, n_used, src_tok, dst_row, wts, n_out_rows = _route_tables(rt, cnt, t)
        contrib = _moe_call(block_e, n_used, src_tok, dst_row, u_ffn, moe_w1, moe_w3, moe_w2, layer, n_out_rows)
    return _final_call(h, contrib, wts, final_norm_w, batch, p)
```

```python
import functools

import jax
import jax.numpy as jnp
import numpy as np
from jax import lax
from jax.experimental import pallas as pl
from jax.experimental.pallas import tpu as pltpu

F32 = jnp.float32
BF16 = jnp.bfloat16

D_MODEL = 2048
N_META = 16
CHUNK = 64
HG_CHUNK = 16
LEAD_PAD = CHUNK - N_META
EPS = 1e-6
NEG = -1e30
LB_FLOOR = 1e-30

HG_HEADS = 4
HG_KDIM = 128
HG_KEY = HG_HEADS * HG_KDIM
HG_WIDTH = HG_HEADS * 128

ML_HEADS = 4
ML_QK = 64
ML_V = 128
ML_QK_W = ML_HEADS * ML_QK
ML_WIDTH = ML_HEADS * ML_V
GATE_CAP = 15.0

SSM_HEADS = 16
SSM_HEADDIM = 64
SSM_WIDTH = SSM_HEADS * SSM_HEADDIM
SSM_STATE = 128
SSM_GROUPS = 4
SSM_HPG = SSM_HEADS // SSM_GROUPS
SSM_BC = SSM_GROUPS * SSM_STATE
CONV_W = 4

D_MIX = HG_WIDTH + ML_WIDTH + SSM_WIDTH

N_GROUPS_MOE = 4
EXPERTS_PER_GROUP = 8
N_EXPERTS = N_GROUPS_MOE * EXPERTS_PER_GROUP
TOP_K = 2
D_EXPERT = 512
MOE_BM = 128

C_HG = 0
C_SZ = 2048
C_SX = 3072
C_MV = 5120
C_MQ = 6144
C_SMALL = 6656
N_PROJ = 6912
LANE_MI = 0
LANE_MF = ML_HEADS
LANE_DT = 2 * ML_HEADS

VMEM_LIMIT = 56 * 1024 * 1024


def _cparams(sem):
    return pltpu.CompilerParams(dimension_semantics=sem, vmem_limit_bytes=VMEM_LIMIT)


def _row_tile(n, target, mult=16):
    best = None
    for t in range(mult, min(n, target) + 1, mult):
        if n % t == 0:
            best = t
    assert best is not None, (n, target, mult)
    return best


def _split3(x):
    hi = x.astype(BF16)
    r = x - hi.astype(F32)
    mid = r.astype(BF16)
    lo = (r - mid.astype(F32)).astype(BF16)
    return hi, mid, lo


def _dot(a, b):
    return jnp.dot(a, b, preferred_element_type=F32)


def _sel_dot(sel, x):
    hi, mid, lo = _split3(x)
    return _dot(sel, hi) + _dot(sel, mid) + _dot(sel, lo)


def _dot_sel(x, sel):
    hi, mid, lo = _split3(x)
    return _dot(hi, sel) + _dot(mid, sel) + _dot(lo, sel)


def _dot_nt(a, b):
    return lax.dot_general(a, b, (((1,), (1,)), ((), ())), preferred_element_type=F32)


def _dot_tn(a, b):
    return lax.dot_general(a, b, (((0,), (0,)), ((), ())), preferred_element_type=F32)


def _log_sigmoid(x):
    return jnp.minimum(x, 0.0) - jnp.log1p(jnp.exp(-jnp.abs(x)))


def _sigmoid(x):
    return 1.0 / (1.0 + jnp.exp(-x))


def _silu(x):
    return x * _sigmoid(x)


SLAB = 8
U32 = jnp.uint32
HI_MASK = 0xFFFF0000


def _bf16_bits(x):
    return lax.bitcast_convert_type(x.astype(BF16).astype(F32), U32)


def _pack_slab(ref, val, lead=(), j0=0):
    rows = val.shape[0]
    for jj in range(val.shape[1] // 256):
        lo = _bf16_bits(val[:, jj * 256:jj * 256 + 128])
        hi = _bf16_bits(val[:, jj * 256 + 128:(jj + 1) * 256])
        ref[(*lead, pl.ds(j0 + jj, rows, stride=SLAB), slice(None))] = (lo >> 16) | (hi & U32(HI_MASK))


def _unpack_slab(ref, j, rows, lead=()):
    w = ref[(*lead, pl.ds(j, rows, stride=SLAB), slice(None))]
    return (lax.bitcast_convert_type(w << 16, F32), lax.bitcast_convert_type(w & U32(HI_MASK), F32))


def _combine_rows(h_ref, c0_ref, c1_ref, wt_ref):
    rows = h_ref.shape[0]
    wt = wt_ref[...]
    w0, w1 = wt[:, 0:1], wt[:, 1:2]
    pieces = []
    for j in range(SLAB):
        a0, b0 = _unpack_slab(c0_ref, j, rows)
        a1, b1 = _unpack_slab(c1_ref, j, rows)
        pieces += [w0 * a0 + w1 * a1, w0 * b0 + w1 * b1]
    return h_ref[...] + jnp.concatenate(pieces, axis=1)


def _norm_kernel(*refs, combine, write_h):
    if combine:
        h_ref, c0_ref, c1_ref, wt_ref, nw_ref = refs[:5]
        outs = refs[5:]
        h = _combine_rows(h_ref, c0_ref, c1_ref, wt_ref)
    else:
        h_ref, nw_ref = refs[:2]
        outs = refs[2:]
        h = h_ref[...]
    if write_h:
        outs[0][...] = h
    ms = jnp.mean(h * h, axis=-1, keepdims=True)
    u_ref = outs[-1]
    u_ref[...] = (h * lax.rsqrt(ms + EPS) * nw_ref[...]).astype(u_ref.dtype)


def _norm_call(h, nw, contrib=None, wts=None, *, write_h, u_dtype, tm_target=264):
    t, d = h.shape
    tm = _row_tile(t, tm_target)
    combine = contrib is not None
    row = lambda i: (i, 0)
    in_specs = [pl.BlockSpec((tm, d), row)]
    args = [h]
    if combine:
        in_specs += [pl.BlockSpec((tm * SLAB, 128), row), pl.BlockSpec((tm * SLAB, 128), lambda i: (t // tm + i, 0)),
                     pl.BlockSpec((tm, 2), row)]
        args += [contrib, contrib, wts]
    in_specs.append(pl.BlockSpec((1, d), lambda i: (0, 0)))
    args.append(nw.reshape(1, d))
    out_shape, out_specs = [], []
    if write_h:
        out_shape.append(jax.ShapeDtypeStruct((t, d), F32))
        out_specs.append(pl.BlockSpec((tm, d), row))
    out_shape.append(jax.ShapeDtypeStruct((t, d), u_dtype))
    out_specs.append(pl.BlockSpec((tm, d), row))
    return pl.pallas_call(
        functools.partial(_norm_kernel, combine=combine, write_h=write_h),
        grid=(t // tm,), in_specs=in_specs, out_specs=out_specs, out_shape=out_shape,
        compiler_params=_cparams(("arbitrary",)), name="combine_norm",
    )(*args)


def _final_kernel(h_ref, c0_ref, c1_ref, wt_ref, nw_ref, o_ref):
    h = _combine_rows(h_ref, c0_ref, c1_ref, wt_ref)
    ms = jnp.mean(h * h, axis=-1, keepdims=True)
    o_ref[...] = h * lax.rsqrt(ms + EPS) * nw_ref[...]


def _final_call(h, contrib, wts, nw, batch, p):
    t, d = h.shape
    seq = p - CHUNK
    tm = _row_tile(seq, 256)
    n_out = seq // tm
    row0 = lambda b, i: b * p + CHUNK + i * tm
    src = lambda b, i: (pl.multiple_of(row0(b, i), CHUNK), 0)
    slab0 = lambda b, i: (pl.multiple_of(row0(b, i) * SLAB, CHUNK), 0)
    slab1 = lambda b, i: (pl.multiple_of((t + row0(b, i)) * SLAB, CHUNK), 0)
    win = lambda r, w, fn: pl.BlockSpec((pl.Element(r), pl.Element(w)), fn)
    return pl.pallas_call(
        _final_kernel, grid=(batch, n_out),
        in_specs=[win(tm, d, src), win(tm * SLAB, 128, slab0), win(tm * SLAB, 128, slab1), win(tm, 2, src),
                  pl.BlockSpec((1, d), lambda b, i: (0, 0))],
        out_specs=pl.BlockSpec((None, tm, d), lambda b, i: (b, i, 0)),
        out_shape=jax.ShapeDtypeStruct((batch, seq, d), F32),
        compiler_params=_cparams(("arbitrary", "arbitrary")), name="final_norm",
    )(h, contrib, contrib, wts, nw.reshape(1, d))


def _matmul_kernel(x_ref, w_ref, o_ref):
    o_ref[...] = _dot(x_ref[...], w_ref[...])


def _inproj_call(u, w, tm_target=1056, tn=768):
    t, d = u.shape
    n = w.shape[1]
    tm = _row_tile(t, tm_target)
    assert n % tn == 0
    return pl.pallas_call(
        _matmul_kernel, grid=(t // tm, n // tn),
        in_specs=[pl.BlockSpec((tm, d), lambda i, j: (i, 0)), pl.BlockSpec((d, tn), lambda i, j: (0, j))],
        out_specs=pl.BlockSpec((tm, tn), lambda i, j: (i, j)),
        out_shape=jax.ShapeDtypeStruct((t, n), F32),
        compiler_params=_cparams(("arbitrary", "arbitrary")), name="in_proj",
    )(u, w)


def _hgrn2_kernel(q_ref, f_ref, i_ref, g_ref, par_ref, o_ref, st_ref, *, rows):
    s = pl.program_id(1)

    @pl.when(s == 0)
    def _():
        st_ref[...] = jnp.zeros_like(st_ref)

    c = HG_CHUNK
    ones = jnp.ones((HG_KDIM, 128), BF16)
    rid = lax.broadcasted_iota(jnp.int32, (c, 128), 0)
    scale = HG_KDIM ** -0.5

    hc = c // 2

    def chunk(ci, carry):
        r0 = pl.multiple_of(ci * c, c)
        pad = (s * rows + r0 + rid) < LEAD_PAD

        def front(h):
            cols = slice(h * 128, (h + 1) * 128)
            a_lb = par_ref[0:1, cols]
            b_lb = par_ref[1:2, cols]
            c_lb = par_ref[2:3, cols]
            z = f_ref[pl.ds(r0, c), cols]
            sg = _sigmoid(z)
            f = a_lb + b_lb * sg
            log_f = jnp.where(pad, 0.0, jnp.log(f))
            k = jnp.where(pad, 0.0, b_lb * (1.0 - sg) - c_lb)
            q = q_ref[pl.ds(r0, c), cols] * scale
            v = i_ref[pl.ds(r0, c), cols]
            cum = log_f
            for sh in (1, 2, 4, 8):
                cum = cum + jnp.where(rid >= sh, pltpu.roll(cum, sh, axis=0), 0.0)
            parts = []
            for s_ in range(c):
                lo = 0 if s_ < hc else hc
                rel = jnp.where(rid[lo:] >= s_, cum[lo:] - cum[s_:s_ + 1, :], NEG)
                parts.append(q[lo:] * (k[s_:s_ + 1, :] * jnp.exp(rel)))
            sc = _dot(jnp.concatenate(parts, axis=0).astype(BF16), ones)
            st = st_ref[h]
            o_inter = _dot_nt((q * jnp.exp(cum)).astype(BF16), st.astype(BF16))
            last = cum[c - 1:c, :]
            kd = (k * jnp.exp(last - cum)).astype(BF16)
            st_ref[h] = st * jnp.exp(last) + _dot_tn(v.astype(BF16), kd)
            return sc, o_inter, v

        def back(h, sc, o_inter, v):
            cols = slice(h * 128, (h + 1) * 128)
            o_top = o_inter[:hc]
            o_bot = o_inter[hc:]
            for s_ in range(hc):
                o_top = o_top + sc[s_ * c:s_ * c + hc, :] * v[s_:s_ + 1, :]
                o_bot = o_bot + sc[s_ * c + hc:(s_ + 1) * c, :] * v[s_:s_ + 1, :]
            for s_ in range(hc, c):
                r_ = hc * c + (s_ - hc) * hc
                o_bot = o_bot + sc[r_:r_ + hc, :] * v[s_:s_ + 1, :]
            o = jnp.concatenate([o_top, o_bot], axis=0)
            ms = jnp.mean(o * o, axis=-1, keepdims=True)
            g = g_ref[pl.ds(r0, c), cols]
            o_ref[pl.ds(r0, c), cols] = (o * lax.rsqrt(ms + EPS) * par_ref[3:4, cols] * _silu(g)).astype(o_ref.dtype)

        pending = front(0)
        for h in range(1, HG_HEADS):
            nxt = front(h)
            back(h - 1, *pending)
            pending = nxt
        back(HG_HEADS - 1, *pending)
        return carry

    n_chunks = rows // c
    lax.fori_loop(0, n_chunks, chunk, 0, unroll=3 if n_chunks % 3 == 0 else 1)


def _hgrn2_call(proj, par, batch, p):
    t = proj.shape[0]
    rows = _row_tile(p, 528)
    nb = p // rows
    w = HG_KEY
    blk = lambda j: pl.BlockSpec((rows, w), lambda b, s, j=j: (b * nb + s, C_HG // w + j))
    return pl.pallas_call(
        functools.partial(_hgrn2_kernel, rows=rows), grid=(batch, nb),
        in_specs=[blk(0), blk(1), blk(2), blk(3), pl.BlockSpec((8, w), lambda b, s: (0, 0))],
        out_specs=pl.BlockSpec((rows, w), lambda b, s: (b * nb + s, 0)),
        out_shape=jax.ShapeDtypeStruct((t, HG_WIDTH), BF16),
        scratch_shapes=[pltpu.VMEM((HG_HEADS, 128, HG_KDIM), F32)],
        compiler_params=_cparams(("arbitrary", "arbitrary")), name="hgrn2",
    )(proj, proj, proj, proj, par)


def _mlstm_kernel(v_ref, o_ref, q_ref, k_ref, sm_ref, par_ref, nw_ref, tri_ref, sel_ref, dg_ref,
                  y_ref, c_ref, m_ref):
    s = pl.program_id(0)

    @pl.when(s == 0)
    def _():
        c_ref[...] = jnp.zeros_like(c_ref)
        m_ref[...] = jnp.zeros_like(m_ref)

    n = CHUNK
    rid = lax.broadcasted_iota(jnp.int32, (n, 128), 0)
    pad = (s * n + rid) < LEAD_PAD
    sel_i = sel_ref[0]
    sel_f = sel_ref[1]
    dg = dg_ref[...]
    w = ML_HEADS * n
    tt = lax.broadcasted_iota(jnp.int32, (n, w), 0)
    ss = lax.broadcasted_iota(jnp.int32, (n, w), 1) & (n - 1)
    causal = tt >= ss
    scale = ML_QK ** -0.5
    lane = lax.broadcasted_iota(jnp.int32, (n, 128), 1)
    one_col = jnp.where(lane == 0, 1.0, 0.0).astype(BF16)
    for b in range(v_ref.shape[0]):
        pre = sm_ref[b] + par_ref[0:1, :]
        cap = GATE_CAP * jnp.tanh(pre * (1.0 / GATE_CAP))
        log_i = jnp.where(pad, NEG, cap)
        log_f = jnp.where(pad, 0.0, _log_sigmoid(cap))
        cum = _sel_dot(tri_ref[...], log_f)
        cum_col = _dot_sel(cum, sel_f)
        cum_row = jnp.sum(cum_col * dg, axis=0, keepdims=True)
        li_row = jnp.sum(_dot_sel(log_i, sel_i) * dg, axis=0, keepdims=True)
        dmat = jnp.where(causal, cum_col - cum_row + li_row, NEG)
        hs = range(ML_HEADS)
        sh = [b * ML_HEADS + h for h in hs]
        d_h = [dmat[:, h * n:(h + 1) * n] for h in hs]
        cum_h = [cum[:, LANE_MF + h:LANE_MF + h + 1] for h in hs]
        li_h = [log_i[:, LANE_MI + h:LANE_MI + h + 1] for h in hs]
        m_st = [m_ref[sh[h]:sh[h] + 1, 0:1] for h in hs]
        q = [(q_ref[b, :, h * ML_QK:(h + 1) * ML_QK] * scale).astype(BF16) for h in hs]
        k = [k_ref[b, :, h * ML_QK:(h + 1) * ML_QK] for h in hs]
        v_aug = [jnp.concatenate([v_ref[b, :, h * ML_V:(h + 1) * ML_V].astype(BF16), one_col], axis=1) for h in hs]
        c_prev = [c_ref[sh[h]] for h in hs]
        inter = [cum_h[h] + m_st[h] for h in hs]
        m_t = [jnp.maximum(inter[h], jnp.max(d_h[h], axis=-1, keepdims=True)) for h in hs]
        qk = [_dot_nt(q[h], k[h].astype(BF16)) for h in hs]
        qc = [_dot(q[h], c_prev[h].astype(BF16)) for h in hs]
        pw = [(qk[h] * jnp.exp(d_h[h] - m_t[h])).astype(BF16) for h in hs]
        nd = [_dot(pw[h], v_aug[h]) + jnp.exp(inter[h] - m_t[h]) * qc[h] for h in hs]
        hh = [nd[h][:, :ML_V] / jnp.maximum(jnp.abs(nd[h][:, ML_V:ML_V + 1]), jnp.exp(-m_t[h])) for h in hs]
        tot = [cum_h[h][n - 1:n, :] for h in hs]
        to_end = [tot[h] - cum_h[h] + li_h[h] for h in hs]
        m_loc = [jnp.max(to_end[h], axis=0, keepdims=True) for h in hs]
        kw = [(k[h] * jnp.exp(to_end[h] - m_loc[h])).astype(BF16) for h in hs]
        c_loc = [_dot_tn(kw[h], v_aug[h]) for h in hs]
        m_new = [jnp.maximum(tot[h] + m_st[h], m_loc[h]) for h in hs]
        for h in hs:
            c_ref[sh[h]] = (jnp.exp(tot[h] + m_st[h] - m_new[h]) * c_prev[h]
                            + jnp.exp(m_loc[h] - m_new[h]) * c_loc[h])
            m_ref[sh[h]:sh[h] + 1, :] = jnp.broadcast_to(m_new[h], (1, 128))
        for h in hs:
            ms = jnp.mean(hh[h] * hh[h], axis=-1, keepdims=True)
            cols = slice(h * ML_V, (h + 1) * ML_V)
            y_ref[b, :, cols] = (hh[h] * lax.rsqrt(ms + EPS) * nw_ref[:, cols]
                                 * _sigmoid(o_ref[b, :, cols])).astype(y_ref.dtype)


def _lane_select(lane0, heads, width):
    m = np.zeros((128, heads * width), np.float32)
    for h in range(heads):
        m[lane0 + h, h * width:(h + 1) * width] = 1.0
    return m


def _diag_mask(n, heads):
    return np.tile(np.eye(n, dtype=np.float32), (1, heads))


def _mlstm_call(proj, par, nw, batch, p):
    t = proj.shape[0]
    n = CHUNK
    nc = p // n
    tri = jnp.asarray(np.tril(np.ones((n, n), np.float32)), BF16)
    sel = jnp.asarray(np.stack([_lane_select(LANE_MI, ML_HEADS, n), _lane_select(LANE_MF, ML_HEADS, n)]), BF16)
    dg = jnp.asarray(_diag_mask(n, ML_HEADS), F32)
    proj3 = proj.reshape(batch, p, proj.shape[1])
    blk = lambda w, off: pl.BlockSpec((batch, n, w), lambda s: (0, s, off // w))
    const = lambda shape: pl.BlockSpec(shape, lambda s: (0,) * len(shape))
    m_rows = -(-batch * ML_HEADS // 8) * 8
    y = pl.pallas_call(
        _mlstm_kernel, grid=(nc,),
        in_specs=[blk(ML_WIDTH, C_MV), blk(ML_WIDTH, C_MV + ML_WIDTH), blk(ML_QK_W, C_MQ),
                  blk(ML_QK_W, C_MQ + ML_QK_W), blk(128, C_SMALL),
                  const((8, 128)), const((1, ML_WIDTH)), const((n, n)), const((2, 128, ML_HEADS * n)),
                  const((n, ML_HEADS * n))],
        out_specs=pl.BlockSpec((batch, n, ML_WIDTH), lambda s: (0, s, 0)),
        out_shape=jax.ShapeDtypeStruct((batch, p, ML_WIDTH), BF16),
        scratch_shapes=[pltpu.VMEM((batch * ML_HEADS, ML_QK, 2 * ML_V), F32), pltpu.VMEM((m_rows, 128), F32)],
        compiler_params=_cparams(("arbitrary",)), name="mlstm",
    )(proj3, proj3, proj3, proj3, proj3, par, nw, tri, sel, dg)
    return y.reshape(t, ML_WIDTH)


def _ssd_kernel(z_ref, x_ref, b_ref, c_ref, sm_ref, cw_ref, cb_ref, par_ref, dsk_ref, nw_ref,
                tri_ref, sel_ref, dg_ref, y_ref, xs_ref, bs_ref, cs_ref, st_ref):
    s = pl.program_id(1)
    n = CHUNK
    tail = 8

    @pl.when(s == 0)
    def _():
        st_ref[...] = jnp.zeros_like(st_ref)
        xs_ref[0:tail, :] = jnp.zeros((tail, xs_ref.shape[1]), F32)
        bs_ref[0:tail, :] = jnp.zeros((tail, bs_ref.shape[1]), F32)
        cs_ref[0:tail, :] = jnp.zeros((tail, cs_ref.shape[1]), F32)

    def conv_silu(src_ref, scr_ref, c0, width, rowmask):
        scr_ref[tail:tail + n, :] = src_ref[...]
        acc = cb_ref[:, c0:c0 + width]
        for j in range(CONV_W):
            off = tail - (CONV_W - 1) + j
            acc = acc + cw_ref[j:j + 1, c0:c0 + width] * scr_ref[off:off + n, :]
        scr_ref[0:tail, :] = scr_ref[n:n + tail, :]
        return jnp.where(rowmask, 0.0, _silu(acc))

    def padmask(width):
        return (s * n + lax.broadcasted_iota(jnp.int32, (n, width), 0)) < LEAD_PAD

    x = conv_silu(x_ref, xs_ref, 0, SSM_WIDTH, padmask(SSM_WIDTH))
    bm = conv_silu(b_ref, bs_ref, SSM_WIDTH, SSM_BC, padmask(SSM_BC))
    cm = conv_silu(c_ref, cs_ref, SSM_WIDTH + SSM_BC, SSM_BC, padmask(SSM_BC))

    pre = sm_ref[...] + par_ref[0:1, :]
    dt = jnp.maximum(pre, 0.0) + jnp.log1p(jnp.exp(-jnp.abs(pre)))
    dt = jnp.where(padmask(128), 0.0, dt)
    da = dt * (-jnp.exp(par_ref[1:2, :]))
    cum = _sel_dot(tri_ref[...], da)
    sel = sel_ref[...]
    dt_col = _dot_sel(dt, sel)
    cum_col = _dot_sel(cum, sel)
    cum_row = jnp.sum(cum_col * dg_ref[...], axis=0, keepdims=True)
    w = SSM_WIDTH
    tt = lax.broadcasted_iota(jnp.int32, (n, w), 0)
    ss = lax.broadcasted_iota(jnp.int32, (n, w), 1) & (n - 1)
    decay = jnp.exp(jnp.where(tt >= ss, cum_col - cum_row, NEG))
    xdt = x * dt_col
    last = cum_col[n - 1:n, :]
    wend = (xdt * jnp.exp(last - cum_col)).astype(BF16)
    chunk_decay = jnp.exp(last)
    ecum = jnp.exp(cum_col)
    gw = SSM_HPG * SSM_HEADDIM
    rr = lax.broadcasted_iota(jnp.int32, (gw, gw), 0) // SSM_HEADDIM
    cc = lax.broadcasted_iota(jnp.int32, (gw, gw), 1) // SSM_HEADDIM
    blockdiag = rr == cc
    ys = []
    for g in range(SSM_GROUPS):
        gl = slice(g * gw, (g + 1) * gw)
        sl = slice(g * SSM_STATE, (g + 1) * SSM_STATE)
        cm_g = cm[:, sl].astype(BF16)
        bm_g = bm[:, sl].astype(BF16)
        cb = _dot_nt(cm_g, jnp.concatenate([bm_g] * SSM_HPG, axis=0))
        m = (cb * decay[:, gl]).astype(BF16)
        xdt_g = xdt[:, gl]
        bd = jnp.where(blockdiag, jnp.concatenate([xdt_g] * SSM_HPG, axis=0), 0.0).astype(BF16)
        st = st_ref[g]
        y_g = _dot(m, bd) + ecum[:, gl] * _dot(cm_g, st.astype(BF16))
        st_ref[g] = st * chunk_decay[:, gl] + _dot_tn(bm_g, wend[:, gl])
        ys.append(y_g)
    y = jnp.concatenate(ys, axis=1) + x * dsk_ref[...]
    y = y * _silu(z_ref[...])
    outs = []
    for g in range(SSM_GROUPS):
        gl = slice(g * gw, (g + 1) * gw)
        y_g = y[:, gl]
        ms = jnp.mean(y_g * y_g, axis=-1, keepdims=True)
        outs.append(y_g * lax.rsqrt(ms + EPS))
    y_ref[...] = (jnp.concatenate(outs, axis=1) * nw_ref[...]).astype(y_ref.dtype)


def _ssd_call(proj, cw, cb, par, dsk, nw, batch, p):
    t = proj.shape[0]
    n = CHUNK
    nc = p // n
    tri = jnp.asarray(np.tril(np.ones((n, n), np.float32)), BF16)
    sel = jnp.asarray(_lane_select(LANE_DT, SSM_HEADS, SSM_HEADDIM), BF16)
    dg = jnp.asarray(_diag_mask(n, SSM_HEADS), F32)
    blk = lambda w, off: pl.BlockSpec((n, w), lambda b, s: (b * nc + s, off // w))
    const = lambda shape: pl.BlockSpec(shape, lambda b, s: (0,) * len(shape))
    cch = SSM_WIDTH + 2 * SSM_BC
    return pl.pallas_call(
        _ssd_kernel, grid=(batch, nc),
        in_specs=[blk(SSM_WIDTH, C_SZ), blk(SSM_WIDTH, C_SX), blk(SSM_BC, C_SX + SSM_WIDTH),
                  blk(SSM_BC, C_SX + SSM_WIDTH + SSM_BC), blk(128, C_SMALL),
                  const((CONV_W, cch)), const((1, cch)), const((8, 128)), const((1, SSM_WIDTH)),
                  const((1, SSM_WIDTH)), const((n, n)), const((128, SSM_WIDTH)), const((n, SSM_WIDTH))],
        out_specs=pl.BlockSpec((n, SSM_WIDTH), lambda b, s: (b * nc + s, 0)),
        out_shape=jax.ShapeDtypeStruct((t, SSM_WIDTH), BF16),
        scratch_shapes=[pltpu.VMEM((n + 8, SSM_WIDTH), F32), pltpu.VMEM((n + 8, SSM_BC), F32),
                        pltpu.VMEM((n + 8, SSM_BC), F32),
                        pltpu.VMEM((SSM_GROUPS, SSM_STATE, SSM_HPG * SSM_HEADDIM), F32)],
        compiler_params=_cparams(("arbitrary", "arbitrary")), name="ssd",
    )(proj, proj, proj, proj, proj, cw, cb, par, dsk, nw, tri, sel, dg)


LANE_E0 = N_GROUPS_MOE
RT_E, RT_RANK, RT_W = 0, 2, 4


def _first_max(vals, lane):
    m = jnp.max(vals, axis=-1, keepdims=True)
    idx = jnp.min(jnp.where(vals == m, lane, 128), axis=-1, keepdims=True)
    return m, idx


def _outproj_kernel(ya_ref, yb_ref, yc_ref, h_ref, w_ref, nw_ref, rh_ref, rl_ref, rb_ref, tri_ref,
                    hm_ref, u_ref, rt_ref, cnt_ref):
    a0, a1 = HG_WIDTH, HG_WIDTH + ML_WIDTH
    h = h_ref[...]
    h = h + _dot(ya_ref[...], w_ref[0:a0, :])
    h = h + _dot(yb_ref[...], w_ref[a0:a1, :])
    h = h + _dot(yc_ref[...], w_ref[a1:, :])
    hm_ref[...] = h
    ms = jnp.mean(h * h, axis=-1, keepdims=True)
    u = h * lax.rsqrt(ms + EPS) * nw_ref[...]
    _pack_slab(u_ref, u)
    u_hi = u.astype(BF16)
    u_lo = (u - u_hi.astype(F32)).astype(BF16)
    lg = _dot(u_hi, rh_ref[...]) + (_dot(u_lo, rh_ref[...]) + _dot(u_hi, rl_ref[...])) + rb_ref[...]

    tm = lg.shape[0]
    lane = lax.broadcasted_iota(jnp.int32, (tm, 128), 1)
    g_mask = lane < N_GROUPS_MOE
    g_max, g_sel = _first_max(jnp.where(g_mask, lg, NEG), lane)
    g_gate = 1.0 / jnp.sum(jnp.where(g_mask, jnp.exp(lg - g_max), 0.0), axis=-1, keepdims=True)
    lo = LANE_E0 + g_sel * EXPERTS_PER_GROUP
    e_vals = jnp.where((lane >= lo) & (lane < lo + EXPERTS_PER_GROUP), lg, NEG)
    v1, i1 = _first_max(e_vals, lane)
    v2, i2 = _first_max(jnp.where(lane == i1, NEG, e_vals), lane)
    a = jnp.exp(v2 - v1)
    w1 = g_gate / (1.0 + a)
    w2 = w1 * a
    @pl.when(pl.program_id(0) == 0)
    def _():
        cnt_ref[...] = jnp.zeros_like(cnt_ref)

    hit1 = lane == i1
    hit2 = lane == i2
    onehot = jnp.where(hit1 | hit2, 1.0, 0.0)
    before = _dot(tri_ref[...], onehot.astype(BF16)) + cnt_ref[0:1, :]
    r1 = jnp.sum(jnp.where(hit1, before, 0.0), axis=-1, keepdims=True)
    r2 = jnp.sum(jnp.where(hit2, before, 0.0), axis=-1, keepdims=True)
    cnt_ref[...] = cnt_ref[...] + jnp.sum(onehot, axis=0, keepdims=True)
    rec = jnp.zeros((tm, 128), F32)
    for ln, val in ((RT_E, (i1 - LANE_E0).astype(F32)), (RT_E + 1, (i2 - LANE_E0).astype(F32)),
                    (RT_RANK, r1), (RT_RANK + 1, r2), (RT_W, w1), (RT_W + 1, w2)):
        rec = jnp.where(lane == ln, val, rec)
    rt_ref[...] = rec


def _outproj_call(ya, yb, yc, h, w, nw, r_hi, r_lo, r_bias, tm_target=352):
    t, d = h.shape
    tm = _row_tile(t, tm_target)
    tri = jnp.asarray(np.tril(np.ones((tm, tm), np.float32), -1), BF16)
    row = lambda i: (i, 0)
    const = lambda shape: pl.BlockSpec(shape, lambda i: (0, 0))
    return pl.pallas_call(
        _outproj_kernel, grid=(t // tm,),
        in_specs=[pl.BlockSpec((tm, HG_WIDTH), row), pl.BlockSpec((tm, ML_WIDTH), row),
                  pl.BlockSpec((tm, SSM_WIDTH), row), pl.BlockSpec((tm, d), row),
                  const((D_MIX, d)), const((1, d)), const((d, 128)), const((d, 128)), const((1, 128)),
                  const((tm, tm))],
        out_specs=[pl.BlockSpec((tm, d), row), pl.BlockSpec((tm * SLAB, 128), row), pl.BlockSpec((tm, 128), row),
                   const((8, 128))],
        out_shape=[jax.ShapeDtypeStruct((t, d), F32), jax.ShapeDtypeStruct((t * SLAB, 128), U32),
                   jax.ShapeDtypeStruct((t, 128), F32), jax.ShapeDtypeStruct((8, 128), F32)],
        compiler_params=_cparams(("arbitrary",)), name="out_proj_router",
    )(ya, yb, yc, h, w, nw.reshape(1, d), r_hi, r_lo, r_bias, tri)


MOE_RING = 3
WT_FIRST, WT_SLOT, WT_NEXT = 0, 1, 2


def _moe_kernel(be_ref, nu_ref, wt_ref, src0_ref, src1_ref, srcn_ref, dstp_ref, dst_ref, u_hbm,
                w1_hbm, w3_hbm, w2_hbm, o_hbm, xbuf, ybuf, st1, st3, st2, w1b, w3b, w2b, gsem, ssem, wsem,
                *, layer):
    i = pl.program_id(0)
    bm = MOE_BM
    n_used = nu_ref[0]
    xs = i % MOE_RING
    slot = xs
    other = (i + MOE_RING - 1) % MOE_RING

    def slab(idx):
        return pl.ds(pl.multiple_of(idx * SLAB, SLAB), SLAB)

    def row_in(tok, r, sl):
        return pltpu.make_async_copy(u_hbm.at[slab(tok), :], xbuf.at[sl, slab(r), :], gsem.at[sl])

    def row_out(r, row, sl):
        return pltpu.make_async_copy(ybuf.at[sl, slab(r), :], o_hbm.at[slab(row), :], ssem.at[sl])

    def gather(idx_ref, sl):
        def body(g, c):
            for j in range(8):
                r = g * 8 + j
                row_in(idx_ref[0, r], r, sl).start(priority=j % 2)
            return c
        lax.fori_loop(0, bm // 8, body, 0)

    def scatter(sl):
        def body(g, c):
            for j in range(8):
                r = g * 8 + j
                row_out(r, dst_ref[0, r], sl).start(priority=j % 2)
            return c
        lax.fori_loop(0, bm // 8, body, 0)

    def wait_gather(sl):
        pltpu.make_async_copy(u_hbm.at[pl.ds(0, bm * SLAB), :], xbuf.at[sl], gsem.at[sl]).wait()

    def wait_scatter(sl):
        pltpu.make_async_copy(ybuf.at[sl], o_hbm.at[pl.ds(0, bm * SLAB), :], ssem.at[sl]).wait()

    def weight_copies(e, p):
        return [pltpu.make_async_copy(src.at[layer, e], dst.at[p], wsem.at[p])
                for src, dst in ((w1_hbm, st1), (w3_hbm, st3), (w2_hbm, st2))]

    n_real = o_hbm.shape[0] // SLAB - MOE_RING * bm

    @pl.when(i == 0)
    def _():
        for c in weight_copies(be_ref[0], 0):
            c.start()
        gather(src0_ref, 0)
        gather(src1_ref, 1)
        ybuf[...] = jnp.zeros_like(ybuf)
        for sl in range(2):
            pltpu.make_async_copy(ybuf.at[sl], o_hbm.at[pl.ds((n_real + sl * bm) * SLAB, bm * SLAB), :],
                                  ssem.at[sl]).start()

    @pl.when(i < n_used)
    def _():
        @pl.when(wt_ref[WT_FIRST, i] == 1)
        def _():
            p = wt_ref[WT_SLOT, i]
            nxt = wt_ref[WT_NEXT, i]

            @pl.when(nxt >= 0)
            def _():
                for c in weight_copies(nxt, 1 - p):
                    c.start()

            for c in weight_copies(be_ref[i], p):
                c.wait()
            w1b[...] = st1[p].astype(BF16)
            w3b[...] = st3[p].astype(BF16)
            w2b[...] = st2[p].astype(BF16)

        wait_gather(xs)

        xn = (i + 2) % MOE_RING

        def issue(part, parts=4):
            for r in range(part * bm // parts, (part + 1) * bm // parts):
                row_in(srcn_ref[0, r], r, xn).start(priority=r % 2)
                row_out(r, dstp_ref[0, r], other).start(priority=(r + 1) % 2)

        f = w1b.shape[1]
        h1 = jnp.zeros((bm, f), F32)
        h3 = jnp.zeros((bm, f), F32)
        for j in range(SLAB):
            lo, hi = _unpack_slab(xbuf, j, bm, lead=(xs,))
            xk = jnp.concatenate([lo, hi], axis=1).astype(BF16)
            h1 = h1 + _dot(xk, w1b[j * 256:(j + 1) * 256, :])
            h3 = h3 + _dot(xk, w3b[j * 256:(j + 1) * 256, :])
            if j % 4 == 3:
                issue(j // 4)
        hid = (_silu(h1) * h3).astype(BF16)
        wait_scatter(slot)
        half = SLAB // 2
        _pack_slab(ybuf, _dot(hid, w2b[:, :half * 256]), lead=(slot,))
        issue(2)
        _pack_slab(ybuf, _dot(hid, w2b[:, half * 256:]), lead=(slot,), j0=half)
        issue(3)

    @pl.when(i == n_used - 1)
    def _():
        scatter(slot)
        for k in range(MOE_RING):
            wait_scatter(k)
        wait_gather((i + 1) % MOE_RING)
        wait_gather((i + 2) % MOE_RING)


def _moe_call(block_e, n_used, w_table, src_tok, dst_row, u, w1, w3, w2, layer, n_out_rows):
    d, f = w1.shape[-2:]
    assert d == SLAB * 256
    bm = MOE_BM
    nb = block_e.shape[0]
    src_tok = src_tok.reshape(nb, 1, bm)
    lead = (n_out_rows - bm + jnp.arange(bm, dtype=jnp.int32)).reshape(1, 1, bm)
    dst_row = jnp.concatenate([lead, dst_row.reshape(nb, 1, bm)], axis=0)
    idx = lambda fn: pl.BlockSpec((None, 1, bm), fn, memory_space=pltpu.SMEM)
    hbm = pl.BlockSpec(memory_space=pl.ANY)
    ring = pltpu.VMEM((MOE_RING, bm * SLAB, 128), U32)
    return pl.pallas_call(
        functools.partial(_moe_kernel, layer=layer),
        grid_spec=pltpu.PrefetchScalarGridSpec(
            num_scalar_prefetch=3, grid=(nb,),
            in_specs=[idx(lambda i, *_: (0, 0, 0)),
                      idx(lambda i, *_: (1, 0, 0)),
                      idx(lambda i, *_: (jnp.minimum(i + 2, nb - 1), 0, 0)),
                      idx(lambda i, *_: (i, 0, 0)),
                      idx(lambda i, *_: (i + 1, 0, 0)),
                      hbm, hbm, hbm, hbm],
            out_specs=hbm,
            scratch_shapes=[ring, ring,
                            pltpu.VMEM((2, d, f), F32), pltpu.VMEM((2, d, f), F32), pltpu.VMEM((2, f, d), F32),
                            pltpu.VMEM((d, f), BF16), pltpu.VMEM((d, f), BF16), pltpu.VMEM((f, d), BF16),
                            pltpu.SemaphoreType.DMA((MOE_RING,)), pltpu.SemaphoreType.DMA((MOE_RING,)),
                            pltpu.SemaphoreType.DMA((2,))]),
        out_shape=jax.ShapeDtypeStruct((n_out_rows * SLAB, 128), U32),
        compiler_params=_cparams(("arbitrary",)), name="moe_ffn",
    )(block_e, n_used, w_table, src_tok, src_tok, src_tok, dst_row, dst_row, u, w1, w3, w2)


def _route_tables(rt, cnt, t):
    bm = MOE_BM
    tk = t * TOP_K
    e = rt[:, RT_E:RT_E + TOP_K].astype(jnp.int32)
    rank = rt[:, RT_RANK:RT_RANK + TOP_K].astype(jnp.int32)
    wts = rt[:, RT_W:RT_W + TOP_K]
    counts = cnt[0, LANE_E0:LANE_E0 + N_EXPERTS].astype(jnp.int32)
    padded = (counts + bm - 1) // bm * bm
    pad_ends = jnp.cumsum(padded)
    pad_starts = pad_ends - padded
    dest = (pad_starts[e] + rank).reshape(-1)
    n_blocks = -(-tk // bm) + N_EXPERTS
    n_rows = n_blocks * bm
    inv = jnp.full((n_rows,), -1, jnp.int32).at[dest].set(jnp.arange(tk, dtype=jnp.int32))
    tok, slot = inv // TOP_K, inv % TOP_K
    src_tok = jnp.where(inv >= 0, tok, 0)
    dst_row = jnp.where(inv >= 0, slot * t + tok, tk + (jnp.arange(n_rows, dtype=jnp.int32) % (MOE_RING * bm)))
    block_ids = jnp.arange(n_blocks, dtype=jnp.int32)
    block_e = jnp.minimum(jnp.sum(pad_ends[None, :] <= block_ids[:, None] * bm, axis=1), N_EXPERTS - 1).astype(jnp.int32)
    n_used = (pad_ends[-1] // bm).astype(jnp.int32).reshape(1)
    e_ids = jnp.arange(N_EXPERTS, dtype=jnp.int32)
    has_rows = counts > 0
    at_or_after = lax.cummin(jnp.where(has_rows, e_ids, N_EXPERTS)[::-1])[::-1]
    next_e = jnp.concatenate([at_or_after[1:], jnp.full((1,), N_EXPERTS, jnp.int32)])
    next_e = jnp.where(next_e >= N_EXPERTS, -1, next_e)
    stage = (jnp.cumsum(has_rows.astype(jnp.int32)) - 1) % 2
    first = jnp.concatenate([jnp.ones((1,), bool), block_e[1:] != block_e[:-1]]) & (block_ids < n_used[0])
    w_table = jnp.stack([first.astype(jnp.int32), stage[block_e], next_e[block_e]]).astype(jnp.int32)
    return block_e, n_used, w_table, src_tok, dst_row, wts, tk + MOE_RING * bm


def _lane_row(pairs):
    row = jnp.zeros((128,), F32)
    for lane0, vals in pairs:
        row = row.at[lane0:lane0 + vals.shape[0]].set(vals.astype(F32))
    return row


def kernel(x, meta_tokens, hg_lb_logits, norm_mix_w, w_in, hg_norm_w, ml_b_i, ml_b_f, ml_norm_w,
           ssm_conv_w, ssm_conv_b, ssm_dt_bias, ssm_a_log, ssm_d, ssm_norm_w, w_out, norm_ffn_w,
           moe_w_group, moe_b_group, moe_w_router, moe_b_router, moe_w1, moe_w3, moe_w2, final_norm_w):
    batch, seq, d = x.shape
    depth = w_in.shape[0]
    p = LEAD_PAD + N_META + seq
    t = batch * p
    meta = jnp.broadcast_to(meta_tokens.astype(x.dtype)[None], (batch, N_META, d))
    h = jnp.concatenate([jnp.zeros((batch, LEAD_PAD, d), x.dtype), meta, x], axis=1).reshape(t, d)

    lb_w = jax.nn.softmax(hg_lb_logits.astype(F32), axis=0)
    lower_bounds = jnp.cumsum(lb_w, axis=0) - lb_w[0]

    contrib = wts = None
    for layer in range(depth):
        w = w_in[layer]
        o_mq, o_mv, o_mi, o_sz = HG_KEY * 2 + HG_WIDTH * 2, 2560, 3584, 3592
        o_sx = o_sz + SSM_WIDTH
        o_dt = o_sx + SSM_WIDTH + 2 * SSM_BC
        w_perm = jnp.concatenate([
            w[:, :o_mq], w[:, o_sz:o_sx], w[:, o_sx:o_dt], w[:, o_mv:o_mi], w[:, o_mq:o_mv],
            w[:, o_mi:o_sz], w[:, o_dt:], jnp.zeros((d, N_PROJ - C_SMALL - 2 * ML_HEADS - SSM_HEADS), w.dtype)],
            axis=1).astype(BF16)
        lb = lower_bounds[layer]
        lbf = jnp.maximum(lb, LB_FLOOR)
        hg_par = jnp.zeros((8, HG_KEY), F32).at[0].set(lbf).at[1].set(1.0 - lb).at[2].set(lbf - lb).at[3].set(hg_norm_w[layer])
        ml_par = jnp.zeros((8, 128), F32).at[0].set(_lane_row([(LANE_MI, ml_b_i[layer]), (LANE_MF, ml_b_f[layer])]))
        ss_par = jnp.zeros((8, 128), F32).at[0].set(_lane_row([(LANE_DT, ssm_dt_bias[layer])]))
        ss_par = ss_par.at[1].set(_lane_row([(LANE_DT, ssm_a_log[layer])]))
        dskip = jnp.repeat(ssm_d[layer].astype(F32), SSM_HEADDIM).reshape(1, SSM_WIDTH)
        w_r = jnp.concatenate([moe_w_group[layer],
                               moe_w_router[layer].transpose(1, 0, 2).reshape(d, N_EXPERTS),
                               jnp.zeros((d, 128 - N_GROUPS_MOE - N_EXPERTS), F32)], axis=1)
        r_hi = w_r.astype(BF16)
        r_lo = (w_r - r_hi.astype(F32)).astype(BF16)

        if layer == 0:
            (u,) = _norm_call(h, norm_mix_w[layer], write_h=False, u_dtype=BF16)
        else:
            h, u = _norm_call(h, norm_mix_w[layer], contrib, wts, write_h=True, u_dtype=BF16)
        proj = _inproj_call(u, w_perm)
        ya = _hgrn2_call(proj, hg_par, batch, p)
        yb = _mlstm_call(proj, ml_par, ml_norm_w[layer].reshape(1, ML_WIDTH), batch, p)
        yc = _ssd_call(proj, ssm_conv_w[layer], ssm_conv_b[layer].reshape(1, -1), ss_par, dskip,
                       ssm_norm_w[layer].reshape(1, SSM_WIDTH), batch, p)
        r_bias = _lane_row([(0, moe_b_group[layer]), (LANE_E0, moe_b_router[layer].reshape(-1))]).reshape(1, 128)
        h, u_ffn, rt, cnt = _outproj_call(ya, yb, yc, h, w_out[layer].astype(BF16), norm_ffn_w[layer],
                                          r_hi, r_lo, r_bias)
        block_e, n_used, w_table, src_tok, dst_row, wts, n_out_rows = _route_tables(rt, cnt, t)
        contrib = _moe_call(block_e, n_used, w_table, src_tok, dst_row, u_ffn, moe_w1, moe_w3, moe_w2, layer,
                            n_out_rows)
    return _final_call(h, contrib, wts, final_norm_w, batch, p)
```

```python
import functools

import jax
import jax.numpy as jnp
import numpy as np
from jax import lax
from jax.experimental import pallas as pl
from jax.experimental.pallas import tpu as pltpu

F32 = jnp.float32
BF16 = jnp.bfloat16

D_MODEL = 2048
N_META = 16
CHUNK = 64
HG_CHUNK = 16
LEAD_PAD = CHUNK - N_META
EPS = 1e-6
NEG = -1e30
LB_FLOOR = 1e-30

HG_HEADS = 4
HG_KDIM = 128
HG_KEY = HG_HEADS * HG_KDIM
HG_WIDTH = HG_HEADS * 128

ML_HEADS = 4
ML_QK = 64
ML_V = 128
ML_QK_W = ML_HEADS * ML_QK
ML_WIDTH = ML_HEADS * ML_V
GATE_CAP = 15.0

SSM_HEADS = 16
SSM_HEADDIM = 64
SSM_WIDTH = SSM_HEADS * SSM_HEADDIM
SSM_STATE = 128
SSM_GROUPS = 4
SSM_HPG = SSM_HEADS // SSM_GROUPS
SSM_BC = SSM_GROUPS * SSM_STATE
CONV_W = 4

D_MIX = HG_WIDTH + ML_WIDTH + SSM_WIDTH

N_GROUPS_MOE = 4
EXPERTS_PER_GROUP = 8
N_EXPERTS = N_GROUPS_MOE * EXPERTS_PER_GROUP
TOP_K = 2
D_EXPERT = 512
MOE_BM = 128

C_HG = 0
C_SZ = 2048
C_SX = 3072
C_MV = 5120
C_MQ = 6144
C_SMALL = 6656
N_PROJ = 6912
LANE_MI = 0
LANE_MF = ML_HEADS
LANE_DT = 2 * ML_HEADS

VMEM_LIMIT = 56 * 1024 * 1024


def _cparams(sem):
    return pltpu.CompilerParams(dimension_semantics=sem, vmem_limit_bytes=VMEM_LIMIT)


def _row_tile(n, target, mult=16):
    best = None
    for t in range(mult, min(n, target) + 1, mult):
        if n % t == 0:
            best = t
    assert best is not None, (n, target, mult)
    return best


def _split3(x):
    hi = x.astype(BF16)
    r = x - hi.astype(F32)
    mid = r.astype(BF16)
    lo = (r - mid.astype(F32)).astype(BF16)
    return hi, mid, lo


def _dot(a, b):
    return jnp.dot(a, b, preferred_element_type=F32)


def _sel_dot(sel, x):
    hi, mid, lo = _split3(x)
    return _dot(sel, hi) + _dot(sel, mid) + _dot(sel, lo)


def _dot_sel(x, sel):
    hi, mid, lo = _split3(x)
    return _dot(hi, sel) + _dot(mid, sel) + _dot(lo, sel)


def _dot_nt(a, b):
    return lax.dot_general(a, b, (((1,), (1,)), ((), ())), preferred_element_type=F32)


def _dot_tn(a, b):
    return lax.dot_general(a, b, (((0,), (0,)), ((), ())), preferred_element_type=F32)


def _log_sigmoid(x):
    return jnp.minimum(x, 0.0) - jnp.log1p(jnp.exp(-jnp.abs(x)))


def _sigmoid(x):
    return 1.0 / (1.0 + jnp.exp(-x))


def _silu(x):
    return x * _sigmoid(x)


SLAB = 8
U32 = jnp.uint32
HI_MASK = 0xFFFF0000


def _bf16_bits(x):
    return lax.bitcast_convert_type(x.astype(BF16).astype(F32), U32)


def _pack_slab(ref, val, lead=(), j0=0):
    rows = val.shape[0]
    for jj in range(val.shape[1] // 256):
        lo = _bf16_bits(val[:, jj * 256:jj * 256 + 128])
        hi = _bf16_bits(val[:, jj * 256 + 128:(jj + 1) * 256])
        ref[(*lead, pl.ds(j0 + jj, rows, stride=SLAB), slice(None))] = (lo >> 16) | (hi & U32(HI_MASK))


def _unpack_slab(ref, j, rows, lead=()):
    w = ref[(*lead, pl.ds(j, rows, stride=SLAB), slice(None))]
    return (lax.bitcast_convert_type(w << 16, F32), lax.bitcast_convert_type(w & U32(HI_MASK), F32))


def _combine_rows(h_ref, c0_ref, c1_ref, wt_ref):
    rows = h_ref.shape[0]
    wt = wt_ref[...]
    w0, w1 = wt[:, 0:1], wt[:, 1:2]
    pieces = []
    for j in range(SLAB):
        a0, b0 = _unpack_slab(c0_ref, j, rows)
        a1, b1 = _unpack_slab(c1_ref, j, rows)
        pieces += [w0 * a0 + w1 * a1, w0 * b0 + w1 * b1]
    return h_ref[...] + jnp.concatenate(pieces, axis=1)


def _norm_kernel(*refs, combine, write_h):
    if combine:
        h_ref, c0_ref, c1_ref, wt_ref, nw_ref = refs[:5]
        outs = refs[5:]
        h = _combine_rows(h_ref, c0_ref, c1_ref, wt_ref)
    else:
        h_ref, nw_ref = refs[:2]
        outs = refs[2:]
        h = h_ref[...]
    if write_h:
        outs[0][...] = h
    ms = jnp.mean(h * h, axis=-1, keepdims=True)
    u_ref = outs[-1]
    u_ref[...] = (h * lax.rsqrt(ms + EPS) * nw_ref[...]).astype(u_ref.dtype)


def _norm_call(h, nw, contrib=None, wts=None, *, write_h, u_dtype, tm_target=264):
    t, d = h.shape
    tm = _row_tile(t, tm_target)
    combine = contrib is not None
    row = lambda i: (i, 0)
    in_specs = [pl.BlockSpec((tm, d), row)]
    args = [h]
    if combine:
        in_specs += [pl.BlockSpec((tm * SLAB, 128), row), pl.BlockSpec((tm * SLAB, 128), lambda i: (t // tm + i, 0)),
                     pl.BlockSpec((tm, 2), row)]
        args += [contrib, contrib, wts]
    in_specs.append(pl.BlockSpec((1, d), lambda i: (0, 0)))
    args.append(nw.reshape(1, d))
    out_shape, out_specs = [], []
    if write_h:
        out_shape.append(jax.ShapeDtypeStruct((t, d), F32))
        out_specs.append(pl.BlockSpec((tm, d), row))
    out_shape.append(jax.ShapeDtypeStruct((t, d), u_dtype))
    out_specs.append(pl.BlockSpec((tm, d), row))
    return pl.pallas_call(
        functools.partial(_norm_kernel, combine=combine, write_h=write_h),
        grid=(t // tm,), in_specs=in_specs, out_specs=out_specs, out_shape=out_shape,
        compiler_params=_cparams(("arbitrary",)), name="combine_norm",
    )(*args)


def _final_kernel(h_ref, c0_ref, c1_ref, wt_ref, nw_ref, o_ref):
    h = _combine_rows(h_ref, c0_ref, c1_ref, wt_ref)
    ms = jnp.mean(h * h, axis=-1, keepdims=True)
    o_ref[...] = h * lax.rsqrt(ms + EPS) * nw_ref[...]


def _final_call(h, contrib, wts, nw, batch, p):
    t, d = h.shape
    seq = p - CHUNK
    tm = _row_tile(seq, 256)
    n_out = seq // tm
    row0 = lambda b, i: b * p + CHUNK + i * tm
    src = lambda b, i: (pl.multiple_of(row0(b, i), CHUNK), 0)
    slab0 = lambda b, i: (pl.multiple_of(row0(b, i) * SLAB, CHUNK), 0)
    slab1 = lambda b, i: (pl.multiple_of((t + row0(b, i)) * SLAB, CHUNK), 0)
    win = lambda r, w, fn: pl.BlockSpec((pl.Element(r), pl.Element(w)), fn)
    return pl.pallas_call(
        _final_kernel, grid=(batch, n_out),
        in_specs=[win(tm, d, src), win(tm * SLAB, 128, slab0), win(tm * SLAB, 128, slab1), win(tm, 2, src),
                  pl.BlockSpec((1, d), lambda b, i: (0, 0))],
        out_specs=pl.BlockSpec((None, tm, d), lambda b, i: (b, i, 0)),
        out_shape=jax.ShapeDtypeStruct((batch, seq, d), F32),
        compiler_params=_cparams(("arbitrary", "arbitrary")), name="final_norm",
    )(h, contrib, contrib, wts, nw.reshape(1, d))


def _matmul_kernel(x_ref, w_ref, o_ref):
    o_ref[...] = _dot(x_ref[...], w_ref[...])


def _inproj_call(u, w, tm_target=1056, tn=768):
    t, d = u.shape
    n = w.shape[1]
    tm = _row_tile(t, tm_target)
    assert n % tn == 0
    return pl.pallas_call(
        _matmul_kernel, grid=(t // tm, n // tn),
        in_specs=[pl.BlockSpec((tm, d), lambda i, j: (i, 0)), pl.BlockSpec((d, tn), lambda i, j: (0, j))],
        out_specs=pl.BlockSpec((tm, tn), lambda i, j: (i, j)),
        out_shape=jax.ShapeDtypeStruct((t, n), F32),
        compiler_params=_cparams(("arbitrary", "arbitrary")), name="in_proj",
    )(u, w)


def _hgrn2_kernel(q_ref, f_ref, i_ref, g_ref, par_ref, o_ref, st_ref, *, rows):
    s = pl.program_id(1)

    @pl.when(s == 0)
    def _():
        st_ref[...] = jnp.zeros_like(st_ref)

    c = HG_CHUNK
    ones = jnp.ones((HG_KDIM, 128), BF16)
    rid = lax.broadcasted_iota(jnp.int32, (c, 128), 0)
    scale = HG_KDIM ** -0.5

    hc = c // 2

    def chunk(ci, carry):
        r0 = pl.multiple_of(ci * c, c)
        pad = (s * rows + r0 + rid) < LEAD_PAD

        def front(h):
            cols = slice(h * 128, (h + 1) * 128)
            a_lb = par_ref[0:1, cols]
            b_lb = par_ref[1:2, cols]
            c_lb = par_ref[2:3, cols]
            z = f_ref[pl.ds(r0, c), cols]
            sg = _sigmoid(z)
            f = a_lb + b_lb * sg
            log_f = jnp.where(pad, 0.0, jnp.log(f))
            k = jnp.where(pad, 0.0, b_lb * (1.0 - sg) - c_lb)
            q = q_ref[pl.ds(r0, c), cols] * scale
            v = i_ref[pl.ds(r0, c), cols]
            cum = log_f
            for sh in (1, 2, 4, 8):
                cum = cum + jnp.where(rid >= sh, pltpu.roll(cum, sh, axis=0), 0.0)
            parts = []
            for s_ in range(c):
                lo = 0 if s_ < hc else hc
                rel = jnp.where(rid[lo:] >= s_, cum[lo:] - cum[s_:s_ + 1, :], NEG)
                parts.append(q[lo:] * (k[s_:s_ + 1, :] * jnp.exp(rel)))
            sc = _dot(jnp.concatenate(parts, axis=0).astype(BF16), ones)
            st = st_ref[h]
            o_inter = _dot_nt((q * jnp.exp(cum)).astype(BF16), st.astype(BF16))
            last = cum[c - 1:c, :]
            kd = (k * jnp.exp(last - cum)).astype(BF16)
            st_ref[h] = st * jnp.exp(last) + _dot_tn(v.astype(BF16), kd)
            return sc, o_inter, v

        def back(h, sc, o_inter, v):
            cols = slice(h * 128, (h + 1) * 128)
            o_top = o_inter[:hc]
            o_bot = o_inter[hc:]
            for s_ in range(hc):
                o_top = o_top + sc[s_ * c:s_ * c + hc, :] * v[s_:s_ + 1, :]
                o_bot = o_bot + sc[s_ * c + hc:(s_ + 1) * c, :] * v[s_:s_ + 1, :]
            for s_ in range(hc, c):
                r_ = hc * c + (s_ - hc) * hc
                o_bot = o_bot + sc[r_:r_ + hc, :] * v[s_:s_ + 1, :]
            o = jnp.concatenate([o_top, o_bot], axis=0)
            ms = jnp.mean(o * o, axis=-1, keepdims=True)
            g = g_ref[pl.ds(r0, c), cols]
            o_ref[pl.ds(r0, c), cols] = (o * lax.rsqrt(ms + EPS) * par_ref[3:4, cols] * _silu(g)).astype(o_ref.dtype)

        pending = front(0)
        for h in range(1, HG_HEADS):
            nxt = front(h)
            back(h - 1, *pending)
            pending = nxt
        back(HG_HEADS - 1, *pending)
        return carry

    n_chunks = rows // c
    lax.fori_loop(0, n_chunks, chunk, 0, unroll=3 if n_chunks % 3 == 0 else 1)


def _hgrn2_call(proj, par, batch, p):
    t = proj.shape[0]
    rows = _row_tile(p, 528)
    nb = p // rows
    w = HG_KEY
    blk = lambda j: pl.BlockSpec((rows, w), lambda b, s, j=j: (b * nb + s, C_HG // w + j))
    return pl.pallas_call(
        functools.partial(_hgrn2_kernel, rows=rows), grid=(batch, nb),
        in_specs=[blk(0), blk(1), blk(2), blk(3), pl.BlockSpec((8, w), lambda b, s: (0, 0))],
        out_specs=pl.BlockSpec((rows, w), lambda b, s: (b * nb + s, 0)),
        out_shape=jax.ShapeDtypeStruct((t, HG_WIDTH), BF16),
        scratch_shapes=[pltpu.VMEM((HG_HEADS, 128, HG_KDIM), F32)],
        compiler_params=_cparams(("arbitrary", "arbitrary")), name="hgrn2",
    )(proj, proj, proj, proj, par)


def _mlstm_kernel(v_ref, o_ref, q_ref, k_ref, sm_ref, par_ref, nw_ref, tri_ref, sel_ref, dg_ref,
                  y_ref, c_ref, m_ref):
    s = pl.program_id(0)

    @pl.when(s == 0)
    def _():
        c_ref[...] = jnp.zeros_like(c_ref)
        m_ref[...] = jnp.zeros_like(m_ref)

    n = CHUNK
    rid = lax.broadcasted_iota(jnp.int32, (n, 128), 0)
    pad = (s * n + rid) < LEAD_PAD
    sel_i = sel_ref[0]
    sel_f = sel_ref[1]
    dg = dg_ref[...]
    w = ML_HEADS * n
    tt = lax.broadcasted_iota(jnp.int32, (n, w), 0)
    ss = lax.broadcasted_iota(jnp.int32, (n, w), 1) & (n - 1)
    causal = tt >= ss
    scale = ML_QK ** -0.5
    lane = lax.broadcasted_iota(jnp.int32, (n, 128), 1)
    one_col = jnp.where(lane == 0, 1.0, 0.0).astype(BF16)
    for b in range(v_ref.shape[0]):
        pre = sm_ref[b] + par_ref[0:1, :]
        cap = GATE_CAP * jnp.tanh(pre * (1.0 / GATE_CAP))
        log_i = jnp.where(pad, NEG, cap)
        log_f = jnp.where(pad, 0.0, _log_sigmoid(cap))
        cum = _sel_dot(tri_ref[...], log_f)
        cum_col = _dot_sel(cum, sel_f)
        cum_row = jnp.sum(cum_col * dg, axis=0, keepdims=True)
        li_row = jnp.sum(_dot_sel(log_i, sel_i) * dg, axis=0, keepdims=True)
        dmat = jnp.where(causal, cum_col - cum_row + li_row, NEG)
        hs = range(ML_HEADS)
        sh = [b * ML_HEADS + h for h in hs]
        d_h = [dmat[:, h * n:(h + 1) * n] for h in hs]
        cum_h = [cum[:, LANE_MF + h:LANE_MF + h + 1] for h in hs]
        li_h = [log_i[:, LANE_MI + h:LANE_MI + h + 1] for h in hs]
        m_st = [m_ref[sh[h]:sh[h] + 1, 0:1] for h in hs]
        q = [(q_ref[b, :, h * ML_QK:(h + 1) * ML_QK] * scale).astype(BF16) for h in hs]
        k = [k_ref[b, :, h * ML_QK:(h + 1) * ML_QK] for h in hs]
        v_aug = [jnp.concatenate([v_ref[b, :, h * ML_V:(h + 1) * ML_V].astype(BF16), one_col], axis=1) for h in hs]
        c_prev = [c_ref[sh[h]] for h in hs]
        inter = [cum_h[h] + m_st[h] for h in hs]
        m_t = [jnp.maximum(inter[h], jnp.max(d_h[h], axis=-1, keepdims=True)) for h in hs]
        qk = [_dot_nt(q[h], k[h].astype(BF16)) for h in hs]
        qc = [_dot(q[h], c_prev[h].astype(BF16)) for h in hs]
        pw = [(qk[h] * jnp.exp(d_h[h] - m_t[h])).astype(BF16) for h in hs]
        nd = [_dot(pw[h], v_aug[h]) + jnp.exp(inter[h] - m_t[h]) * qc[h] for h in hs]
        hh = [nd[h][:, :ML_V] / jnp.maximum(jnp.abs(nd[h][:, ML_V:ML_V + 1]), jnp.exp(-m_t[h])) for h in hs]
        tot = [cum_h[h][n - 1:n, :] for h in hs]
        to_end = [tot[h] - cum_h[h] + li_h[h] for h in hs]
        m_loc = [jnp.max(to_end[h], axis=0, keepdims=True) for h in hs]
        kw = [(k[h] * jnp.exp(to_end[h] - m_loc[h])).astype(BF16) for h in hs]
        c_loc = [_dot_tn(kw[h], v_aug[h]) for h in hs]
        m_new = [jnp.maximum(tot[h] + m_st[h], m_loc[h]) for h in hs]
        for h in hs:
            c_ref[sh[h]] = (jnp.exp(tot[h] + m_st[h] - m_new[h]) * c_prev[h]
                            + jnp.exp(m_loc[h] - m_new[h]) * c_loc[h])
            m_ref[sh[h]:sh[h] + 1, :] = jnp.broadcast_to(m_new[h], (1, 128))
        for h in hs:
            ms = jnp.mean(hh[h] * hh[h], axis=-1, keepdims=True)
            cols = slice(h * ML_V, (h + 1) * ML_V)
            y_ref[b, :, cols] = (hh[h] * lax.rsqrt(ms + EPS) * nw_ref[:, cols]
                                 * _sigmoid(o_ref[b, :, cols])).astype(y_ref.dtype)


def _lane_select(lane0, heads, width):
    m = np.zeros((128, heads * width), np.float32)
    for h in range(heads):
        m[lane0 + h, h * width:(h + 1) * width] = 1.0
    return m


def _diag_mask(n, heads):
    return np.tile(np.eye(n, dtype=np.float32), (1, heads))


def _mlstm_call(proj, par, nw, batch, p):
    t = proj.shape[0]
    n = CHUNK
    nc = p // n
    tri = jnp.asarray(np.tril(np.ones((n, n), np.float32)), BF16)
    sel = jnp.asarray(np.stack([_lane_select(LANE_MI, ML_HEADS, n), _lane_select(LANE_MF, ML_HEADS, n)]), BF16)
    dg = jnp.asarray(_diag_mask(n, ML_HEADS), F32)
    proj3 = proj.reshape(batch, p, proj.shape[1])
    blk = lambda w, off: pl.BlockSpec((batch, n, w), lambda s: (0, s, off // w))
    const = lambda shape: pl.BlockSpec(shape, lambda s: (0,) * len(shape))
    m_rows = -(-batch * ML_HEADS // 8) * 8
    y = pl.pallas_call(
        _mlstm_kernel, grid=(nc,),
        in_specs=[blk(ML_WIDTH, C_MV), blk(ML_WIDTH, C_MV + ML_WIDTH), blk(ML_QK_W, C_MQ),
                  blk(ML_QK_W, C_MQ + ML_QK_W), blk(128, C_SMALL),
                  const((8, 128)), const((1, ML_WIDTH)), const((n, n)), const((2, 128, ML_HEADS * n)),
                  const((n, ML_HEADS * n))],
        out_specs=pl.BlockSpec((batch, n, ML_WIDTH), lambda s: (0, s, 0)),
        out_shape=jax.ShapeDtypeStruct((batch, p, ML_WIDTH), BF16),
        scratch_shapes=[pltpu.VMEM((batch * ML_HEADS, ML_QK, 2 * ML_V), F32), pltpu.VMEM((m_rows, 128), F32)],
        compiler_params=_cparams(("arbitrary",)), name="mlstm",
    )(proj3, proj3, proj3, proj3, proj3, par, nw, tri, sel, dg)
    return y.reshape(t, ML_WIDTH)


def _ssd_kernel(z_ref, x_ref, b_ref, c_ref, sm_ref, cw_ref, cb_ref, par_ref, dsk_ref, nw_ref,
                tri_ref, sel_ref, dg_ref, y_ref, xs_ref, bs_ref, cs_ref, st_ref):
    s = pl.program_id(1)
    n = CHUNK
    tail = 8

    @pl.when(s == 0)
    def _():
        st_ref[...] = jnp.zeros_like(st_ref)
        xs_ref[0:tail, :] = jnp.zeros((tail, xs_ref.shape[1]), F32)
        bs_ref[0:tail, :] = jnp.zeros((tail, bs_ref.shape[1]), F32)
        cs_ref[0:tail, :] = jnp.zeros((tail, cs_ref.shape[1]), F32)

    def conv_silu(src_ref, scr_ref, c0, width, rowmask):
        scr_ref[tail:tail + n, :] = src_ref[...]
        acc = cb_ref[:, c0:c0 + width]
        for j in range(CONV_W):
            off = tail - (CONV_W - 1) + j
            acc = acc + cw_ref[j:j + 1, c0:c0 + width] * scr_ref[off:off + n, :]
        scr_ref[0:tail, :] = scr_ref[n:n + tail, :]
        return jnp.where(rowmask, 0.0, _silu(acc))

    def padmask(width):
        return (s * n + lax.broadcasted_iota(jnp.int32, (n, width), 0)) < LEAD_PAD

    x = conv_silu(x_ref, xs_ref, 0, SSM_WIDTH, padmask(SSM_WIDTH))
    bm = conv_silu(b_ref, bs_ref, SSM_WIDTH, SSM_BC, padmask(SSM_BC))
    cm = conv_silu(c_ref, cs_ref, SSM_WIDTH + SSM_BC, SSM_BC, padmask(SSM_BC))

    pre = sm_ref[...] + par_ref[0:1, :]
    dt = jnp.maximum(pre, 0.0) + jnp.log1p(jnp.exp(-jnp.abs(pre)))
    dt = jnp.where(padmask(128), 0.0, dt)
    da = dt * (-jnp.exp(par_ref[1:2, :]))
    cum = _sel_dot(tri_ref[...], da)
    sel = sel_ref[...]
    dt_col = _dot_sel(dt, sel)
    cum_col = _dot_sel(cum, sel)
    cum_row = jnp.sum(cum_col * dg_ref[...], axis=0, keepdims=True)
    w = SSM_WIDTH
    tt = lax.broadcasted_iota(jnp.int32, (n, w), 0)
    ss = lax.broadcasted_iota(jnp.int32, (n, w), 1) & (n - 1)
    decay = jnp.exp(jnp.where(tt >= ss, cum_col - cum_row, NEG))
    xdt = x * dt_col
    last = cum_col[n - 1:n, :]
    wend = (xdt * jnp.exp(last - cum_col)).astype(BF16)
    chunk_decay = jnp.exp(last)
    ecum = jnp.exp(cum_col)
    gw = SSM_HPG * SSM_HEADDIM
    rr = lax.broadcasted_iota(jnp.int32, (gw, gw), 0) // SSM_HEADDIM
    cc = lax.broadcasted_iota(jnp.int32, (gw, gw), 1) // SSM_HEADDIM
    blockdiag = rr == cc
    ys = []
    for g in range(SSM_GROUPS):
        gl = slice(g * gw, (g + 1) * gw)
        sl = slice(g * SSM_STATE, (g + 1) * SSM_STATE)
        cm_g = cm[:, sl].astype(BF16)
        bm_g = bm[:, sl].astype(BF16)
        cb = _dot_nt(cm_g, jnp.concatenate([bm_g] * SSM_HPG, axis=0))
        m = (cb * decay[:, gl]).astype(BF16)
        xdt_g = xdt[:, gl]
        bd = jnp.where(blockdiag, jnp.concatenate([xdt_g] * SSM_HPG, axis=0), 0.0).astype(BF16)
        st = st_ref[g]
        y_g = _dot(m, bd) + ecum[:, gl] * _dot(cm_g, st.astype(BF16))
        st_ref[g] = st * chunk_decay[:, gl] + _dot_tn(bm_g, wend[:, gl])
        ys.append(y_g)
    y = jnp.concatenate(ys, axis=1) + x * dsk_ref[...]
    y = y * _silu(z_ref[...])
    outs = []
    for g in range(SSM_GROUPS):
        gl = slice(g * gw, (g + 1) * gw)
        y_g = y[:, gl]
        ms = jnp.mean(y_g * y_g, axis=-1, keepdims=True)
        outs.append(y_g * lax.rsqrt(ms + EPS))
    y_ref[...] = (jnp.concatenate(outs, axis=1) * nw_ref[...]).astype(y_ref.dtype)


def _ssd_call(proj, cw, cb, par, dsk, nw, batch, p):
    t = proj.shape[0]
    n = CHUNK
    nc = p // n
    tri = jnp.asarray(np.tril(np.ones((n, n), np.float32)), BF16)
    sel = jnp.asarray(_lane_select(LANE_DT, SSM_HEADS, SSM_HEADDIM), BF16)
    dg = jnp.asarray(_diag_mask(n, SSM_HEADS), F32)
    blk = lambda w, off: pl.BlockSpec((n, w), lambda b, s: (b * nc + s, off // w))
    const = lambda shape: pl.BlockSpec(shape, lambda b, s: (0,) * len(shape))
    cch = SSM_WIDTH + 2 * SSM_BC
    return pl.pallas_call(
        _ssd_kernel, grid=(batch, nc),
        in_specs=[blk(SSM_WIDTH, C_SZ), blk(SSM_WIDTH, C_SX), blk(SSM_BC, C_SX + SSM_WIDTH),
                  blk(SSM_BC, C_SX + SSM_WIDTH + SSM_BC), blk(128, C_SMALL),
                  const((CONV_W, cch)), const((1, cch)), const((8, 128)), const((1, SSM_WIDTH)),
                  const((1, SSM_WIDTH)), const((n, n)), const((128, SSM_WIDTH)), const((n, SSM_WIDTH))],
        out_specs=pl.BlockSpec((n, SSM_WIDTH), lambda b, s: (b * nc + s, 0)),
        out_shape=jax.ShapeDtypeStruct((t, SSM_WIDTH), BF16),
        scratch_shapes=[pltpu.VMEM((n + 8, SSM_WIDTH), F32), pltpu.VMEM((n + 8, SSM_BC), F32),
                        pltpu.VMEM((n + 8, SSM_BC), F32),
                        pltpu.VMEM((SSM_GROUPS, SSM_STATE, SSM_HPG * SSM_HEADDIM), F32)],
        compiler_params=_cparams(("arbitrary", "arbitrary")), name="ssd",
    )(proj, proj, proj, proj, proj, cw, cb, par, dsk, nw, tri, sel, dg)


LANE_E0 = N_GROUPS_MOE
RT_E, RT_RANK, RT_W = 0, 2, 4


def _first_max(vals, lane):
    m = jnp.max(vals, axis=-1, keepdims=True)
    idx = jnp.min(jnp.where(vals == m, lane, 128), axis=-1, keepdims=True)
    return m, idx


def _outproj_kernel(ya_ref, yb_ref, yc_ref, h_ref, w_ref, nw_ref, rh_ref, rl_ref, rb_ref, tri_ref,
                    hm_ref, u_ref, rt_ref, cnt_ref):
    a0, a1 = HG_WIDTH, HG_WIDTH + ML_WIDTH
    h = h_ref[...]
    h = h + _dot(ya_ref[...], w_ref[0:a0, :])
    h = h + _dot(yb_ref[...], w_ref[a0:a1, :])
    h = h + _dot(yc_ref[...], w_ref[a1:, :])
    hm_ref[...] = h
    ms = jnp.mean(h * h, axis=-1, keepdims=True)
    u = h * lax.rsqrt(ms + EPS) * nw_ref[...]
    _pack_slab(u_ref, u)
    u_hi = u.astype(BF16)
    u_lo = (u - u_hi.astype(F32)).astype(BF16)
    lg = _dot(u_hi, rh_ref[...]) + (_dot(u_lo, rh_ref[...]) + _dot(u_hi, rl_ref[...])) + rb_ref[...]

    tm = lg.shape[0]
    lane = lax.broadcasted_iota(jnp.int32, (tm, 128), 1)
    g_mask = lane < N_GROUPS_MOE
    g_max, g_sel = _first_max(jnp.where(g_mask, lg, NEG), lane)
    g_gate = 1.0 / jnp.sum(jnp.where(g_mask, jnp.exp(lg - g_max), 0.0), axis=-1, keepdims=True)
    lo = LANE_E0 + g_sel * EXPERTS_PER_GROUP
    e_vals = jnp.where((lane >= lo) & (lane < lo + EXPERTS_PER_GROUP), lg, NEG)
    v1, i1 = _first_max(e_vals, lane)
    v2, i2 = _first_max(jnp.where(lane == i1, NEG, e_vals), lane)
    a = jnp.exp(v2 - v1)
    w1 = g_gate / (1.0 + a)
    w2 = w1 * a
    @pl.when(pl.program_id(0) == 0)
    def _():
        cnt_ref[...] = jnp.zeros_like(cnt_ref)

    hit1 = lane == i1
    hit2 = lane == i2
    onehot = jnp.where(hit1 | hit2, 1.0, 0.0)
    before = _dot(tri_ref[...], onehot.astype(BF16)) + cnt_ref[0:1, :]
    r1 = jnp.sum(jnp.where(hit1, before, 0.0), axis=-1, keepdims=True)
    r2 = jnp.sum(jnp.where(hit2, before, 0.0), axis=-1, keepdims=True)
    cnt_ref[...] = cnt_ref[...] + jnp.sum(onehot, axis=0, keepdims=True)
    rec = jnp.zeros((tm, 128), F32)
    for ln, val in ((RT_E, (i1 - LANE_E0).astype(F32)), (RT_E + 1, (i2 - LANE_E0).astype(F32)),
                    (RT_RANK, r1), (RT_RANK + 1, r2), (RT_W, w1), (RT_W + 1, w2)):
        rec = jnp.where(lane == ln, val, rec)
    rt_ref[...] = rec


def _outproj_call(ya, yb, yc, h, w, nw, r_hi, r_lo, r_bias, tm_target=352):
    t, d = h.shape
    tm = _row_tile(t, tm_target)
    tri = jnp.asarray(np.tril(np.ones((tm, tm), np.float32), -1), BF16)
    row = lambda i: (i, 0)
    const = lambda shape: pl.BlockSpec(shape, lambda i: (0, 0))
    return pl.pallas_call(
        _outproj_kernel, grid=(t // tm,),
        in_specs=[pl.BlockSpec((tm, HG_WIDTH), row), pl.BlockSpec((tm, ML_WIDTH), row),
                  pl.BlockSpec((tm, SSM_WIDTH), row), pl.BlockSpec((tm, d), row),
                  const((D_MIX, d)), const((1, d)), const((d, 128)), const((d, 128)), const((1, 128)),
                  const((tm, tm))],
        out_specs=[pl.BlockSpec((tm, d), row), pl.BlockSpec((tm * SLAB, 128), row), pl.BlockSpec((tm, 128), row),
                   const((8, 128))],
        out_shape=[jax.ShapeDtypeStruct((t, d), F32), jax.ShapeDtypeStruct((t * SLAB, 128), U32),
                   jax.ShapeDtypeStruct((t, 128), F32), jax.ShapeDtypeStruct((8, 128), F32)],
        compiler_params=_cparams(("arbitrary",)), name="out_proj_router",
    )(ya, yb, yc, h, w, nw.reshape(1, d), r_hi, r_lo, r_bias, tri)


MOE_RING = 3
WT_FIRST, WT_SLOT, WT_NEXT = 0, 1, 2


TAB_SRC2, TAB_DST_PREV, TAB_DST = 0, 1, 2


def _moe_kernel(be_ref, nu_ref, wt_ref, src01_ref, tab_ref, u_hbm,
                w1_hbm, w3_hbm, w2_hbm, o_hbm, xbuf, ybuf, st1, st3, st2, w1b, w3b, w2b, gsem, ssem, wsem,
                *, layer):
    i = pl.program_id(0)
    bm = MOE_BM
    n_used = nu_ref[0]
    xs = i % MOE_RING
    slot = xs
    other = (i + MOE_RING - 1) % MOE_RING

    def slab(idx):
        return pl.ds(pl.multiple_of(idx * SLAB, SLAB), SLAB)

    def row_in(tok, r, sl):
        return pltpu.make_async_copy(u_hbm.at[slab(tok), :], xbuf.at[sl, slab(r), :], gsem.at[sl])

    def row_out(r, row, sl):
        return pltpu.make_async_copy(ybuf.at[sl, slab(r), :], o_hbm.at[slab(row), :], ssem.at[sl])

    def gather(section, sl):
        def body(g, c):
            for j in range(8):
                r = g * 8 + j
                row_in(src01_ref[0, section * bm + r], r, sl).start(priority=j % 2)
            return c
        lax.fori_loop(0, bm // 8, body, 0)

    def scatter(sl):
        def body(g, c):
            for j in range(8):
                r = g * 8 + j
                row_out(r, tab_ref[0, TAB_DST * bm + r], sl).start(priority=j % 2)
            return c
        lax.fori_loop(0, bm // 8, body, 0)

    def wait_gather(sl):
        pltpu.make_async_copy(u_hbm.at[pl.ds(0, bm * SLAB), :], xbuf.at[sl], gsem.at[sl]).wait()

    def wait_scatter(sl):
        pltpu.make_async_copy(ybuf.at[sl], o_hbm.at[pl.ds(0, bm * SLAB), :], ssem.at[sl]).wait()

    def weight_copies(e, p):
        return [pltpu.make_async_copy(src.at[layer, e], dst.at[p], wsem.at[p])
                for src, dst in ((w1_hbm, st1), (w3_hbm, st3), (w2_hbm, st2))]

    n_real = o_hbm.shape[0] // SLAB - MOE_RING * bm

    @pl.when(i == 0)
    def _():
        for c in weight_copies(be_ref[0], 0):
            c.start()
        gather(0, 0)
        gather(1, 1)
        ybuf[...] = jnp.zeros_like(ybuf)
        for sl in range(2):
            pltpu.make_async_copy(ybuf.at[sl], o_hbm.at[pl.ds((n_real + sl * bm) * SLAB, bm * SLAB), :],
                                  ssem.at[sl]).start()

    @pl.when(i < n_used)
    def _():
        @pl.when(wt_ref[WT_FIRST, i] == 1)
        def _():
            p = wt_ref[WT_SLOT, i]
            nxt = wt_ref[WT_NEXT, i]

            @pl.when(nxt >= 0)
            def _():
                for c in weight_copies(nxt, 1 - p):
                    c.start()

            for c in weight_copies(be_ref[i], p):
                c.wait()
            w1b[...] = st1[p].astype(BF16)
            w3b[...] = st3[p].astype(BF16)
            w2b[...] = st2[p].astype(BF16)

        wait_gather(xs)

        xn = (i + 2) % MOE_RING

        def issue(part, parts=4):
            for r in range(part * bm // parts, (part + 1) * bm // parts):
                row_in(tab_ref[0, TAB_SRC2 * bm + r], r, xn).start(priority=r % 2)
                row_out(r, tab_ref[0, TAB_DST_PREV * bm + r], other).start(priority=(r + 1) % 2)

        f = w1b.shape[1]
        h1 = jnp.zeros((bm, f), F32)
        h3 = jnp.zeros((bm, f), F32)
        for j in range(SLAB):
            lo, hi = _unpack_slab(xbuf, j, bm, lead=(xs,))
            xk = jnp.concatenate([lo, hi], axis=1).astype(BF16)
            h1 = h1 + _dot(xk, w1b[j * 256:(j + 1) * 256, :])
            h3 = h3 + _dot(xk, w3b[j * 256:(j + 1) * 256, :])
            if j % 4 == 3:
                issue(j // 4)
        hid = (_silu(h1) * h3).astype(BF16)
        wait_scatter(slot)
        half = SLAB // 2
        _pack_slab(ybuf, _dot(hid, w2b[:, :half * 256]), lead=(slot,))
        issue(2)
        _pack_slab(ybuf, _dot(hid, w2b[:, half * 256:]), lead=(slot,), j0=half)
        issue(3)

    @pl.when(i == n_used - 1)
    def _():
        scatter(slot)
        for k in range(MOE_RING):
            wait_scatter(k)
        wait_gather((i + 1) % MOE_RING)
        wait_gather((i + 2) % MOE_RING)


def _moe_call(block_e, n_used, w_table, src_tok, dst_row, u, w1, w3, w2, layer, n_out_rows):
    d, f = w1.shape[-2:]
    assert d == SLAB * 256
    bm = MOE_BM
    nb = block_e.shape[0]
    src_tok = src_tok.reshape(nb, bm)
    dst_row = dst_row.reshape(nb, bm)
    lead = (n_out_rows - bm + jnp.arange(bm, dtype=jnp.int32)).reshape(1, bm)
    src_ahead = jnp.concatenate([src_tok[2:], src_tok[-1:], src_tok[-1:]], axis=0)
    dst_prev = jnp.concatenate([lead, dst_row[:-1]], axis=0)
    table = jnp.concatenate([src_ahead, dst_prev, dst_row], axis=1).reshape(nb, 1, 3 * bm)
    src01 = src_tok[:2].reshape(1, 1, 2 * bm)
    idx = lambda w, fn: pl.BlockSpec((None, 1, w), fn, memory_space=pltpu.SMEM)
    hbm = pl.BlockSpec(memory_space=pl.ANY)
    ring = pltpu.VMEM((MOE_RING, bm * SLAB, 128), U32)
    return pl.pallas_call(
        functools.partial(_moe_kernel, layer=layer),
        grid_spec=pltpu.PrefetchScalarGridSpec(
            num_scalar_prefetch=3, grid=(nb,),
            in_specs=[idx(2 * bm, lambda i, *_: (0, 0, 0)), idx(3 * bm, lambda i, *_: (i, 0, 0)),
                      hbm, hbm, hbm, hbm],
            out_specs=hbm,
            scratch_shapes=[ring, ring,
                            pltpu.VMEM((2, d, f), F32), pltpu.VMEM((2, d, f), F32), pltpu.VMEM((2, f, d), F32),
                            pltpu.VMEM((d, f), BF16), pltpu.VMEM((d, f), BF16), pltpu.VMEM((f, d), BF16),
                            pltpu.SemaphoreType.DMA((MOE_RING,)), pltpu.SemaphoreType.DMA((MOE_RING,)),
                            pltpu.SemaphoreType.DMA((2,))]),
        out_shape=jax.ShapeDtypeStruct((n_out_rows * SLAB, 128), U32),
        compiler_params=_cparams(("arbitrary",)), name="moe_ffn",
    )(block_e, n_used, w_table, src01, table, u, w1, w3, w2)


def _route_tables(rt, cnt, t):
    bm = MOE_BM
    tk = t * TOP_K
    e = rt[:, RT_E:RT_E + TOP_K].astype(jnp.int32)
    rank = rt[:, RT_RANK:RT_RANK + TOP_K].astype(jnp.int32)
    wts = rt[:, RT_W:RT_W + TOP_K]
    counts = cnt[0, LANE_E0:LANE_E0 + N_EXPERTS].astype(jnp.int32)
    padded = (counts + bm - 1) // bm * bm
    pad_ends = jnp.cumsum(padded)
    pad_starts = pad_ends - padded
    e_ids = jnp.arange(N_EXPERTS, dtype=jnp.int32)

    def lookup(table, idx):
        return jnp.sum(jnp.where(idx[..., None] == e_ids, table, 0), axis=-1)

    dest = (lookup(pad_starts, e) + rank).reshape(-1)
    n_blocks = -(-tk // bm) + N_EXPERTS
    n_rows = n_blocks * bm
    inv = jnp.full((n_rows,), -1, jnp.int32).at[dest].set(jnp.arange(tk, dtype=jnp.int32))
    tok, slot = inv // TOP_K, inv % TOP_K
    src_tok = jnp.where(inv >= 0, tok, 0)
    dst_row = jnp.where(inv >= 0, slot * t + tok, tk + (jnp.arange(n_rows, dtype=jnp.int32) % (MOE_RING * bm)))
    block_ids = jnp.arange(n_blocks, dtype=jnp.int32)
    block_e = jnp.minimum(jnp.sum(pad_ends[None, :] <= block_ids[:, None] * bm, axis=1), N_EXPERTS - 1).astype(jnp.int32)
    n_used = (pad_ends[-1] // bm).astype(jnp.int32).reshape(1)
    has_rows = counts > 0
    at_or_after = lax.cummin(jnp.where(has_rows, e_ids, N_EXPERTS)[::-1])[::-1]
    next_e = jnp.concatenate([at_or_after[1:], jnp.full((1,), N_EXPERTS, jnp.int32)])
    next_e = jnp.where(next_e >= N_EXPERTS, -1, next_e)
    stage = (jnp.cumsum(has_rows.astype(jnp.int32)) - 1) % 2
    first = jnp.concatenate([jnp.ones((1,), bool), block_e[1:] != block_e[:-1]]) & (block_ids < n_used[0])
    w_table = jnp.stack([first.astype(jnp.int32), lookup(stage, block_e), lookup(next_e, block_e)]).astype(jnp.int32)
    return block_e, n_used, w_table, src_tok, dst_row, wts, tk + MOE_RING * bm


def _lane_row(pairs):
    row = jnp.zeros((128,), F32)
    for lane0, vals in pairs:
        row = row.at[lane0:lane0 + vals.shape[0]].set(vals.astype(F32))
    return row


def kernel(x, meta_tokens, hg_lb_logits, norm_mix_w, w_in, hg_norm_w, ml_b_i, ml_b_f, ml_norm_w,
           ssm_conv_w, ssm_conv_b, ssm_dt_bias, ssm_a_log, ssm_d, ssm_norm_w, w_out, norm_ffn_w,
           moe_w_group, moe_b_group, moe_w_router, moe_b_router, moe_w1, moe_w3, moe_w2, final_norm_w):
    batch, seq, d = x.shape
    depth = w_in.shape[0]
    p = LEAD_PAD + N_META + seq
    t = batch * p
    meta = jnp.broadcast_to(meta_tokens.astype(x.dtype)[None], (batch, N_META, d))
    h = jnp.concatenate([jnp.zeros((batch, LEAD_PAD, d), x.dtype), meta, x], axis=1).reshape(t, d)

    lb_w = jax.nn.softmax(hg_lb_logits.astype(F32), axis=0)
    lower_bounds = jnp.cumsum(lb_w, axis=0) - lb_w[0]

    contrib = wts = None
    for layer in range(depth):
        w = w_in[layer]
        o_mq, o_mv, o_mi, o_sz = HG_KEY * 2 + HG_WIDTH * 2, 2560, 3584, 3592
        o_sx = o_sz + SSM_WIDTH
        o_dt = o_sx + SSM_WIDTH + 2 * SSM_BC
        w_perm = jnp.concatenate([
            w[:, :o_mq], w[:, o_sz:o_sx], w[:, o_sx:o_dt], w[:, o_mv:o_mi], w[:, o_mq:o_mv],
            w[:, o_mi:o_sz], w[:, o_dt:], jnp.zeros((d, N_PROJ - C_SMALL - 2 * ML_HEADS - SSM_HEADS), w.dtype)],
            axis=1).astype(BF16)
        lb = lower_bounds[layer]
        lbf = jnp.maximum(lb, LB_FLOOR)
        hg_par = jnp.zeros((8, HG_KEY), F32).at[0].set(lbf).at[1].set(1.0 - lb).at[2].set(lbf - lb).at[3].set(hg_norm_w[layer])
        ml_par = jnp.zeros((8, 128), F32).at[0].set(_lane_row([(LANE_MI, ml_b_i[layer]), (LANE_MF, ml_b_f[layer])]))
        ss_par = jnp.zeros((8, 128), F32).at[0].set(_lane_row([(LANE_DT, ssm_dt_bias[layer])]))
        ss_par = ss_par.at[1].set(_lane_row([(LANE_DT, ssm_a_log[layer])]))
        dskip = jnp.repeat(ssm_d[layer].astype(F32), SSM_HEADDIM).reshape(1, SSM_WIDTH)
        w_r = jnp.concatenate([moe_w_group[layer],
                               moe_w_router[layer].transpose(1, 0, 2).reshape(d, N_EXPERTS),
                               jnp.zeros((d, 128 - N_GROUPS_MOE - N_EXPERTS), F32)], axis=1)
        r_hi = w_r.astype(BF16)
        r_lo = (w_r - r_hi.astype(F32)).astype(BF16)

        if layer == 0:
            (u,) = _norm_call(h, norm_mix_w[layer], write_h=False, u_dtype=BF16)
        else:
            h, u = _norm_call(h, norm_mix_w[layer], contrib, wts, write_h=True, u_dtype=BF16)
        proj = _inproj_call(u, w_perm)
        ya = _hgrn2_call(proj, hg_par, batch, p)
        yb = _mlstm_call(proj, ml_par, ml_norm_w[layer].reshape(1, ML_WIDTH), batch, p)
        yc = _ssd_call(proj, ssm_conv_w[layer], ssm_conv_b[layer].reshape(1, -1), ss_par, dskip,
                       ssm_norm_w[layer].reshape(1, SSM_WIDTH), batch, p)
        r_bias = _lane_row([(0, moe_b_group[layer]), (LANE_E0, moe_b_router[layer].reshape(-1))]).reshape(1, 128)
        h, u_ffn, rt, cnt = _outproj_call(ya, yb, yc, h, w_out[layer].astype(BF16), norm_ffn_w[layer],
                                          r_hi, r_lo, r_bias)
        block_e, n_used, w_table, src_tok, dst_row, wts, n_out_rows = _route_tables(rt, cnt, t)
        contrib = _moe_call(block_e, n_used, w_table, src_tok, dst_row, u_ffn, moe_w1, moe_w3, moe_w2, layer,
                            n_out_rows)
    return _final_call(h, contrib, wts, final_norm_w, batch, p)
```

```python
import functools

import jax
import jax.numpy as jnp
import numpy as np
from jax import lax
from jax.experimental import pallas as pl
from jax.experimental.pallas import tpu as pltpu

F32 = jnp.float32
BF16 = jnp.bfloat16

D_MODEL = 2048
N_META = 16
CHUNK = 64
HG_CHUNK = 16
LEAD_PAD = CHUNK - N_META
EPS = 1e-6
NEG = -1e30
LB_FLOOR = 1e-30

HG_HEADS = 4
HG_KDIM = 128
HG_KEY = HG_HEADS * HG_KDIM
HG_WIDTH = HG_HEADS * 128

ML_HEADS = 4
ML_QK = 64
ML_V = 128
ML_QK_W = ML_HEADS * ML_QK
ML_WIDTH = ML_HEADS * ML_V
GATE_CAP = 15.0

SSM_HEADS = 16
SSM_HEADDIM = 64
SSM_WIDTH = SSM_HEADS * SSM_HEADDIM
SSM_STATE = 128
SSM_GROUPS = 4
SSM_HPG = SSM_HEADS // SSM_GROUPS
SSM_BC = SSM_GROUPS * SSM_STATE
CONV_W = 4

D_MIX = HG_WIDTH + ML_WIDTH + SSM_WIDTH

N_GROUPS_MOE = 4
EXPERTS_PER_GROUP = 8
N_EXPERTS = N_GROUPS_MOE * EXPERTS_PER_GROUP
TOP_K = 2
D_EXPERT = 512
MOE_BM = 128

C_HG = 0
C_SZ = 2048
C_SX = 3072
C_MV = 5120
C_MQ = 6144
C_SMALL = 6656
N_PROJ = 6912
LANE_MI = 0
LANE_MF = ML_HEADS
LANE_DT = 2 * ML_HEADS

VMEM_LIMIT = 56 * 1024 * 1024


def _cparams(sem):
    return pltpu.CompilerParams(dimension_semantics=sem, vmem_limit_bytes=VMEM_LIMIT)


def _row_tile(n, target, mult=16):
    best = None
    for t in range(mult, min(n, target) + 1, mult):
        if n % t == 0:
            best = t
    assert best is not None, (n, target, mult)
    return best


def _split3(x):
    hi = x.astype(BF16)
    r = x - hi.astype(F32)
    mid = r.astype(BF16)
    lo = (r - mid.astype(F32)).astype(BF16)
    return hi, mid, lo


def _dot(a, b):
    return jnp.dot(a, b, preferred_element_type=F32)


def _sel_dot(sel, x):
    hi, mid, lo = _split3(x)
    return _dot(sel, hi) + _dot(sel, mid) + _dot(sel, lo)


def _dot_sel(x, sel):
    hi, mid, lo = _split3(x)
    return _dot(hi, sel) + _dot(mid, sel) + _dot(lo, sel)


def _dot_nt(a, b):
    return lax.dot_general(a, b, (((1,), (1,)), ((), ())), preferred_element_type=F32)


def _dot_tn(a, b):
    return lax.dot_general(a, b, (((0,), (0,)), ((), ())), preferred_element_type=F32)


def _log_sigmoid(x):
    return jnp.minimum(x, 0.0) - jnp.log1p(jnp.exp(-jnp.abs(x)))


def _sigmoid(x):
    return 1.0 / (1.0 + jnp.exp(-x))


def _silu(x):
    return x * _sigmoid(x)


SLAB = 8
U32 = jnp.uint32
HI_MASK = 0xFFFF0000


def _bf16_bits(x):
    return lax.bitcast_convert_type(x.astype(BF16).astype(F32), U32)


def _pack_slab(ref, val, lead=(), j0=0):
    rows = val.shape[0]
    for jj in range(val.shape[1] // 256):
        lo = _bf16_bits(val[:, jj * 256:jj * 256 + 128])
        hi = _bf16_bits(val[:, jj * 256 + 128:(jj + 1) * 256])
        ref[(*lead, pl.ds(j0 + jj, rows, stride=SLAB), slice(None))] = (lo >> 16) | (hi & U32(HI_MASK))


def _unpack_slab(ref, j, rows, lead=()):
    w = ref[(*lead, pl.ds(j, rows, stride=SLAB), slice(None))]
    return (lax.bitcast_convert_type(w << 16, F32), lax.bitcast_convert_type(w & U32(HI_MASK), F32))


def _combine_rows(h_ref, c0_ref, c1_ref, wt_ref):
    rows = h_ref.shape[0]
    wt = wt_ref[...]
    w0, w1 = wt[:, 0:1], wt[:, 1:2]
    pieces = []
    for j in range(SLAB):
        a0, b0 = _unpack_slab(c0_ref, j, rows)
        a1, b1 = _unpack_slab(c1_ref, j, rows)
        pieces += [w0 * a0 + w1 * a1, w0 * b0 + w1 * b1]
    return h_ref[...] + jnp.concatenate(pieces, axis=1)


def _norm_kernel(*refs, combine, write_h):
    if combine:
        h_ref, c0_ref, c1_ref, wt_ref, nw_ref = refs[:5]
        outs = refs[5:]
        h = _combine_rows(h_ref, c0_ref, c1_ref, wt_ref)
    else:
        h_ref, nw_ref = refs[:2]
        outs = refs[2:]
        h = h_ref[...]
    if write_h:
        outs[0][...] = h
    ms = jnp.mean(h * h, axis=-1, keepdims=True)
    u_ref = outs[-1]
    u_ref[...] = (h * lax.rsqrt(ms + EPS) * nw_ref[...]).astype(u_ref.dtype)


def _norm_call(h, nw, contrib=None, wts=None, *, write_h, u_dtype, tm_target=264):
    t, d = h.shape
    tm = _row_tile(t, tm_target)
    combine = contrib is not None
    row = lambda i: (i, 0)
    in_specs = [pl.BlockSpec((tm, d), row)]
    args = [h]
    if combine:
        in_specs += [pl.BlockSpec((tm * SLAB, 128), row), pl.BlockSpec((tm * SLAB, 128), lambda i: (t // tm + i, 0)),
                     pl.BlockSpec((tm, 2), row)]
        args += [contrib, contrib, wts]
    in_specs.append(pl.BlockSpec((1, d), lambda i: (0, 0)))
    args.append(nw.reshape(1, d))
    out_shape, out_specs = [], []
    if write_h:
        out_shape.append(jax.ShapeDtypeStruct((t, d), F32))
        out_specs.append(pl.BlockSpec((tm, d), row))
    out_shape.append(jax.ShapeDtypeStruct((t, d), u_dtype))
    out_specs.append(pl.BlockSpec((tm, d), row))
    return pl.pallas_call(
        functools.partial(_norm_kernel, combine=combine, write_h=write_h),
        grid=(t // tm,), in_specs=in_specs, out_specs=out_specs, out_shape=out_shape,
        compiler_params=_cparams(("arbitrary",)), name="combine_norm",
    )(*args)


def _final_kernel(h_ref, c0_ref, c1_ref, wt_ref, nw_ref, o_ref):
    h = _combine_rows(h_ref, c0_ref, c1_ref, wt_ref)
    ms = jnp.mean(h * h, axis=-1, keepdims=True)
    o_ref[...] = h * lax.rsqrt(ms + EPS) * nw_ref[...]


def _final_call(h, contrib, wts, nw, batch, p):
    t, d = h.shape
    seq = p - CHUNK
    tm = _row_tile(seq, 256)
    n_out = seq // tm
    row0 = lambda b, i: b * p + CHUNK + i * tm
    src = lambda b, i: (pl.multiple_of(row0(b, i), CHUNK), 0)
    slab0 = lambda b, i: (pl.multiple_of(row0(b, i) * SLAB, CHUNK), 0)
    slab1 = lambda b, i: (pl.multiple_of((t + row0(b, i)) * SLAB, CHUNK), 0)
    win = lambda r, w, fn: pl.BlockSpec((pl.Element(r), pl.Element(w)), fn)
    return pl.pallas_call(
        _final_kernel, grid=(batch, n_out),
        in_specs=[win(tm, d, src), win(tm * SLAB, 128, slab0), win(tm * SLAB, 128, slab1), win(tm, 2, src),
                  pl.BlockSpec((1, d), lambda b, i: (0, 0))],
        out_specs=pl.BlockSpec((None, tm, d), lambda b, i: (b, i, 0)),
        out_shape=jax.ShapeDtypeStruct((batch, seq, d), F32),
        compiler_params=_cparams(("arbitrary", "arbitrary")), name="final_norm",
    )(h, contrib, contrib, wts, nw.reshape(1, d))


def _matmul_kernel(x_ref, w_ref, o_ref):
    o_ref[...] = _dot(x_ref[...], w_ref[...])


def _inproj_call(u, w, tm_target=1056, tn=768):
    t, d = u.shape
    n = w.shape[1]
    tm = _row_tile(t, tm_target)
    assert n % tn == 0
    return pl.pallas_call(
        _matmul_kernel, grid=(t // tm, n // tn),
        in_specs=[pl.BlockSpec((tm, d), lambda i, j: (i, 0)), pl.BlockSpec((d, tn), lambda i, j: (0, j))],
        out_specs=pl.BlockSpec((tm, tn), lambda i, j: (i, j)),
        out_shape=jax.ShapeDtypeStruct((t, n), F32),
        compiler_params=_cparams(("arbitrary", "arbitrary")), name="in_proj",
    )(u, w)


def _hgrn2_kernel(q_ref, f_ref, i_ref, g_ref, par_ref, o_ref, st_ref, *, rows):
    s = pl.program_id(1)

    @pl.when(s == 0)
    def _():
        st_ref[...] = jnp.zeros_like(st_ref)

    c = HG_CHUNK
    ones = jnp.ones((HG_KDIM, 128), BF16)
    rid = lax.broadcasted_iota(jnp.int32, (c, 128), 0)
    scale = HG_KDIM ** -0.5

    hc = c // 2

    def chunk(ci, carry):
        r0 = pl.multiple_of(ci * c, c)
        pad = (s * rows + r0 + rid) < LEAD_PAD

        def front(h):
            cols = slice(h * 128, (h + 1) * 128)
            a_lb = par_ref[0:1, cols]
            b_lb = par_ref[1:2, cols]
            c_lb = par_ref[2:3, cols]
            z = f_ref[pl.ds(r0, c), cols]
            sg = _sigmoid(z)
            f = a_lb + b_lb * sg
            log_f = jnp.where(pad, 0.0, jnp.log(f))
            k = jnp.where(pad, 0.0, b_lb * (1.0 - sg) - c_lb)
            q = q_ref[pl.ds(r0, c), cols] * scale
            v = i_ref[pl.ds(r0, c), cols]
            cum = log_f
            for sh in (1, 2, 4, 8):
                cum = cum + jnp.where(rid >= sh, pltpu.roll(cum, sh, axis=0), 0.0)
            parts = []
            for s_ in range(c):
                lo = 0 if s_ < hc else hc
                rel = jnp.where(rid[lo:] >= s_, cum[lo:] - cum[s_:s_ + 1, :], NEG)
                parts.append(q[lo:] * (k[s_:s_ + 1, :] * jnp.exp(rel)))
            sc = _dot(jnp.concatenate(parts, axis=0).astype(BF16), ones)
            st = st_ref[h]
            o_inter = _dot_nt((q * jnp.exp(cum)).astype(BF16), st.astype(BF16))
            last = cum[c - 1:c, :]
            kd = (k * jnp.exp(last - cum)).astype(BF16)
            st_ref[h] = st * jnp.exp(last) + _dot_tn(v.astype(BF16), kd)
            return sc, o_inter, v

        def back(h, sc, o_inter, v):
            cols = slice(h * 128, (h + 1) * 128)
            o_top = o_inter[:hc]
            o_bot = o_inter[hc:]
            for s_ in range(hc):
                o_top = o_top + sc[s_ * c:s_ * c + hc, :] * v[s_:s_ + 1, :]
                o_bot = o_bot + sc[s_ * c + hc:(s_ + 1) * c, :] * v[s_:s_ + 1, :]
            for s_ in range(hc, c):
                r_ = hc * c + (s_ - hc) * hc
                o_bot = o_bot + sc[r_:r_ + hc, :] * v[s_:s_ + 1, :]
            o = jnp.concatenate([o_top, o_bot], axis=0)
            ms = jnp.mean(o * o, axis=-1, keepdims=True)
            g = g_ref[pl.ds(r0, c), cols]
            o_ref[pl.ds(r0, c), cols] = (o * lax.rsqrt(ms + EPS) * par_ref[3:4, cols] * _silu(g)).astype(o_ref.dtype)

        pending = front(0)
        for h in range(1, HG_HEADS):
            nxt = front(h)
            back(h - 1, *pending)
            pending = nxt
        back(HG_HEADS - 1, *pending)
        return carry

    n_chunks = rows // c
    lax.fori_loop(0, n_chunks, chunk, 0, unroll=3 if n_chunks % 3 == 0 else 1)


def _hgrn2_call(proj, par, batch, p):
    t = proj.shape[0]
    rows = _row_tile(p, 528)
    nb = p // rows
    w = HG_KEY
    blk = lambda j: pl.BlockSpec((rows, w), lambda b, s, j=j: (b * nb + s, C_HG // w + j))
    return pl.pallas_call(
        functools.partial(_hgrn2_kernel, rows=rows), grid=(batch, nb),
        in_specs=[blk(0), blk(1), blk(2), blk(3), pl.BlockSpec((8, w), lambda b, s: (0, 0))],
        out_specs=pl.BlockSpec((rows, w), lambda b, s: (b * nb + s, 0)),
        out_shape=jax.ShapeDtypeStruct((t, HG_WIDTH), BF16),
        scratch_shapes=[pltpu.VMEM((HG_HEADS, 128, HG_KDIM), F32)],
        compiler_params=_cparams(("arbitrary", "arbitrary")), name="hgrn2",
    )(proj, proj, proj, proj, par)


def _mlstm_kernel(v_ref, o_ref, q_ref, k_ref, sm_ref, par_ref, nw_ref, tri_ref, sel_ref, dg_ref,
                  y_ref, c_ref, m_ref):
    s = pl.program_id(0)

    @pl.when(s == 0)
    def _():
        c_ref[...] = jnp.zeros_like(c_ref)
        m_ref[...] = jnp.zeros_like(m_ref)

    n = CHUNK
    rid = lax.broadcasted_iota(jnp.int32, (n, 128), 0)
    pad = (s * n + rid) < LEAD_PAD
    sel_i = sel_ref[0]
    sel_f = sel_ref[1]
    dg = dg_ref[...]
    w = ML_HEADS * n
    tt = lax.broadcasted_iota(jnp.int32, (n, w), 0)
    ss = lax.broadcasted_iota(jnp.int32, (n, w), 1) & (n - 1)
    causal = tt >= ss
    scale = ML_QK ** -0.5
    lane = lax.broadcasted_iota(jnp.int32, (n, 128), 1)
    one_col = jnp.where(lane == 0, 1.0, 0.0).astype(BF16)
    for b in range(v_ref.shape[0]):
        pre = sm_ref[b] + par_ref[0:1, :]
        cap = GATE_CAP * jnp.tanh(pre * (1.0 / GATE_CAP))
        log_i = jnp.where(pad, NEG, cap)
        log_f = jnp.where(pad, 0.0, _log_sigmoid(cap))
        cum = _sel_dot(tri_ref[...], log_f)
        cum_col = _dot_sel(cum, sel_f)
        cum_row = jnp.sum(cum_col * dg, axis=0, keepdims=True)
        li_row = jnp.sum(_dot_sel(log_i, sel_i) * dg, axis=0, keepdims=True)
        dmat = jnp.where(causal, cum_col - cum_row + li_row, NEG)
        hs = range(ML_HEADS)
        sh = [b * ML_HEADS + h for h in hs]
        d_h = [dmat[:, h * n:(h + 1) * n] for h in hs]
        cum_h = [cum[:, LANE_MF + h:LANE_MF + h + 1] for h in hs]
        li_h = [log_i[:, LANE_MI + h:LANE_MI + h + 1] for h in hs]
        m_st = [m_ref[sh[h]:sh[h] + 1, 0:1] for h in hs]
        q = [(q_ref[b, :, h * ML_QK:(h + 1) * ML_QK] * scale).astype(BF16) for h in hs]
        k = [k_ref[b, :, h * ML_QK:(h + 1) * ML_QK] for h in hs]
        v_aug = [jnp.concatenate([v_ref[b, :, h * ML_V:(h + 1) * ML_V].astype(BF16), one_col], axis=1) for h in hs]
        c_prev = [c_ref[sh[h]] for h in hs]
        inter = [cum_h[h] + m_st[h] for h in hs]
        m_t = [jnp.maximum(inter[h], jnp.max(d_h[h], axis=-1, keepdims=True)) for h in hs]
        qk = [_dot_nt(q[h], k[h].astype(BF16)) for h in hs]
        qc = [_dot(q[h], c_prev[h].astype(BF16)) for h in hs]
        pw = [(qk[h] * jnp.exp(d_h[h] - m_t[h])).astype(BF16) for h in hs]
        nd = [_dot(pw[h], v_aug[h]) + jnp.exp(inter[h] - m_t[h]) * qc[h] for h in hs]
        hh = [nd[h][:, :ML_V] / jnp.maximum(jnp.abs(nd[h][:, ML_V:ML_V + 1]), jnp.exp(-m_t[h])) for h in hs]
        tot = [cum_h[h][n - 1:n, :] for h in hs]
        to_end = [tot[h] - cum_h[h] + li_h[h] for h in hs]
        m_loc = [jnp.max(to_end[h], axis=0, keepdims=True) for h in hs]
        kw = [(k[h] * jnp.exp(to_end[h] - m_loc[h])).astype(BF16) for h in hs]
        c_loc = [_dot_tn(kw[h], v_aug[h]) for h in hs]
        m_new = [jnp.maximum(tot[h] + m_st[h], m_loc[h]) for h in hs]
        for h in hs:
            c_ref[sh[h]] = (jnp.exp(tot[h] + m_st[h] - m_new[h]) * c_prev[h]
                            + jnp.exp(m_loc[h] - m_new[h]) * c_loc[h])
            m_ref[sh[h]:sh[h] + 1, :] = jnp.broadcast_to(m_new[h], (1, 128))
        for h in hs:
            ms = jnp.mean(hh[h] * hh[h], axis=-1, keepdims=True)
            cols = slice(h * ML_V, (h + 1) * ML_V)
            y_ref[b, :, cols] = (hh[h] * lax.rsqrt(ms + EPS) * nw_ref[:, cols]
                                 * _sigmoid(o_ref[b, :, cols])).astype(y_ref.dtype)


def _lane_select(lane0, heads, width):
    m = np.zeros((128, heads * width), np.float32)
    for h in range(heads):
        m[lane0 + h, h * width:(h + 1) * width] = 1.0
    return m


def _diag_mask(n, heads):
    return np.tile(np.eye(n, dtype=np.float32), (1, heads))


def _mlstm_call(proj, par, nw, batch, p):
    t = proj.shape[0]
    n = CHUNK
    nc = p // n
    tri = jnp.asarray(np.tril(np.ones((n, n), np.float32)), BF16)
    sel = jnp.asarray(np.stack([_lane_select(LANE_MI, ML_HEADS, n), _lane_select(LANE_MF, ML_HEADS, n)]), BF16)
    dg = jnp.asarray(_diag_mask(n, ML_HEADS), F32)
    proj3 = proj.reshape(batch, p, proj.shape[1])
    blk = lambda w, off: pl.BlockSpec((batch, n, w), lambda s: (0, s, off // w))
    const = lambda shape: pl.BlockSpec(shape, lambda s: (0,) * len(shape))
    m_rows = -(-batch * ML_HEADS // 8) * 8
    y = pl.pallas_call(
        _mlstm_kernel, grid=(nc,),
        in_specs=[blk(ML_WIDTH, C_MV), blk(ML_WIDTH, C_MV + ML_WIDTH), blk(ML_QK_W, C_MQ),
                  blk(ML_QK_W, C_MQ + ML_QK_W), blk(128, C_SMALL),
                  const((8, 128)), const((1, ML_WIDTH)), const((n, n)), const((2, 128, ML_HEADS * n)),
                  const((n, ML_HEADS * n))],
        out_specs=pl.BlockSpec((batch, n, ML_WIDTH), lambda s: (0, s, 0)),
        out_shape=jax.ShapeDtypeStruct((batch, p, ML_WIDTH), BF16),
        scratch_shapes=[pltpu.VMEM((batch * ML_HEADS, ML_QK, 2 * ML_V), F32), pltpu.VMEM((m_rows, 128), F32)],
        compiler_params=_cparams(("arbitrary",)), name="mlstm",
    )(proj3, proj3, proj3, proj3, proj3, par, nw, tri, sel, dg)
    return y.reshape(t, ML_WIDTH)


def _ssd_kernel(z_ref, x_ref, b_ref, c_ref, sm_ref, cw_ref, cb_ref, par_ref, dsk_ref, nw_ref,
                tri_ref, sel_ref, dg_ref, y_ref, xs_ref, bs_ref, cs_ref, st_ref):
    s = pl.program_id(1)
    n = CHUNK
    tail = 8

    @pl.when(s == 0)
    def _():
        st_ref[...] = jnp.zeros_like(st_ref)
        xs_ref[0:tail, :] = jnp.zeros((tail, xs_ref.shape[1]), F32)
        bs_ref[0:tail, :] = jnp.zeros((tail, bs_ref.shape[1]), F32)
        cs_ref[0:tail, :] = jnp.zeros((tail, cs_ref.shape[1]), F32)

    def conv_silu(src_ref, scr_ref, c0, width, rowmask):
        scr_ref[tail:tail + n, :] = src_ref[...]
        acc = cb_ref[:, c0:c0 + width]
        for j in range(CONV_W):
            off = tail - (CONV_W - 1) + j
            acc = acc + cw_ref[j:j + 1, c0:c0 + width] * scr_ref[off:off + n, :]
        scr_ref[0:tail, :] = scr_ref[n:n + tail, :]
        return jnp.where(rowmask, 0.0, _silu(acc))

    def padmask(width):
        return (s * n + lax.broadcasted_iota(jnp.int32, (n, width), 0)) < LEAD_PAD

    x = conv_silu(x_ref, xs_ref, 0, SSM_WIDTH, padmask(SSM_WIDTH))
    bm = conv_silu(b_ref, bs_ref, SSM_WIDTH, SSM_BC, padmask(SSM_BC))
    cm = conv_silu(c_ref, cs_ref, SSM_WIDTH + SSM_BC, SSM_BC, padmask(SSM_BC))

    pre = sm_ref[...] + par_ref[0:1, :]
    dt = jnp.maximum(pre, 0.0) + jnp.log1p(jnp.exp(-jnp.abs(pre)))
    dt = jnp.where(padmask(128), 0.0, dt)
    da = dt * (-jnp.exp(par_ref[1:2, :]))
    cum = _sel_dot(tri_ref[...], da)
    sel = sel_ref[...]
    dt_col = _dot_sel(dt, sel)
    cum_col = _dot_sel(cum, sel)
    cum_row = jnp.sum(cum_col * dg_ref[...], axis=0, keepdims=True)
    w = SSM_WIDTH
    tt = lax.broadcasted_iota(jnp.int32, (n, w), 0)
    ss = lax.broadcasted_iota(jnp.int32, (n, w), 1) & (n - 1)
    decay = jnp.exp(jnp.where(tt >= ss, cum_col - cum_row, NEG))
    xdt = x * dt_col
    last = cum_col[n - 1:n, :]
    wend = (xdt * jnp.exp(last - cum_col)).astype(BF16)
    chunk_decay = jnp.exp(last)
    ecum = jnp.exp(cum_col)
    gw = SSM_HPG * SSM_HEADDIM
    rr = lax.broadcasted_iota(jnp.int32, (gw, gw), 0) // SSM_HEADDIM
    cc = lax.broadcasted_iota(jnp.int32, (gw, gw), 1) // SSM_HEADDIM
    blockdiag = rr == cc
    ys = []
    for g in range(SSM_GROUPS):
        gl = slice(g * gw, (g + 1) * gw)
        sl = slice(g * SSM_STATE, (g + 1) * SSM_STATE)
        cm_g = cm[:, sl].astype(BF16)
        bm_g = bm[:, sl].astype(BF16)
        cb = _dot_nt(cm_g, jnp.concatenate([bm_g] * SSM_HPG, axis=0))
        m = (cb * decay[:, gl]).astype(BF16)
        xdt_g = xdt[:, gl]
        bd = jnp.where(blockdiag, jnp.concatenate([xdt_g] * SSM_HPG, axis=0), 0.0).astype(BF16)
        st = st_ref[g]
        y_g = _dot(m, bd) + ecum[:, gl] * _dot(cm_g, st.astype(BF16))
        st_ref[g] = st * chunk_decay[:, gl] + _dot_tn(bm_g, wend[:, gl])
        ys.append(y_g)
    y = jnp.concatenate(ys, axis=1) + x * dsk_ref[...]
    y = y * _silu(z_ref[...])
    outs = []
    for g in range(SSM_GROUPS):
        gl = slice(g * gw, (g + 1) * gw)
        y_g = y[:, gl]
        ms = jnp.mean(y_g * y_g, axis=-1, keepdims=True)
        outs.append(y_g * lax.rsqrt(ms + EPS))
    y_ref[...] = (jnp.concatenate(outs, axis=1) * nw_ref[...]).astype(y_ref.dtype)


def _ssd_call(proj, cw, cb, par, dsk, nw, batch, p):
    t = proj.shape[0]
    n = CHUNK
    nc = p // n
    tri = jnp.asarray(np.tril(np.ones((n, n), np.float32)), BF16)
    sel = jnp.asarray(_lane_select(LANE_DT, SSM_HEADS, SSM_HEADDIM), BF16)
    dg = jnp.asarray(_diag_mask(n, SSM_HEADS), F32)
    blk = lambda w, off: pl.BlockSpec((n, w), lambda b, s: (b * nc + s, off // w))
    const = lambda shape: pl.BlockSpec(shape, lambda b, s: (0,) * len(shape))
    cch = SSM_WIDTH + 2 * SSM_BC
    return pl.pallas_call(
        _ssd_kernel, grid=(batch, nc),
        in_specs=[blk(SSM_WIDTH, C_SZ), blk(SSM_WIDTH, C_SX), blk(SSM_BC, C_SX + SSM_WIDTH),
                  blk(SSM_BC, C_SX + SSM_WIDTH + SSM_BC), blk(128, C_SMALL),
                  const((CONV_W, cch)), const((1, cch)), const((8, 128)), const((1, SSM_WIDTH)),
                  const((1, SSM_WIDTH)), const((n, n)), const((128, SSM_WIDTH)), const((n, SSM_WIDTH))],
        out_specs=pl.BlockSpec((n, SSM_WIDTH), lambda b, s: (b * nc + s, 0)),
        out_shape=jax.ShapeDtypeStruct((t, SSM_WIDTH), BF16),
        scratch_shapes=[pltpu.VMEM((n + 8, SSM_WIDTH), F32), pltpu.VMEM((n + 8, SSM_BC), F32),
                        pltpu.VMEM((n + 8, SSM_BC), F32),
                        pltpu.VMEM((SSM_GROUPS, SSM_STATE, SSM_HPG * SSM_HEADDIM), F32)],
        compiler_params=_cparams(("arbitrary", "arbitrary")), name="ssd",
    )(proj, proj, proj, proj, proj, cw, cb, par, dsk, nw, tri, sel, dg)


LANE_E0 = N_GROUPS_MOE
RT_E, RT_RANK, RT_W = 0, 2, 4


def _first_max(vals, lane):
    m = jnp.max(vals, axis=-1, keepdims=True)
    idx = jnp.min(jnp.where(vals == m, lane, 128), axis=-1, keepdims=True)
    return m, idx


def _outproj_kernel(ya_ref, yb_ref, yc_ref, h_ref, w_ref, nw_ref, rh_ref, rl_ref, rb_ref, tri_ref,
                    hm_ref, u_ref, rt_ref, cnt_ref):
    a0, a1 = HG_WIDTH, HG_WIDTH + ML_WIDTH
    h = h_ref[...]
    h = h + _dot(ya_ref[...], w_ref[0:a0, :])
    h = h + _dot(yb_ref[...], w_ref[a0:a1, :])
    h = h + _dot(yc_ref[...], w_ref[a1:, :])
    hm_ref[...] = h
    ms = jnp.mean(h * h, axis=-1, keepdims=True)
    u = h * lax.rsqrt(ms + EPS) * nw_ref[...]
    _pack_slab(u_ref, u)
    u_hi = u.astype(BF16)
    u_lo = (u - u_hi.astype(F32)).astype(BF16)
    lg = _dot(u_hi, rh_ref[...]) + (_dot(u_lo, rh_ref[...]) + _dot(u_hi, rl_ref[...])) + rb_ref[...]

    tm = lg.shape[0]
    lane = lax.broadcasted_iota(jnp.int32, (tm, 128), 1)
    g_mask = lane < N_GROUPS_MOE
    g_max, g_sel = _first_max(jnp.where(g_mask, lg, NEG), lane)
    g_gate = 1.0 / jnp.sum(jnp.where(g_mask, jnp.exp(lg - g_max), 0.0), axis=-1, keepdims=True)
    lo = LANE_E0 + g_sel * EXPERTS_PER_GROUP
    e_vals = jnp.where((lane >= lo) & (lane < lo + EXPERTS_PER_GROUP), lg, NEG)
    v1, i1 = _first_max(e_vals, lane)
    v2, i2 = _first_max(jnp.where(lane == i1, NEG, e_vals), lane)
    a = jnp.exp(v2 - v1)
    w1 = g_gate / (1.0 + a)
    w2 = w1 * a
    @pl.when(pl.program_id(0) == 0)
    def _():
        cnt_ref[...] = jnp.zeros_like(cnt_ref)

    hit1 = lane == i1
    hit2 = lane == i2
    onehot = jnp.where(hit1 | hit2, 1.0, 0.0)
    before = _dot(tri_ref[...], onehot.astype(BF16)) + cnt_ref[0:1, :]
    r1 = jnp.sum(jnp.where(hit1, before, 0.0), axis=-1, keepdims=True)
    r2 = jnp.sum(jnp.where(hit2, before, 0.0), axis=-1, keepdims=True)
    cnt_ref[...] = cnt_ref[...] + jnp.sum(onehot, axis=0, keepdims=True)
    rec = jnp.zeros((tm, 128), F32)
    for ln, val in ((RT_E, (i1 - LANE_E0).astype(F32)), (RT_E + 1, (i2 - LANE_E0).astype(F32)),
                    (RT_RANK, r1), (RT_RANK + 1, r2), (RT_W, w1), (RT_W + 1, w2)):
        rec = jnp.where(lane == ln, val, rec)
    rt_ref[...] = rec


def _outproj_call(ya, yb, yc, h, w, nw, r_hi, r_lo, r_bias, tm_target=352):
    t, d = h.shape
    tm = _row_tile(t, tm_target)
    tri = jnp.asarray(np.tril(np.ones((tm, tm), np.float32), -1), BF16)
    row = lambda i: (i, 0)
    const = lambda shape: pl.BlockSpec(shape, lambda i: (0, 0))
    return pl.pallas_call(
        _outproj_kernel, grid=(t // tm,),
        in_specs=[pl.BlockSpec((tm, HG_WIDTH), row), pl.BlockSpec((tm, ML_WIDTH), row),
                  pl.BlockSpec((tm, SSM_WIDTH), row), pl.BlockSpec((tm, d), row),
                  const((D_MIX, d)), const((1, d)), const((d, 128)), const((d, 128)), const((1, 128)),
                  const((tm, tm))],
        out_specs=[pl.BlockSpec((tm, d), row), pl.BlockSpec((tm * SLAB, 128), row), pl.BlockSpec((tm, 128), row),
                   const((8, 128))],
        out_shape=[jax.ShapeDtypeStruct((t, d), F32), jax.ShapeDtypeStruct((t * SLAB, 128), U32),
                   jax.ShapeDtypeStruct((t, 128), F32), jax.ShapeDtypeStruct((8, 128), F32)],
        compiler_params=_cparams(("arbitrary",)), name="out_proj_router",
    )(ya, yb, yc, h, w, nw.reshape(1, d), r_hi, r_lo, r_bias, tri)


MOE_RING = 3
WT_FIRST, WT_SLOT, WT_NEXT = 0, 1, 2


TAB_SRC2, TAB_DST_PREV, TAB_DST = 0, 1, 2


def _moe_kernel(be_ref, nu_ref, wt_ref, src01_ref, tab_ref, u_hbm,
                w1_hbm, w3_hbm, w2_hbm, o_hbm, xbuf, ybuf, st1, st3, st2, w1b, w3b, w2b, gsem, ssem, wsem,
                *, layer):
    i = pl.program_id(0)
    bm = MOE_BM
    n_used = nu_ref[0]
    xs = i % MOE_RING
    slot = xs
    other = (i + MOE_RING - 1) % MOE_RING

    def slab(idx):
        return pl.ds(pl.multiple_of(idx * SLAB, SLAB), SLAB)

    def row_in(tok, r, sl):
        return pltpu.make_async_copy(u_hbm.at[slab(tok), :], xbuf.at[sl, slab(r), :], gsem.at[sl])

    def row_out(r, row, sl):
        return pltpu.make_async_copy(ybuf.at[sl, slab(r), :], o_hbm.at[slab(row), :], ssem.at[sl])

    def gather(section, sl):
        def body(g, c):
            for j in range(8):
                r = g * 8 + j
                row_in(src01_ref[0, section * bm + r], r, sl).start(priority=j % 2)
            return c
        lax.fori_loop(0, bm // 8, body, 0)

    def scatter(sl):
        def body(g, c):
            for j in range(8):
                r = g * 8 + j
                row_out(r, tab_ref[0, TAB_DST * bm + r], sl).start(priority=j % 2)
            return c
        lax.fori_loop(0, bm // 8, body, 0)

    def wait_gather(sl):
        pltpu.make_async_copy(u_hbm.at[pl.ds(0, bm * SLAB), :], xbuf.at[sl], gsem.at[sl]).wait()

    def wait_scatter(sl):
        pltpu.make_async_copy(ybuf.at[sl], o_hbm.at[pl.ds(0, bm * SLAB), :], ssem.at[sl]).wait()

    def weight_copies(e, p):
        return [pltpu.make_async_copy(src.at[layer, e], dst.at[p], wsem.at[p])
                for src, dst in ((w1_hbm, st1), (w3_hbm, st3), (w2_hbm, st2))]

    n_real = o_hbm.shape[0] // SLAB - MOE_RING * bm

    @pl.when(i == 0)
    def _():
        for c in weight_copies(be_ref[0], 0):
            c.start()
        gather(0, 0)
        gather(1, 1)
        ybuf[...] = jnp.zeros_like(ybuf)
        for sl in range(2):
            pltpu.make_async_copy(ybuf.at[sl], o_hbm.at[pl.ds((n_real + sl * bm) * SLAB, bm * SLAB), :],
                                  ssem.at[sl]).start()

    @pl.when(i < n_used)
    def _():
        @pl.when(wt_ref[WT_FIRST, i] == 1)
        def _():
            p = wt_ref[WT_SLOT, i]
            nxt = wt_ref[WT_NEXT, i]

            @pl.when(nxt >= 0)
            def _():
                for c in weight_copies(nxt, 1 - p):
                    c.start(priority=1)

            for c in weight_copies(be_ref[i], p):
                c.wait()
            w1b[...] = st1[p].astype(BF16)
            w3b[...] = st3[p].astype(BF16)
            w2b[...] = st2[p].astype(BF16)

        wait_gather(xs)

        xn = (i + 2) % MOE_RING

        def issue(part, parts=4):
            for r in range(part * bm // parts, (part + 1) * bm // parts):
                row_in(tab_ref[0, TAB_SRC2 * bm + r], r, xn).start(priority=0)
                row_out(r, tab_ref[0, TAB_DST_PREV * bm + r], other).start(priority=1)

        f = w1b.shape[1]
        h1 = jnp.zeros((bm, f), F32)
        h3 = jnp.zeros((bm, f), F32)
        for j in range(SLAB):
            lo, hi = _unpack_slab(xbuf, j, bm, lead=(xs,))
            xk = jnp.concatenate([lo, hi], axis=1).astype(BF16)
            h1 = h1 + _dot(xk, w1b[j * 256:(j + 1) * 256, :])
            h3 = h3 + _dot(xk, w3b[j * 256:(j + 1) * 256, :])
            if j % 4 == 3:
                issue(j // 4)
        hid = (_silu(h1) * h3).astype(BF16)
        wait_scatter(slot)
        half = SLAB // 2
        _pack_slab(ybuf, _dot(hid, w2b[:, :half * 256]), lead=(slot,))
        issue(2)
        _pack_slab(ybuf, _dot(hid, w2b[:, half * 256:]), lead=(slot,), j0=half)
        issue(3)

    @pl.when(i == n_used - 1)
    def _():
        scatter(slot)
        for k in range(MOE_RING):
            wait_scatter(k)
        wait_gather((i + 1) % MOE_RING)
        wait_gather((i + 2) % MOE_RING)


def _moe_call(block_e, n_used, w_table, src_tok, dst_row, u, w1, w3, w2, layer, n_out_rows):
    d, f = w1.shape[-2:]
    assert d == SLAB * 256
    bm = MOE_BM
    nb = block_e.shape[0]
    src_tok = src_tok.reshape(nb, bm)
    dst_row = dst_row.reshape(nb, bm)
    lead = (n_out_rows - bm + jnp.arange(bm, dtype=jnp.int32)).reshape(1, bm)
    src_ahead = jnp.concatenate([src_tok[2:], src_tok[-1:], src_tok[-1:]], axis=0)
    dst_prev = jnp.concatenate([lead, dst_row[:-1]], axis=0)
    table = jnp.concatenate([src_ahead, dst_prev, dst_row], axis=1).reshape(nb, 1, 3 * bm)
    src01 = src_tok[:2].reshape(1, 1, 2 * bm)
    idx = lambda w, fn: pl.BlockSpec((None, 1, w), fn, memory_space=pltpu.SMEM)
    hbm = pl.BlockSpec(memory_space=pl.ANY)
    ring = pltpu.VMEM((MOE_RING, bm * SLAB, 128), U32)
    return pl.pallas_call(
        functools.partial(_moe_kernel, layer=layer),
        grid_spec=pltpu.PrefetchScalarGridSpec(
            num_scalar_prefetch=3, grid=(nb,),
            in_specs=[idx(2 * bm, lambda i, *_: (0, 0, 0)), idx(3 * bm, lambda i, *_: (i, 0, 0)),
                      hbm, hbm, hbm, hbm],
            out_specs=hbm,
            scratch_shapes=[ring, ring,
                            pltpu.VMEM((2, d, f), F32), pltpu.VMEM((2, d, f), F32), pltpu.VMEM((2, f, d), F32),
                            pltpu.VMEM((d, f), BF16), pltpu.VMEM((d, f), BF16), pltpu.VMEM((f, d), BF16),
                            pltpu.SemaphoreType.DMA((MOE_RING,)), pltpu.SemaphoreType.DMA((MOE_RING,)),
                            pltpu.SemaphoreType.DMA((2,))]),
        out_shape=jax.ShapeDtypeStruct((n_out_rows * SLAB, 128), U32),
        compiler_params=_cparams(("arbitrary",)), name="moe_ffn",
    )(block_e, n_used, w_table, src01, table, u, w1, w3, w2)


def _route_tables(rt, cnt, t):
    bm = MOE_BM
    tk = t * TOP_K
    e = rt[:, RT_E:RT_E + TOP_K].astype(jnp.int32)
    rank = rt[:, RT_RANK:RT_RANK + TOP_K].astype(jnp.int32)
    wts = rt[:, RT_W:RT_W + TOP_K]
    counts = cnt[0, LANE_E0:LANE_E0 + N_EXPERTS].astype(jnp.int32)
    padded = (counts + bm - 1) // bm * bm
    pad_ends = jnp.cumsum(padded)
    pad_starts = pad_ends - padded
    e_ids = jnp.arange(N_EXPERTS, dtype=jnp.int32)

    def lookup(table, idx):
        return jnp.sum(jnp.where(idx[..., None] == e_ids, table, 0), axis=-1)

    dest = (lookup(pad_starts, e) + rank).reshape(-1)
    n_blocks = -(-tk // bm) + N_EXPERTS
    n_rows = n_blocks * bm
    inv = jnp.full((n_rows,), -1, jnp.int32).at[dest].set(jnp.arange(tk, dtype=jnp.int32))
    tok, slot = inv // TOP_K, inv % TOP_K
    src_tok = jnp.where(inv >= 0, tok, 0)
    dst_row = jnp.where(inv >= 0, slot * t + tok, tk + (jnp.arange(n_rows, dtype=jnp.int32) % (MOE_RING * bm)))
    block_ids = jnp.arange(n_blocks, dtype=jnp.int32)
    block_e = jnp.minimum(jnp.sum(pad_ends[None, :] <= block_ids[:, None] * bm, axis=1), N_EXPERTS - 1).astype(jnp.int32)
    n_used = (pad_ends[-1] // bm).astype(jnp.int32).reshape(1)
    has_rows = counts > 0
    at_or_after = lax.cummin(jnp.where(has_rows, e_ids, N_EXPERTS)[::-1])[::-1]
    next_e = jnp.concatenate([at_or_after[1:], jnp.full((1,), N_EXPERTS, jnp.int32)])
    next_e = jnp.where(next_e >= N_EXPERTS, -1, next_e)
    stage = (jnp.cumsum(has_rows.astype(jnp.int32)) - 1) % 2
    first = jnp.concatenate([jnp.ones((1,), bool), block_e[1:] != block_e[:-1]]) & (block_ids < n_used[0])
    w_table = jnp.stack([first.astype(jnp.int32), lookup(stage, block_e), lookup(next_e, block_e)]).astype(jnp.int32)
    return block_e, n_used, w_table, src_tok, dst_row, wts, tk + MOE_RING * bm


def _lane_row(pairs):
    row = jnp.zeros((128,), F32)
    for lane0, vals in pairs:
        row = row.at[lane0:lane0 + vals.shape[0]].set(vals.astype(F32))
    return row


def kernel(x, meta_tokens, hg_lb_logits, norm_mix_w, w_in, hg_norm_w, ml_b_i, ml_b_f, ml_norm_w,
           ssm_conv_w, ssm_conv_b, ssm_dt_bias, ssm_a_log, ssm_d, ssm_norm_w, w_out, norm_ffn_w,
           moe_w_group, moe_b_group, moe_w_router, moe_b_router, moe_w1, moe_w3, moe_w2, final_norm_w):
    batch, seq, d = x.shape
    depth = w_in.shape[0]
    p = LEAD_PAD + N_META + seq
    t = batch * p
    meta = jnp.broadcast_to(meta_tokens.astype(x.dtype)[None], (batch, N_META, d))
    h = jnp.concatenate([jnp.zeros((batch, LEAD_PAD, d), x.dtype), meta, x], axis=1).reshape(t, d)

    lb_w = jax.nn.softmax(hg_lb_logits.astype(F32), axis=0)
    lower_bounds = jnp.cumsum(lb_w, axis=0) - lb_w[0]

    contrib = wts = None
    for layer in range(depth):
        w = w_in[layer]
        o_mq, o_mv, o_mi, o_sz = HG_KEY * 2 + HG_WIDTH * 2, 2560, 3584, 3592
        o_sx = o_sz + SSM_WIDTH
        o_dt = o_sx + SSM_WIDTH + 2 * SSM_BC
        w_perm = jnp.concatenate([
            w[:, :o_mq], w[:, o_sz:o_sx], w[:, o_sx:o_dt], w[:, o_mv:o_mi], w[:, o_mq:o_mv],
            w[:, o_mi:o_sz], w[:, o_dt:], jnp.zeros((d, N_PROJ - C_SMALL - 2 * ML_HEADS - SSM_HEADS), w.dtype)],
            axis=1).astype(BF16)
        lb = lower_bounds[layer]
        lbf = jnp.maximum(lb, LB_FLOOR)
        hg_par = jnp.zeros((8, HG_KEY), F32).at[0].set(lbf).at[1].set(1.0 - lb).at[2].set(lbf - lb).at[3].set(hg_norm_w[layer])
        ml_par = jnp.zeros((8, 128), F32).at[0].set(_lane_row([(LANE_MI, ml_b_i[layer]), (LANE_MF, ml_b_f[layer])]))
        ss_par = jnp.zeros((8, 128), F32).at[0].set(_lane_row([(LANE_DT, ssm_dt_bias[layer])]))
        ss_par = ss_par.at[1].set(_lane_row([(LANE_DT, ssm_a_log[layer])]))
        dskip = jnp.repeat(ssm_d[layer].astype(F32), SSM_HEADDIM).reshape(1, SSM_WIDTH)
        w_r = jnp.concatenate([moe_w_group[layer],
                               moe_w_router[layer].transpose(1, 0, 2).reshape(d, N_EXPERTS),
                               jnp.zeros((d, 128 - N_GROUPS_MOE - N_EXPERTS), F32)], axis=1)
        r_hi = w_r.astype(BF16)
        r_lo = (w_r - r_hi.astype(F32)).astype(BF16)

        if layer == 0:
            (u,) = _norm_call(h, norm_mix_w[layer], write_h=False, u_dtype=BF16)
        else:
            h, u = _norm_call(h, norm_mix_w[layer], contrib, wts, write_h=True, u_dtype=BF16)
        proj = _inproj_call(u, w_perm)
        ya = _hgrn2_call(proj, hg_par, batch, p)
        yb = _mlstm_call(proj, ml_par, ml_norm_w[layer].reshape(1, ML_WIDTH), batch, p)
        yc = _ssd_call(proj, ssm_conv_w[layer], ssm_conv_b[layer].reshape(1, -1), ss_par, dskip,
                       ssm_norm_w[layer].reshape(1, SSM_WIDTH), batch, p)
        r_bias = _lane_row([(0, moe_b_group[layer]), (LANE_E0, moe_b_router[layer].reshape(-1))]).reshape(1, 128)
        h, u_ffn, rt, cnt = _outproj_call(ya, yb, yc, h, w_out[layer].astype(BF16), norm_ffn_w[layer],
                                          r_hi, r_lo, r_bias)
        block_e, n_used, w_table, src_tok, dst_row, wts, n_out_rows = _route_tables(rt, cnt, t)
        contrib = _moe_call(block_e, n_used, w_table, src_tok, dst_row, u_ffn, moe_w1, moe_w3, moe_w2, layer,
                            n_out_rows)
    return _final_call(h, contrib, wts, final_norm_w, batch, p)
```

```python
import functools

import jax
import jax.numpy as jnp
import numpy as np
from jax import lax
from jax.experimental import pallas as pl
from jax.experimental.pallas import tpu as pltpu

F32 = jnp.float32
BF16 = jnp.bfloat16

D_MODEL = 2048
N_META = 16
CHUNK = 64
HG_CHUNK = 16
LEAD_PAD = CHUNK - N_META
EPS = 1e-6
NEG = -1e30
LB_FLOOR = 1e-30

HG_HEADS = 4
HG_KDIM = 128
HG_KEY = HG_HEADS * HG_KDIM
HG_WIDTH = HG_HEADS * 128

ML_HEADS = 4
ML_QK = 64
ML_V = 128
ML_QK_W = ML_HEADS * ML_QK
ML_WIDTH = ML_HEADS * ML_V
GATE_CAP = 15.0

SSM_HEADS = 16
SSM_HEADDIM = 64
SSM_WIDTH = SSM_HEADS * SSM_HEADDIM
SSM_STATE = 128
SSM_GROUPS = 4
SSM_HPG = SSM_HEADS // SSM_GROUPS
SSM_BC = SSM_GROUPS * SSM_STATE
CONV_W = 4

D_MIX = HG_WIDTH + ML_WIDTH + SSM_WIDTH

N_GROUPS_MOE = 4
EXPERTS_PER_GROUP = 8
N_EXPERTS = N_GROUPS_MOE * EXPERTS_PER_GROUP
TOP_K = 2
D_EXPERT = 512
MOE_BM = 128

C_HG = 0
C_SZ = 2048
C_SX = 3072
C_MV = 5120
C_MQ = 6144
C_SMALL = 6656
N_PROJ = 6912
LANE_MI = 0
LANE_MF = ML_HEADS
LANE_DT = 2 * ML_HEADS

VMEM_LIMIT = 56 * 1024 * 1024


def _cparams(sem):
    return pltpu.CompilerParams(dimension_semantics=sem, vmem_limit_bytes=VMEM_LIMIT)


def _row_tile(n, target, mult=16):
    best = None
    for t in range(mult, min(n, target) + 1, mult):
        if n % t == 0:
            best = t
    assert best is not None, (n, target, mult)
    return best


def _split3(x):
    hi = x.astype(BF16)
    r = x - hi.astype(F32)
    mid = r.astype(BF16)
    lo = (r - mid.astype(F32)).astype(BF16)
    return hi, mid, lo


def _dot(a, b):
    return jnp.dot(a, b, preferred_element_type=F32)


def _sel_dot(sel, x):
    hi, mid, lo = _split3(x)
    return _dot(sel, hi) + _dot(sel, mid) + _dot(sel, lo)


def _dot_sel(x, sel):
    hi, mid, lo = _split3(x)
    return _dot(hi, sel) + _dot(mid, sel) + _dot(lo, sel)


def _dot_nt(a, b):
    return lax.dot_general(a, b, (((1,), (1,)), ((), ())), preferred_element_type=F32)


def _dot_tn(a, b):
    return lax.dot_general(a, b, (((0,), (0,)), ((), ())), preferred_element_type=F32)


def _log_sigmoid(x):
    return jnp.minimum(x, 0.0) - jnp.log1p(jnp.exp(-jnp.abs(x)))


def _sigmoid(x):
    return 1.0 / (1.0 + jnp.exp(-x))


def _silu(x):
    return x * _sigmoid(x)


SLAB = 8
U32 = jnp.uint32
HI_MASK = 0xFFFF0000


def _bf16_bits(x):
    return lax.bitcast_convert_type(x.astype(BF16).astype(F32), U32)


def _pack_slab(ref, val, lead=(), j0=0):
    rows = val.shape[0]
    for jj in range(val.shape[1] // 256):
        lo = _bf16_bits(val[:, jj * 256:jj * 256 + 128])
        hi = _bf16_bits(val[:, jj * 256 + 128:(jj + 1) * 256])
        ref[(*lead, pl.ds(j0 + jj, rows, stride=SLAB), slice(None))] = (lo >> 16) | (hi & U32(HI_MASK))


def _unpack_slab(ref, j, rows, lead=()):
    w = ref[(*lead, pl.ds(j, rows, stride=SLAB), slice(None))]
    return (lax.bitcast_convert_type(w << 16, F32), lax.bitcast_convert_type(w & U32(HI_MASK), F32))


def _combine_rows(h_ref, c0_ref, c1_ref, wt_ref):
    rows = h_ref.shape[0]
    wt = wt_ref[...]
    w0, w1 = wt[:, 0:1], wt[:, 1:2]
    pieces = []
    for j in range(SLAB):
        a0, b0 = _unpack_slab(c0_ref, j, rows)
        a1, b1 = _unpack_slab(c1_ref, j, rows)
        pieces += [w0 * a0 + w1 * a1, w0 * b0 + w1 * b1]
    return h_ref[...] + jnp.concatenate(pieces, axis=1)


def _norm_kernel(*refs, combine, write_h):
    if combine:
        h_ref, c0_ref, c1_ref, wt_ref, nw_ref = refs[:5]
        outs = refs[5:]
        h = _combine_rows(h_ref, c0_ref, c1_ref, wt_ref)
    else:
        h_ref, nw_ref = refs[:2]
        outs = refs[2:]
        h = h_ref[...]
    if write_h:
        outs[0][...] = h
    ms = jnp.mean(h * h, axis=-1, keepdims=True)
    u_ref = outs[-1]
    u_ref[...] = (h * lax.rsqrt(ms + EPS) * nw_ref[...]).astype(u_ref.dtype)


def _norm_call(h, nw, contrib=None, wts=None, *, write_h, u_dtype, tm_target=264):
    t, d = h.shape
    tm = _row_tile(t, tm_target)
    combine = contrib is not None
    row = lambda i: (i, 0)
    in_specs = [pl.BlockSpec((tm, d), row)]
    args = [h]
    if combine:
        in_specs += [pl.BlockSpec((tm * SLAB, 128), row), pl.BlockSpec((tm * SLAB, 128), lambda i: (t // tm + i, 0)),
                     pl.BlockSpec((tm, 2), row)]
        args += [contrib, contrib, wts]
    in_specs.append(pl.BlockSpec((1, d), lambda i: (0, 0)))
    args.append(nw.reshape(1, d))
    out_shape, out_specs = [], []
    if write_h:
        out_shape.append(jax.ShapeDtypeStruct((t, d), F32))
        out_specs.append(pl.BlockSpec((tm, d), row))
    out_shape.append(jax.ShapeDtypeStruct((t, d), u_dtype))
    out_specs.append(pl.BlockSpec((tm, d), row))
    return pl.pallas_call(
        functools.partial(_norm_kernel, combine=combine, write_h=write_h),
        grid=(t // tm,), in_specs=in_specs, out_specs=out_specs, out_shape=out_shape,
        compiler_params=_cparams(("arbitrary",)), name="combine_norm",
    )(*args)


def _final_kernel(h_ref, c0_ref, c1_ref, wt_ref, nw_ref, o_ref):
    h = _combine_rows(h_ref, c0_ref, c1_ref, wt_ref)
    ms = jnp.mean(h * h, axis=-1, keepdims=True)
    o_ref[...] = h * lax.rsqrt(ms + EPS) * nw_ref[...]


def _final_call(h, contrib, wts, nw, batch, p):
    t, d = h.shape
    seq = p - CHUNK
    tm = _row_tile(seq, 256)
    n_out = seq // tm
    row0 = lambda b, i: b * p + CHUNK + i * tm
    src = lambda b, i: (pl.multiple_of(row0(b, i), CHUNK), 0)
    slab0 = lambda b, i: (pl.multiple_of(row0(b, i) * SLAB, CHUNK), 0)
    slab1 = lambda b, i: (pl.multiple_of((t + row0(b, i)) * SLAB, CHUNK), 0)
    win = lambda r, w, fn: pl.BlockSpec((pl.Element(r), pl.Element(w)), fn)
    return pl.pallas_call(
        _final_kernel, grid=(batch, n_out),
        in_specs=[win(tm, d, src), win(tm * SLAB, 128, slab0), win(tm * SLAB, 128, slab1), win(tm, 2, src),
                  pl.BlockSpec((1, d), lambda b, i: (0, 0))],
        out_specs=pl.BlockSpec((None, tm, d), lambda b, i: (b, i, 0)),
        out_shape=jax.ShapeDtypeStruct((batch, seq, d), F32),
        compiler_params=_cparams(("arbitrary", "arbitrary")), name="final_norm",
    )(h, contrib, contrib, wts, nw.reshape(1, d))


def _matmul_kernel(x_ref, w_ref, o_ref):
    o_ref[...] = _dot(x_ref[...], w_ref[...])


def _inproj_call(u, w, tm_target=1056, tn=768):
    t, d = u.shape
    n = w.shape[1]
    tm = _row_tile(t, tm_target)
    assert n % tn == 0
    return pl.pallas_call(
        _matmul_kernel, grid=(t // tm, n // tn),
        in_specs=[pl.BlockSpec((tm, d), lambda i, j: (i, 0)), pl.BlockSpec((d, tn), lambda i, j: (0, j))],
        out_specs=pl.BlockSpec((tm, tn), lambda i, j: (i, j)),
        out_shape=jax.ShapeDtypeStruct((t, n), F32),
        compiler_params=_cparams(("arbitrary", "arbitrary")), name="in_proj",
    )(u, w)


def _hgrn2_kernel(q_ref, f_ref, i_ref, g_ref, par_ref, o_ref, st_ref, *, rows):
    s = pl.program_id(1)

    @pl.when(s == 0)
    def _():
        st_ref[...] = jnp.zeros_like(st_ref)

    c = HG_CHUNK
    ones = jnp.ones((HG_KDIM, 128), BF16)
    rid = lax.broadcasted_iota(jnp.int32, (c, 128), 0)
    scale = HG_KDIM ** -0.5

    hc = c // 2

    def chunk(ci, carry):
        r0 = pl.multiple_of(ci * c, c)
        pad = (s * rows + r0 + rid) < LEAD_PAD

        def front(h):
            cols = slice(h * 128, (h + 1) * 128)
            a_lb = par_ref[0:1, cols]
            b_lb = par_ref[1:2, cols]
            c_lb = par_ref[2:3, cols]
            z = f_ref[pl.ds(r0, c), cols]
            sg = _sigmoid(z)
            f = a_lb + b_lb * sg
            log_f = jnp.where(pad, 0.0, jnp.log(f))
            k = jnp.where(pad, 0.0, b_lb * (1.0 - sg) - c_lb)
            q = q_ref[pl.ds(r0, c), cols] * scale
            v = i_ref[pl.ds(r0, c), cols]
            cum = log_f
            for sh in (1, 2, 4, 8):
                cum = cum + jnp.where(rid >= sh, pltpu.roll(cum, sh, axis=0), 0.0)
            parts = []
            for s_ in range(c):
                lo = 0 if s_ < hc else hc
                rel = jnp.where(rid[lo:] >= s_, cum[lo:] - cum[s_:s_ + 1, :], NEG)
                parts.append(q[lo:] * (k[s_:s_ + 1, :] * jnp.exp(rel)))
            sc = _dot(jnp.concatenate(parts, axis=0).astype(BF16), ones)
            st = st_ref[h]
            o_inter = _dot_nt((q * jnp.exp(cum)).astype(BF16), st.astype(BF16))
            last = cum[c - 1:c, :]
            kd = (k * jnp.exp(last - cum)).astype(BF16)
            st_ref[h] = st * jnp.exp(last) + _dot_tn(v.astype(BF16), kd)
            return sc, o_inter, v

        def back(h, sc, o_inter, v):
            cols = slice(h * 128, (h + 1) * 128)
            o_top = o_inter[:hc]
            o_bot = o_inter[hc:]
            for s_ in range(hc):
                o_top = o_top + sc[s_ * c:s_ * c + hc, :] * v[s_:s_ + 1, :]
                o_bot = o_bot + sc[s_ * c + hc:(s_ + 1) * c, :] * v[s_:s_ + 1, :]
            for s_ in range(hc, c):
                r_ = hc * c + (s_ - hc) * hc
                o_bot = o_bot + sc[r_:r_ + hc, :] * v[s_:s_ + 1, :]
            o = jnp.concatenate([o_top, o_bot], axis=0)
            ms = jnp.mean(o * o, axis=-1, keepdims=True)
            g = g_ref[pl.ds(r0, c), cols]
            o_ref[pl.ds(r0, c), cols] = (o * lax.rsqrt(ms + EPS) * par_ref[3:4, cols] * _silu(g)).astype(o_ref.dtype)

        pending = front(0)
        for h in range(1, HG_HEADS):
            nxt = front(h)
            back(h - 1, *pending)
            pending = nxt
        back(HG_HEADS - 1, *pending)
        return carry

    n_chunks = rows // c
    lax.fori_loop(0, n_chunks, chunk, 0, unroll=3 if n_chunks % 3 == 0 else 1)


def _hgrn2_call(proj, par, batch, p):
    t = proj.shape[0]
    rows = _row_tile(p, 528)
    nb = p // rows
    w = HG_KEY
    blk = lambda j: pl.BlockSpec((rows, w), lambda b, s, j=j: (b * nb + s, C_HG // w + j))
    return pl.pallas_call(
        functools.partial(_hgrn2_kernel, rows=rows), grid=(batch, nb),
        in_specs=[blk(0), blk(1), blk(2), blk(3), pl.BlockSpec((8, w), lambda b, s: (0, 0))],
        out_specs=pl.BlockSpec((rows, w), lambda b, s: (b * nb + s, 0)),
        out_shape=jax.ShapeDtypeStruct((t, HG_WIDTH), BF16),
        scratch_shapes=[pltpu.VMEM((HG_HEADS, 128, HG_KDIM), F32)],
        compiler_params=_cparams(("arbitrary", "arbitrary")), name="hgrn2",
    )(proj, proj, proj, proj, par)


def _mlstm_kernel(v_ref, o_ref, q_ref, k_ref, sm_ref, par_ref, nw_ref, tri_ref, sel_ref, dg_ref,
                  y_ref, c_ref, m_ref):
    s = pl.program_id(0)

    @pl.when(s == 0)
    def _():
        c_ref[...] = jnp.zeros_like(c_ref)
        m_ref[...] = jnp.zeros_like(m_ref)

    n = CHUNK
    rid = lax.broadcasted_iota(jnp.int32, (n, 128), 0)
    pad = (s * n + rid) < LEAD_PAD
    sel_i = sel_ref[0]
    sel_f = sel_ref[1]
    dg = dg_ref[...]
    w = ML_HEADS * n
    tt = lax.broadcasted_iota(jnp.int32, (n, w), 0)
    ss = lax.broadcasted_iota(jnp.int32, (n, w), 1) & (n - 1)
    causal = tt >= ss
    scale = ML_QK ** -0.5
    lane = lax.broadcasted_iota(jnp.int32, (n, 128), 1)
    one_col = jnp.where(lane == 0, 1.0, 0.0).astype(BF16)
    for b in range(v_ref.shape[0]):
        pre = sm_ref[b] + par_ref[0:1, :]
        cap = GATE_CAP * jnp.tanh(pre * (1.0 / GATE_CAP))
        log_i = jnp.where(pad, NEG, cap)
        log_f = jnp.where(pad, 0.0, _log_sigmoid(cap))
        cum = _sel_dot(tri_ref[...], log_f)
        cum_col = _dot_sel(cum, sel_f)
        cum_row = jnp.sum(cum_col * dg, axis=0, keepdims=True)
        li_row = jnp.sum(_dot_sel(log_i, sel_i) * dg, axis=0, keepdims=True)
        dmat = jnp.where(causal, cum_col - cum_row + li_row, NEG)
        hs = range(ML_HEADS)
        sh = [b * ML_HEADS + h for h in hs]
        d_h = [dmat[:, h * n:(h + 1) * n] for h in hs]
        cum_h = [cum[:, LANE_MF + h:LANE_MF + h + 1] for h in hs]
        li_h = [log_i[:, LANE_MI + h:LANE_MI + h + 1] for h in hs]
        m_st = [m_ref[sh[h]:sh[h] + 1, 0:1] for h in hs]
        q = [(q_ref[b, :, h * ML_QK:(h + 1) * ML_QK] * scale).astype(BF16) for h in hs]
        k = [k_ref[b, :, h * ML_QK:(h + 1) * ML_QK] for h in hs]
        v_aug = [jnp.concatenate([v_ref[b, :, h * ML_V:(h + 1) * ML_V].astype(BF16), one_col], axis=1) for h in hs]
        c_prev = [c_ref[sh[h]] for h in hs]
        inter = [cum_h[h] + m_st[h] for h in hs]
        m_t = [jnp.maximum(inter[h], jnp.max(d_h[h], axis=-1, keepdims=True)) for h in hs]
        qk = [_dot_nt(q[h], k[h].astype(BF16)) for h in hs]
        qc = [_dot(q[h], c_prev[h].astype(BF16)) for h in hs]
        pw = [(qk[h] * jnp.exp(d_h[h] - m_t[h])).astype(BF16) for h in hs]
        nd = [_dot(pw[h], v_aug[h]) + jnp.exp(inter[h] - m_t[h]) * qc[h] for h in hs]
        hh = [nd[h][:, :ML_V] / jnp.maximum(jnp.abs(nd[h][:, ML_V:ML_V + 1]), jnp.exp(-m_t[h])) for h in hs]
        tot = [cum_h[h][n - 1:n, :] for h in hs]
        to_end = [tot[h] - cum_h[h] + li_h[h] for h in hs]
        m_loc = [jnp.max(to_end[h], axis=0, keepdims=True) for h in hs]
        kw = [(k[h] * jnp.exp(to_end[h] - m_loc[h])).astype(BF16) for h in hs]
        c_loc = [_dot_tn(kw[h], v_aug[h]) for h in hs]
        m_new = [jnp.maximum(tot[h] + m_st[h], m_loc[h]) for h in hs]
        for h in hs:
            c_ref[sh[h]] = (jnp.exp(tot[h] + m_st[h] - m_new[h]) * c_prev[h]
                            + jnp.exp(m_loc[h] - m_new[h]) * c_loc[h])
            m_ref[sh[h]:sh[h] + 1, :] = jnp.broadcast_to(m_new[h], (1, 128))
        for h in hs:
            ms = jnp.mean(hh[h] * hh[h], axis=-1, keepdims=True)
            cols = slice(h * ML_V, (h + 1) * ML_V)
            y_ref[b, :, cols] = (hh[h] * lax.rsqrt(ms + EPS) * nw_ref[:, cols]
                                 * _sigmoid(o_ref[b, :, cols])).astype(y_ref.dtype)


def _lane_select(lane0, heads, width):
    m = np.zeros((128, heads * width), np.float32)
    for h in range(heads):
        m[lane0 + h, h * width:(h + 1) * width] = 1.0
    return m


def _diag_mask(n, heads):
    return np.tile(np.eye(n, dtype=np.float32), (1, heads))


def _mlstm_call(proj, par, nw, batch, p):
    t = proj.shape[0]
    n = CHUNK
    nc = p // n
    tri = jnp.asarray(np.tril(np.ones((n, n), np.float32)), BF16)
    sel = jnp.asarray(np.stack([_lane_select(LANE_MI, ML_HEADS, n), _lane_select(LANE_MF, ML_HEADS, n)]), BF16)
    dg = jnp.asarray(_diag_mask(n, ML_HEADS), F32)
    proj3 = proj.reshape(batch, p, proj.shape[1])
    blk = lambda w, off: pl.BlockSpec((batch, n, w), lambda s: (0, s, off // w))
    const = lambda shape: pl.BlockSpec(shape, lambda s: (0,) * len(shape))
    m_rows = -(-batch * ML_HEADS // 8) * 8
    y = pl.pallas_call(
        _mlstm_kernel, grid=(nc,),
        in_specs=[blk(ML_WIDTH, C_MV), blk(ML_WIDTH, C_MV + ML_WIDTH), blk(ML_QK_W, C_MQ),
                  blk(ML_QK_W, C_MQ + ML_QK_W), blk(128, C_SMALL),
                  const((8, 128)), const((1, ML_WIDTH)), const((n, n)), const((2, 128, ML_HEADS * n)),
                  const((n, ML_HEADS * n))],
        out_specs=pl.BlockSpec((batch, n, ML_WIDTH), lambda s: (0, s, 0)),
        out_shape=jax.ShapeDtypeStruct((batch, p, ML_WIDTH), BF16),
        scratch_shapes=[pltpu.VMEM((batch * ML_HEADS, ML_QK, 2 * ML_V), F32), pltpu.VMEM((m_rows, 128), F32)],
        compiler_params=_cparams(("arbitrary",)), name="mlstm",
    )(proj3, proj3, proj3, proj3, proj3, par, nw, tri, sel, dg)
    return y.reshape(t, ML_WIDTH)


def _ssd_kernel(z_ref, x_ref, b_ref, c_ref, sm_ref, cw_ref, cb_ref, par_ref, dsk_ref, nw_ref,
                tri_ref, sel_ref, dg_ref, y_ref, xs_ref, bs_ref, cs_ref, st_ref):
    s = pl.program_id(1)
    n = CHUNK
    tail = 8

    @pl.when(s == 0)
    def _():
        st_ref[...] = jnp.zeros_like(st_ref)
        xs_ref[0:tail, :] = jnp.zeros((tail, xs_ref.shape[1]), F32)
        bs_ref[0:tail, :] = jnp.zeros((tail, bs_ref.shape[1]), F32)
        cs_ref[0:tail, :] = jnp.zeros((tail, cs_ref.shape[1]), F32)

    def conv_silu(src_ref, scr_ref, c0, width, rowmask):
        scr_ref[tail:tail + n, :] = src_ref[...]
        acc = cb_ref[:, c0:c0 + width]
        for j in range(CONV_W):
            off = tail - (CONV_W - 1) + j
            acc = acc + cw_ref[j:j + 1, c0:c0 + width] * scr_ref[off:off + n, :]
        scr_ref[0:tail, :] = scr_ref[n:n + tail, :]
        return jnp.where(rowmask, 0.0, _silu(acc))

    def padmask(width):
        return (s * n + lax.broadcasted_iota(jnp.int32, (n, width), 0)) < LEAD_PAD

    x = conv_silu(x_ref, xs_ref, 0, SSM_WIDTH, padmask(SSM_WIDTH))
    bm = conv_silu(b_ref, bs_ref, SSM_WIDTH, SSM_BC, padmask(SSM_BC))
    cm = conv_silu(c_ref, cs_ref, SSM_WIDTH + SSM_BC, SSM_BC, padmask(SSM_BC))

    pre = sm_ref[...] + par_ref[0:1, :]
    dt = jnp.maximum(pre, 0.0) + jnp.log1p(jnp.exp(-jnp.abs(pre)))
    dt = jnp.where(padmask(128), 0.0, dt)
    da = dt * (-jnp.exp(par_ref[1:2, :]))
    cum = _sel_dot(tri_ref[...], da)
    sel = sel_ref[...]
    dt_col = _dot_sel(dt, sel)
    cum_col = _dot_sel(cum, sel)
    cum_row = jnp.sum(cum_col * dg_ref[...], axis=0, keepdims=True)
    w = SSM_WIDTH
    tt = lax.broadcasted_iota(jnp.int32, (n, w), 0)
    ss = lax.broadcasted_iota(jnp.int32, (n, w), 1) & (n - 1)
    decay = jnp.exp(jnp.where(tt >= ss, cum_col - cum_row, NEG))
    xdt = x * dt_col
    last = cum_col[n - 1:n, :]
    wend = (xdt * jnp.exp(last - cum_col)).astype(BF16)
    chunk_decay = jnp.exp(last)
    ecum = jnp.exp(cum_col)
    gw = SSM_HPG * SSM_HEADDIM
    rr = lax.broadcasted_iota(jnp.int32, (gw, gw), 0) // SSM_HEADDIM
    cc = lax.broadcasted_iota(jnp.int32, (gw, gw), 1) // SSM_HEADDIM
    blockdiag = rr == cc
    ys = []
    for g in range(SSM_GROUPS):
        gl = slice(g * gw, (g + 1) * gw)
        sl = slice(g * SSM_STATE, (g + 1) * SSM_STATE)
        cm_g = cm[:, sl].astype(BF16)
        bm_g = bm[:, sl].astype(BF16)
        cb = _dot_nt(cm_g, jnp.concatenate([bm_g] * SSM_HPG, axis=0))
        m = (cb * decay[:, gl]).astype(BF16)
        xdt_g = xdt[:, gl]
        bd = jnp.where(blockdiag, jnp.concatenate([xdt_g] * SSM_HPG, axis=0), 0.0).astype(BF16)
        st = st_ref[g]
        y_g = _dot(m, bd) + ecum[:, gl] * _dot(cm_g, st.astype(BF16))
        st_ref[g] = st * chunk_decay[:, gl] + _dot_tn(bm_g, wend[:, gl])
        ys.append(y_g)
    y = jnp.concatenate(ys, axis=1) + x * dsk_ref[...]
    y = y * _silu(z_ref[...])
    outs = []
    for g in range(SSM_GROUPS):
        gl = slice(g * gw, (g + 1) * gw)
        y_g = y[:, gl]
        ms = jnp.mean(y_g * y_g, axis=-1, keepdims=True)
        outs.append(y_g * lax.rsqrt(ms + EPS))
    y_ref[...] = (jnp.concatenate(outs, axis=1) * nw_ref[...]).astype(y_ref.dtype)


def _ssd_call(proj, cw, cb, par, dsk, nw, batch, p):
    t = proj.shape[0]
    n = CHUNK
    nc = p // n
    tri = jnp.asarray(np.tril(np.ones((n, n), np.float32)), BF16)
    sel = jnp.asarray(_lane_select(LANE_DT, SSM_HEADS, SSM_HEADDIM), BF16)
    dg = jnp.asarray(_diag_mask(n, SSM_HEADS), F32)
    blk = lambda w, off: pl.BlockSpec((n, w), lambda b, s: (b * nc + s, off // w))
    const = lambda shape: pl.BlockSpec(shape, lambda b, s: (0,) * len(shape))
    cch = SSM_WIDTH + 2 * SSM_BC
    return pl.pallas_call(
        _ssd_kernel, grid=(batch, nc),
        in_specs=[blk(SSM_WIDTH, C_SZ), blk(SSM_WIDTH, C_SX), blk(SSM_BC, C_SX + SSM_WIDTH),
                  blk(SSM_BC, C_SX + SSM_WIDTH + SSM_BC), blk(128, C_SMALL),
                  const((CONV_W, cch)), const((1, cch)), const((8, 128)), const((1, SSM_WIDTH)),
                  const((1, SSM_WIDTH)), const((n, n)), const((128, SSM_WIDTH)), const((n, SSM_WIDTH))],
        out_specs=pl.BlockSpec((n, SSM_WIDTH), lambda b, s: (b * nc + s, 0)),
        out_shape=jax.ShapeDtypeStruct((t, SSM_WIDTH), BF16),
        scratch_shapes=[pltpu.VMEM((n + 8, SSM_WIDTH), F32), pltpu.VMEM((n + 8, SSM_BC), F32),
                        pltpu.VMEM((n + 8, SSM_BC), F32),
                        pltpu.VMEM((SSM_GROUPS, SSM_STATE, SSM_HPG * SSM_HEADDIM), F32)],
        compiler_params=_cparams(("arbitrary", "arbitrary")), name="ssd",
    )(proj, proj, proj, proj, proj, cw, cb, par, dsk, nw, tri, sel, dg)


LANE_E0 = N_GROUPS_MOE
RT_E, RT_RANK, RT_W = 0, 2, 4


def _first_max(vals, lane):
    m = jnp.max(vals, axis=-1, keepdims=True)
    idx = jnp.min(jnp.where(vals == m, lane.astype(F32), 128.0), axis=-1, keepdims=True)
    return m, idx.astype(jnp.int32)


def _outproj_kernel(ya_ref, yb_ref, yc_ref, h_ref, w_ref, nw_ref, rh_ref, rl_ref, rb_ref, tri_ref,
                    hm_ref, u_ref, rt_ref, cnt_ref):
    a0, a1 = HG_WIDTH, HG_WIDTH + ML_WIDTH
    h = h_ref[...]
    h = h + _dot(ya_ref[...], w_ref[0:a0, :])
    h = h + _dot(yb_ref[...], w_ref[a0:a1, :])
    h = h + _dot(yc_ref[...], w_ref[a1:, :])
    hm_ref[...] = h
    ms = jnp.mean(h * h, axis=-1, keepdims=True)
    u = h * lax.rsqrt(ms + EPS) * nw_ref[...]
    _pack_slab(u_ref, u)
    u_hi = u.astype(BF16)
    u_lo = (u - u_hi.astype(F32)).astype(BF16)
    lg = _dot(u_hi, rh_ref[...]) + (_dot(u_lo, rh_ref[...]) + _dot(u_hi, rl_ref[...])) + rb_ref[...]

    tm = lg.shape[0]
    lane = lax.broadcasted_iota(jnp.int32, (tm, 128), 1)
    g_mask = lane < N_GROUPS_MOE
    g_max, g_sel = _first_max(jnp.where(g_mask, lg, NEG), lane)
    g_gate = 1.0 / jnp.sum(jnp.where(g_mask, jnp.exp(lg - g_max), 0.0), axis=-1, keepdims=True)
    lo = LANE_E0 + g_sel * EXPERTS_PER_GROUP
    e_vals = jnp.where((lane >= lo) & (lane < lo + EXPERTS_PER_GROUP), lg, NEG)
    v1, i1 = _first_max(e_vals, lane)
    v2, i2 = _first_max(jnp.where(lane == i1, NEG, e_vals), lane)
    a = jnp.exp(v2 - v1)
    w1 = g_gate / (1.0 + a)
    w2 = w1 * a
    @pl.when(pl.program_id(0) == 0)
    def _():
        cnt_ref[...] = jnp.zeros_like(cnt_ref)

    hit1 = lane == i1
    hit2 = lane == i2
    onehot = jnp.where(hit1 | hit2, 1.0, 0.0)
    before = _dot(tri_ref[...], onehot.astype(BF16)) + cnt_ref[0:1, :]
    r1 = jnp.sum(jnp.where(hit1, before, 0.0), axis=-1, keepdims=True)
    r2 = jnp.sum(jnp.where(hit2, before, 0.0), axis=-1, keepdims=True)
    cnt_ref[...] = cnt_ref[...] + jnp.sum(onehot, axis=0, keepdims=True)
    rec = jnp.zeros((tm, 128), F32)
    for ln, val in ((RT_E, (i1 - LANE_E0).astype(F32)), (RT_E + 1, (i2 - LANE_E0).astype(F32)),
                    (RT_RANK, r1), (RT_RANK + 1, r2), (RT_W, w1), (RT_W + 1, w2)):
        rec = jnp.where(lane == ln, val, rec)
    rt_ref[...] = rec


def _outproj_call(ya, yb, yc, h, w, nw, r_hi, r_lo, r_bias, tm_target=352):
    t, d = h.shape
    tm = _row_tile(t, tm_target)
    tri = jnp.asarray(np.tril(np.ones((tm, tm), np.float32), -1), BF16)
    row = lambda i: (i, 0)
    const = lambda shape: pl.BlockSpec(shape, lambda i: (0, 0))
    return pl.pallas_call(
        _outproj_kernel, grid=(t // tm,),
        in_specs=[pl.BlockSpec((tm, HG_WIDTH), row), pl.BlockSpec((tm, ML_WIDTH), row),
                  pl.BlockSpec((tm, SSM_WIDTH), row), pl.BlockSpec((tm, d), row),
                  const((D_MIX, d)), const((1, d)), const((d, 128)), const((d, 128)), const((1, 128)),
                  const((tm, tm))],
        out_specs=[pl.BlockSpec((tm, d), row), pl.BlockSpec((tm * SLAB, 128), row), pl.BlockSpec((tm, 128), row),
                   const((8, 128))],
        out_shape=[jax.ShapeDtypeStruct((t, d), F32), jax.ShapeDtypeStruct((t * SLAB, 128), U32),
                   jax.ShapeDtypeStruct((t, 128), F32), jax.ShapeDtypeStruct((8, 128), F32)],
        compiler_params=_cparams(("arbitrary",)), name="out_proj_router",
    )(ya, yb, yc, h, w, nw.reshape(1, d), r_hi, r_lo, r_bias, tri)


MOE_RING = 3
WT_FIRST, WT_SLOT, WT_NEXT = 0, 1, 2


TAB_SRC2, TAB_DST_PREV, TAB_DST = 0, 1, 2


def _moe_kernel(be_ref, nu_ref, wt_ref, src01_ref, tab_ref, u_hbm,
                w1_hbm, w3_hbm, w2_hbm, o_hbm, xbuf, ybuf, st1, st3, st2, w1b, w3b, w2b, gsem, ssem, wsem,
                *, layer):
    i = pl.program_id(0)
    bm = MOE_BM
    n_used = nu_ref[0]
    xs = i % MOE_RING
    slot = xs
    other = (i + MOE_RING - 1) % MOE_RING

    def slab(idx):
        return pl.ds(pl.multiple_of(idx * SLAB, SLAB), SLAB)

    def row_in(tok, r, sl):
        return pltpu.make_async_copy(u_hbm.at[slab(tok), :], xbuf.at[sl, slab(r), :], gsem.at[sl])

    def row_out(r, row, sl):
        return pltpu.make_async_copy(ybuf.at[sl, slab(r), :], o_hbm.at[slab(row), :], ssem.at[sl])

    def gather(section, sl):
        def body(g, c):
            for j in range(8):
                r = g * 8 + j
                row_in(src01_ref[0, section * bm + r], r, sl).start(priority=j % 2)
            return c
        lax.fori_loop(0, bm // 8, body, 0)

    def scatter(sl):
        def body(g, c):
            for j in range(8):
                r = g * 8 + j
                row_out(r, tab_ref[0, TAB_DST * bm + r], sl).start(priority=j % 2)
            return c
        lax.fori_loop(0, bm // 8, body, 0)

    def wait_gather(sl):
        pltpu.make_async_copy(u_hbm.at[pl.ds(0, bm * SLAB), :], xbuf.at[sl], gsem.at[sl]).wait()

    def wait_scatter(sl):
        pltpu.make_async_copy(ybuf.at[sl], o_hbm.at[pl.ds(0, bm * SLAB), :], ssem.at[sl]).wait()

    def weight_copies(e, p):
        return [pltpu.make_async_copy(src.at[layer, e], dst.at[p], wsem.at[p])
                for src, dst in ((w1_hbm, st1), (w3_hbm, st3), (w2_hbm, st2))]

    n_real = o_hbm.shape[0] // SLAB - MOE_RING * bm

    @pl.when(i == 0)
    def _():
        for c in weight_copies(be_ref[0], 0):
            c.start()
        gather(0, 0)
        gather(1, 1)
        ybuf[...] = jnp.zeros_like(ybuf)
        for sl in range(2):
            pltpu.make_async_copy(ybuf.at[sl], o_hbm.at[pl.ds((n_real + sl * bm) * SLAB, bm * SLAB), :],
                                  ssem.at[sl]).start()

    @pl.when(i < n_used)
    def _():
        @pl.when(wt_ref[WT_FIRST, i] == 1)
        def _():
            p = wt_ref[WT_SLOT, i]
            nxt = wt_ref[WT_NEXT, i]

            @pl.when(nxt >= 0)
            def _():
                for c in weight_copies(nxt, 1 - p):
                    c.start(priority=1)

            for c in weight_copies(be_ref[i], p):
                c.wait()
            w1b[...] = st1[p].astype(BF16)
            w3b[...] = st3[p].astype(BF16)
            w2b[...] = st2[p].astype(BF16)

        wait_gather(xs)

        xn = (i + 2) % MOE_RING

        def issue(g, c):
            for j in range(8):
                r = g * 8 + j
                row_in(tab_ref[0, TAB_SRC2 * bm + r], r, xn).start(priority=0)
                row_out(r, tab_ref[0, TAB_DST_PREV * bm + r], other).start(priority=1)
            return c
        lax.fori_loop(0, bm // 8, issue, 0)

        f = w1b.shape[1]
        h1 = jnp.zeros((bm, f), F32)
        h3 = jnp.zeros((bm, f), F32)
        for j in range(SLAB):
            lo, hi = _unpack_slab(xbuf, j, bm, lead=(xs,))
            xk = jnp.concatenate([lo, hi], axis=1).astype(BF16)
            h1 = h1 + _dot(xk, w1b[j * 256:(j + 1) * 256, :])
            h3 = h3 + _dot(xk, w3b[j * 256:(j + 1) * 256, :])
        hid = (_silu(h1) * h3).astype(BF16)
        wait_scatter(slot)
        half = SLAB // 2
        _pack_slab(ybuf, _dot(hid, w2b[:, :half * 256]), lead=(slot,))
        _pack_slab(ybuf, _dot(hid, w2b[:, half * 256:]), lead=(slot,), j0=half)

    @pl.when(i == n_used - 1)
    def _():
        scatter(slot)
        for k in range(MOE_RING):
            wait_scatter(k)
        wait_gather((i + 1) % MOE_RING)
        wait_gather((i + 2) % MOE_RING)


def _moe_call(block_e, n_used, w_table, src_tok, dst_row, u, w1, w3, w2, layer, n_out_rows):
    d, f = w1.shape[-2:]
    assert d == SLAB * 256
    bm = MOE_BM
    nb = block_e.shape[0]
    src_tok = src_tok.reshape(nb, bm)
    dst_row = dst_row.reshape(nb, bm)
    lead = (n_out_rows - bm + jnp.arange(bm, dtype=jnp.int32)).reshape(1, bm)
    src_ahead = jnp.concatenate([src_tok[2:], src_tok[-1:], src_tok[-1:]], axis=0)
    dst_prev = jnp.concatenate([lead, dst_row[:-1]], axis=0)
    table = jnp.concatenate([src_ahead, dst_prev, dst_row], axis=1).reshape(nb, 1, 3 * bm)
    src01 = src_tok[:2].reshape(1, 1, 2 * bm)
    idx = lambda w, fn: pl.BlockSpec((None, 1, w), fn, memory_space=pltpu.SMEM)
    hbm = pl.BlockSpec(memory_space=pl.ANY)
    ring = pltpu.VMEM((MOE_RING, bm * SLAB, 128), U32)
    return pl.pallas_call(
        functools.partial(_moe_kernel, layer=layer),
        grid_spec=pltpu.PrefetchScalarGridSpec(
            num_scalar_prefetch=3, grid=(nb,),
            in_specs=[idx(2 * bm, lambda i, *_: (0, 0, 0)), idx(3 * bm, lambda i, *_: (i, 0, 0)),
                      hbm, hbm, hbm, hbm],
            out_specs=hbm,
            scratch_shapes=[ring, ring,
                            pltpu.VMEM((2, d, f), F32), pltpu.VMEM((2, d, f), F32), pltpu.VMEM((2, f, d), F32),
                            pltpu.VMEM((d, f), BF16), pltpu.VMEM((d, f), BF16), pltpu.VMEM((f, d), BF16),
                            pltpu.SemaphoreType.DMA((MOE_RING,)), pltpu.SemaphoreType.DMA((MOE_RING,)),
                            pltpu.SemaphoreType.DMA((2,))]),
        out_shape=jax.ShapeDtypeStruct((n_out_rows * SLAB, 128), U32),
        compiler_params=_cparams(("arbitrary",)), name="moe_ffn",
    )(block_e, n_used, w_table, src01, table, u, w1, w3, w2)


def _route_tables(rt, cnt, t):
    bm = MOE_BM
    tk = t * TOP_K
    e = rt[:, RT_E:RT_E + TOP_K].astype(jnp.int32)
    rank = rt[:, RT_RANK:RT_RANK + TOP_K].astype(jnp.int32)
    wts = rt[:, RT_W:RT_W + TOP_K]
    counts = cnt[0, LANE_E0:LANE_E0 + N_EXPERTS].astype(jnp.int32)
    padded = (counts + bm - 1) // bm * bm
    pad_ends = jnp.cumsum(padded)
    pad_starts = pad_ends - padded
    e_ids = jnp.arange(N_EXPERTS, dtype=jnp.int32)

    def lookup(table, idx):
        return jnp.sum(jnp.where(idx[..., None] == e_ids, table, 0), axis=-1)

    dest = (lookup(pad_starts, e) + rank).reshape(-1)
    n_blocks = -(-tk // bm) + N_EXPERTS
    n_rows = n_blocks * bm
    inv = jnp.full((n_rows,), -1, jnp.int32).at[dest].set(jnp.arange(tk, dtype=jnp.int32))
    tok, slot = inv // TOP_K, inv % TOP_K
    src_tok = jnp.where(inv >= 0, tok, 0)
    dst_row = jnp.where(inv >= 0, slot * t + tok, tk + (jnp.arange(n_rows, dtype=jnp.int32) % (MOE_RING * bm)))
    block_ids = jnp.arange(n_blocks, dtype=jnp.int32)
    block_e = jnp.minimum(jnp.sum(pad_ends[None, :] <= block_ids[:, None] * bm, axis=1), N_EXPERTS - 1).astype(jnp.int32)
    n_used = (pad_ends[-1] // bm).astype(jnp.int32).reshape(1)
    has_rows = counts > 0
    at_or_after = lax.cummin(jnp.where(has_rows, e_ids, N_EXPERTS)[::-1])[::-1]
    next_e = jnp.concatenate([at_or_after[1:], jnp.full((1,), N_EXPERTS, jnp.int32)])
    next_e = jnp.where(next_e >= N_EXPERTS, -1, next_e)
    stage = (jnp.cumsum(has_rows.astype(jnp.int32)) - 1) % 2
    first = jnp.concatenate([jnp.ones((1,), bool), block_e[1:] != block_e[:-1]]) & (block_ids < n_used[0])
    w_table = jnp.stack([first.astype(jnp.int32), lookup(stage, block_e), lookup(next_e, block_e)]).astype(jnp.int32)
    return block_e, n_used, w_table, src_tok, dst_row, wts, tk + MOE_RING * bm


def _lane_row(pairs):
    row = jnp.zeros((128,), F32)
    for lane0, vals in pairs:
        row = row.at[lane0:lane0 + vals.shape[0]].set(vals.astype(F32))
    return row


def kernel(x, meta_tokens, hg_lb_logits, norm_mix_w, w_in, hg_norm_w, ml_b_i, ml_b_f, ml_norm_w,
           ssm_conv_w, ssm_conv_b, ssm_dt_bias, ssm_a_log, ssm_d, ssm_norm_w, w_out, norm_ffn_w,
           moe_w_group, moe_b_group, moe_w_router, moe_b_router, moe_w1, moe_w3, moe_w2, final_norm_w):
    batch, seq, d = x.shape
    depth = w_in.shape[0]
    p = LEAD_PAD + N_META + seq
    t = batch * p
    meta = jnp.broadcast_to(meta_tokens.astype(x.dtype)[None], (batch, N_META, d))
    h = jnp.concatenate([jnp.zeros((batch, LEAD_PAD, d), x.dtype), meta, x], axis=1).reshape(t, d)

    lb_w = jax.nn.softmax(hg_lb_logits.astype(F32), axis=0)
    lower_bounds = jnp.cumsum(lb_w, axis=0) - lb_w[0]

    contrib = wts = None
    for layer in range(depth):
        w = w_in[layer]
        o_mq, o_mv, o_mi, o_sz = HG_KEY * 2 + HG_WIDTH * 2, 2560, 3584, 3592
        o_sx = o_sz + SSM_WIDTH
        o_dt = o_sx + SSM_WIDTH + 2 * SSM_BC
        w_perm = jnp.concatenate([
            w[:, :o_mq], w[:, o_sz:o_sx], w[:, o_sx:o_dt], w[:, o_mv:o_mi], w[:, o_mq:o_mv],
            w[:, o_mi:o_sz], w[:, o_dt:], jnp.zeros((d, N_PROJ - C_SMALL - 2 * ML_HEADS - SSM_HEADS), w.dtype)],
            axis=1).astype(BF16)
        lb = lower_bounds[layer]
        lbf = jnp.maximum(lb, LB_FLOOR)
        hg_par = jnp.zeros((8, HG_KEY), F32).at[0].set(lbf).at[1].set(1.0 - lb).at[2].set(lbf - lb).at[3].set(hg_norm_w[layer])
        ml_par = jnp.zeros((8, 128), F32).at[0].set(_lane_row([(LANE_MI, ml_b_i[layer]), (LANE_MF, ml_b_f[layer])]))
        ss_par = jnp.zeros((8, 128), F32).at[0].set(_lane_row([(LANE_DT, ssm_dt_bias[layer])]))
        ss_par = ss_par.at[1].set(_lane_row([(LANE_DT, ssm_a_log[layer])]))
        dskip = jnp.repeat(ssm_d[layer].astype(F32), SSM_HEADDIM).reshape(1, SSM_WIDTH)
        w_r = jnp.concatenate([moe_w_group[layer],
                               moe_w_router[layer].transpose(1, 0, 2).reshape(d, N_EXPERTS),
                               jnp.zeros((d, 128 - N_GROUPS_MOE - N_EXPERTS), F32)], axis=1)
        r_hi = w_r.astype(BF16)
        r_lo = (w_r - r_hi.astype(F32)).astype(BF16)

        if layer == 0:
            (u,) = _norm_call(h, norm_mix_w[layer], write_h=False, u_dtype=BF16)
        else:
            h, u = _norm_call(h, norm_mix_w[layer], contrib, wts, write_h=True, u_dtype=BF16)
        proj = _inproj_call(u, w_perm)
        ya = _hgrn2_call(proj, hg_par, batch, p)
        yb = _mlstm_call(proj, ml_par, ml_norm_w[layer].reshape(1, ML_WIDTH), batch, p)
        yc = _ssd_call(proj, ssm_conv_w[layer], ssm_conv_b[layer].reshape(1, -1), ss_par, dskip,
                       ssm_norm_w[layer].reshape(1, SSM_WIDTH), batch, p)
        r_bias = _lane_row([(0, moe_b_group[layer]), (LANE_E0, moe_b_router[layer].reshape(-1))]).reshape(1, 128)
        h, u_ffn, rt, cnt = _outproj_call(ya, yb, yc, h, w_out[layer].astype(BF16), norm_ffn_w[layer],
                                          r_hi, r_lo, r_bias)
        block_e, n_used, w_table, src_tok, dst_row, wts, n_out_rows = _route_tables(rt, cnt, t)
        contrib = _moe_call(block_e, n_used, w_table, src_tok, dst_row, u_ffn, moe_w1, moe_w3, moe_w2, layer,
                            n_out_rows)
    return _final_call(h, contrib, wts, final_norm_w, batch, p)
```

```python
import functools

import jax
import jax.numpy as jnp
import numpy as np
from jax import lax
from jax.experimental import pallas as pl
from jax.experimental.pallas import tpu as pltpu

F32 = jnp.float32
BF16 = jnp.bfloat16

D_MODEL = 2048
N_META = 16
CHUNK = 64
HG_CHUNK = 16
LEAD_PAD = CHUNK - N_META
EPS = 1e-6
NEG = -1e30
LB_FLOOR = 1e-30

HG_HEADS = 4
HG_KDIM = 128
HG_KEY = HG_HEADS * HG_KDIM
HG_WIDTH = HG_HEADS * 128

ML_HEADS = 4
ML_QK = 64
ML_V = 128
ML_QK_W = ML_HEADS * ML_QK
ML_WIDTH = ML_HEADS * ML_V
GATE_CAP = 15.0

SSM_HEADS = 16
SSM_HEADDIM = 64
SSM_WIDTH = SSM_HEADS * SSM_HEADDIM
SSM_STATE = 128
SSM_GROUPS = 4
SSM_HPG = SSM_HEADS // SSM_GROUPS
SSM_BC = SSM_GROUPS * SSM_STATE
CONV_W = 4

D_MIX = HG_WIDTH + ML_WIDTH + SSM_WIDTH

N_GROUPS_MOE = 4
EXPERTS_PER_GROUP = 8
N_EXPERTS = N_GROUPS_MOE * EXPERTS_PER_GROUP
TOP_K = 2
D_EXPERT = 512
MOE_BM = 128

C_HG = 0
C_SZ = 2048
C_SX = 3072
C_MV = 5120
C_MQ = 6144
C_SMALL = 6656
N_PROJ = 6912
LANE_MI = 0
LANE_MF = ML_HEADS
LANE_DT = 2 * ML_HEADS

VMEM_LIMIT = 56 * 1024 * 1024


def _cparams(sem):
    return pltpu.CompilerParams(dimension_semantics=sem, vmem_limit_bytes=VMEM_LIMIT)


def _row_tile(n, target, mult=16):
    best = None
    for t in range(mult, min(n, target) + 1, mult):
        if n % t == 0:
            best = t
    assert best is not None, (n, target, mult)
    return best


def _split3(x):
    hi = x.astype(BF16)
    r = x - hi.astype(F32)
    mid = r.astype(BF16)
    lo = (r - mid.astype(F32)).astype(BF16)
    return hi, mid, lo


def _dot(a, b):
    return jnp.dot(a, b, preferred_element_type=F32)


def _sel_dot(sel, x):
    hi, mid, lo = _split3(x)
    return _dot(sel, hi) + _dot(sel, mid) + _dot(sel, lo)


def _dot_sel(x, sel):
    hi, mid, lo = _split3(x)
    return _dot(hi, sel) + _dot(mid, sel) + _dot(lo, sel)


def _dot_nt(a, b):
    return lax.dot_general(a, b, (((1,), (1,)), ((), ())), preferred_element_type=F32)


def _dot_tn(a, b):
    return lax.dot_general(a, b, (((0,), (0,)), ((), ())), preferred_element_type=F32)


def _log_sigmoid(x):
    return jnp.minimum(x, 0.0) - jnp.log1p(jnp.exp(-jnp.abs(x)))


def _sigmoid(x):
    return 1.0 / (1.0 + jnp.exp(-x))


def _silu(x):
    return x * _sigmoid(x)


SLAB = 8
U32 = jnp.uint32
HI_MASK = 0xFFFF0000


def _bf16_bits(x):
    return lax.bitcast_convert_type(x.astype(BF16).astype(F32), U32)


def _pack_slab(ref, val, lead=(), j0=0):
    rows = val.shape[0]
    for jj in range(val.shape[1] // 256):
        lo = _bf16_bits(val[:, jj * 256:jj * 256 + 128])
        hi = _bf16_bits(val[:, jj * 256 + 128:(jj + 1) * 256])
        ref[(*lead, pl.ds(j0 + jj, rows, stride=SLAB), slice(None))] = (lo >> 16) | (hi & U32(HI_MASK))


def _unpack_slab(ref, j, rows, lead=()):
    w = ref[(*lead, pl.ds(j, rows, stride=SLAB), slice(None))]
    return (lax.bitcast_convert_type(w << 16, F32), lax.bitcast_convert_type(w & U32(HI_MASK), F32))


def _combine_rows(h_ref, c0_ref, c1_ref, wt_ref):
    rows = h_ref.shape[0]
    wt = wt_ref[...]
    w0, w1 = wt[:, 0:1], wt[:, 1:2]
    pieces = []
    for j in range(SLAB):
        a0, b0 = _unpack_slab(c0_ref, j, rows)
        a1, b1 = _unpack_slab(c1_ref, j, rows)
        pieces += [w0 * a0 + w1 * a1, w0 * b0 + w1 * b1]
    return h_ref[...] + jnp.concatenate(pieces, axis=1)


def _norm_kernel(*refs, combine, write_h):
    if combine:
        h_ref, c0_ref, c1_ref, wt_ref, nw_ref = refs[:5]
        outs = refs[5:]
        h = _combine_rows(h_ref, c0_ref, c1_ref, wt_ref)
    else:
        h_ref, nw_ref = refs[:2]
        outs = refs[2:]
        h = h_ref[...]
    if write_h:
        outs[0][...] = h
    ms = jnp.mean(h * h, axis=-1, keepdims=True)
    u_ref = outs[-1]
    u_ref[...] = (h * lax.rsqrt(ms + EPS) * nw_ref[...]).astype(u_ref.dtype)


def _norm_call(h, nw, contrib=None, wts=None, *, write_h, u_dtype, tm_target=264):
    t, d = h.shape
    tm = _row_tile(t, tm_target)
    combine = contrib is not None
    row = lambda i: (i, 0)
    in_specs = [pl.BlockSpec((tm, d), row)]
    args = [h]
    if combine:
        in_specs += [pl.BlockSpec((tm * SLAB, 128), row), pl.BlockSpec((tm * SLAB, 128), lambda i: (t // tm + i, 0)),
                     pl.BlockSpec((tm, 2), row)]
        args += [contrib, contrib, wts]
    in_specs.append(pl.BlockSpec((1, d), lambda i: (0, 0)))
    args.append(nw.reshape(1, d))
    out_shape, out_specs = [], []
    if write_h:
        out_shape.append(jax.ShapeDtypeStruct((t, d), F32))
        out_specs.append(pl.BlockSpec((tm, d), row))
    out_shape.append(jax.ShapeDtypeStruct((t, d), u_dtype))
    out_specs.append(pl.BlockSpec((tm, d), row))
    return pl.pallas_call(
        functools.partial(_norm_kernel, combine=combine, write_h=write_h),
        grid=(t // tm,), in_specs=in_specs, out_specs=out_specs, out_shape=out_shape,
        compiler_params=_cparams(("arbitrary",)), name="combine_norm",
    )(*args)


def _final_kernel(h_ref, c0_ref, c1_ref, wt_ref, nw_ref, o_ref):
    h = _combine_rows(h_ref, c0_ref, c1_ref, wt_ref)
    ms = jnp.mean(h * h, axis=-1, keepdims=True)
    o_ref[...] = h * lax.rsqrt(ms + EPS) * nw_ref[...]


def _final_call(h, contrib, wts, nw, batch, p):
    t, d = h.shape
    seq = p - CHUNK
    tm = _row_tile(seq, 256)
    n_out = seq // tm
    row0 = lambda b, i: b * p + CHUNK + i * tm
    src = lambda b, i: (pl.multiple_of(row0(b, i), CHUNK), 0)
    slab0 = lambda b, i: (pl.multiple_of(row0(b, i) * SLAB, CHUNK), 0)
    slab1 = lambda b, i: (pl.multiple_of((t + row0(b, i)) * SLAB, CHUNK), 0)
    win = lambda r, w, fn: pl.BlockSpec((pl.Element(r), pl.Element(w)), fn)
    return pl.pallas_call(
        _final_kernel, grid=(batch, n_out),
        in_specs=[win(tm, d, src), win(tm * SLAB, 128, slab0), win(tm * SLAB, 128, slab1), win(tm, 2, src),
                  pl.BlockSpec((1, d), lambda b, i: (0, 0))],
        out_specs=pl.BlockSpec((None, tm, d), lambda b, i: (b, i, 0)),
        out_shape=jax.ShapeDtypeStruct((batch, seq, d), F32),
        compiler_params=_cparams(("arbitrary", "arbitrary")), name="final_norm",
    )(h, contrib, contrib, wts, nw.reshape(1, d))


def _matmul_kernel(x_ref, w_ref, o_ref):
    o_ref[...] = _dot(x_ref[...], w_ref[...])


def _inproj_call(u, w, tm_target=2112, tn=768):
    t, d = u.shape
    n = w.shape[1]
    tm = _row_tile(t, tm_target)
    assert n % tn == 0
    return pl.pallas_call(
        _matmul_kernel, grid=(t // tm, n // tn),
        in_specs=[pl.BlockSpec((tm, d), lambda i, j: (i, 0)), pl.BlockSpec((d, tn), lambda i, j: (0, j))],
        out_specs=pl.BlockSpec((tm, tn), lambda i, j: (i, j)),
        out_shape=jax.ShapeDtypeStruct((t, n), F32),
        compiler_params=_cparams(("arbitrary", "arbitrary")), name="in_proj",
    )(u, w)


def _hgrn2_kernel(q_ref, f_ref, i_ref, g_ref, par_ref, o_ref, st_ref, *, rows):
    s = pl.program_id(1)

    @pl.when(s == 0)
    def _():
        st_ref[...] = jnp.zeros_like(st_ref)

    c = HG_CHUNK
    ones = jnp.ones((HG_KDIM, 128), BF16)
    rid = lax.broadcasted_iota(jnp.int32, (c, 128), 0)
    scale = HG_KDIM ** -0.5

    hc = c // 2

    def chunk(ci, carry):
        r0 = pl.multiple_of(ci * c, c)
        pad = (s * rows + r0 + rid) < LEAD_PAD

        def front(h):
            cols = slice(h * 128, (h + 1) * 128)
            a_lb = par_ref[0:1, cols]
            b_lb = par_ref[1:2, cols]
            c_lb = par_ref[2:3, cols]
            z = f_ref[pl.ds(r0, c), cols]
            sg = _sigmoid(z)
            f = a_lb + b_lb * sg
            log_f = jnp.where(pad, 0.0, jnp.log(f))
            k = jnp.where(pad, 0.0, b_lb * (1.0 - sg) - c_lb)
            q = q_ref[pl.ds(r0, c), cols] * scale
            v = i_ref[pl.ds(r0, c), cols]
            cum = log_f
            for sh in (1, 2, 4, 8):
                cum = cum + jnp.where(rid >= sh, pltpu.roll(cum, sh, axis=0), 0.0)
            parts = []
            for s_ in range(c):
                lo = 0 if s_ < hc else hc
                rel = jnp.where(rid[lo:] >= s_, cum[lo:] - cum[s_:s_ + 1, :], NEG)
                parts.append(q[lo:] * (k[s_:s_ + 1, :] * jnp.exp(rel)))
            sc = _dot(jnp.concatenate(parts, axis=0).astype(BF16), ones)
            st = st_ref[h]
            o_inter = _dot_nt((q * jnp.exp(cum)).astype(BF16), st.astype(BF16))
            last = cum[c - 1:c, :]
            kd = (k * jnp.exp(last - cum)).astype(BF16)
            st_ref[h] = st * jnp.exp(last) + _dot_tn(v.astype(BF16), kd)
            return sc, o_inter, v

        def back(h, sc, o_inter, v):
            cols = slice(h * 128, (h + 1) * 128)
            o_top = o_inter[:hc]
            o_bot = o_inter[hc:]
            for s_ in range(hc):
                o_top = o_top + sc[s_ * c:s_ * c + hc, :] * v[s_:s_ + 1, :]
                o_bot = o_bot + sc[s_ * c + hc:(s_ + 1) * c, :] * v[s_:s_ + 1, :]
            for s_ in range(hc, c):
                r_ = hc * c + (s_ - hc) * hc
                o_bot = o_bot + sc[r_:r_ + hc, :] * v[s_:s_ + 1, :]
            o = jnp.concatenate([o_top, o_bot], axis=0)
            ms = jnp.mean(o * o, axis=-1, keepdims=True)
            g = g_ref[pl.ds(r0, c), cols]
            o_ref[pl.ds(r0, c), cols] = (o * lax.rsqrt(ms + EPS) * par_ref[3:4, cols] * _silu(g)).astype(o_ref.dtype)

        pending = front(0)
        for h in range(1, HG_HEADS):
            nxt = front(h)
            back(h - 1, *pending)
            pending = nxt
        back(HG_HEADS - 1, *pending)
        return carry

    n_chunks = rows // c
    lax.fori_loop(0, n_chunks, chunk, 0, unroll=3 if n_chunks % 3 == 0 else 1)


def _hgrn2_call(proj, par, batch, p):
    t = proj.shape[0]
    rows = _row_tile(p, 528)
    nb = p // rows
    w = HG_KEY
    blk = lambda j: pl.BlockSpec((rows, w), lambda b, s, j=j: (b * nb + s, C_HG // w + j))
    return pl.pallas_call(
        functools.partial(_hgrn2_kernel, rows=rows), grid=(batch, nb),
        in_specs=[blk(0), blk(1), blk(2), blk(3), pl.BlockSpec((8, w), lambda b, s: (0, 0))],
        out_specs=pl.BlockSpec((rows, w), lambda b, s: (b * nb + s, 0)),
        out_shape=jax.ShapeDtypeStruct((t, HG_WIDTH), BF16),
        scratch_shapes=[pltpu.VMEM((HG_HEADS, 128, HG_KDIM), F32)],
        compiler_params=_cparams(("arbitrary", "arbitrary")), name="hgrn2",
    )(proj, proj, proj, proj, par)


def _mlstm_kernel(v_ref, o_ref, q_ref, k_ref, sm_ref, par_ref, nw_ref, tri_ref, sel_ref, dg_ref,
                  y_ref, c_ref, m_ref):
    s = pl.program_id(0)

    @pl.when(s == 0)
    def _():
        c_ref[...] = jnp.zeros_like(c_ref)
        m_ref[...] = jnp.zeros_like(m_ref)

    n = CHUNK
    rid = lax.broadcasted_iota(jnp.int32, (n, 128), 0)
    pad = (s * n + rid) < LEAD_PAD
    sel_i = sel_ref[0]
    sel_f = sel_ref[1]
    dg = dg_ref[...]
    w = ML_HEADS * n
    tt = lax.broadcasted_iota(jnp.int32, (n, w), 0)
    ss = lax.broadcasted_iota(jnp.int32, (n, w), 1) & (n - 1)
    causal = tt >= ss
    scale = ML_QK ** -0.5
    lane = lax.broadcasted_iota(jnp.int32, (n, 128), 1)
    one_col = jnp.where(lane == 0, 1.0, 0.0).astype(BF16)
    for b in range(v_ref.shape[0]):
        pre = sm_ref[b] + par_ref[0:1, :]
        cap = GATE_CAP * jnp.tanh(pre * (1.0 / GATE_CAP))
        log_i = jnp.where(pad, NEG, cap)
        log_f = jnp.where(pad, 0.0, _log_sigmoid(cap))
        cum = _sel_dot(tri_ref[...], log_f)
        cum_col = _dot_sel(cum, sel_f)
        cum_row = jnp.sum(cum_col * dg, axis=0, keepdims=True)
        li_row = jnp.sum(_dot_sel(log_i, sel_i) * dg, axis=0, keepdims=True)
        dmat = jnp.where(causal, cum_col - cum_row + li_row, NEG)
        hs = range(ML_HEADS)
        sh = [b * ML_HEADS + h for h in hs]
        d_h = [dmat[:, h * n:(h + 1) * n] for h in hs]
        cum_h = [cum[:, LANE_MF + h:LANE_MF + h + 1] for h in hs]
        li_h = [log_i[:, LANE_MI + h:LANE_MI + h + 1] for h in hs]
        m_st = [m_ref[sh[h]:sh[h] + 1, 0:1] for h in hs]
        q = [(q_ref[b, :, h * ML_QK:(h + 1) * ML_QK] * scale).astype(BF16) for h in hs]
        k = [k_ref[b, :, h * ML_QK:(h + 1) * ML_QK] for h in hs]
        v_aug = [jnp.concatenate([v_ref[b, :, h * ML_V:(h + 1) * ML_V].astype(BF16), one_col], axis=1) for h in hs]
        c_prev = [c_ref[sh[h]] for h in hs]
        inter = [cum_h[h] + m_st[h] for h in hs]
        m_t = [jnp.maximum(inter[h], jnp.max(d_h[h], axis=-1, keepdims=True)) for h in hs]
        qk = [_dot_nt(q[h], k[h].astype(BF16)) for h in hs]
        qc = [_dot(q[h], c_prev[h].astype(BF16)) for h in hs]
        pw = [(qk[h] * jnp.exp(d_h[h] - m_t[h])).astype(BF16) for h in hs]
        nd = [_dot(pw[h], v_aug[h]) + jnp.exp(inter[h] - m_t[h]) * qc[h] for h in hs]
        hh = [nd[h][:, :ML_V] / jnp.maximum(jnp.abs(nd[h][:, ML_V:ML_V + 1]), jnp.exp(-m_t[h])) for h in hs]
        tot = [cum_h[h][n - 1:n, :] for h in hs]
        to_end = [tot[h] - cum_h[h] + li_h[h] for h in hs]
        m_loc = [jnp.max(to_end[h], axis=0, keepdims=True) for h in hs]
        kw = [(k[h] * jnp.exp(to_end[h] - m_loc[h])).astype(BF16) for h in hs]
        c_loc = [_dot_tn(kw[h], v_aug[h]) for h in hs]
        m_new = [jnp.maximum(tot[h] + m_st[h], m_loc[h]) for h in hs]
        for h in hs:
            c_ref[sh[h]] = (jnp.exp(tot[h] + m_st[h] - m_new[h]) * c_prev[h]
                            + jnp.exp(m_loc[h] - m_new[h]) * c_loc[h])
            m_ref[sh[h]:sh[h] + 1, :] = jnp.broadcast_to(m_new[h], (1, 128))
        for h in hs:
            ms = jnp.mean(hh[h] * hh[h], axis=-1, keepdims=True)
            cols = slice(h * ML_V, (h + 1) * ML_V)
            y_ref[b, :, cols] = (hh[h] * lax.rsqrt(ms + EPS) * nw_ref[:, cols]
                                 * _sigmoid(o_ref[b, :, cols])).astype(y_ref.dtype)


def _lane_select(lane0, heads, width):
    m = np.zeros((128, heads * width), np.float32)
    for h in range(heads):
        m[lane0 + h, h * width:(h + 1) * width] = 1.0
    return m


def _diag_mask(n, heads):
    return np.tile(np.eye(n, dtype=np.float32), (1, heads))


def _mlstm_call(proj, par, nw, batch, p):
    t = proj.shape[0]
    n = CHUNK
    nc = p // n
    tri = jnp.asarray(np.tril(np.ones((n, n), np.float32)), BF16)
    sel = jnp.asarray(np.stack([_lane_select(LANE_MI, ML_HEADS, n), _lane_select(LANE_MF, ML_HEADS, n)]), BF16)
    dg = jnp.asarray(_diag_mask(n, ML_HEADS), F32)
    proj3 = proj.reshape(batch, p, proj.shape[1])
    blk = lambda w, off: pl.BlockSpec((batch, n, w), lambda s: (0, s, off // w))
    const = lambda shape: pl.BlockSpec(shape, lambda s: (0,) * len(shape))
    m_rows = -(-batch * ML_HEADS // 8) * 8
    y = pl.pallas_call(
        _mlstm_kernel, grid=(nc,),
        in_specs=[blk(ML_WIDTH, C_MV), blk(ML_WIDTH, C_MV + ML_WIDTH), blk(ML_QK_W, C_MQ),
                  blk(ML_QK_W, C_MQ + ML_QK_W), blk(128, C_SMALL),
                  const((8, 128)), const((1, ML_WIDTH)), const((n, n)), const((2, 128, ML_HEADS * n)),
                  const((n, ML_HEADS * n))],
        out_specs=pl.BlockSpec((batch, n, ML_WIDTH), lambda s: (0, s, 0)),
        out_shape=jax.ShapeDtypeStruct((batch, p, ML_WIDTH), BF16),
        scratch_shapes=[pltpu.VMEM((batch * ML_HEADS, ML_QK, 2 * ML_V), F32), pltpu.VMEM((m_rows, 128), F32)],
        compiler_params=_cparams(("arbitrary",)), name="mlstm",
    )(proj3, proj3, proj3, proj3, proj3, par, nw, tri, sel, dg)
    return y.reshape(t, ML_WIDTH)


def _ssd_kernel(z_ref, x_ref, b_ref, c_ref, sm_ref, cw_ref, cb_ref, par_ref, dsk_ref, nw_ref,
                tri_ref, sel_ref, dg_ref, y_ref, xs_ref, bs_ref, cs_ref, st_ref):
    s = pl.program_id(1)
    n = CHUNK
    tail = 8
    chunks = z_ref.shape[0] // n

    @pl.when(s == 0)
    def _():
        st_ref[...] = jnp.zeros_like(st_ref)
        xs_ref[0:tail, :] = jnp.zeros((tail, xs_ref.shape[1]), F32)
        bs_ref[0:tail, :] = jnp.zeros((tail, bs_ref.shape[1]), F32)
        cs_ref[0:tail, :] = jnp.zeros((tail, cs_ref.shape[1]), F32)

    def chunk(ci, carry):
        _ssd_chunk(ci, s * chunks + ci, z_ref, x_ref, b_ref, c_ref, sm_ref, cw_ref, cb_ref, par_ref, dsk_ref,
                   nw_ref, tri_ref, sel_ref, dg_ref, y_ref, xs_ref, bs_ref, cs_ref, st_ref)
        return carry

    lax.fori_loop(0, chunks, chunk, 0)


def _ssd_chunk(ci, chunk_id, z_ref, x_ref, b_ref, c_ref, sm_ref, cw_ref, cb_ref, par_ref, dsk_ref, nw_ref,
               tri_ref, sel_ref, dg_ref, y_ref, xs_ref, bs_ref, cs_ref, st_ref):
    n = CHUNK
    tail = 8
    rows = pl.ds(pl.multiple_of(ci * n, n), n)

    def conv_silu(src_ref, scr_ref, c0, width, rowmask):
        scr_ref[tail:tail + n, :] = src_ref[rows, :]
        acc = cb_ref[:, c0:c0 + width]
        for j in range(CONV_W):
            off = tail - (CONV_W - 1) + j
            acc = acc + cw_ref[j:j + 1, c0:c0 + width] * scr_ref[off:off + n, :]
        scr_ref[0:tail, :] = scr_ref[n:n + tail, :]
        return jnp.where(rowmask, 0.0, _silu(acc))

    def padmask(width):
        return (chunk_id * n + lax.broadcasted_iota(jnp.int32, (n, width), 0)) < LEAD_PAD

    x = conv_silu(x_ref, xs_ref, 0, SSM_WIDTH, padmask(SSM_WIDTH))
    bm = conv_silu(b_ref, bs_ref, SSM_WIDTH, SSM_BC, padmask(SSM_BC))
    cm = conv_silu(c_ref, cs_ref, SSM_WIDTH + SSM_BC, SSM_BC, padmask(SSM_BC))

    pre = sm_ref[rows, :] + par_ref[0:1, :]
    dt = jnp.maximum(pre, 0.0) + jnp.log1p(jnp.exp(-jnp.abs(pre)))
    dt = jnp.where(padmask(128), 0.0, dt)
    da = dt * (-jnp.exp(par_ref[1:2, :]))
    cum = _sel_dot(tri_ref[...], da)
    sel = sel_ref[...]
    dt_col = _dot_sel(dt, sel)
    cum_col = _dot_sel(cum, sel)
    cum_row = jnp.sum(cum_col * dg_ref[...], axis=0, keepdims=True)
    w = SSM_WIDTH
    tt = lax.broadcasted_iota(jnp.int32, (n, w), 0)
    ss = lax.broadcasted_iota(jnp.int32, (n, w), 1) & (n - 1)
    decay = jnp.exp(jnp.where(tt >= ss, cum_col - cum_row, NEG))
    xdt = x * dt_col
    last = cum_col[n - 1:n, :]
    wend = (xdt * jnp.exp(last - cum_col)).astype(BF16)
    chunk_decay = jnp.exp(last)
    ecum = jnp.exp(cum_col)
    gw = SSM_HPG * SSM_HEADDIM
    rr = lax.broadcasted_iota(jnp.int32, (gw, gw), 0) // SSM_HEADDIM
    cc = lax.broadcasted_iota(jnp.int32, (gw, gw), 1) // SSM_HEADDIM
    blockdiag = rr == cc
    ys = []
    for g in range(SSM_GROUPS):
        gl = slice(g * gw, (g + 1) * gw)
        sl = slice(g * SSM_STATE, (g + 1) * SSM_STATE)
        cm_g = cm[:, sl].astype(BF16)
        bm_g = bm[:, sl].astype(BF16)
        cb = _dot_nt(cm_g, jnp.concatenate([bm_g] * SSM_HPG, axis=0))
        m = (cb * decay[:, gl]).astype(BF16)
        xdt_g = xdt[:, gl]
        bd = jnp.where(blockdiag, jnp.concatenate([xdt_g] * SSM_HPG, axis=0), 0.0).astype(BF16)
        st = st_ref[g]
        y_g = _dot(m, bd) + ecum[:, gl] * _dot(cm_g, st.astype(BF16))
        st_ref[g] = st * chunk_decay[:, gl] + _dot_tn(bm_g, wend[:, gl])
        ys.append(y_g)
    y = jnp.concatenate(ys, axis=1) + x * dsk_ref[...]
    y = y * _silu(z_ref[rows, :])
    outs = []
    for g in range(SSM_GROUPS):
        gl = slice(g * gw, (g + 1) * gw)
        y_g = y[:, gl]
        ms = jnp.mean(y_g * y_g, axis=-1, keepdims=True)
        outs.append(y_g * lax.rsqrt(ms + EPS))
    y_ref[rows, :] = (jnp.concatenate(outs, axis=1) * nw_ref[...]).astype(y_ref.dtype)


def _ssd_call(proj, cw, cb, par, dsk, nw, batch, p):
    t = proj.shape[0]
    n = CHUNK
    nc = p // n
    tri = jnp.asarray(np.tril(np.ones((n, n), np.float32)), BF16)
    sel = jnp.asarray(_lane_select(LANE_DT, SSM_HEADS, SSM_HEADDIM), BF16)
    dg = jnp.asarray(_diag_mask(n, SSM_HEADS), F32)
    rows = _row_tile(p, 704, mult=n)
    nc = p // rows
    blk = lambda w, off: pl.BlockSpec((rows, w), lambda b, s: (b * nc + s, off // w))
    const = lambda shape: pl.BlockSpec(shape, lambda b, s: (0,) * len(shape))
    cch = SSM_WIDTH + 2 * SSM_BC
    return pl.pallas_call(
        _ssd_kernel, grid=(batch, nc),
        in_specs=[blk(SSM_WIDTH, C_SZ), blk(SSM_WIDTH, C_SX), blk(SSM_BC, C_SX + SSM_WIDTH),
                  blk(SSM_BC, C_SX + SSM_WIDTH + SSM_BC), blk(128, C_SMALL),
                  const((CONV_W, cch)), const((1, cch)), const((8, 128)), const((1, SSM_WIDTH)),
                  const((1, SSM_WIDTH)), const((n, n)), const((128, SSM_WIDTH)), const((n, SSM_WIDTH))],
        out_specs=pl.BlockSpec((rows, SSM_WIDTH), lambda b, s: (b * nc + s, 0)),
        out_shape=jax.ShapeDtypeStruct((t, SSM_WIDTH), BF16),
        scratch_shapes=[pltpu.VMEM((n + 8, SSM_WIDTH), F32), pltpu.VMEM((n + 8, SSM_BC), F32),
                        pltpu.VMEM((n + 8, SSM_BC), F32),
                        pltpu.VMEM((SSM_GROUPS, SSM_STATE, SSM_HPG * SSM_HEADDIM), F32)],
        compiler_params=_cparams(("arbitrary", "arbitrary")), name="ssd",
    )(proj, proj, proj, proj, proj, cw, cb, par, dsk, nw, tri, sel, dg)


LANE_E0 = N_GROUPS_MOE
RT_E, RT_RANK, RT_W = 0, 2, 4


def _first_max(vals, lane):
    m = jnp.max(vals, axis=-1, keepdims=True)
    idx = jnp.min(jnp.where(vals == m, lane.astype(F32), 128.0), axis=-1, keepdims=True)
    return m, idx.astype(jnp.int32)


def _outproj_kernel(ya_ref, yb_ref, yc_ref, h_ref, w_ref, nw_ref, rh_ref, rl_ref, rb_ref, tri_ref,
                    hm_ref, u_ref, rt_ref, cnt_ref):
    a0, a1 = HG_WIDTH, HG_WIDTH + ML_WIDTH
    h = h_ref[...]
    h = h + _dot(ya_ref[...], w_ref[0:a0, :])
    h = h + _dot(yb_ref[...], w_ref[a0:a1, :])
    h = h + _dot(yc_ref[...], w_ref[a1:, :])
    hm_ref[...] = h
    ms = jnp.mean(h * h, axis=-1, keepdims=True)
    u = h * lax.rsqrt(ms + EPS) * nw_ref[...]
    _pack_slab(u_ref, u)
    u_hi = u.astype(BF16)
    u_lo = (u - u_hi.astype(F32)).astype(BF16)
    lg = _dot(u_hi, rh_ref[...]) + (_dot(u_lo, rh_ref[...]) + _dot(u_hi, rl_ref[...])) + rb_ref[...]

    tm = lg.shape[0]
    lane = lax.broadcasted_iota(jnp.int32, (tm, 128), 1)
    g_mask = lane < N_GROUPS_MOE
    g_max, g_sel = _first_max(jnp.where(g_mask, lg, NEG), lane)
    g_gate = 1.0 / jnp.sum(jnp.where(g_mask, jnp.exp(lg - g_max), 0.0), axis=-1, keepdims=True)
    lo = LANE_E0 + g_sel * EXPERTS_PER_GROUP
    e_vals = jnp.where((lane >= lo) & (lane < lo + EXPERTS_PER_GROUP), lg, NEG)
    v1, i1 = _first_max(e_vals, lane)
    v2, i2 = _first_max(jnp.where(lane == i1, NEG, e_vals), lane)
    a = jnp.exp(v2 - v1)
    w1 = g_gate / (1.0 + a)
    w2 = w1 * a
    @pl.when(pl.program_id(0) == 0)
    def _():
        cnt_ref[...] = jnp.zeros_like(cnt_ref)

    hit1 = lane == i1
    hit2 = lane == i2
    onehot = jnp.where(hit1 | hit2, 1.0, 0.0)
    before = _dot(tri_ref[...], onehot.astype(BF16)) + cnt_ref[0:1, :]
    r1 = jnp.sum(jnp.where(hit1, before, 0.0), axis=-1, keepdims=True)
    r2 = jnp.sum(jnp.where(hit2, before, 0.0), axis=-1, keepdims=True)
    cnt_ref[...] = cnt_ref[...] + jnp.sum(onehot, axis=0, keepdims=True)
    rec = jnp.zeros((tm, 128), F32)
    for ln, val in ((RT_E, (i1 - LANE_E0).astype(F32)), (RT_E + 1, (i2 - LANE_E0).astype(F32)),
                    (RT_RANK, r1), (RT_RANK + 1, r2), (RT_W, w1), (RT_W + 1, w2)):
        rec = jnp.where(lane == ln, val, rec)
    rt_ref[...] = rec


def _outproj_call(ya, yb, yc, h, w, nw, r_hi, r_lo, r_bias, tm_target=352):
    t, d = h.shape
    tm = _row_tile(t, tm_target)
    tri = jnp.asarray(np.tril(np.ones((tm, tm), np.float32), -1), BF16)
    row = lambda i: (i, 0)
    const = lambda shape: pl.BlockSpec(shape, lambda i: (0, 0))
    return pl.pallas_call(
        _outproj_kernel, grid=(t // tm,),
        in_specs=[pl.BlockSpec((tm, HG_WIDTH), row), pl.BlockSpec((tm, ML_WIDTH), row),
                  pl.BlockSpec((tm, SSM_WIDTH), row), pl.BlockSpec((tm, d), row),
                  const((D_MIX, d)), const((1, d)), const((d, 128)), const((d, 128)), const((1, 128)),
                  const((tm, tm))],
        out_specs=[pl.BlockSpec((tm, d), row), pl.BlockSpec((tm * SLAB, 128), row), pl.BlockSpec((tm, 128), row),
                   const((8, 128))],
        out_shape=[jax.ShapeDtypeStruct((t, d), F32), jax.ShapeDtypeStruct((t * SLAB, 128), U32),
                   jax.ShapeDtypeStruct((t, 128), F32), jax.ShapeDtypeStruct((8, 128), F32)],
        compiler_params=_cparams(("arbitrary",)), name="out_proj_router",
    )(ya, yb, yc, h, w, nw.reshape(1, d), r_hi, r_lo, r_bias, tri)


MOE_RING = 3
WT_FIRST, WT_SLOT, WT_NEXT = 0, 1, 2


TAB_SRC2, TAB_DST_PREV, TAB_DST = 0, 1, 2


def _moe_kernel(be_ref, nu_ref, wt_ref, src01_ref, tab_ref, u_hbm,
                w1_hbm, w3_hbm, w2_hbm, o_hbm, xbuf, ybuf, st1, st3, st2, w1b, w3b, w2b, gsem, ssem, wsem,
                *, layer):
    i = pl.program_id(0)
    bm = MOE_BM
    n_used = nu_ref[0]
    xs = i % MOE_RING
    slot = xs
    other = (i + MOE_RING - 1) % MOE_RING

    def slab(idx):
        return pl.ds(pl.multiple_of(idx * SLAB, SLAB), SLAB)

    def row_in(tok, r, sl):
        return pltpu.make_async_copy(u_hbm.at[slab(tok), :], xbuf.at[sl, slab(r), :], gsem.at[sl])

    def row_out(r, row, sl):
        return pltpu.make_async_copy(ybuf.at[sl, slab(r), :], o_hbm.at[slab(row), :], ssem.at[sl])

    def gather(section, sl):
        def body(g, c):
            for j in range(8):
                r = g * 8 + j
                row_in(src01_ref[0, section * bm + r], r, sl).start(priority=j % 2)
            return c
        lax.fori_loop(0, bm // 8, body, 0)

    def scatter(sl):
        def body(g, c):
            for j in range(8):
                r = g * 8 + j
                row_out(r, tab_ref[0, TAB_DST * bm + r], sl).start(priority=j % 2)
            return c
        lax.fori_loop(0, bm // 8, body, 0)

    def wait_gather(sl):
        pltpu.make_async_copy(u_hbm.at[pl.ds(0, bm * SLAB), :], xbuf.at[sl], gsem.at[sl]).wait()

    def wait_scatter(sl):
        pltpu.make_async_copy(ybuf.at[sl], o_hbm.at[pl.ds(0, bm * SLAB), :], ssem.at[sl]).wait()

    def weight_copies(e, p):
        return [pltpu.make_async_copy(src.at[layer, e], dst.at[p], wsem.at[p])
                for src, dst in ((w1_hbm, st1), (w3_hbm, st3), (w2_hbm, st2))]

    n_real = o_hbm.shape[0] // SLAB - MOE_RING * bm

    @pl.when(i == 0)
    def _():
        for c in weight_copies(be_ref[0], 0):
            c.start()
        gather(0, 0)
        gather(1, 1)
        ybuf[...] = jnp.zeros_like(ybuf)
        for sl in range(2):
            pltpu.make_async_copy(ybuf.at[sl], o_hbm.at[pl.ds((n_real + sl * bm) * SLAB, bm * SLAB), :],
                                  ssem.at[sl]).start()

    @pl.when(i < n_used)
    def _():
        @pl.when(wt_ref[WT_FIRST, i] == 1)
        def _():
            p = wt_ref[WT_SLOT, i]
            nxt = wt_ref[WT_NEXT, i]

            @pl.when(nxt >= 0)
            def _():
                for c in weight_copies(nxt, 1 - p):
                    c.start(priority=1)

            for c in weight_copies(be_ref[i], p):
                c.wait()
            w1b[...] = st1[p].astype(BF16)
            w3b[...] = st3[p].astype(BF16)
            w2b[...] = st2[p].astype(BF16)

        wait_gather(xs)

        xn = (i + 2) % MOE_RING

        def issue(part, parts=4):
            for r in range(part * bm // parts, (part + 1) * bm // parts):
                row_in(tab_ref[0, TAB_SRC2 * bm + r], r, xn).start(priority=0)
                row_out(r, tab_ref[0, TAB_DST_PREV * bm + r], other).start(priority=1)

        f = w1b.shape[1]
        h1 = jnp.zeros((bm, f), F32)
        h3 = jnp.zeros((bm, f), F32)
        for j in range(SLAB):
            lo, hi = _unpack_slab(xbuf, j, bm, lead=(xs,))
            xk = jnp.concatenate([lo, hi], axis=1).astype(BF16)
            h1 = h1 + _dot(xk, w1b[j * 256:(j + 1) * 256, :])
            h3 = h3 + _dot(xk, w3b[j * 256:(j + 1) * 256, :])
            if j % 4 == 3:
                issue(j // 4)
        hid = (_silu(h1) * h3).astype(BF16)
        wait_scatter(slot)
        half = SLAB // 2
        _pack_slab(ybuf, _dot(hid, w2b[:, :half * 256]), lead=(slot,))
        issue(2)
        _pack_slab(ybuf, _dot(hid, w2b[:, half * 256:]), lead=(slot,), j0=half)
        issue(3)

    @pl.when(i == n_used - 1)
    def _():
        scatter(slot)
        for k in range(MOE_RING):
            wait_scatter(k)
        wait_gather((i + 1) % MOE_RING)
        wait_gather((i + 2) % MOE_RING)


def _moe_call(block_e, n_used, w_table, src_tok, dst_row, u, w1, w3, w2, layer, n_out_rows):
    d, f = w1.shape[-2:]
    assert d == SLAB * 256
    bm = MOE_BM
    nb = block_e.shape[0]
    src_tok = src_tok.reshape(nb, bm)
    dst_row = dst_row.reshape(nb, bm)
    lead = (n_out_rows - bm + jnp.arange(bm, dtype=jnp.int32)).reshape(1, bm)
    src_ahead = jnp.concatenate([src_tok[2:], src_tok[-1:], src_tok[-1:]], axis=0)
    dst_prev = jnp.concatenate([lead, dst_row[:-1]], axis=0)
    table = jnp.concatenate([src_ahead, dst_prev, dst_row], axis=1).reshape(nb, 1, 3 * bm)
    src01 = src_tok[:2].reshape(1, 1, 2 * bm)
    idx = lambda w, fn: pl.BlockSpec((None, 1, w), fn, memory_space=pltpu.SMEM)
    hbm = pl.BlockSpec(memory_space=pl.ANY)
    ring = pltpu.VMEM((MOE_RING, bm * SLAB, 128), U32)
    return pl.pallas_call(
        functools.partial(_moe_kernel, layer=layer),
        grid_spec=pltpu.PrefetchScalarGridSpec(
            num_scalar_prefetch=3, grid=(nb,),
            in_specs=[idx(2 * bm, lambda i, *_: (0, 0, 0)), idx(3 * bm, lambda i, *_: (i, 0, 0)),
                      hbm, hbm, hbm, hbm],
            out_specs=hbm,
            scratch_shapes=[ring, ring,
                            pltpu.VMEM((2, d, f), F32), pltpu.VMEM((2, d, f), F32), pltpu.VMEM((2, f, d), F32),
                            pltpu.VMEM((d, f), BF16), pltpu.VMEM((d, f), BF16), pltpu.VMEM((f, d), BF16),
                            pltpu.SemaphoreType.DMA((MOE_RING,)), pltpu.SemaphoreType.DMA((MOE_RING,)),
                            pltpu.SemaphoreType.DMA((2,))]),
        out_shape=jax.ShapeDtypeStruct((n_out_rows * SLAB, 128), U32),
        compiler_params=_cparams(("arbitrary",)), name="moe_ffn",
    )(block_e, n_used, w_table, src01, table, u, w1, w3, w2)


def _route_tables(rt, cnt, t):
    bm = MOE_BM
    tk = t * TOP_K
    e = rt[:, RT_E:RT_E + TOP_K].astype(jnp.int32)
    rank = rt[:, RT_RANK:RT_RANK + TOP_K].astype(jnp.int32)
    wts = rt[:, RT_W:RT_W + TOP_K]
    counts = cnt[0, LANE_E0:LANE_E0 + N_EXPERTS].astype(jnp.int32)
    padded = (counts + bm - 1) // bm * bm
    pad_ends = jnp.cumsum(padded)
    pad_starts = pad_ends - padded
    e_ids = jnp.arange(N_EXPERTS, dtype=jnp.int32)

    def lookup(table, idx):
        return jnp.sum(jnp.where(idx[..., None] == e_ids, table, 0), axis=-1)

    dest = (lookup(pad_starts, e) + rank).reshape(-1)
    n_blocks = -(-tk // bm) + N_EXPERTS
    n_rows = n_blocks * bm
    inv = jnp.full((n_rows,), -1, jnp.int32).at[dest].set(jnp.arange(tk, dtype=jnp.int32))
    tok, slot = inv // TOP_K, inv % TOP_K
    src_tok = jnp.where(inv >= 0, tok, 0)
    dst_row = jnp.where(inv >= 0, slot * t + tok, tk + (jnp.arange(n_rows, dtype=jnp.int32) % (MOE_RING * bm)))
    block_ids = jnp.arange(n_blocks, dtype=jnp.int32)
    block_e = jnp.minimum(jnp.sum(pad_ends[None, :] <= block_ids[:, None] * bm, axis=1), N_EXPERTS - 1).astype(jnp.int32)
    n_used = (pad_ends[-1] // bm).astype(jnp.int32).reshape(1)
    has_rows = counts > 0
    at_or_after = lax.cummin(jnp.where(has_rows, e_ids, N_EXPERTS)[::-1])[::-1]
    next_e = jnp.concatenate([at_or_after[1:], jnp.full((1,), N_EXPERTS, jnp.int32)])
    next_e = jnp.where(next_e >= N_EXPERTS, -1, next_e)
    stage = (jnp.cumsum(has_rows.astype(jnp.int32)) - 1) % 2
    first = jnp.concatenate([jnp.ones((1,), bool), block_e[1:] != block_e[:-1]]) & (block_ids < n_used[0])
    w_table = jnp.stack([first.astype(jnp.int32), lookup(stage, block_e), lookup(next_e, block_e)]).astype(jnp.int32)
    return block_e, n_used, w_table, src_tok, dst_row, wts, tk + MOE_RING * bm


def _lane_row(pairs):
    row = jnp.zeros((128,), F32)
    for lane0, vals in pairs:
        row = row.at[lane0:lane0 + vals.shape[0]].set(vals.astype(F32))
    return row


def kernel(x, meta_tokens, hg_lb_logits, norm_mix_w, w_in, hg_norm_w, ml_b_i, ml_b_f, ml_norm_w,
           ssm_conv_w, ssm_conv_b, ssm_dt_bias, ssm_a_log, ssm_d, ssm_norm_w, w_out, norm_ffn_w,
           moe_w_group, moe_b_group, moe_w_router, moe_b_router, moe_w1, moe_w3, moe_w2, final_norm_w):
    batch, seq, d = x.shape
    depth = w_in.shape[0]
    p = LEAD_PAD + N_META + seq
    t = batch * p
    meta = jnp.broadcast_to(meta_tokens.astype(x.dtype)[None], (batch, N_META, d))
    h = jnp.concatenate([jnp.zeros((batch, LEAD_PAD, d), x.dtype), meta, x], axis=1).reshape(t, d)

    lb_w = jax.nn.softmax(hg_lb_logits.astype(F32), axis=0)
    lower_bounds = jnp.cumsum(lb_w, axis=0) - lb_w[0]

    contrib = wts = None
    for layer in range(depth):
        w = w_in[layer]
        o_mq, o_mv, o_mi, o_sz = HG_KEY * 2 + HG_WIDTH * 2, 2560, 3584, 3592
        o_sx = o_sz + SSM_WIDTH
        o_dt = o_sx + SSM_WIDTH + 2 * SSM_BC
        w_perm = jnp.concatenate([
            w[:, :o_mq], w[:, o_sz:o_sx], w[:, o_sx:o_dt], w[:, o_mv:o_mi], w[:, o_mq:o_mv],
            w[:, o_mi:o_sz], w[:, o_dt:], jnp.zeros((d, N_PROJ - C_SMALL - 2 * ML_HEADS - SSM_HEADS), w.dtype)],
            axis=1).astype(BF16)
        lb = lower_bounds[layer]
        lbf = jnp.maximum(lb, LB_FLOOR)
        hg_par = jnp.zeros((8, HG_KEY), F32).at[0].set(lbf).at[1].set(1.0 - lb).at[2].set(lbf - lb).at[3].set(hg_norm_w[layer])
        ml_par = jnp.zeros((8, 128), F32).at[0].set(_lane_row([(LANE_MI, ml_b_i[layer]), (LANE_MF, ml_b_f[layer])]))
        ss_par = jnp.zeros((8, 128), F32).at[0].set(_lane_row([(LANE_DT, ssm_dt_bias[layer])]))
        ss_par = ss_par.at[1].set(_lane_row([(LANE_DT, ssm_a_log[layer])]))
        dskip = jnp.repeat(ssm_d[layer].astype(F32), SSM_HEADDIM).reshape(1, SSM_WIDTH)
        w_r = jnp.concatenate([moe_w_group[layer],
                               moe_w_router[layer].transpose(1, 0, 2).reshape(d, N_EXPERTS),
                               jnp.zeros((d, 128 - N_GROUPS_MOE - N_EXPERTS), F32)], axis=1)
        r_hi = w_r.astype(BF16)
        r_lo = (w_r - r_hi.astype(F32)).astype(BF16)

        if layer == 0:
            (u,) = _norm_call(h, norm_mix_w[layer], write_h=False, u_dtype=BF16)
        else:
            h, u = _norm_call(h, norm_mix_w[layer], contrib, wts, write_h=True, u_dtype=BF16)
        proj = _inproj_call(u, w_perm)
        ya = _hgrn2_call(proj, hg_par, batch, p)
        yb = _mlstm_call(proj, ml_par, ml_norm_w[layer].reshape(1, ML_WIDTH), batch, p)
        yc = _ssd_call(proj, ssm_conv_w[layer], ssm_conv_b[layer].reshape(1, -1), ss_par, dskip,
                       ssm_norm_w[layer].reshape(1, SSM_WIDTH), batch, p)
        r_bias = _lane_row([(0, moe_b_group[layer]), (LANE_E0, moe_b_router[layer].reshape(-1))]).reshape(1, 128)
        h, u_ffn, rt, cnt = _outproj_call(ya, yb, yc, h, w_out[layer].astype(BF16), norm_ffn_w[layer],
                                          r_hi, r_lo, r_bias)
        block_e, n_used, w_table, src_tok, dst_row, wts, n_out_rows = _route_tables(rt, cnt, t)
        contrib = _moe_call(block_e, n_used, w_table, src_tok, dst_row, u_ffn, moe_w1, moe_w3, moe_w2, layer,
                            n_out_rows)
    return _final_call(h, contrib, wts, final_norm_w, batch, p)
```

```python
import functools

import jax
import jax.numpy as jnp
import numpy as np
from jax import lax
from jax.experimental import pallas as pl
from jax.experimental.pallas import tpu as pltpu

F32 = jnp.float32
BF16 = jnp.bfloat16

D_MODEL = 2048
N_META = 16
CHUNK = 64
HG_CHUNK = 16
LEAD_PAD = CHUNK - N_META
EPS = 1e-6
NEG = -1e30
LB_FLOOR = 1e-30

HG_HEADS = 4
HG_KDIM = 128
HG_KEY = HG_HEADS * HG_KDIM
HG_WIDTH = HG_HEADS * 128

ML_HEADS = 4
ML_QK = 64
ML_V = 128
ML_QK_W = ML_HEADS * ML_QK
ML_WIDTH = ML_HEADS * ML_V
GATE_CAP = 15.0

SSM_HEADS = 16
SSM_HEADDIM = 64
SSM_WIDTH = SSM_HEADS * SSM_HEADDIM
SSM_STATE = 128
SSM_GROUPS = 4
SSM_HPG = SSM_HEADS // SSM_GROUPS
SSM_BC = SSM_GROUPS * SSM_STATE
CONV_W = 4

D_MIX = HG_WIDTH + ML_WIDTH + SSM_WIDTH

N_GROUPS_MOE = 4
EXPERTS_PER_GROUP = 8
N_EXPERTS = N_GROUPS_MOE * EXPERTS_PER_GROUP
TOP_K = 2
D_EXPERT = 512
MOE_BM = 128

C_HG = 0
C_MQ = 2048
C_MV = 2560
N_HEAD = 3584
C_SB = 3584
C_SZ = 4096
C_SX = 5120
C_SC = 6144
C_SMALL = 6656
N_PROJ = 6912
LANE_MI = 0
LANE_MF = ML_HEADS
LANE_DT = 2 * ML_HEADS

VMEM_LIMIT = 56 * 1024 * 1024


def _cparams(sem):
    return pltpu.CompilerParams(dimension_semantics=sem, vmem_limit_bytes=VMEM_LIMIT)


def _row_tile(n, target, mult=16):
    best = None
    for t in range(mult, min(n, target) + 1, mult):
        if n % t == 0:
            best = t
    assert best is not None, (n, target, mult)
    return best


def _split3(x):
    hi = x.astype(BF16)
    r = x - hi.astype(F32)
    mid = r.astype(BF16)
    lo = (r - mid.astype(F32)).astype(BF16)
    return hi, mid, lo


def _dot(a, b):
    return jnp.dot(a, b, preferred_element_type=F32)


def _sel_dot(sel, x):
    hi, mid, lo = _split3(x)
    return _dot(sel, hi) + _dot(sel, mid) + _dot(sel, lo)


def _dot_sel(x, sel):
    hi, mid, lo = _split3(x)
    return _dot(hi, sel) + _dot(mid, sel) + _dot(lo, sel)


def _dot_nt(a, b):
    return lax.dot_general(a, b, (((1,), (1,)), ((), ())), preferred_element_type=F32)


def _dot_tn(a, b):
    return lax.dot_general(a, b, (((0,), (0,)), ((), ())), preferred_element_type=F32)


def _log_sigmoid(x):
    return jnp.minimum(x, 0.0) - jnp.log1p(jnp.exp(-jnp.abs(x)))


def _sigmoid(x):
    return 1.0 / (1.0 + jnp.exp(-x))


def _silu(x):
    return x * _sigmoid(x)


SLAB = 8
U32 = jnp.uint32
HI_MASK = 0xFFFF0000


def _bf16_bits(x):
    return lax.bitcast_convert_type(x.astype(BF16).astype(F32), U32)


def _pack_slab(ref, val, lead=(), j0=0):
    rows = val.shape[0]
    for jj in range(val.shape[1] // 256):
        lo = _bf16_bits(val[:, jj * 256:jj * 256 + 128])
        hi = _bf16_bits(val[:, jj * 256 + 128:(jj + 1) * 256])
        ref[(*lead, pl.ds(j0 + jj, rows, stride=SLAB), slice(None))] = (lo >> 16) | (hi & U32(HI_MASK))


def _unpack_slab(ref, j, rows, lead=()):
    w = ref[(*lead, pl.ds(j, rows, stride=SLAB), slice(None))]
    return (lax.bitcast_convert_type(w << 16, F32), lax.bitcast_convert_type(w & U32(HI_MASK), F32))


def _combine_rows(h_ref, c0_ref, c1_ref, wt_ref):
    rows = h_ref.shape[0]
    wt = wt_ref[...]
    w0, w1 = wt[:, 0:1], wt[:, 1:2]
    pieces = []
    for j in range(SLAB):
        a0, b0 = _unpack_slab(c0_ref, j, rows)
        a1, b1 = _unpack_slab(c1_ref, j, rows)
        pieces += [w0 * a0 + w1 * a1, w0 * b0 + w1 * b1]
    return h_ref[...] + jnp.concatenate(pieces, axis=1)


def _norm_kernel(*refs, combine, write_h):
    if combine:
        h_ref, c0_ref, c1_ref, wt_ref, nw_ref = refs[:5]
        outs = refs[5:]
        h = _combine_rows(h_ref, c0_ref, c1_ref, wt_ref)
    else:
        h_ref, nw_ref = refs[:2]
        outs = refs[2:]
        h = h_ref[...]
    if write_h:
        outs[0][...] = h
    ms = jnp.mean(h * h, axis=-1, keepdims=True)
    u_ref = outs[-1]
    u_ref[...] = (h * lax.rsqrt(ms + EPS) * nw_ref[...]).astype(u_ref.dtype)


def _norm_call(h, nw, contrib=None, wts=None, *, write_h, u_dtype, tm_target=264):
    t, d = h.shape
    tm = _row_tile(t, tm_target)
    combine = contrib is not None
    row = lambda i: (i, 0)
    in_specs = [pl.BlockSpec((tm, d), row)]
    args = [h]
    if combine:
        in_specs += [pl.BlockSpec((tm * SLAB, 128), row), pl.BlockSpec((tm * SLAB, 128), lambda i: (t // tm + i, 0)),
                     pl.BlockSpec((tm, 2), row)]
        args += [contrib, contrib, wts]
    in_specs.append(pl.BlockSpec((1, d), lambda i: (0, 0)))
    args.append(nw.reshape(1, d))
    out_shape, out_specs = [], []
    if write_h:
        out_shape.append(jax.ShapeDtypeStruct((t, d), F32))
        out_specs.append(pl.BlockSpec((tm, d), row))
    out_shape.append(jax.ShapeDtypeStruct((t, d), u_dtype))
    out_specs.append(pl.BlockSpec((tm, d), row))
    return pl.pallas_call(
        functools.partial(_norm_kernel, combine=combine, write_h=write_h),
        grid=(t // tm,), in_specs=in_specs, out_specs=out_specs, out_shape=out_shape,
        compiler_params=_cparams(("arbitrary",)), name="combine_norm",
    )(*args)


def _final_kernel(h_ref, c0_ref, c1_ref, wt_ref, nw_ref, o_ref):
    h = _combine_rows(h_ref, c0_ref, c1_ref, wt_ref)
    ms = jnp.mean(h * h, axis=-1, keepdims=True)
    o_ref[...] = h * lax.rsqrt(ms + EPS) * nw_ref[...]


def _final_call(h, contrib, wts, nw, batch, p):
    t, d = h.shape
    seq = p - CHUNK
    tm = _row_tile(seq, 256)
    n_out = seq // tm
    row0 = lambda b, i: b * p + CHUNK + i * tm
    src = lambda b, i: (pl.multiple_of(row0(b, i), CHUNK), 0)
    slab0 = lambda b, i: (pl.multiple_of(row0(b, i) * SLAB, CHUNK), 0)
    slab1 = lambda b, i: (pl.multiple_of((t + row0(b, i)) * SLAB, CHUNK), 0)
    win = lambda r, w, fn: pl.BlockSpec((pl.Element(r), pl.Element(w)), fn)
    return pl.pallas_call(
        _final_kernel, grid=(batch, n_out),
        in_specs=[win(tm, d, src), win(tm * SLAB, 128, slab0), win(tm * SLAB, 128, slab1), win(tm, 2, src),
                  pl.BlockSpec((1, d), lambda b, i: (0, 0))],
        out_specs=pl.BlockSpec((None, tm, d), lambda b, i: (b, i, 0)),
        out_shape=jax.ShapeDtypeStruct((batch, seq, d), F32),
        compiler_params=_cparams(("arbitrary", "arbitrary")), name="final_norm",
    )(h, contrib, contrib, wts, nw.reshape(1, d))


def _inproj_kernel(x_ref, wh_ref, wt_ref, o_ref, *, head_tiles):
    j = pl.program_id(1)

    @pl.when(j < head_tiles)
    def _():
        o_ref[...] = _dot(x_ref[...], wh_ref[...].astype(BF16))

    @pl.when(j >= head_tiles)
    def _():
        o_ref[...] = _dot(x_ref[...], wt_ref[...].astype(BF16))


def _inproj_call(u, w_all, layer, w_tail, tm_target=2112, tn=256):
    t, d = u.shape
    n = N_HEAD + w_tail.shape[1]
    tm = _row_tile(t, tm_target)
    assert N_HEAD % tn == 0 and n % tn == 0
    head_tiles = N_HEAD // tn
    return pl.pallas_call(
        functools.partial(_inproj_kernel, head_tiles=head_tiles), grid=(t // tm, n // tn),
        in_specs=[pl.BlockSpec((tm, d), lambda i, j: (i, 0)),
                  pl.BlockSpec((None, d, tn), lambda i, j: (layer, 0, jnp.minimum(j, head_tiles - 1))),
                  pl.BlockSpec((d, tn), lambda i, j: (0, jnp.maximum(j - head_tiles, 0)))],
        out_specs=pl.BlockSpec((tm, tn), lambda i, j: (i, j)),
        out_shape=jax.ShapeDtypeStruct((t, n), F32),
        compiler_params=_cparams(("arbitrary", "arbitrary")), name="in_proj",
    )(u, w_all, w_tail)


def _hgrn2_kernel(q_ref, f_ref, i_ref, g_ref, par_ref, o_ref, st_ref, *, rows):
    s = pl.program_id(1)

    @pl.when(s == 0)
    def _():
        st_ref[...] = jnp.zeros_like(st_ref)

    c = HG_CHUNK
    ones = jnp.ones((HG_KDIM, 128), BF16)
    rid = lax.broadcasted_iota(jnp.int32, (c, 128), 0)
    scale = HG_KDIM ** -0.5

    hc = c // 2

    def chunk(ci, carry):
        r0 = pl.multiple_of(ci * c, c)
        pad = (s * rows + r0 + rid) < LEAD_PAD

        def front(h):
            cols = slice(h * 128, (h + 1) * 128)
            a_lb = par_ref[0:1, cols]
            b_lb = par_ref[1:2, cols]
            c_lb = par_ref[2:3, cols]
            z = f_ref[pl.ds(r0, c), cols]
            sg = _sigmoid(z)
            f = a_lb + b_lb * sg
            log_f = jnp.where(pad, 0.0, jnp.log(f))
            k = jnp.where(pad, 0.0, b_lb * (1.0 - sg) - c_lb)
            q = q_ref[pl.ds(r0, c), cols] * scale
            v = i_ref[pl.ds(r0, c), cols]
            cum = log_f
            for sh in (1, 2, 4, 8):
                cum = cum + jnp.where(rid >= sh, pltpu.roll(cum, sh, axis=0), 0.0)
            parts = []
            for s_ in range(c):
                lo = 0 if s_ < hc else hc
                rel = jnp.where(rid[lo:] >= s_, cum[lo:] - cum[s_:s_ + 1, :], NEG)
                parts.append(q[lo:] * (k[s_:s_ + 1, :] * jnp.exp(rel)))
            sc = _dot(jnp.concatenate(parts, axis=0).astype(BF16), ones)
            st = st_ref[h]
            o_inter = _dot_nt((q * jnp.exp(cum)).astype(BF16), st.astype(BF16))
            last = cum[c - 1:c, :]
            kd = (k * jnp.exp(last - cum)).astype(BF16)
            st_ref[h] = st * jnp.exp(last) + _dot_tn(v.astype(BF16), kd)
            return sc, o_inter, v

        def back(h, sc, o_inter, v):
            cols = slice(h * 128, (h + 1) * 128)
            o_top = o_inter[:hc]
            o_bot = o_inter[hc:]
            for s_ in range(hc):
                o_top = o_top + sc[s_ * c:s_ * c + hc, :] * v[s_:s_ + 1, :]
                o_bot = o_bot + sc[s_ * c + hc:(s_ + 1) * c, :] * v[s_:s_ + 1, :]
            for s_ in range(hc, c):
                r_ = hc * c + (s_ - hc) * hc
                o_bot = o_bot + sc[r_:r_ + hc, :] * v[s_:s_ + 1, :]
            o = jnp.concatenate([o_top, o_bot], axis=0)
            ms = jnp.mean(o * o, axis=-1, keepdims=True)
            g = g_ref[pl.ds(r0, c), cols]
            o_ref[pl.ds(r0, c), cols] = (o * lax.rsqrt(ms + EPS) * par_ref[3:4, cols] * _silu(g)).astype(o_ref.dtype)

        pending = front(0)
        for h in range(1, HG_HEADS):
            nxt = front(h)
            back(h - 1, *pending)
            pending = nxt
        back(HG_HEADS - 1, *pending)
        return carry

    n_chunks = rows // c
    lax.fori_loop(0, n_chunks, chunk, 0, unroll=3 if n_chunks % 3 == 0 else 1)


def _hgrn2_call(proj, par, batch, p):
    t = proj.shape[0]
    rows = _row_tile(p, 528)
    nb = p // rows
    w = HG_KEY
    blk = lambda j: pl.BlockSpec((rows, w), lambda b, s, j=j: (b * nb + s, C_HG // w + j))
    return pl.pallas_call(
        functools.partial(_hgrn2_kernel, rows=rows), grid=(batch, nb),
        in_specs=[blk(0), blk(1), blk(2), blk(3), pl.BlockSpec((8, w), lambda b, s: (0, 0))],
        out_specs=pl.BlockSpec((rows, w), lambda b, s: (b * nb + s, 0)),
        out_shape=jax.ShapeDtypeStruct((t, HG_WIDTH), BF16),
        scratch_shapes=[pltpu.VMEM((HG_HEADS, 128, HG_KDIM), F32)],
        compiler_params=_cparams(("arbitrary", "arbitrary")), name="hgrn2",
    )(proj, proj, proj, proj, par)


def _mlstm_kernel(v_ref, o_ref, q_ref, k_ref, sm_ref, par_ref, nw_ref, tri_ref, sel_ref, dg_ref,
                  y_ref, c_ref, m_ref):
    s = pl.program_id(0)

    @pl.when(s == 0)
    def _():
        c_ref[...] = jnp.zeros_like(c_ref)
        m_ref[...] = jnp.zeros_like(m_ref)

    n = CHUNK
    rid = lax.broadcasted_iota(jnp.int32, (n, 128), 0)
    pad = (s * n + rid) < LEAD_PAD
    sel_i = sel_ref[0]
    sel_f = sel_ref[1]
    dg = dg_ref[...]
    w = ML_HEADS * n
    tt = lax.broadcasted_iota(jnp.int32, (n, w), 0)
    ss = lax.broadcasted_iota(jnp.int32, (n, w), 1) & (n - 1)
    causal = tt >= ss
    scale = ML_QK ** -0.5
    lane = lax.broadcasted_iota(jnp.int32, (n, 128), 1)
    one_col = jnp.where(lane == 0, 1.0, 0.0).astype(BF16)
    for b in range(v_ref.shape[0]):
        pre = sm_ref[b] + par_ref[0:1, :]
        cap = GATE_CAP * jnp.tanh(pre * (1.0 / GATE_CAP))
        log_i = jnp.where(pad, NEG, cap)
        log_f = jnp.where(pad, 0.0, _log_sigmoid(cap))
        cum = _sel_dot(tri_ref[...], log_f)
        cum_col = _dot_sel(cum, sel_f)
        cum_row = jnp.sum(cum_col * dg, axis=0, keepdims=True)
        li_row = jnp.sum(_dot_sel(log_i, sel_i) * dg, axis=0, keepdims=True)
        dmat = jnp.where(causal, cum_col - cum_row + li_row, NEG)
        hs = range(ML_HEADS)
        sh = [b * ML_HEADS + h for h in hs]
        d_h = [dmat[:, h * n:(h + 1) * n] for h in hs]
        cum_h = [cum[:, LANE_MF + h:LANE_MF + h + 1] for h in hs]
        li_h = [log_i[:, LANE_MI + h:LANE_MI + h + 1] for h in hs]
        m_st = [m_ref[sh[h]:sh[h] + 1, 0:1] for h in hs]
        q = [(q_ref[b, :, h * ML_QK:(h + 1) * ML_QK] * scale).astype(BF16) for h in hs]
        k = [k_ref[b, :, h * ML_QK:(h + 1) * ML_QK] for h in hs]
        v_aug = [jnp.concatenate([v_ref[b, :, h * ML_V:(h + 1) * ML_V].astype(BF16), one_col], axis=1) for h in hs]
        c_prev = [c_ref[sh[h]] for h in hs]
        inter = [cum_h[h] + m_st[h] for h in hs]
        m_t = [jnp.maximum(inter[h], jnp.max(d_h[h], axis=-1, keepdims=True)) for h in hs]
        qk = [_dot_nt(q[h], k[h].astype(BF16)) for h in hs]
        qc = [_dot(q[h], c_prev[h].astype(BF16)) for h in hs]
        pw = [(qk[h] * jnp.exp(d_h[h] - m_t[h])).astype(BF16) for h in hs]
        nd = [_dot(pw[h], v_aug[h]) + jnp.exp(inter[h] - m_t[h]) * qc[h] for h in hs]
        hh = [nd[h][:, :ML_V] / jnp.maximum(jnp.abs(nd[h][:, ML_V:ML_V + 1]), jnp.exp(-m_t[h])) for h in hs]
        tot = [cum_h[h][n - 1:n, :] for h in hs]
        to_end = [tot[h] - cum_h[h] + li_h[h] for h in hs]
        m_loc = [jnp.max(to_end[h], axis=0, keepdims=True) for h in hs]
        kw = [(k[h] * jnp.exp(to_end[h] - m_loc[h])).astype(BF16) for h in hs]
        c_loc = [_dot_tn(kw[h], v_aug[h]) for h in hs]
        m_new = [jnp.maximum(tot[h] + m_st[h], m_loc[h]) for h in hs]
        for h in hs:
            c_ref[sh[h]] = (jnp.exp(tot[h] + m_st[h] - m_new[h]) * c_prev[h]
                            + jnp.exp(m_loc[h] - m_new[h]) * c_loc[h])
            m_ref[sh[h]:sh[h] + 1, :] = jnp.broadcast_to(m_new[h], (1, 128))
        for h in hs:
            ms = jnp.mean(hh[h] * hh[h], axis=-1, keepdims=True)
            cols = slice(h * ML_V, (h + 1) * ML_V)
            y_ref[b, :, cols] = (hh[h] * lax.rsqrt(ms + EPS) * nw_ref[:, cols]
                                 * _sigmoid(o_ref[b, :, cols])).astype(y_ref.dtype)


def _lane_select(lane0, heads, width):
    m = np.zeros((128, heads * width), np.float32)
    for h in range(heads):
        m[lane0 + h, h * width:(h + 1) * width] = 1.0
    return m


def _diag_mask(n, heads):
    return np.tile(np.eye(n, dtype=np.float32), (1, heads))


def _mlstm_call(proj, par, nw, batch, p):
    t = proj.shape[0]
    n = CHUNK
    nc = p // n
    tri = jnp.asarray(np.tril(np.ones((n, n), np.float32)), BF16)
    sel = jnp.asarray(np.stack([_lane_select(LANE_MI, ML_HEADS, n), _lane_select(LANE_MF, ML_HEADS, n)]), BF16)
    dg = jnp.asarray(_diag_mask(n, ML_HEADS), F32)
    proj3 = proj.reshape(batch, p, proj.shape[1])
    blk = lambda w, off: pl.BlockSpec((batch, n, w), lambda s: (0, s, off // w))
    const = lambda shape: pl.BlockSpec(shape, lambda s: (0,) * len(shape))
    m_rows = -(-batch * ML_HEADS // 8) * 8
    y = pl.pallas_call(
        _mlstm_kernel, grid=(nc,),
        in_specs=[blk(ML_WIDTH, C_MV), blk(ML_WIDTH, C_MV + ML_WIDTH), blk(ML_QK_W, C_MQ),
                  blk(ML_QK_W, C_MQ + ML_QK_W), blk(128, C_SMALL),
                  const((8, 128)), const((1, ML_WIDTH)), const((n, n)), const((2, 128, ML_HEADS * n)),
                  const((n, ML_HEADS * n))],
        out_specs=pl.BlockSpec((batch, n, ML_WIDTH), lambda s: (0, s, 0)),
        out_shape=jax.ShapeDtypeStruct((batch, p, ML_WIDTH), BF16),
        scratch_shapes=[pltpu.VMEM((batch * ML_HEADS, ML_QK, 2 * ML_V), F32), pltpu.VMEM((m_rows, 128), F32)],
        compiler_params=_cparams(("arbitrary",)), name="mlstm",
    )(proj3, proj3, proj3, proj3, proj3, par, nw, tri, sel, dg)
    return y.reshape(t, ML_WIDTH)


def _ssd_kernel(z_ref, x_ref, b_ref, c_ref, sm_ref, cw_ref, cb_ref, par_ref, dsk_ref, nw_ref,
                tri_ref, sel_ref, dg_ref, y_ref, xs_ref, bs_ref, cs_ref, st_ref):
    s = pl.program_id(1)
    n = CHUNK
    tail = 8
    chunks = z_ref.shape[0] // n

    @pl.when(s == 0)
    def _():
        st_ref[...] = jnp.zeros_like(st_ref)
        xs_ref[0:tail, :] = jnp.zeros((tail, xs_ref.shape[1]), F32)
        bs_ref[0:tail, :] = jnp.zeros((tail, bs_ref.shape[1]), F32)
        cs_ref[0:tail, :] = jnp.zeros((tail, cs_ref.shape[1]), F32)

    def chunk(ci, carry):
        _ssd_chunk(ci, s * chunks + ci, z_ref, x_ref, b_ref, c_ref, sm_ref, cw_ref, cb_ref, par_ref, dsk_ref,
                   nw_ref, tri_ref, sel_ref, dg_ref, y_ref, xs_ref, bs_ref, cs_ref, st_ref)
        return carry

    lax.fori_loop(0, chunks, chunk, 0)


def _ssd_chunk(ci, chunk_id, z_ref, x_ref, b_ref, c_ref, sm_ref, cw_ref, cb_ref, par_ref, dsk_ref, nw_ref,
               tri_ref, sel_ref, dg_ref, y_ref, xs_ref, bs_ref, cs_ref, st_ref):
    n = CHUNK
    tail = 8
    rows = pl.ds(pl.multiple_of(ci * n, n), n)

    def conv_silu(src_ref, scr_ref, c0, width, rowmask):
        scr_ref[tail:tail + n, :] = src_ref[rows, :]
        acc = cb_ref[:, c0:c0 + width]
        for j in range(CONV_W):
            off = tail - (CONV_W - 1) + j
            acc = acc + cw_ref[j:j + 1, c0:c0 + width] * scr_ref[off:off + n, :]
        scr_ref[0:tail, :] = scr_ref[n:n + tail, :]
        return jnp.where(rowmask, 0.0, _silu(acc))

    def padmask(width):
        return (chunk_id * n + lax.broadcasted_iota(jnp.int32, (n, width), 0)) < LEAD_PAD

    x = conv_silu(x_ref, xs_ref, 0, SSM_WIDTH, padmask(SSM_WIDTH))
    bm = conv_silu(b_ref, bs_ref, SSM_WIDTH, SSM_BC, padmask(SSM_BC))
    cm = conv_silu(c_ref, cs_ref, SSM_WIDTH + SSM_BC, SSM_BC, padmask(SSM_BC))

    pre = sm_ref[rows, :] + par_ref[0:1, :]
    dt = jnp.maximum(pre, 0.0) + jnp.log1p(jnp.exp(-jnp.abs(pre)))
    dt = jnp.where(padmask(128), 0.0, dt)
    da = dt * (-jnp.exp(par_ref[1:2, :]))
    cum = _sel_dot(tri_ref[...], da)
    sel = sel_ref[...]
    dt_col = _dot_sel(dt, sel)
    cum_col = _dot_sel(cum, sel)
    cum_row = jnp.sum(cum_col * dg_ref[...], axis=0, keepdims=True)
    w = SSM_WIDTH
    tt = lax.broadcasted_iota(jnp.int32, (n, w), 0)
    ss = lax.broadcasted_iota(jnp.int32, (n, w), 1) & (n - 1)
    decay = jnp.exp(jnp.where(tt >= ss, cum_col - cum_row, NEG))
    xdt = x * dt_col
    last = cum_col[n - 1:n, :]
    wend = (xdt * jnp.exp(last - cum_col)).astype(BF16)
    chunk_decay = jnp.exp(last)
    ecum = jnp.exp(cum_col)
    gw = SSM_HPG * SSM_HEADDIM
    rr = lax.broadcasted_iota(jnp.int32, (gw, gw), 0) // SSM_HEADDIM
    cc = lax.broadcasted_iota(jnp.int32, (gw, gw), 1) // SSM_HEADDIM
    blockdiag = rr == cc
    ys = []
    for g in range(SSM_GROUPS):
        gl = slice(g * gw, (g + 1) * gw)
        sl = slice(g * SSM_STATE, (g + 1) * SSM_STATE)
        cm_g = cm[:, sl].astype(BF16)
        bm_g = bm[:, sl].astype(BF16)
        cb = _dot_nt(cm_g, jnp.concatenate([bm_g] * SSM_HPG, axis=0))
        m = (cb * decay[:, gl]).astype(BF16)
        xdt_g = xdt[:, gl]
        bd = jnp.where(blockdiag, jnp.concatenate([xdt_g] * SSM_HPG, axis=0), 0.0).astype(BF16)
        st = st_ref[g]
        y_g = _dot(m, bd) + ecum[:, gl] * _dot(cm_g, st.astype(BF16))
        st_ref[g] = st * chunk_decay[:, gl] + _dot_tn(bm_g, wend[:, gl])
        ys.append(y_g)
    y = jnp.concatenate(ys, axis=1) + x * dsk_ref[...]
    y = y * _silu(z_ref[rows, :])
    outs = []
    for g in range(SSM_GROUPS):
        gl = slice(g * gw, (g + 1) * gw)
        y_g = y[:, gl]
        ms = jnp.mean(y_g * y_g, axis=-1, keepdims=True)
        outs.append(y_g * lax.rsqrt(ms + EPS))
    y_ref[rows, :] = (jnp.concatenate(outs, axis=1) * nw_ref[...]).astype(y_ref.dtype)


def _ssd_call(proj, cw, cb, par, dsk, nw, batch, p):
    t = proj.shape[0]
    n = CHUNK
    nc = p // n
    tri = jnp.asarray(np.tril(np.ones((n, n), np.float32)), BF16)
    sel = jnp.asarray(_lane_select(LANE_DT, SSM_HEADS, SSM_HEADDIM), BF16)
    dg = jnp.asarray(_diag_mask(n, SSM_HEADS), F32)
    rows = _row_tile(p, 704, mult=n)
    nc = p // rows
    blk = lambda w, off: pl.BlockSpec((rows, w), lambda b, s: (b * nc + s, off // w))
    const = lambda shape: pl.BlockSpec(shape, lambda b, s: (0,) * len(shape))
    cch = SSM_WIDTH + 2 * SSM_BC
    return pl.pallas_call(
        _ssd_kernel, grid=(batch, nc),
        in_specs=[blk(SSM_WIDTH, C_SZ), blk(SSM_WIDTH, C_SX), blk(SSM_BC, C_SB), blk(SSM_BC, C_SC),
                  blk(128, C_SMALL),
                  const((CONV_W, cch)), const((1, cch)), const((8, 128)), const((1, SSM_WIDTH)),
                  const((1, SSM_WIDTH)), const((n, n)), const((128, SSM_WIDTH)), const((n, SSM_WIDTH))],
        out_specs=pl.BlockSpec((rows, SSM_WIDTH), lambda b, s: (b * nc + s, 0)),
        out_shape=jax.ShapeDtypeStruct((t, SSM_WIDTH), BF16),
        scratch_shapes=[pltpu.VMEM((n + 8, SSM_WIDTH), F32), pltpu.VMEM((n + 8, SSM_BC), F32),
                        pltpu.VMEM((n + 8, SSM_BC), F32),
                        pltpu.VMEM((SSM_GROUPS, SSM_STATE, SSM_HPG * SSM_HEADDIM), F32)],
        compiler_params=_cparams(("arbitrary", "arbitrary")), name="ssd",
    )(proj, proj, proj, proj, proj, cw, cb, par, dsk, nw, tri, sel, dg)


LANE_E0 = N_GROUPS_MOE
RT_E, RT_RANK, RT_W = 0, 2, 4


def _first_max(vals, lane):
    m = jnp.max(vals, axis=-1, keepdims=True)
    idx = jnp.min(jnp.where(vals == m, lane.astype(F32), 128.0), axis=-1, keepdims=True)
    return m, idx.astype(jnp.int32)


def _outproj_kernel(ya_ref, yb_ref, yc_ref, h_ref, w_ref, nw_ref, rh_ref, rl_ref, rb_ref, tri_ref,
                    hm_ref, u_ref, rt_ref, cnt_ref):
    a0, a1 = HG_WIDTH, HG_WIDTH + ML_WIDTH
    h = h_ref[...]
    h = h + _dot(ya_ref[...], w_ref[0:a0, :])
    h = h + _dot(yb_ref[...], w_ref[a0:a1, :])
    h = h + _dot(yc_ref[...], w_ref[a1:, :])
    hm_ref[...] = h
    ms = jnp.mean(h * h, axis=-1, keepdims=True)
    u = h * lax.rsqrt(ms + EPS) * nw_ref[...]
    _pack_slab(u_ref, u)
    u_hi = u.astype(BF16)
    u_lo = (u - u_hi.astype(F32)).astype(BF16)
    lg = _dot(u_hi, rh_ref[...]) + (_dot(u_lo, rh_ref[...]) + _dot(u_hi, rl_ref[...])) + rb_ref[...]

    tm = lg.shape[0]
    lane = lax.broadcasted_iota(jnp.int32, (tm, 128), 1)
    g_mask = lane < N_GROUPS_MOE
    g_max, g_sel = _first_max(jnp.where(g_mask, lg, NEG), lane)
    g_gate = 1.0 / jnp.sum(jnp.where(g_mask, jnp.exp(lg - g_max), 0.0), axis=-1, keepdims=True)
    lo = LANE_E0 + g_sel * EXPERTS_PER_GROUP
    e_vals = jnp.where((lane >= lo) & (lane < lo + EXPERTS_PER_GROUP), lg, NEG)
    v1, i1 = _first_max(e_vals, lane)
    v2, i2 = _first_max(jnp.where(lane == i1, NEG, e_vals), lane)
    a = jnp.exp(v2 - v1)
    w1 = g_gate / (1.0 + a)
    w2 = w1 * a
    @pl.when(pl.program_id(0) == 0)
    def _():
        cnt_ref[...] = jnp.zeros_like(cnt_ref)

    hit1 = lane == i1
    hit2 = lane == i2
    onehot = jnp.where(hit1 | hit2, 1.0, 0.0)
    before = _dot(tri_ref[...], onehot.astype(BF16)) + cnt_ref[0:1, :]
    r1 = jnp.sum(jnp.where(hit1, before, 0.0), axis=-1, keepdims=True)
    r2 = jnp.sum(jnp.where(hit2, before, 0.0), axis=-1, keepdims=True)
    cnt_ref[...] = cnt_ref[...] + jnp.sum(onehot, axis=0, keepdims=True)
    rec = jnp.zeros((tm, 128), F32)
    for ln, val in ((RT_E, (i1 - LANE_E0).astype(F32)), (RT_E + 1, (i2 - LANE_E0).astype(F32)),
                    (RT_RANK, r1), (RT_RANK + 1, r2), (RT_W, w1), (RT_W + 1, w2)):
        rec = jnp.where(lane == ln, val, rec)
    rt_ref[...] = rec


def _outproj_call(ya, yb, yc, h, w, nw, r_hi, r_lo, r_bias, tm_target=352):
    t, d = h.shape
    tm = _row_tile(t, tm_target)
    tri = jnp.asarray(np.tril(np.ones((tm, tm), np.float32), -1), BF16)
    row = lambda i: (i, 0)
    const = lambda shape: pl.BlockSpec(shape, lambda i: (0, 0))
    return pl.pallas_call(
        _outproj_kernel, grid=(t // tm,),
        in_specs=[pl.BlockSpec((tm, HG_WIDTH), row), pl.BlockSpec((tm, ML_WIDTH), row),
                  pl.BlockSpec((tm, SSM_WIDTH), row), pl.BlockSpec((tm, d), row),
                  const((D_MIX, d)), const((1, d)), const((d, 128)), const((d, 128)), const((1, 128)),
                  const((tm, tm))],
        out_specs=[pl.BlockSpec((tm, d), row), pl.BlockSpec((tm * SLAB, 128), row), pl.BlockSpec((tm, 128), row),
                   const((8, 128))],
        out_shape=[jax.ShapeDtypeStruct((t, d), F32), jax.ShapeDtypeStruct((t * SLAB, 128), U32),
                   jax.ShapeDtypeStruct((t, 128), F32), jax.ShapeDtypeStruct((8, 128), F32)],
        compiler_params=_cparams(("arbitrary",)), name="out_proj_router",
    )(ya, yb, yc, h, w, nw.reshape(1, d), r_hi, r_lo, r_bias, tri)


MOE_RING = 3
WT_FIRST, WT_SLOT, WT_NEXT = 0, 1, 2


TAB_SRC2, TAB_DST_PREV, TAB_DST = 0, 1, 2


def _moe_kernel(be_ref, nu_ref, wt_ref, src01_ref, tab_ref, u_hbm,
                w1_hbm, w3_hbm, w2_hbm, o_hbm, xbuf, ybuf, st1, st3, st2, w1b, w3b, w2b, gsem, ssem, wsem,
                *, layer):
    i = pl.program_id(0)
    bm = MOE_BM
    n_used = nu_ref[0]
    xs = i % MOE_RING
    slot = xs
    other = (i + MOE_RING - 1) % MOE_RING

    def slab(idx):
        return pl.ds(pl.multiple_of(idx * SLAB, SLAB), SLAB)

    def row_in(tok, r, sl):
        return pltpu.make_async_copy(u_hbm.at[slab(tok), :], xbuf.at[sl, slab(r), :], gsem.at[sl])

    def row_out(r, row, sl):
        return pltpu.make_async_copy(ybuf.at[sl, slab(r), :], o_hbm.at[slab(row), :], ssem.at[sl])

    def gather(section, sl):
        def body(g, c):
            for j in range(8):
                r = g * 8 + j
                row_in(src01_ref[0, section * bm + r], r, sl).start(priority=j % 2)
            return c
        lax.fori_loop(0, bm // 8, body, 0)

    def scatter(sl):
        def body(g, c):
            for j in range(8):
                r = g * 8 + j
                row_out(r, tab_ref[0, TAB_DST * bm + r], sl).start(priority=j % 2)
            return c
        lax.fori_loop(0, bm // 8, body, 0)

    def wait_gather(sl):
        pltpu.make_async_copy(u_hbm.at[pl.ds(0, bm * SLAB), :], xbuf.at[sl], gsem.at[sl]).wait()

    def wait_scatter(sl):
        pltpu.make_async_copy(ybuf.at[sl], o_hbm.at[pl.ds(0, bm * SLAB), :], ssem.at[sl]).wait()

    def weight_copies(e, p):
        return [pltpu.make_async_copy(src.at[layer, e], dst.at[p], wsem.at[p])
                for src, dst in ((w1_hbm, st1), (w3_hbm, st3), (w2_hbm, st2))]

    n_real = o_hbm.shape[0] // SLAB - MOE_RING * bm

    @pl.when(i == 0)
    def _():
        for c in weight_copies(be_ref[0], 0):
            c.start()
        gather(0, 0)
        gather(1, 1)
        ybuf[...] = jnp.zeros_like(ybuf)
        for sl in range(2):
            pltpu.make_async_copy(ybuf.at[sl], o_hbm.at[pl.ds((n_real + sl * bm) * SLAB, bm * SLAB), :],
                                  ssem.at[sl]).start()

    @pl.when(i < n_used)
    def _():
        @pl.when(wt_ref[WT_FIRST, i] == 1)
        def _():
            p = wt_ref[WT_SLOT, i]
            nxt = wt_ref[WT_NEXT, i]

            @pl.when(nxt >= 0)
            def _():
                for c in weight_copies(nxt, 1 - p):
                    c.start(priority=1)

            for c in weight_copies(be_ref[i], p):
                c.wait()
            w1b[...] = st1[p].astype(BF16)
            w3b[...] = st3[p].astype(BF16)
            w2b[...] = st2[p].astype(BF16)

        wait_gather(xs)

        xn = (i + 2) % MOE_RING

        def issue(part, parts=4):
            for r in range(part * bm // parts, (part + 1) * bm // parts):
                row_in(tab_ref[0, TAB_SRC2 * bm + r], r, xn).start(priority=0)
                row_out(r, tab_ref[0, TAB_DST_PREV * bm + r], other).start(priority=1)

        f = w1b.shape[1]
        h1 = jnp.zeros((bm, f), F32)
        h3 = jnp.zeros((bm, f), F32)
        for j in range(SLAB):
            lo, hi = _unpack_slab(xbuf, j, bm, lead=(xs,))
            xk = jnp.concatenate([lo, hi], axis=1).astype(BF16)
            h1 = h1 + _dot(xk, w1b[j * 256:(j + 1) * 256, :])
            h3 = h3 + _dot(xk, w3b[j * 256:(j + 1) * 256, :])
            if j % 4 == 3:
                issue(j // 4)
        hid = (_silu(h1) * h3).astype(BF16)
        wait_scatter(slot)
        half = SLAB // 2
        _pack_slab(ybuf, _dot(hid, w2b[:, :half * 256]), lead=(slot,))
        issue(2)
        _pack_slab(ybuf, _dot(hid, w2b[:, half * 256:]), lead=(slot,), j0=half)
        issue(3)

    @pl.when(i == n_used - 1)
    def _():
        scatter(slot)
        for k in range(MOE_RING):
            wait_scatter(k)
        wait_gather((i + 1) % MOE_RING)
        wait_gather((i + 2) % MOE_RING)


def _moe_call(block_e, n_used, w_table, src_tok, dst_row, u, w1, w3, w2, layer, n_out_rows):
    d, f = w1.shape[-2:]
    assert d == SLAB * 256
    bm = MOE_BM
    nb = block_e.shape[0]
    src_tok = src_tok.reshape(nb, bm)
    dst_row = dst_row.reshape(nb, bm)
    lead = (n_out_rows - bm + jnp.arange(bm, dtype=jnp.int32)).reshape(1, bm)
    src_ahead = jnp.concatenate([src_tok[2:], src_tok[-1:], src_tok[-1:]], axis=0)
    dst_prev = jnp.concatenate([lead, dst_row[:-1]], axis=0)
    table = jnp.concatenate([src_ahead, dst_prev, dst_row], axis=1).reshape(nb, 1, 3 * bm)
    src01 = src_tok[:2].reshape(1, 1, 2 * bm)
    idx = lambda w, fn: pl.BlockSpec((None, 1, w), fn, memory_space=pltpu.SMEM)
    hbm = pl.BlockSpec(memory_space=pl.ANY)
    ring = pltpu.VMEM((MOE_RING, bm * SLAB, 128), U32)
    return pl.pallas_call(
        functools.partial(_moe_kernel, layer=layer),
        grid_spec=pltpu.PrefetchScalarGridSpec(
            num_scalar_prefetch=3, grid=(nb,),
            in_specs=[idx(2 * bm, lambda i, *_: (0, 0, 0)), idx(3 * bm, lambda i, *_: (i, 0, 0)),
                      hbm, hbm, hbm, hbm],
            out_specs=hbm,
            scratch_shapes=[ring, ring,
                            pltpu.VMEM((2, d, f), F32), pltpu.VMEM((2, d, f), F32), pltpu.VMEM((2, f, d), F32),
                            pltpu.VMEM((d, f), BF16), pltpu.VMEM((d, f), BF16), pltpu.VMEM((f, d), BF16),
                            pltpu.SemaphoreType.DMA((MOE_RING,)), pltpu.SemaphoreType.DMA((MOE_RING,)),
                            pltpu.SemaphoreType.DMA((2,))]),
        out_shape=jax.ShapeDtypeStruct((n_out_rows * SLAB, 128), U32),
        compiler_params=_cparams(("arbitrary",)), name="moe_ffn",
    )(block_e, n_used, w_table, src01, table, u, w1, w3, w2)


def _route_tables(rt, cnt, t):
    bm = MOE_BM
    tk = t * TOP_K
    e = rt[:, RT_E:RT_E + TOP_K].astype(jnp.int32)
    rank = rt[:, RT_RANK:RT_RANK + TOP_K].astype(jnp.int32)
    wts = rt[:, RT_W:RT_W + TOP_K]
    counts = cnt[0, LANE_E0:LANE_E0 + N_EXPERTS].astype(jnp.int32)
    padded = (counts + bm - 1) // bm * bm
    pad_ends = jnp.cumsum(padded)
    pad_starts = pad_ends - padded
    e_ids = jnp.arange(N_EXPERTS, dtype=jnp.int32)

    def lookup(table, idx):
        return jnp.sum(jnp.where(idx[..., None] == e_ids, table, 0), axis=-1)

    dest = (lookup(pad_starts, e) + rank).reshape(-1)
    n_blocks = -(-tk // bm) + N_EXPERTS
    n_rows = n_blocks * bm
    inv = jnp.full((n_rows,), -1, jnp.int32).at[dest].set(jnp.arange(tk, dtype=jnp.int32))
    tok, slot = inv // TOP_K, inv % TOP_K
    src_tok = jnp.where(inv >= 0, tok, 0)
    dst_row = jnp.where(inv >= 0, slot * t + tok, tk + (jnp.arange(n_rows, dtype=jnp.int32) % (MOE_RING * bm)))
    block_ids = jnp.arange(n_blocks, dtype=jnp.int32)
    block_e = jnp.minimum(jnp.sum(pad_ends[None, :] <= block_ids[:, None] * bm, axis=1), N_EXPERTS - 1).astype(jnp.int32)
    n_used = (pad_ends[-1] // bm).astype(jnp.int32).reshape(1)
    has_rows = counts > 0
    at_or_after = lax.cummin(jnp.where(has_rows, e_ids, N_EXPERTS)[::-1])[::-1]
    next_e = jnp.concatenate([at_or_after[1:], jnp.full((1,), N_EXPERTS, jnp.int32)])
    next_e = jnp.where(next_e >= N_EXPERTS, -1, next_e)
    stage = (jnp.cumsum(has_rows.astype(jnp.int32)) - 1) % 2
    first = jnp.concatenate([jnp.ones((1,), bool), block_e[1:] != block_e[:-1]]) & (block_ids < n_used[0])
    w_table = jnp.stack([first.astype(jnp.int32), lookup(stage, block_e), lookup(next_e, block_e)]).astype(jnp.int32)
    return block_e, n_used, w_table, src_tok, dst_row, wts, tk + MOE_RING * bm


def _lane_row(pairs):
    row = jnp.zeros((128,), F32)
    for lane0, vals in pairs:
        row = row.at[lane0:lane0 + vals.shape[0]].set(vals.astype(F32))
    return row


def kernel(x, meta_tokens, hg_lb_logits, norm_mix_w, w_in, hg_norm_w, ml_b_i, ml_b_f, ml_norm_w,
           ssm_conv_w, ssm_conv_b, ssm_dt_bias, ssm_a_log, ssm_d, ssm_norm_w, w_out, norm_ffn_w,
           moe_w_group, moe_b_group, moe_w_router, moe_b_router, moe_w1, moe_w3, moe_w2, final_norm_w):
    batch, seq, d = x.shape
    depth = w_in.shape[0]
    p = LEAD_PAD + N_META + seq
    t = batch * p
    meta = jnp.broadcast_to(meta_tokens.astype(x.dtype)[None], (batch, N_META, d))
    h = jnp.concatenate([jnp.zeros((batch, LEAD_PAD, d), x.dtype), meta, x], axis=1).reshape(t, d)

    lb_w = jax.nn.softmax(hg_lb_logits.astype(F32), axis=0)
    lower_bounds = jnp.cumsum(lb_w, axis=0) - lb_w[0]

    contrib = wts = None
    for layer in range(depth):
        w = w_in[layer]
        o_sz = N_HEAD + 2 * ML_HEADS
        o_sx = o_sz + SSM_WIDTH
        o_sb = o_sx + SSM_WIDTH
        o_sc = o_sb + SSM_BC
        o_dt = o_sc + SSM_BC
        w_tail = jnp.concatenate([
            w[:, o_sb:o_sc], w[:, o_sz:o_sx], w[:, o_sx:o_sb], w[:, o_sc:o_dt], w[:, N_HEAD:o_sz], w[:, o_dt:],
            jnp.zeros((d, N_PROJ - C_SMALL - 2 * ML_HEADS - SSM_HEADS), w.dtype)], axis=1)
        lb = lower_bounds[layer]
        lbf = jnp.maximum(lb, LB_FLOOR)
        hg_par = jnp.zeros((8, HG_KEY), F32).at[0].set(lbf).at[1].set(1.0 - lb).at[2].set(lbf - lb).at[3].set(hg_norm_w[layer])
        ml_par = jnp.zeros((8, 128), F32).at[0].set(_lane_row([(LANE_MI, ml_b_i[layer]), (LANE_MF, ml_b_f[layer])]))
        ss_par = jnp.zeros((8, 128), F32).at[0].set(_lane_row([(LANE_DT, ssm_dt_bias[layer])]))
        ss_par = ss_par.at[1].set(_lane_row([(LANE_DT, ssm_a_log[layer])]))
        dskip = jnp.repeat(ssm_d[layer].astype(F32), SSM_HEADDIM).reshape(1, SSM_WIDTH)
        w_r = jnp.concatenate([moe_w_group[layer],
                               moe_w_router[layer].transpose(1, 0, 2).reshape(d, N_EXPERTS),
                               jnp.zeros((d, 128 - N_GROUPS_MOE - N_EXPERTS), F32)], axis=1)
        r_hi = w_r.astype(BF16)
        r_lo = (w_r - r_hi.astype(F32)).astype(BF16)

        if layer == 0:
            (u,) = _norm_call(h, norm_mix_w[layer], write_h=False, u_dtype=BF16)
        else:
            h, u = _norm_call(h, norm_mix_w[layer], contrib, wts, write_h=True, u_dtype=BF16)
        proj = _inproj_call(u, w_in, layer, w_tail)
        ya = _hgrn2_call(proj, hg_par, batch, p)
        yb = _mlstm_call(proj, ml_par, ml_norm_w[layer].reshape(1, ML_WIDTH), batch, p)
        yc = _ssd_call(proj, ssm_conv_w[layer], ssm_conv_b[layer].reshape(1, -1), ss_par, dskip,
                       ssm_norm_w[layer].reshape(1, SSM_WIDTH), batch, p)
        r_bias = _lane_row([(0, moe_b_group[layer]), (LANE_E0, moe_b_router[layer].reshape(-1))]).reshape(1, 128)
        h, u_ffn, rt, cnt = _outproj_call(ya, yb, yc, h, w_out[layer].astype(BF16), norm_ffn_w[layer],
                                          r_hi, r_lo, r_bias)
        block_e, n_used, w_table, src_tok, dst_row, wts, n_out_rows = _route_tables(rt, cnt, t)
        contrib = _moe_call(block_e, n_used, w_table, src_tok, dst_row, u_ffn, moe_w1, moe_w3, moe_w2, layer,
                            n_out_rows)
    return _final_call(h, contrib, wts, final_norm_w, batch, p)
```

```python
import functools

import jax
import jax.numpy as jnp
import numpy as np
from jax import lax
from jax.experimental import pallas as pl
from jax.experimental.pallas import tpu as pltpu

F32 = jnp.float32
BF16 = jnp.bfloat16

D_MODEL = 2048
N_META = 16
CHUNK = 64
HG_CHUNK = 16
LEAD_PAD = CHUNK - N_META
EPS = 1e-6
NEG = -1e30
LB_FLOOR = 1e-30

HG_HEADS = 4
HG_KDIM = 128
HG_KEY = HG_HEADS * HG_KDIM
HG_WIDTH = HG_HEADS * 128

ML_HEADS = 4
ML_QK = 64
ML_V = 128
ML_QK_W = ML_HEADS * ML_QK
ML_WIDTH = ML_HEADS * ML_V
GATE_CAP = 15.0

SSM_HEADS = 16
SSM_HEADDIM = 64
SSM_WIDTH = SSM_HEADS * SSM_HEADDIM
SSM_STATE = 128
SSM_GROUPS = 4
SSM_HPG = SSM_HEADS // SSM_GROUPS
SSM_BC = SSM_GROUPS * SSM_STATE
CONV_W = 4

D_MIX = HG_WIDTH + ML_WIDTH + SSM_WIDTH

N_GROUPS_MOE = 4
EXPERTS_PER_GROUP = 8
N_EXPERTS = N_GROUPS_MOE * EXPERTS_PER_GROUP
TOP_K = 2
D_EXPERT = 512
MOE_BM = 128

C_HG = 0
C_SZ = 2048
C_SX = 3072
C_MV = 5120
C_MQ = 6144
C_SMALL = 6656
N_PROJ = 6912
LANE_MI = 0
LANE_MF = ML_HEADS
LANE_DT = 2 * ML_HEADS

VMEM_LIMIT = 56 * 1024 * 1024


def _cparams(sem):
    return pltpu.CompilerParams(dimension_semantics=sem, vmem_limit_bytes=VMEM_LIMIT)


def _row_tile(n, target, mult=16):
    best = None
    for t in range(mult, min(n, target) + 1, mult):
        if n % t == 0:
            best = t
    assert best is not None, (n, target, mult)
    return best


def _split3(x):
    hi = x.astype(BF16)
    r = x - hi.astype(F32)
    mid = r.astype(BF16)
    lo = (r - mid.astype(F32)).astype(BF16)
    return hi, mid, lo


def _dot(a, b):
    return jnp.dot(a, b, preferred_element_type=F32)


def _sel_dot(sel, x):
    hi, mid, lo = _split3(x)
    return _dot(sel, hi) + _dot(sel, mid) + _dot(sel, lo)


def _dot_sel(x, sel):
    hi, mid, lo = _split3(x)
    return _dot(hi, sel) + _dot(mid, sel) + _dot(lo, sel)


def _dot_nt(a, b):
    return lax.dot_general(a, b, (((1,), (1,)), ((), ())), preferred_element_type=F32)


def _dot_tn(a, b):
    return lax.dot_general(a, b, (((0,), (0,)), ((), ())), preferred_element_type=F32)


def _log_sigmoid(x):
    return jnp.minimum(x, 0.0) - jnp.log1p(jnp.exp(-jnp.abs(x)))


def _sigmoid(x):
    return 1.0 / (1.0 + jnp.exp(-x))


def _silu(x):
    return x * _sigmoid(x)


SLAB = 8
U32 = jnp.uint32
HI_MASK = 0xFFFF0000


def _bf16_bits(x):
    return lax.bitcast_convert_type(x.astype(BF16).astype(F32), U32)


def _pack_slab(ref, val, lead=(), j0=0):
    rows = val.shape[0]
    for jj in range(val.shape[1] // 256):
        lo = _bf16_bits(val[:, jj * 256:jj * 256 + 128])
        hi = _bf16_bits(val[:, jj * 256 + 128:(jj + 1) * 256])
        ref[(*lead, pl.ds(j0 + jj, rows, stride=SLAB), slice(None))] = (lo >> 16) | (hi & U32(HI_MASK))


def _unpack_slab(ref, j, rows, lead=()):
    w = ref[(*lead, pl.ds(j, rows, stride=SLAB), slice(None))]
    return (lax.bitcast_convert_type(w << 16, F32), lax.bitcast_convert_type(w & U32(HI_MASK), F32))


def _combine_rows(h_ref, c0_ref, c1_ref, wt_ref):
    rows = h_ref.shape[0]
    wt = wt_ref[...]
    w0, w1 = wt[:, 0:1], wt[:, 1:2]
    pieces = []
    for j in range(SLAB):
        a0, b0 = _unpack_slab(c0_ref, j, rows)
        a1, b1 = _unpack_slab(c1_ref, j, rows)
        pieces += [w0 * a0 + w1 * a1, w0 * b0 + w1 * b1]
    return h_ref[...] + jnp.concatenate(pieces, axis=1)


def _norm_kernel(*refs, combine, write_h):
    if combine:
        h_ref, c0_ref, c1_ref, wt_ref, nw_ref = refs[:5]
        outs = refs[5:]
        h = _combine_rows(h_ref, c0_ref, c1_ref, wt_ref)
    else:
        h_ref, nw_ref = refs[:2]
        outs = refs[2:]
        h = h_ref[...]
    if write_h:
        outs[0][...] = h
    ms = jnp.mean(h * h, axis=-1, keepdims=True)
    u_ref = outs[-1]
    u_ref[...] = (h * lax.rsqrt(ms + EPS) * nw_ref[...]).astype(u_ref.dtype)


def _norm_call(h, nw, contrib=None, wts=None, *, write_h, u_dtype, tm_target=264):
    t, d = h.shape
    tm = _row_tile(t, tm_target)
    combine = contrib is not None
    row = lambda i: (i, 0)
    in_specs = [pl.BlockSpec((tm, d), row)]
    args = [h]
    if combine:
        in_specs += [pl.BlockSpec((tm * SLAB, 128), row), pl.BlockSpec((tm * SLAB, 128), lambda i: (t // tm + i, 0)),
                     pl.BlockSpec((tm, 2), row)]
        args += [contrib, contrib, wts]
    in_specs.append(pl.BlockSpec((1, d), lambda i: (0, 0)))
    args.append(nw.reshape(1, d))
    out_shape, out_specs = [], []
    if write_h:
        out_shape.append(jax.ShapeDtypeStruct((t, d), F32))
        out_specs.append(pl.BlockSpec((tm, d), row))
    out_shape.append(jax.ShapeDtypeStruct((t, d), u_dtype))
    out_specs.append(pl.BlockSpec((tm, d), row))
    return pl.pallas_call(
        functools.partial(_norm_kernel, combine=combine, write_h=write_h),
        grid=(t // tm,), in_specs=in_specs, out_specs=out_specs, out_shape=out_shape,
        compiler_params=_cparams(("arbitrary",)), name="combine_norm",
    )(*args)


def _final_kernel(h_ref, c0_ref, c1_ref, wt_ref, nw_ref, o_ref):
    h = _combine_rows(h_ref, c0_ref, c1_ref, wt_ref)
    ms = jnp.mean(h * h, axis=-1, keepdims=True)
    o_ref[...] = h * lax.rsqrt(ms + EPS) * nw_ref[...]


def _final_call(h, contrib, wts, nw, batch, p):
    t, d = h.shape
    seq = p - CHUNK
    tm = _row_tile(seq, 256)
    n_out = seq // tm
    row0 = lambda b, i: b * p + CHUNK + i * tm
    src = lambda b, i: (pl.multiple_of(row0(b, i), CHUNK), 0)
    slab0 = lambda b, i: (pl.multiple_of(row0(b, i) * SLAB, CHUNK), 0)
    slab1 = lambda b, i: (pl.multiple_of((t + row0(b, i)) * SLAB, CHUNK), 0)
    win = lambda r, w, fn: pl.BlockSpec((pl.Element(r), pl.Element(w)), fn)
    return pl.pallas_call(
        _final_kernel, grid=(batch, n_out),
        in_specs=[win(tm, d, src), win(tm * SLAB, 128, slab0), win(tm * SLAB, 128, slab1), win(tm, 2, src),
                  pl.BlockSpec((1, d), lambda b, i: (0, 0))],
        out_specs=pl.BlockSpec((None, tm, d), lambda b, i: (b, i, 0)),
        out_shape=jax.ShapeDtypeStruct((batch, seq, d), F32),
        compiler_params=_cparams(("arbitrary", "arbitrary")), name="final_norm",
    )(h, contrib, contrib, wts, nw.reshape(1, d))


def _matmul_kernel(x_ref, w_ref, o_ref):
    o_ref[...] = _dot(x_ref[...], w_ref[...])


def _inproj_call(u, w, layer, tm_target=2112, tn=768):
    t, d = u.shape
    n = w.shape[-1]
    tm = _row_tile(t, tm_target)
    assert n % tn == 0
    return pl.pallas_call(
        _matmul_kernel, grid=(t // tm, n // tn),
        in_specs=[pl.BlockSpec((tm, d), lambda i, j: (i, 0)),
                  pl.BlockSpec((None, d, tn), lambda i, j: (layer, 0, j))],
        out_specs=pl.BlockSpec((tm, tn), lambda i, j: (i, j)),
        out_shape=jax.ShapeDtypeStruct((t, n), F32),
        compiler_params=_cparams(("arbitrary", "arbitrary")), name="in_proj",
    )(u, w)


def _hgrn2_kernel(q_ref, f_ref, i_ref, g_ref, par_ref, o_ref, st_ref, *, rows):
    s = pl.program_id(1)

    @pl.when(s == 0)
    def _():
        st_ref[...] = jnp.zeros_like(st_ref)

    c = HG_CHUNK
    ones = jnp.ones((HG_KDIM, 128), BF16)
    rid = lax.broadcasted_iota(jnp.int32, (c, 128), 0)
    scale = HG_KDIM ** -0.5

    hc = c // 2

    def chunk(ci, carry):
        r0 = pl.multiple_of(ci * c, c)
        pad = (s * rows + r0 + rid) < LEAD_PAD

        def front(h):
            cols = slice(h * 128, (h + 1) * 128)
            a_lb = par_ref[0:1, cols]
            b_lb = par_ref[1:2, cols]
            c_lb = par_ref[2:3, cols]
            z = f_ref[pl.ds(r0, c), cols]
            sg = _sigmoid(z)
            f = a_lb + b_lb * sg
            log_f = jnp.where(pad, 0.0, jnp.log(f))
            k = jnp.where(pad, 0.0, b_lb * (1.0 - sg) - c_lb)
            q = q_ref[pl.ds(r0, c), cols] * scale
            v = i_ref[pl.ds(r0, c), cols]
            cum = log_f
            for sh in (1, 2, 4, 8):
                cum = cum + jnp.where(rid >= sh, pltpu.roll(cum, sh, axis=0), 0.0)
            parts = []
            for s_ in range(c):
                lo = 0 if s_ < hc else hc
                rel = jnp.where(rid[lo:] >= s_, cum[lo:] - cum[s_:s_ + 1, :], NEG)
                parts.append(q[lo:] * (k[s_:s_ + 1, :] * jnp.exp(rel)))
            sc = _dot(jnp.concatenate(parts, axis=0).astype(BF16), ones)
            st = st_ref[h]
            o_inter = _dot_nt((q * jnp.exp(cum)).astype(BF16), st.astype(BF16))
            last = cum[c - 1:c, :]
            kd = (k * jnp.exp(last - cum)).astype(BF16)
            st_ref[h] = st * jnp.exp(last) + _dot_tn(v.astype(BF16), kd)
            return sc, o_inter, v

        def back(h, sc, o_inter, v):
            cols = slice(h * 128, (h + 1) * 128)
            o_top = o_inter[:hc]
            o_bot = o_inter[hc:]
            for s_ in range(hc):
                o_top = o_top + sc[s_ * c:s_ * c + hc, :] * v[s_:s_ + 1, :]
                o_bot = o_bot + sc[s_ * c + hc:(s_ + 1) * c, :] * v[s_:s_ + 1, :]
            for s_ in range(hc, c):
                r_ = hc * c + (s_ - hc) * hc
                o_bot = o_bot + sc[r_:r_ + hc, :] * v[s_:s_ + 1, :]
            o = jnp.concatenate([o_top, o_bot], axis=0)
            ms = jnp.mean(o * o, axis=-1, keepdims=True)
            g = g_ref[pl.ds(r0, c), cols]
            o_ref[pl.ds(r0, c), cols] = (o * lax.rsqrt(ms + EPS) * par_ref[3:4, cols] * _silu(g)).astype(o_ref.dtype)

        pending = front(0)
        for h in range(1, HG_HEADS):
            nxt = front(h)
            back(h - 1, *pending)
            pending = nxt
        back(HG_HEADS - 1, *pending)
        return carry

    n_chunks = rows // c
    lax.fori_loop(0, n_chunks, chunk, 0, unroll=3 if n_chunks % 3 == 0 else 1)


def _hgrn2_call(proj, par, batch, p):
    t = proj.shape[0]
    rows = _row_tile(p, 528)
    nb = p // rows
    w = HG_KEY
    blk = lambda j: pl.BlockSpec((rows, w), lambda b, s, j=j: (b * nb + s, C_HG // w + j))
    return pl.pallas_call(
        functools.partial(_hgrn2_kernel, rows=rows), grid=(batch, nb),
        in_specs=[blk(0), blk(1), blk(2), blk(3), pl.BlockSpec((8, w), lambda b, s: (0, 0))],
        out_specs=pl.BlockSpec((rows, w), lambda b, s: (b * nb + s, 0)),
        out_shape=jax.ShapeDtypeStruct((t, HG_WIDTH), BF16),
        scratch_shapes=[pltpu.VMEM((HG_HEADS, 128, HG_KDIM), F32)],
        compiler_params=_cparams(("arbitrary", "arbitrary")), name="hgrn2",
    )(proj, proj, proj, proj, par)


def _mlstm_kernel(v_ref, o_ref, q_ref, k_ref, sm_ref, par_ref, nw_ref, tri_ref, sel_ref, dg_ref,
                  y_ref, c_ref, m_ref):
    s = pl.program_id(0)

    @pl.when(s == 0)
    def _():
        c_ref[...] = jnp.zeros_like(c_ref)
        m_ref[...] = jnp.zeros_like(m_ref)

    n = CHUNK
    rid = lax.broadcasted_iota(jnp.int32, (n, 128), 0)
    pad = (s * n + rid) < LEAD_PAD
    sel_i = sel_ref[0]
    sel_f = sel_ref[1]
    dg = dg_ref[...]
    w = ML_HEADS * n
    tt = lax.broadcasted_iota(jnp.int32, (n, w), 0)
    ss = lax.broadcasted_iota(jnp.int32, (n, w), 1) & (n - 1)
    causal = tt >= ss
    scale = ML_QK ** -0.5
    lane = lax.broadcasted_iota(jnp.int32, (n, 128), 1)
    one_col = jnp.where(lane == 0, 1.0, 0.0).astype(BF16)
    for b in range(v_ref.shape[0]):
        pre = sm_ref[b] + par_ref[0:1, :]
        cap = GATE_CAP * jnp.tanh(pre * (1.0 / GATE_CAP))
        log_i = jnp.where(pad, NEG, cap)
        log_f = jnp.where(pad, 0.0, _log_sigmoid(cap))
        cum = _sel_dot(tri_ref[...], log_f)
        cum_col = _dot_sel(cum, sel_f)
        cum_row = jnp.sum(cum_col * dg, axis=0, keepdims=True)
        li_row = jnp.sum(_dot_sel(log_i, sel_i) * dg, axis=0, keepdims=True)
        dmat = jnp.where(causal, cum_col - cum_row + li_row, NEG)
        hs = range(ML_HEADS)
        sh = [b * ML_HEADS + h for h in hs]
        d_h = [dmat[:, h * n:(h + 1) * n] for h in hs]
        cum_h = [cum[:, LANE_MF + h:LANE_MF + h + 1] for h in hs]
        li_h = [log_i[:, LANE_MI + h:LANE_MI + h + 1] for h in hs]
        m_st = [m_ref[sh[h]:sh[h] + 1, 0:1] for h in hs]
        q = [(q_ref[b, :, h * ML_QK:(h + 1) * ML_QK] * scale).astype(BF16) for h in hs]
        k = [k_ref[b, :, h * ML_QK:(h + 1) * ML_QK] for h in hs]
        v_aug = [jnp.concatenate([v_ref[b, :, h * ML_V:(h + 1) * ML_V].astype(BF16), one_col], axis=1) for h in hs]
        c_prev = [c_ref[sh[h]] for h in hs]
        inter = [cum_h[h] + m_st[h] for h in hs]
        m_t = [jnp.maximum(inter[h], jnp.max(d_h[h], axis=-1, keepdims=True)) for h in hs]
        qk = [_dot_nt(q[h], k[h].astype(BF16)) for h in hs]
        qc = [_dot(q[h], c_prev[h].astype(BF16)) for h in hs]
        pw = [(qk[h] * jnp.exp(d_h[h] - m_t[h])).astype(BF16) for h in hs]
        nd = [_dot(pw[h], v_aug[h]) + jnp.exp(inter[h] - m_t[h]) * qc[h] for h in hs]
        hh = [nd[h][:, :ML_V] / jnp.maximum(jnp.abs(nd[h][:, ML_V:ML_V + 1]), jnp.exp(-m_t[h])) for h in hs]
        tot = [cum_h[h][n - 1:n, :] for h in hs]
        to_end = [tot[h] - cum_h[h] + li_h[h] for h in hs]
        m_loc = [jnp.max(to_end[h], axis=0, keepdims=True) for h in hs]
        kw = [(k[h] * jnp.exp(to_end[h] - m_loc[h])).astype(BF16) for h in hs]
        c_loc = [_dot_tn(kw[h], v_aug[h]) for h in hs]
        m_new = [jnp.maximum(tot[h] + m_st[h], m_loc[h]) for h in hs]
        for h in hs:
            c_ref[sh[h]] = (jnp.exp(tot[h] + m_st[h] - m_new[h]) * c_prev[h]
                            + jnp.exp(m_loc[h] - m_new[h]) * c_loc[h])
            m_ref[sh[h]:sh[h] + 1, :] = jnp.broadcast_to(m_new[h], (1, 128))
        for h in hs:
            ms = jnp.mean(hh[h] * hh[h], axis=-1, keepdims=True)
            cols = slice(h * ML_V, (h + 1) * ML_V)
            y_ref[b, :, cols] = (hh[h] * lax.rsqrt(ms + EPS) * nw_ref[:, cols]
                                 * _sigmoid(o_ref[b, :, cols])).astype(y_ref.dtype)


def _lane_select(lane0, heads, width):
    m = np.zeros((128, heads * width), np.float32)
    for h in range(heads):
        m[lane0 + h, h * width:(h + 1) * width] = 1.0
    return m


def _diag_mask(n, heads):
    return np.tile(np.eye(n, dtype=np.float32), (1, heads))


def _mlstm_call(proj, par, nw, batch, p):
    t = proj.shape[0]
    n = CHUNK
    nc = p // n
    tri = jnp.asarray(np.tril(np.ones((n, n), np.float32)), BF16)
    sel = jnp.asarray(np.stack([_lane_select(LANE_MI, ML_HEADS, n), _lane_select(LANE_MF, ML_HEADS, n)]), BF16)
    dg = jnp.asarray(_diag_mask(n, ML_HEADS), F32)
    proj3 = proj.reshape(batch, p, proj.shape[1])
    blk = lambda w, off: pl.BlockSpec((batch, n, w), lambda s: (0, s, off // w))
    const = lambda shape: pl.BlockSpec(shape, lambda s: (0,) * len(shape))
    m_rows = -(-batch * ML_HEADS // 8) * 8
    y = pl.pallas_call(
        _mlstm_kernel, grid=(nc,),
        in_specs=[blk(ML_WIDTH, C_MV), blk(ML_WIDTH, C_MV + ML_WIDTH), blk(ML_QK_W, C_MQ),
                  blk(ML_QK_W, C_MQ + ML_QK_W), blk(128, C_SMALL),
                  const((8, 128)), const((1, ML_WIDTH)), const((n, n)), const((2, 128, ML_HEADS * n)),
                  const((n, ML_HEADS * n))],
        out_specs=pl.BlockSpec((batch, n, ML_WIDTH), lambda s: (0, s, 0)),
        out_shape=jax.ShapeDtypeStruct((batch, p, ML_WIDTH), BF16),
        scratch_shapes=[pltpu.VMEM((batch * ML_HEADS, ML_QK, 2 * ML_V), F32), pltpu.VMEM((m_rows, 128), F32)],
        compiler_params=_cparams(("arbitrary",)), name="mlstm",
    )(proj3, proj3, proj3, proj3, proj3, par, nw, tri, sel, dg)
    return y.reshape(t, ML_WIDTH)


def _ssd_kernel(z_ref, x_ref, b_ref, c_ref, sm_ref, cw_ref, cb_ref, par_ref, dsk_ref, nw_ref,
                tri_ref, sel_ref, dg_ref, y_ref, xs_ref, bs_ref, cs_ref, st_ref):
    s = pl.program_id(1)
    n = CHUNK
    tail = 8
    chunks = z_ref.shape[0] // n

    @pl.when(s == 0)
    def _():
        st_ref[...] = jnp.zeros_like(st_ref)
        xs_ref[0:tail, :] = jnp.zeros((tail, xs_ref.shape[1]), F32)
        bs_ref[0:tail, :] = jnp.zeros((tail, bs_ref.shape[1]), F32)
        cs_ref[0:tail, :] = jnp.zeros((tail, cs_ref.shape[1]), F32)

    def chunk(ci, carry):
        _ssd_chunk(ci, s * chunks + ci, z_ref, x_ref, b_ref, c_ref, sm_ref, cw_ref, cb_ref, par_ref, dsk_ref,
                   nw_ref, tri_ref, sel_ref, dg_ref, y_ref, xs_ref, bs_ref, cs_ref, st_ref)
        return carry

    lax.fori_loop(0, chunks, chunk, 0)


def _ssd_chunk(ci, chunk_id, z_ref, x_ref, b_ref, c_ref, sm_ref, cw_ref, cb_ref, par_ref, dsk_ref, nw_ref,
               tri_ref, sel_ref, dg_ref, y_ref, xs_ref, bs_ref, cs_ref, st_ref):
    n = CHUNK
    tail = 8
    rows = pl.ds(pl.multiple_of(ci * n, n), n)

    def conv_silu(src_ref, scr_ref, c0, width, rowmask):
        scr_ref[tail:tail + n, :] = src_ref[rows, :]
        acc = cb_ref[:, c0:c0 + width]
        for j in range(CONV_W):
            off = tail - (CONV_W - 1) + j
            acc = acc + cw_ref[j:j + 1, c0:c0 + width] * scr_ref[off:off + n, :]
        scr_ref[0:tail, :] = scr_ref[n:n + tail, :]
        return jnp.where(rowmask, 0.0, _silu(acc))

    def padmask(width):
        return (chunk_id * n + lax.broadcasted_iota(jnp.int32, (n, width), 0)) < LEAD_PAD

    x = conv_silu(x_ref, xs_ref, 0, SSM_WIDTH, padmask(SSM_WIDTH))
    bm = conv_silu(b_ref, bs_ref, SSM_WIDTH, SSM_BC, padmask(SSM_BC))
    cm = conv_silu(c_ref, cs_ref, SSM_WIDTH + SSM_BC, SSM_BC, padmask(SSM_BC))

    pre = sm_ref[rows, :] + par_ref[0:1, :]
    dt = jnp.maximum(pre, 0.0) + jnp.log1p(jnp.exp(-jnp.abs(pre)))
    dt = jnp.where(padmask(128), 0.0, dt)
    da = dt * (-jnp.exp(par_ref[1:2, :]))
    cum = _sel_dot(tri_ref[...], da)
    sel = sel_ref[...]
    dt_col = _dot_sel(dt, sel)
    cum_col = _dot_sel(cum, sel)
    cum_row = jnp.sum(cum_col * dg_ref[...], axis=0, keepdims=True)
    w = SSM_WIDTH
    tt = lax.broadcasted_iota(jnp.int32, (n, w), 0)
    ss = lax.broadcasted_iota(jnp.int32, (n, w), 1) & (n - 1)
    decay = jnp.exp(jnp.where(tt >= ss, cum_col - cum_row, NEG))
    xdt = x * dt_col
    last = cum_col[n - 1:n, :]
    wend = (xdt * jnp.exp(last - cum_col)).astype(BF16)
    chunk_decay = jnp.exp(last)
    ecum = jnp.exp(cum_col)
    gw = SSM_HPG * SSM_HEADDIM
    rr = lax.broadcasted_iota(jnp.int32, (gw, gw), 0) // SSM_HEADDIM
    cc = lax.broadcasted_iota(jnp.int32, (gw, gw), 1) // SSM_HEADDIM
    blockdiag = rr == cc
    ys = []
    for g in range(SSM_GROUPS):
        gl = slice(g * gw, (g + 1) * gw)
        sl = slice(g * SSM_STATE, (g + 1) * SSM_STATE)
        cm_g = cm[:, sl].astype(BF16)
        bm_g = bm[:, sl].astype(BF16)
        cb = _dot_nt(cm_g, jnp.concatenate([bm_g] * SSM_HPG, axis=0))
        m = (cb * decay[:, gl]).astype(BF16)
        xdt_g = xdt[:, gl]
        bd = jnp.where(blockdiag, jnp.concatenate([xdt_g] * SSM_HPG, axis=0), 0.0).astype(BF16)
        st = st_ref[g]
        y_g = _dot(m, bd) + ecum[:, gl] * _dot(cm_g, st.astype(BF16))
        st_ref[g] = st * chunk_decay[:, gl] + _dot_tn(bm_g, wend[:, gl])
        ys.append(y_g)
    y = jnp.concatenate(ys, axis=1) + x * dsk_ref[...]
    y = y * _silu(z_ref[rows, :])
    outs = []
    for g in range(SSM_GROUPS):
        gl = slice(g * gw, (g + 1) * gw)
        y_g = y[:, gl]
        ms = jnp.mean(y_g * y_g, axis=-1, keepdims=True)
        outs.append(y_g * lax.rsqrt(ms + EPS))
    y_ref[rows, :] = (jnp.concatenate(outs, axis=1) * nw_ref[...]).astype(y_ref.dtype)


def _ssd_call(proj, cw, cb, par, dsk, nw, batch, p):
    t = proj.shape[0]
    n = CHUNK
    nc = p // n
    tri = jnp.asarray(np.tril(np.ones((n, n), np.float32)), BF16)
    sel = jnp.asarray(_lane_select(LANE_DT, SSM_HEADS, SSM_HEADDIM), BF16)
    dg = jnp.asarray(_diag_mask(n, SSM_HEADS), F32)
    rows = _row_tile(p, 704, mult=n)
    nc = p // rows
    blk = lambda w, off: pl.BlockSpec((rows, w), lambda b, s: (b * nc + s, off // w))
    const = lambda shape: pl.BlockSpec(shape, lambda b, s: (0,) * len(shape))
    cch = SSM_WIDTH + 2 * SSM_BC
    return pl.pallas_call(
        _ssd_kernel, grid=(batch, nc),
        in_specs=[blk(SSM_WIDTH, C_SZ), blk(SSM_WIDTH, C_SX), blk(SSM_BC, C_SX + SSM_WIDTH),
                  blk(SSM_BC, C_SX + SSM_WIDTH + SSM_BC), blk(128, C_SMALL),
                  const((CONV_W, cch)), const((1, cch)), const((8, 128)), const((1, SSM_WIDTH)),
                  const((1, SSM_WIDTH)), const((n, n)), const((128, SSM_WIDTH)), const((n, SSM_WIDTH))],
        out_specs=pl.BlockSpec((rows, SSM_WIDTH), lambda b, s: (b * nc + s, 0)),
        out_shape=jax.ShapeDtypeStruct((t, SSM_WIDTH), BF16),
        scratch_shapes=[pltpu.VMEM((n + 8, SSM_WIDTH), F32), pltpu.VMEM((n + 8, SSM_BC), F32),
                        pltpu.VMEM((n + 8, SSM_BC), F32),
                        pltpu.VMEM((SSM_GROUPS, SSM_STATE, SSM_HPG * SSM_HEADDIM), F32)],
        compiler_params=_cparams(("arbitrary", "arbitrary")), name="ssd",
    )(proj, proj, proj, proj, proj, cw, cb, par, dsk, nw, tri, sel, dg)


LANE_E0 = N_GROUPS_MOE
RT_E, RT_RANK, RT_W = 0, 2, 4


def _first_max(vals, lane):
    m = jnp.max(vals, axis=-1, keepdims=True)
    idx = jnp.min(jnp.where(vals == m, lane.astype(F32), 128.0), axis=-1, keepdims=True)
    return m, idx.astype(jnp.int32)


def _outproj_kernel(ya_ref, yb_ref, yc_ref, h_ref, w_ref, nw_ref, rh_ref, rl_ref, rb_ref, tri_ref,
                    hm_ref, u_ref, rt_ref, cnt_ref):
    a0, a1 = HG_WIDTH, HG_WIDTH + ML_WIDTH
    h = h_ref[...]
    h = h + _dot(ya_ref[...], w_ref[0:a0, :])
    h = h + _dot(yb_ref[...], w_ref[a0:a1, :])
    h = h + _dot(yc_ref[...], w_ref[a1:, :])
    hm_ref[...] = h
    ms = jnp.mean(h * h, axis=-1, keepdims=True)
    u = h * lax.rsqrt(ms + EPS) * nw_ref[...]
    _pack_slab(u_ref, u)
    u_hi = u.astype(BF16)
    u_lo = (u - u_hi.astype(F32)).astype(BF16)
    lg = _dot(u_hi, rh_ref[...]) + (_dot(u_lo, rh_ref[...]) + _dot(u_hi, rl_ref[...])) + rb_ref[...]

    tm = lg.shape[0]
    lane = lax.broadcasted_iota(jnp.int32, (tm, 128), 1)
    g_mask = lane < N_GROUPS_MOE
    g_max, g_sel = _first_max(jnp.where(g_mask, lg, NEG), lane)
    g_gate = 1.0 / jnp.sum(jnp.where(g_mask, jnp.exp(lg - g_max), 0.0), axis=-1, keepdims=True)
    lo = LANE_E0 + g_sel * EXPERTS_PER_GROUP
    e_vals = jnp.where((lane >= lo) & (lane < lo + EXPERTS_PER_GROUP), lg, NEG)
    v1, i1 = _first_max(e_vals, lane)
    v2, i2 = _first_max(jnp.where(lane == i1, NEG, e_vals), lane)
    a = jnp.exp(v2 - v1)
    w1 = g_gate / (1.0 + a)
    w2 = w1 * a
    @pl.when(pl.program_id(0) == 0)
    def _():
        cnt_ref[...] = jnp.zeros_like(cnt_ref)

    hit1 = lane == i1
    hit2 = lane == i2
    onehot = jnp.where(hit1 | hit2, 1.0, 0.0)
    before = _dot(tri_ref[...], onehot.astype(BF16)) + cnt_ref[0:1, :]
    r1 = jnp.sum(jnp.where(hit1, before, 0.0), axis=-1, keepdims=True)
    r2 = jnp.sum(jnp.where(hit2, before, 0.0), axis=-1, keepdims=True)
    cnt_ref[...] = cnt_ref[...] + jnp.sum(onehot, axis=0, keepdims=True)
    rec = jnp.zeros((tm, 128), F32)
    for ln, val in ((RT_E, (i1 - LANE_E0).astype(F32)), (RT_E + 1, (i2 - LANE_E0).astype(F32)),
                    (RT_RANK, r1), (RT_RANK + 1, r2), (RT_W, w1), (RT_W + 1, w2)):
        rec = jnp.where(lane == ln, val, rec)
    rt_ref[...] = rec


def _outproj_call(ya, yb, yc, h, w, layer, nw, r_hi, r_lo, r_bias, tm_target=352):
    t, d = h.shape
    tm = _row_tile(t, tm_target)
    tri = jnp.asarray(np.tril(np.ones((tm, tm), np.float32), -1), BF16)
    row = lambda i: (i, 0)
    const = lambda shape: pl.BlockSpec(shape, lambda i: (0, 0))
    return pl.pallas_call(
        _outproj_kernel, grid=(t // tm,),
        in_specs=[pl.BlockSpec((tm, HG_WIDTH), row), pl.BlockSpec((tm, ML_WIDTH), row),
                  pl.BlockSpec((tm, SSM_WIDTH), row), pl.BlockSpec((tm, d), row),
                  pl.BlockSpec((None, D_MIX, d), lambda i: (layer, 0, 0)),
                  const((1, d)), const((d, 128)), const((d, 128)), const((1, 128)), const((tm, tm))],
        out_specs=[pl.BlockSpec((tm, d), row), pl.BlockSpec((tm * SLAB, 128), row), pl.BlockSpec((tm, 128), row),
                   const((8, 128))],
        out_shape=[jax.ShapeDtypeStruct((t, d), F32), jax.ShapeDtypeStruct((t * SLAB, 128), U32),
                   jax.ShapeDtypeStruct((t, 128), F32), jax.ShapeDtypeStruct((8, 128), F32)],
        compiler_params=_cparams(("arbitrary",)), name="out_proj_router",
    )(ya, yb, yc, h, w, nw.reshape(1, d), r_hi, r_lo, r_bias, tri)


MOE_RING = 3
WT_FIRST, WT_SLOT, WT_NEXT = 0, 1, 2


TAB_SRC2, TAB_DST_PREV, TAB_DST = 0, 1, 2


def _moe_kernel(be_ref, nu_ref, wt_ref, src01_ref, tab_ref, u_hbm,
                w1_hbm, w3_hbm, w2_hbm, o_hbm, xbuf, ybuf, st1, st3, st2, w1b, w3b, w2b, gsem, ssem, wsem,
                *, layer):
    i = pl.program_id(0)
    bm = MOE_BM
    n_used = nu_ref[0]
    xs = i % MOE_RING
    slot = xs
    other = (i + MOE_RING - 1) % MOE_RING

    def slab(idx):
        return pl.ds(pl.multiple_of(idx * SLAB, SLAB), SLAB)

    def row_in(tok, r, sl):
        return pltpu.make_async_copy(u_hbm.at[slab(tok), :], xbuf.at[sl, slab(r), :], gsem.at[sl])

    def row_out(r, row, sl):
        return pltpu.make_async_copy(ybuf.at[sl, slab(r), :], o_hbm.at[slab(row), :], ssem.at[sl])

    def gather(section, sl):
        def body(g, c):
            for j in range(8):
                r = g * 8 + j
                row_in(src01_ref[0, section * bm + r], r, sl).start(priority=j % 2)
            return c
        lax.fori_loop(0, bm // 8, body, 0)

    def scatter(sl):
        def body(g, c):
            for j in range(8):
                r = g * 8 + j
                row_out(r, tab_ref[0, TAB_DST * bm + r], sl).start(priority=j % 2)
            return c
        lax.fori_loop(0, bm // 8, body, 0)

    def wait_gather(sl):
        pltpu.make_async_copy(u_hbm.at[pl.ds(0, bm * SLAB), :], xbuf.at[sl], gsem.at[sl]).wait()

    def wait_scatter(sl):
        pltpu.make_async_copy(ybuf.at[sl], o_hbm.at[pl.ds(0, bm * SLAB), :], ssem.at[sl]).wait()

    def weight_copies(e, p):
        return [pltpu.make_async_copy(src.at[layer, e], dst.at[p], wsem.at[p])
                for src, dst in ((w1_hbm, st1), (w3_hbm, st3), (w2_hbm, st2))]

    n_real = o_hbm.shape[0] // SLAB - MOE_RING * bm

    @pl.when(i == 0)
    def _():
        for c in weight_copies(be_ref[0], 0):
            c.start()
        gather(0, 0)
        gather(1, 1)
        ybuf[...] = jnp.zeros_like(ybuf)
        for sl in range(2):
            pltpu.make_async_copy(ybuf.at[sl], o_hbm.at[pl.ds((n_real + sl * bm) * SLAB, bm * SLAB), :],
                                  ssem.at[sl]).start()

    @pl.when(i < n_used)
    def _():
        @pl.when(wt_ref[WT_FIRST, i] == 1)
        def _():
            p = wt_ref[WT_SLOT, i]
            nxt = wt_ref[WT_NEXT, i]

            @pl.when(nxt >= 0)
            def _():
                for c in weight_copies(nxt, 1 - p):
                    c.start(priority=1)

            for c in weight_copies(be_ref[i], p):
                c.wait()
            w1b[...] = st1[p].astype(BF16)
            w3b[...] = st3[p].astype(BF16)
            w2b[...] = st2[p].astype(BF16)

        wait_gather(xs)

        xn = (i + 2) % MOE_RING

        def issue(part, parts=4):
            for r in range(part * bm // parts, (part + 1) * bm // parts):
                row_in(tab_ref[0, TAB_SRC2 * bm + r], r, xn).start(priority=0)
                row_out(r, tab_ref[0, TAB_DST_PREV * bm + r], other).start(priority=1)

        f = w1b.shape[1]
        h1 = jnp.zeros((bm, f), F32)
        h3 = jnp.zeros((bm, f), F32)
        for j in range(SLAB):
            lo, hi = _unpack_slab(xbuf, j, bm, lead=(xs,))
            xk = jnp.concatenate([lo, hi], axis=1).astype(BF16)
            h1 = h1 + _dot(xk, w1b[j * 256:(j + 1) * 256, :])
            h3 = h3 + _dot(xk, w3b[j * 256:(j + 1) * 256, :])
            if j % 4 == 3:
                issue(j // 4)
        hid = (_silu(h1) * h3).astype(BF16)
        wait_scatter(slot)
        half = SLAB // 2
        _pack_slab(ybuf, _dot(hid, w2b[:, :half * 256]), lead=(slot,))
        issue(2)
        _pack_slab(ybuf, _dot(hid, w2b[:, half * 256:]), lead=(slot,), j0=half)
        issue(3)

    @pl.when(i == n_used - 1)
    def _():
        scatter(slot)
        for k in range(MOE_RING):
            wait_scatter(k)
        wait_gather((i + 1) % MOE_RING)
        wait_gather((i + 2) % MOE_RING)


def _moe_call(block_e, n_used, w_table, src_tok, dst_row, u, w1, w3, w2, layer, n_out_rows):
    d, f = w1.shape[-2:]
    assert d == SLAB * 256
    bm = MOE_BM
    nb = block_e.shape[0]
    src_tok = src_tok.reshape(nb, bm)
    dst_row = dst_row.reshape(nb, bm)
    lead = (n_out_rows - bm + jnp.arange(bm, dtype=jnp.int32)).reshape(1, bm)
    src_ahead = jnp.concatenate([src_tok[2:], src_tok[-1:], src_tok[-1:]], axis=0)
    dst_prev = jnp.concatenate([lead, dst_row[:-1]], axis=0)
    table = jnp.concatenate([src_ahead, dst_prev, dst_row], axis=1).reshape(nb, 1, 3 * bm)
    src01 = src_tok[:2].reshape(1, 1, 2 * bm)
    idx = lambda w, fn: pl.BlockSpec((None, 1, w), fn, memory_space=pltpu.SMEM)
    hbm = pl.BlockSpec(memory_space=pl.ANY)
    ring = pltpu.VMEM((MOE_RING, bm * SLAB, 128), U32)
    return pl.pallas_call(
        functools.partial(_moe_kernel, layer=layer),
        grid_spec=pltpu.PrefetchScalarGridSpec(
            num_scalar_prefetch=3, grid=(nb,),
            in_specs=[idx(2 * bm, lambda i, *_: (0, 0, 0)), idx(3 * bm, lambda i, *_: (i, 0, 0)),
                      hbm, hbm, hbm, hbm],
            out_specs=hbm,
            scratch_shapes=[ring, ring,
                            pltpu.VMEM((2, d, f), F32), pltpu.VMEM((2, d, f), F32), pltpu.VMEM((2, f, d), F32),
                            pltpu.VMEM((d, f), BF16), pltpu.VMEM((d, f), BF16), pltpu.VMEM((f, d), BF16),
                            pltpu.SemaphoreType.DMA((MOE_RING,)), pltpu.SemaphoreType.DMA((MOE_RING,)),
                            pltpu.SemaphoreType.DMA((2,))]),
        out_shape=jax.ShapeDtypeStruct((n_out_rows * SLAB, 128), U32),
        compiler_params=_cparams(("arbitrary",)), name="moe_ffn",
    )(block_e, n_used, w_table, src01, table, u, w1, w3, w2)


def _route_tables(rt, cnt, t):
    bm = MOE_BM
    tk = t * TOP_K
    e = rt[:, RT_E:RT_E + TOP_K].astype(jnp.int32)
    rank = rt[:, RT_RANK:RT_RANK + TOP_K].astype(jnp.int32)
    wts = rt[:, RT_W:RT_W + TOP_K]
    counts = cnt[0, LANE_E0:LANE_E0 + N_EXPERTS].astype(jnp.int32)
    padded = (counts + bm - 1) // bm * bm
    pad_ends = jnp.cumsum(padded)
    pad_starts = pad_ends - padded
    e_ids = jnp.arange(N_EXPERTS, dtype=jnp.int32)

    def lookup(table, idx):
        return jnp.sum(jnp.where(idx[..., None] == e_ids, table, 0), axis=-1)

    dest = (lookup(pad_starts, e) + rank).reshape(-1)
    n_blocks = -(-tk // bm) + N_EXPERTS
    n_rows = n_blocks * bm
    inv = jnp.full((n_rows,), -1, jnp.int32).at[dest].set(jnp.arange(tk, dtype=jnp.int32))
    tok, slot = inv // TOP_K, inv % TOP_K
    src_tok = jnp.where(inv >= 0, tok, 0)
    dst_row = jnp.where(inv >= 0, slot * t + tok, tk + (jnp.arange(n_rows, dtype=jnp.int32) % (MOE_RING * bm)))
    block_ids = jnp.arange(n_blocks, dtype=jnp.int32)
    block_e = jnp.minimum(jnp.sum(pad_ends[None, :] <= block_ids[:, None] * bm, axis=1), N_EXPERTS - 1).astype(jnp.int32)
    n_used = (pad_ends[-1] // bm).astype(jnp.int32).reshape(1)
    has_rows = counts > 0
    at_or_after = lax.cummin(jnp.where(has_rows, e_ids, N_EXPERTS)[::-1])[::-1]
    next_e = jnp.concatenate([at_or_after[1:], jnp.full((1,), N_EXPERTS, jnp.int32)])
    next_e = jnp.where(next_e >= N_EXPERTS, -1, next_e)
    stage = (jnp.cumsum(has_rows.astype(jnp.int32)) - 1) % 2
    first = jnp.concatenate([jnp.ones((1,), bool), block_e[1:] != block_e[:-1]]) & (block_ids < n_used[0])
    w_table = jnp.stack([first.astype(jnp.int32), lookup(stage, block_e), lookup(next_e, block_e)]).astype(jnp.int32)
    return block_e, n_used, w_table, src_tok, dst_row, wts, tk + MOE_RING * bm


def _lane_row(pairs):
    row = jnp.zeros((128,), F32)
    for lane0, vals in pairs:
        row = row.at[lane0:lane0 + vals.shape[0]].set(vals.astype(F32))
    return row


def kernel(x, meta_tokens, hg_lb_logits, norm_mix_w, w_in, hg_norm_w, ml_b_i, ml_b_f, ml_norm_w,
           ssm_conv_w, ssm_conv_b, ssm_dt_bias, ssm_a_log, ssm_d, ssm_norm_w, w_out, norm_ffn_w,
           moe_w_group, moe_b_group, moe_w_router, moe_b_router, moe_w1, moe_w3, moe_w2, final_norm_w):
    batch, seq, d = x.shape
    depth = w_in.shape[0]
    p = LEAD_PAD + N_META + seq
    t = batch * p
    meta = jnp.broadcast_to(meta_tokens.astype(x.dtype)[None], (batch, N_META, d))
    h = jnp.concatenate([jnp.zeros((batch, LEAD_PAD, d), x.dtype), meta, x], axis=1).reshape(t, d)

    lb_w = jax.nn.softmax(hg_lb_logits.astype(F32), axis=0)
    lower_bounds = jnp.cumsum(lb_w, axis=0) - lb_w[0]

    o_mq, o_mv, o_mi, o_sz = HG_KEY * 2 + HG_WIDTH * 2, 2560, 3584, 3592
    o_sx = o_sz + SSM_WIDTH
    o_dt = o_sx + SSM_WIDTH + 2 * SSM_BC
    w_perm = jnp.concatenate([
        w_in[..., :o_mq], w_in[..., o_sz:o_sx], w_in[..., o_sx:o_dt], w_in[..., o_mv:o_mi], w_in[..., o_mq:o_mv],
        w_in[..., o_mi:o_sz], w_in[..., o_dt:],
        jnp.zeros((depth, d, N_PROJ - C_SMALL - 2 * ML_HEADS - SSM_HEADS), w_in.dtype)], axis=-1).astype(BF16)
    w_out_b = w_out.astype(BF16)

    contrib = wts = None
    for layer in range(depth):
        lb = lower_bounds[layer]
        lbf = jnp.maximum(lb, LB_FLOOR)
        hg_par = jnp.zeros((8, HG_KEY), F32).at[0].set(lbf).at[1].set(1.0 - lb).at[2].set(lbf - lb).at[3].set(hg_norm_w[layer])
        ml_par = jnp.zeros((8, 128), F32).at[0].set(_lane_row([(LANE_MI, ml_b_i[layer]), (LANE_MF, ml_b_f[layer])]))
        ss_par = jnp.zeros((8, 128), F32).at[0].set(_lane_row([(LANE_DT, ssm_dt_bias[layer])]))
        ss_par = ss_par.at[1].set(_lane_row([(LANE_DT, ssm_a_log[layer])]))
        dskip = jnp.repeat(ssm_d[layer].astype(F32), SSM_HEADDIM).reshape(1, SSM_WIDTH)
        w_r = jnp.concatenate([moe_w_group[layer],
                               moe_w_router[layer].transpose(1, 0, 2).reshape(d, N_EXPERTS),
                               jnp.zeros((d, 128 - N_GROUPS_MOE - N_EXPERTS), F32)], axis=1)
        r_hi = w_r.astype(BF16)
        r_lo = (w_r - r_hi.astype(F32)).astype(BF16)

        if layer == 0:
            (u,) = _norm_call(h, norm_mix_w[layer], write_h=False, u_dtype=BF16)
        else:
            h, u = _norm_call(h, norm_mix_w[layer], contrib, wts, write_h=True, u_dtype=BF16)
        proj = _inproj_call(u, w_perm, layer)
        ya = _hgrn2_call(proj, hg_par, batch, p)
        yb = _mlstm_call(proj, ml_par, ml_norm_w[layer].reshape(1, ML_WIDTH), batch, p)
        yc = _ssd_call(proj, ssm_conv_w[layer], ssm_conv_b[layer].reshape(1, -1), ss_par, dskip,
                       ssm_norm_w[layer].reshape(1, SSM_WIDTH), batch, p)
        r_bias = _lane_row([(0, moe_b_group[layer]), (LANE_E0, moe_b_router[layer].reshape(-1))]).reshape(1, 128)
        h, u_ffn, rt, cnt = _outproj_call(ya, yb, yc, h, w_out_b, layer, norm_ffn_w[layer], r_hi, r_lo, r_bias)
        block_e, n_used, w_table, src_tok, dst_row, wts, n_out_rows = _route_tables(rt, cnt, t)
        contrib = _moe_call(block_e, n_used, w_table, src_tok, dst_row, u_ffn, moe_w1, moe_w3, moe_w2, layer,
                            n_out_rows)
    return _final_call(h, contrib, wts, final_norm_w, batch, p)
```

```python
import functools

import jax
import jax.numpy as jnp
import numpy as np
from jax import lax
from jax.experimental import pallas as pl
from jax.experimental.pallas import tpu as pltpu

F32 = jnp.float32
BF16 = jnp.bfloat16

D_MODEL = 2048
N_META = 16
CHUNK = 64
HG_CHUNK = 16
LEAD_PAD = CHUNK - N_META
EPS = 1e-6
NEG = -1e30
LB_FLOOR = 1e-30

HG_HEADS = 4
HG_KDIM = 128
HG_KEY = HG_HEADS * HG_KDIM
HG_WIDTH = HG_HEADS * 128

ML_HEADS = 4
ML_QK = 64
ML_V = 128
ML_QK_W = ML_HEADS * ML_QK
ML_WIDTH = ML_HEADS * ML_V
GATE_CAP = 15.0

SSM_HEADS = 16
SSM_HEADDIM = 64
SSM_WIDTH = SSM_HEADS * SSM_HEADDIM
SSM_STATE = 128
SSM_GROUPS = 4
SSM_HPG = SSM_HEADS // SSM_GROUPS
SSM_BC = SSM_GROUPS * SSM_STATE
CONV_W = 4

D_MIX = HG_WIDTH + ML_WIDTH + SSM_WIDTH

N_GROUPS_MOE = 4
EXPERTS_PER_GROUP = 8
N_EXPERTS = N_GROUPS_MOE * EXPERTS_PER_GROUP
TOP_K = 2
D_EXPERT = 512
MOE_BM = 128

C_HG = 0
C_SZ = 2048
C_SX = 3072
C_MV = 5120
C_MQ = 6144
C_SMALL = 6656
N_PROJ = 6912
LANE_MI = 0
LANE_MF = ML_HEADS
LANE_DT = 2 * ML_HEADS

VMEM_LIMIT = 56 * 1024 * 1024

ROW_TILES = {
    "norm": 264,
    "final": 256,
    "in_proj": 2112,
    "hgrn2": 528,
    "ssd": 704,
    "out_proj": 352,
}
IN_PROJ_COLS = 768


def _cparams(sem):
    return pltpu.CompilerParams(dimension_semantics=sem, vmem_limit_bytes=VMEM_LIMIT)


def _row_tile(n, target, mult=16):
    best = None
    for t in range(mult, min(n, target) + 1, mult):
        if n % t == 0:
            best = t
    assert best is not None, (n, target, mult)
    return best


def _split3(x):
    hi = x.astype(BF16)
    r = x - hi.astype(F32)
    mid = r.astype(BF16)
    lo = (r - mid.astype(F32)).astype(BF16)
    return hi, mid, lo


def _dot(a, b):
    return jnp.dot(a, b, preferred_element_type=F32)


def _sel_dot(sel, x):
    hi, mid, lo = _split3(x)
    return _dot(sel, hi) + _dot(sel, mid) + _dot(sel, lo)


def _dot_sel(x, sel):
    hi, mid, lo = _split3(x)
    return _dot(hi, sel) + _dot(mid, sel) + _dot(lo, sel)


def _dot_nt(a, b):
    return lax.dot_general(a, b, (((1,), (1,)), ((), ())), preferred_element_type=F32)


def _dot_tn(a, b):
    return lax.dot_general(a, b, (((0,), (0,)), ((), ())), preferred_element_type=F32)


def _log_sigmoid(x):
    return jnp.minimum(x, 0.0) - jnp.log1p(jnp.exp(-jnp.abs(x)))


def _sigmoid(x):
    return 1.0 / (1.0 + jnp.exp(-x))


def _silu(x):
    return x * _sigmoid(x)


SLAB = 8
U32 = jnp.uint32
HI_MASK = 0xFFFF0000


def _bf16_bits(x):
    return lax.bitcast_convert_type(x.astype(BF16).astype(F32), U32)


def _pack_slab(ref, val, lead=(), j0=0):
    rows = val.shape[0]
    for jj in range(val.shape[1] // 256):
        lo = _bf16_bits(val[:, jj * 256:jj * 256 + 128])
        hi = _bf16_bits(val[:, jj * 256 + 128:(jj + 1) * 256])
        ref[(*lead, pl.ds(j0 + jj, rows, stride=SLAB), slice(None))] = (lo >> 16) | (hi & U32(HI_MASK))


def _unpack_slab(ref, j, rows, lead=()):
    w = ref[(*lead, pl.ds(j, rows, stride=SLAB), slice(None))]
    return (lax.bitcast_convert_type(w << 16, F32), lax.bitcast_convert_type(w & U32(HI_MASK), F32))


def _combine_rows(h_ref, c0_ref, c1_ref, wt_ref):
    rows = h_ref.shape[0]
    wt = wt_ref[...]
    w0, w1 = wt[:, 0:1], wt[:, 1:2]
    pieces = []
    for j in range(SLAB):
        a0, b0 = _unpack_slab(c0_ref, j, rows)
        a1, b1 = _unpack_slab(c1_ref, j, rows)
        pieces += [w0 * a0 + w1 * a1, w0 * b0 + w1 * b1]
    return h_ref[...] + jnp.concatenate(pieces, axis=1)


def _norm_kernel(*refs, combine, write_h):
    if combine:
        h_ref, c0_ref, c1_ref, wt_ref, nw_ref = refs[:5]
        outs = refs[5:]
        h = _combine_rows(h_ref, c0_ref, c1_ref, wt_ref)
    else:
        h_ref, nw_ref = refs[:2]
        outs = refs[2:]
        h = h_ref[...]
    if write_h:
        outs[0][...] = h
    ms = jnp.mean(h * h, axis=-1, keepdims=True)
    u_ref = outs[-1]
    u_ref[...] = (h * lax.rsqrt(ms + EPS) * nw_ref[...]).astype(u_ref.dtype)


def _norm_call(h, nw, contrib=None, wts=None, *, write_h, u_dtype):
    t, d = h.shape
    tm = _row_tile(t, ROW_TILES["norm"])
    combine = contrib is not None
    row = lambda i: (i, 0)
    in_specs = [pl.BlockSpec((tm, d), row)]
    args = [h]
    if combine:
        in_specs += [pl.BlockSpec((tm * SLAB, 128), row), pl.BlockSpec((tm * SLAB, 128), lambda i: (t // tm + i, 0)),
                     pl.BlockSpec((tm, 2), row)]
        args += [contrib, contrib, wts]
    in_specs.append(pl.BlockSpec((1, d), lambda i: (0, 0)))
    args.append(nw.reshape(1, d))
    out_shape, out_specs = [], []
    if write_h:
        out_shape.append(jax.ShapeDtypeStruct((t, d), F32))
        out_specs.append(pl.BlockSpec((tm, d), row))
    out_shape.append(jax.ShapeDtypeStruct((t, d), u_dtype))
    out_specs.append(pl.BlockSpec((tm, d), row))
    return pl.pallas_call(
        functools.partial(_norm_kernel, combine=combine, write_h=write_h),
        grid=(t // tm,), in_specs=in_specs, out_specs=out_specs, out_shape=out_shape,
        compiler_params=_cparams(("arbitrary",)), name="combine_norm",
    )(*args)


def _final_kernel(h_ref, c0_ref, c1_ref, wt_ref, nw_ref, o_ref):
    h = _combine_rows(h_ref, c0_ref, c1_ref, wt_ref)
    ms = jnp.mean(h * h, axis=-1, keepdims=True)
    o_ref[...] = h * lax.rsqrt(ms + EPS) * nw_ref[...]


def _final_call(h, contrib, wts, nw, batch, p):
    t, d = h.shape
    seq = p - CHUNK
    tm = _row_tile(seq, ROW_TILES["final"])
    n_out = seq // tm
    row0 = lambda b, i: b * p + CHUNK + i * tm
    src = lambda b, i: (pl.multiple_of(row0(b, i), CHUNK), 0)
    slab0 = lambda b, i: (pl.multiple_of(row0(b, i) * SLAB, CHUNK), 0)
    slab1 = lambda b, i: (pl.multiple_of((t + row0(b, i)) * SLAB, CHUNK), 0)
    win = lambda r, w, fn: pl.BlockSpec((pl.Element(r), pl.Element(w)), fn)
    return pl.pallas_call(
        _final_kernel, grid=(batch, n_out),
        in_specs=[win(tm, d, src), win(tm * SLAB, 128, slab0), win(tm * SLAB, 128, slab1), win(tm, 2, src),
                  pl.BlockSpec((1, d), lambda b, i: (0, 0))],
        out_specs=pl.BlockSpec((None, tm, d), lambda b, i: (b, i, 0)),
        out_shape=jax.ShapeDtypeStruct((batch, seq, d), F32),
        compiler_params=_cparams(("arbitrary", "arbitrary")), name="final_norm",
    )(h, contrib, contrib, wts, nw.reshape(1, d))


def _matmul_kernel(x_ref, w_ref, o_ref):
    o_ref[...] = _dot(x_ref[...], w_ref[...])


def _inproj_call(u, w, layer):
    t, d = u.shape
    n = w.shape[-1]
    tm = _row_tile(t, ROW_TILES["in_proj"])
    tn = IN_PROJ_COLS
    assert n % tn == 0
    return pl.pallas_call(
        _matmul_kernel, grid=(t // tm, n // tn),
        in_specs=[pl.BlockSpec((tm, d), lambda i, j: (i, 0)),
                  pl.BlockSpec((None, d, tn), lambda i, j: (layer, 0, j))],
        out_specs=pl.BlockSpec((tm, tn), lambda i, j: (i, j)),
        out_shape=jax.ShapeDtypeStruct((t, n), F32),
        compiler_params=_cparams(("arbitrary", "arbitrary")), name="in_proj",
    )(u, w)


def _hgrn2_kernel(q_ref, f_ref, i_ref, g_ref, par_ref, o_ref, st_ref, *, rows):
    s = pl.program_id(1)

    @pl.when(s == 0)
    def _():
        st_ref[...] = jnp.zeros_like(st_ref)

    c = HG_CHUNK
    ones = jnp.ones((HG_KDIM, 128), BF16)
    rid = lax.broadcasted_iota(jnp.int32, (c, 128), 0)
    scale = HG_KDIM ** -0.5

    hc = c // 2

    def chunk(ci, carry):
        r0 = pl.multiple_of(ci * c, c)
        pad = (s * rows + r0 + rid) < LEAD_PAD

        def front(h):
            cols = slice(h * 128, (h + 1) * 128)
            a_lb = par_ref[0:1, cols]
            b_lb = par_ref[1:2, cols]
            c_lb = par_ref[2:3, cols]
            z = f_ref[pl.ds(r0, c), cols]
            sg = _sigmoid(z)
            f = a_lb + b_lb * sg
            log_f = jnp.where(pad, 0.0, jnp.log(f))
            k = jnp.where(pad, 0.0, b_lb * (1.0 - sg) - c_lb)
            q = q_ref[pl.ds(r0, c), cols] * scale
            v = i_ref[pl.ds(r0, c), cols]
            cum = log_f
            for sh in (1, 2, 4, 8):
                cum = cum + jnp.where(rid >= sh, pltpu.roll(cum, sh, axis=0), 0.0)
            parts = []
            for s_ in range(c):
                lo = 0 if s_ < hc else hc
                rel = jnp.where(rid[lo:] >= s_, cum[lo:] - cum[s_:s_ + 1, :], NEG)
                parts.append(q[lo:] * (k[s_:s_ + 1, :] * jnp.exp(rel)))
            sc = _dot(jnp.concatenate(parts, axis=0).astype(BF16), ones)
            st = st_ref[h]
            o_inter = _dot_nt((q * jnp.exp(cum)).astype(BF16), st.astype(BF16))
            last = cum[c - 1:c, :]
            kd = (k * jnp.exp(last - cum)).astype(BF16)
            st_ref[h] = st * jnp.exp(last) + _dot_tn(v.astype(BF16), kd)
            return sc, o_inter, v

        def back(h, sc, o_inter, v):
            cols = slice(h * 128, (h + 1) * 128)
            o_top = o_inter[:hc]
            o_bot = o_inter[hc:]
            for s_ in range(hc):
                o_top = o_top + sc[s_ * c:s_ * c + hc, :] * v[s_:s_ + 1, :]
                o_bot = o_bot + sc[s_ * c + hc:(s_ + 1) * c, :] * v[s_:s_ + 1, :]
            for s_ in range(hc, c):
                r_ = hc * c + (s_ - hc) * hc
                o_bot = o_bot + sc[r_:r_ + hc, :] * v[s_:s_ + 1, :]
            o = jnp.concatenate([o_top, o_bot], axis=0)
            ms = jnp.mean(o * o, axis=-1, keepdims=True)
            g = g_ref[pl.ds(r0, c), cols]
            o_ref[pl.ds(r0, c), cols] = (o * lax.rsqrt(ms + EPS) * par_ref[3:4, cols] * _silu(g)).astype(o_ref.dtype)

        pending = front(0)
        for h in range(1, HG_HEADS):
            nxt = front(h)
            back(h - 1, *pending)
            pending = nxt
        back(HG_HEADS - 1, *pending)
        return carry

    n_chunks = rows // c
    lax.fori_loop(0, n_chunks, chunk, 0, unroll=3 if n_chunks % 3 == 0 else 1)


def _hgrn2_call(proj, par, batch, p):
    t = proj.shape[0]
    rows = _row_tile(p, ROW_TILES["hgrn2"])
    nb = p // rows
    w = HG_KEY
    blk = lambda j: pl.BlockSpec((rows, w), lambda b, s, j=j: (b * nb + s, C_HG // w + j))
    return pl.pallas_call(
        functools.partial(_hgrn2_kernel, rows=rows), grid=(batch, nb),
        in_specs=[blk(0), blk(1), blk(2), blk(3), pl.BlockSpec((8, w), lambda b, s: (0, 0))],
        out_specs=pl.BlockSpec((rows, w), lambda b, s: (b * nb + s, 0)),
        out_shape=jax.ShapeDtypeStruct((t, HG_WIDTH), BF16),
        scratch_shapes=[pltpu.VMEM((HG_HEADS, 128, HG_KDIM), F32)],
        compiler_params=_cparams(("arbitrary", "arbitrary")), name="hgrn2",
    )(proj, proj, proj, proj, par)


def _mlstm_kernel(v_ref, o_ref, q_ref, k_ref, sm_ref, par_ref, nw_ref, tri_ref, sel_ref, dg_ref,
                  y_ref, c_ref, m_ref):
    s = pl.program_id(0)

    @pl.when(s == 0)
    def _():
        c_ref[...] = jnp.zeros_like(c_ref)
        m_ref[...] = jnp.zeros_like(m_ref)

    n = CHUNK
    rid = lax.broadcasted_iota(jnp.int32, (n, 128), 0)
    pad = (s * n + rid) < LEAD_PAD
    sel_i = sel_ref[0]
    sel_f = sel_ref[1]
    dg = dg_ref[...]
    w = ML_HEADS * n
    tt = lax.broadcasted_iota(jnp.int32, (n, w), 0)
    ss = lax.broadcasted_iota(jnp.int32, (n, w), 1) & (n - 1)
    causal = tt >= ss
    scale = ML_QK ** -0.5
    lane = lax.broadcasted_iota(jnp.int32, (n, 128), 1)
    one_col = jnp.where(lane == 0, 1.0, 0.0).astype(BF16)
    for b in range(v_ref.shape[0]):
        pre = sm_ref[b] + par_ref[0:1, :]
        cap = GATE_CAP * jnp.tanh(pre * (1.0 / GATE_CAP))
        log_i = jnp.where(pad, NEG, cap)
        log_f = jnp.where(pad, 0.0, _log_sigmoid(cap))
        cum = _sel_dot(tri_ref[...], log_f)
        cum_col = _dot_sel(cum, sel_f)
        cum_row = jnp.sum(cum_col * dg, axis=0, keepdims=True)
        li_row = jnp.sum(_dot_sel(log_i, sel_i) * dg, axis=0, keepdims=True)
        dmat = jnp.where(causal, cum_col - cum_row + li_row, NEG)
        hs = range(ML_HEADS)
        sh = [b * ML_HEADS + h for h in hs]
        d_h = [dmat[:, h * n:(h + 1) * n] for h in hs]
        cum_h = [cum[:, LANE_MF + h:LANE_MF + h + 1] for h in hs]
        li_h = [log_i[:, LANE_MI + h:LANE_MI + h + 1] for h in hs]
        m_st = [m_ref[sh[h]:sh[h] + 1, 0:1] for h in hs]
        q = [(q_ref[b, :, h * ML_QK:(h + 1) * ML_QK] * scale).astype(BF16) for h in hs]
        k = [k_ref[b, :, h * ML_QK:(h + 1) * ML_QK] for h in hs]
        v_aug = [jnp.concatenate([v_ref[b, :, h * ML_V:(h + 1) * ML_V].astype(BF16), one_col], axis=1) for h in hs]
        c_prev = [c_ref[sh[h]] for h in hs]
        inter = [cum_h[h] + m_st[h] for h in hs]
        m_t = [jnp.maximum(inter[h], jnp.max(d_h[h], axis=-1, keepdims=True)) for h in hs]
        qk = [_dot_nt(q[h], k[h].astype(BF16)) for h in hs]
        qc = [_dot(q[h], c_prev[h].astype(BF16)) for h in hs]
        pw = [(qk[h] * jnp.exp(d_h[h] - m_t[h])).astype(BF16) for h in hs]
        nd = [_dot(pw[h], v_aug[h]) + jnp.exp(inter[h] - m_t[h]) * qc[h] for h in hs]
        hh = [nd[h][:, :ML_V] / jnp.maximum(jnp.abs(nd[h][:, ML_V:ML_V + 1]), jnp.exp(-m_t[h])) for h in hs]
        tot = [cum_h[h][n - 1:n, :] for h in hs]
        to_end = [tot[h] - cum_h[h] + li_h[h] for h in hs]
        m_loc = [jnp.max(to_end[h], axis=0, keepdims=True) for h in hs]
        kw = [(k[h] * jnp.exp(to_end[h] - m_loc[h])).astype(BF16) for h in hs]
        c_loc = [_dot_tn(kw[h], v_aug[h]) for h in hs]
        m_new = [jnp.maximum(tot[h] + m_st[h], m_loc[h]) for h in hs]
        for h in hs:
            c_ref[sh[h]] = (jnp.exp(tot[h] + m_st[h] - m_new[h]) * c_prev[h]
                            + jnp.exp(m_loc[h] - m_new[h]) * c_loc[h])
            m_ref[sh[h]:sh[h] + 1, :] = jnp.broadcast_to(m_new[h], (1, 128))
        for h in hs:
            ms = jnp.mean(hh[h] * hh[h], axis=-1, keepdims=True)
            cols = slice(h * ML_V, (h + 1) * ML_V)
            y_ref[b, :, cols] = (hh[h] * lax.rsqrt(ms + EPS) * nw_ref[:, cols]
                                 * _sigmoid(o_ref[b, :, cols])).astype(y_ref.dtype)


def _lane_select(lane0, heads, width):
    m = np.zeros((128, heads * width), np.float32)
    for h in range(heads):
        m[lane0 + h, h * width:(h + 1) * width] = 1.0
    return m


def _diag_mask(n, heads):
    return np.tile(np.eye(n, dtype=np.float32), (1, heads))


def _mlstm_call(proj, par, nw, batch, p):
    t = proj.shape[0]
    n = CHUNK
    nc = p // n
    tri = jnp.asarray(np.tril(np.ones((n, n), np.float32)), BF16)
    sel = jnp.asarray(np.stack([_lane_select(LANE_MI, ML_HEADS, n), _lane_select(LANE_MF, ML_HEADS, n)]), BF16)
    dg = jnp.asarray(_diag_mask(n, ML_HEADS), F32)
    proj3 = proj.reshape(batch, p, proj.shape[1])
    blk = lambda w, off: pl.BlockSpec((batch, n, w), lambda s: (0, s, off // w))
    const = lambda shape: pl.BlockSpec(shape, lambda s: (0,) * len(shape))
    m_rows = -(-batch * ML_HEADS // 8) * 8
    y = pl.pallas_call(
        _mlstm_kernel, grid=(nc,),
        in_specs=[blk(ML_WIDTH, C_MV), blk(ML_WIDTH, C_MV + ML_WIDTH), blk(ML_QK_W, C_MQ),
                  blk(ML_QK_W, C_MQ + ML_QK_W), blk(128, C_SMALL),
                  const((8, 128)), const((1, ML_WIDTH)), const((n, n)), const((2, 128, ML_HEADS * n)),
                  const((n, ML_HEADS * n))],
        out_specs=pl.BlockSpec((batch, n, ML_WIDTH), lambda s: (0, s, 0)),
        out_shape=jax.ShapeDtypeStruct((batch, p, ML_WIDTH), BF16),
        scratch_shapes=[pltpu.VMEM((batch * ML_HEADS, ML_QK, 2 * ML_V), F32), pltpu.VMEM((m_rows, 128), F32)],
        compiler_params=_cparams(("arbitrary",)), name="mlstm",
    )(proj3, proj3, proj3, proj3, proj3, par, nw, tri, sel, dg)
    return y.reshape(t, ML_WIDTH)


def _ssd_kernel(z_ref, x_ref, b_ref, c_ref, sm_ref, cw_ref, cb_ref, par_ref, dsk_ref, nw_ref,
                tri_ref, sel_ref, dg_ref, y_ref, xs_ref, bs_ref, cs_ref, st_ref):
    s = pl.program_id(1)
    n = CHUNK
    tail = 8
    chunks = z_ref.shape[0] // n

    @pl.when(s == 0)
    def _():
        st_ref[...] = jnp.zeros_like(st_ref)
        xs_ref[0:tail, :] = jnp.zeros((tail, xs_ref.shape[1]), F32)
        bs_ref[0:tail, :] = jnp.zeros((tail, bs_ref.shape[1]), F32)
        cs_ref[0:tail, :] = jnp.zeros((tail, cs_ref.shape[1]), F32)

    def chunk(ci, carry):
        _ssd_chunk(ci, s * chunks + ci, z_ref, x_ref, b_ref, c_ref, sm_ref, cw_ref, cb_ref, par_ref, dsk_ref,
                   nw_ref, tri_ref, sel_ref, dg_ref, y_ref, xs_ref, bs_ref, cs_ref, st_ref)
        return carry

    lax.fori_loop(0, chunks, chunk, 0)


def _ssd_chunk(ci, chunk_id, z_ref, x_ref, b_ref, c_ref, sm_ref, cw_ref, cb_ref, par_ref, dsk_ref, nw_ref,
               tri_ref, sel_ref, dg_ref, y_ref, xs_ref, bs_ref, cs_ref, st_ref):
    n = CHUNK
    tail = 8
    rows = pl.ds(pl.multiple_of(ci * n, n), n)

    def conv_silu(src_ref, scr_ref, c0, width, rowmask):
        scr_ref[tail:tail + n, :] = src_ref[rows, :]
        acc = cb_ref[:, c0:c0 + width]
        for j in range(CONV_W):
            off = tail - (CONV_W - 1) + j
            acc = acc + cw_ref[j:j + 1, c0:c0 + width] * scr_ref[off:off + n, :]
        scr_ref[0:tail, :] = scr_ref[n:n + tail, :]
        return jnp.where(rowmask, 0.0, _silu(acc))

    def padmask(width):
        return (chunk_id * n + lax.broadcasted_iota(jnp.int32, (n, width), 0)) < LEAD_PAD

    x = conv_silu(x_ref, xs_ref, 0, SSM_WIDTH, padmask(SSM_WIDTH))
    bm = conv_silu(b_ref, bs_ref, SSM_WIDTH, SSM_BC, padmask(SSM_BC))
    cm = conv_silu(c_ref, cs_ref, SSM_WIDTH + SSM_BC, SSM_BC, padmask(SSM_BC))

    pre = sm_ref[rows, :] + par_ref[0:1, :]
    dt = jnp.maximum(pre, 0.0) + jnp.log1p(jnp.exp(-jnp.abs(pre)))
    dt = jnp.where(padmask(128), 0.0, dt)
    da = dt * (-jnp.exp(par_ref[1:2, :]))
    cum = _sel_dot(tri_ref[...], da)
    sel = sel_ref[...]
    dt_col = _dot_sel(dt, sel)
    cum_col = _dot_sel(cum, sel)
    cum_row = jnp.sum(cum_col * dg_ref[...], axis=0, keepdims=True)
    w = SSM_WIDTH
    tt = lax.broadcasted_iota(jnp.int32, (n, w), 0)
    ss = lax.broadcasted_iota(jnp.int32, (n, w), 1) & (n - 1)
    decay = jnp.exp(jnp.where(tt >= ss, cum_col - cum_row, NEG))
    xdt = x * dt_col
    last = cum_col[n - 1:n, :]
    wend = (xdt * jnp.exp(last - cum_col)).astype(BF16)
    chunk_decay = jnp.exp(last)
    ecum = jnp.exp(cum_col)
    gw = SSM_HPG * SSM_HEADDIM
    rr = lax.broadcasted_iota(jnp.int32, (gw, gw), 0) // SSM_HEADDIM
    cc = lax.broadcasted_iota(jnp.int32, (gw, gw), 1) // SSM_HEADDIM
    blockdiag = rr == cc
    ys = []
    for g in range(SSM_GROUPS):
        gl = slice(g * gw, (g + 1) * gw)
        sl = slice(g * SSM_STATE, (g + 1) * SSM_STATE)
        cm_g = cm[:, sl].astype(BF16)
        bm_g = bm[:, sl].astype(BF16)
        cb = _dot_nt(cm_g, jnp.concatenate([bm_g] * SSM_HPG, axis=0))
        m = (cb * decay[:, gl]).astype(BF16)
        xdt_g = xdt[:, gl]
        bd = jnp.where(blockdiag, jnp.concatenate([xdt_g] * SSM_HPG, axis=0), 0.0).astype(BF16)
        st = st_ref[g]
        y_g = _dot(m, bd) + ecum[:, gl] * _dot(cm_g, st.astype(BF16))
        st_ref[g] = st * chunk_decay[:, gl] + _dot_tn(bm_g, wend[:, gl])
        ys.append(y_g)
    y = jnp.concatenate(ys, axis=1) + x * dsk_ref[...]
    y = y * _silu(z_ref[rows, :])
    outs = []
    for g in range(SSM_GROUPS):
        gl = slice(g * gw, (g + 1) * gw)
        y_g = y[:, gl]
        ms = jnp.mean(y_g * y_g, axis=-1, keepdims=True)
        outs.append(y_g * lax.rsqrt(ms + EPS))
    y_ref[rows, :] = (jnp.concatenate(outs, axis=1) * nw_ref[...]).astype(y_ref.dtype)


def _ssd_call(proj, cw, cb, par, dsk, nw, batch, p):
    t = proj.shape[0]
    n = CHUNK
    nc = p // n
    tri = jnp.asarray(np.tril(np.ones((n, n), np.float32)), BF16)
    sel = jnp.asarray(_lane_select(LANE_DT, SSM_HEADS, SSM_HEADDIM), BF16)
    dg = jnp.asarray(_diag_mask(n, SSM_HEADS), F32)
    rows = _row_tile(p, ROW_TILES["ssd"], mult=n)
    nc = p // rows
    blk = lambda w, off: pl.BlockSpec((rows, w), lambda b, s: (b * nc + s, off // w))
    const = lambda shape: pl.BlockSpec(shape, lambda b, s: (0,) * len(shape))
    cch = SSM_WIDTH + 2 * SSM_BC
    return pl.pallas_call(
        _ssd_kernel, grid=(batch, nc),
        in_specs=[blk(SSM_WIDTH, C_SZ), blk(SSM_WIDTH, C_SX), blk(SSM_BC, C_SX + SSM_WIDTH),
                  blk(SSM_BC, C_SX + SSM_WIDTH + SSM_BC), blk(128, C_SMALL),
                  const((CONV_W, cch)), const((1, cch)), const((8, 128)), const((1, SSM_WIDTH)),
                  const((1, SSM_WIDTH)), const((n, n)), const((128, SSM_WIDTH)), const((n, SSM_WIDTH))],
        out_specs=pl.BlockSpec((rows, SSM_WIDTH), lambda b, s: (b * nc + s, 0)),
        out_shape=jax.ShapeDtypeStruct((t, SSM_WIDTH), BF16),
        scratch_shapes=[pltpu.VMEM((n + 8, SSM_WIDTH), F32), pltpu.VMEM((n + 8, SSM_BC), F32),
                        pltpu.VMEM((n + 8, SSM_BC), F32),
                        pltpu.VMEM((SSM_GROUPS, SSM_STATE, SSM_HPG * SSM_HEADDIM), F32)],
        compiler_params=_cparams(("arbitrary", "arbitrary")), name="ssd",
    )(proj, proj, proj, proj, proj, cw, cb, par, dsk, nw, tri, sel, dg)


LANE_E0 = N_GROUPS_MOE
RT_E, RT_RANK, RT_W = 0, 2, 4


def _first_max(vals, lane):
    m = jnp.max(vals, axis=-1, keepdims=True)
    idx = jnp.min(jnp.where(vals == m, lane.astype(F32), 128.0), axis=-1, keepdims=True)
    return m, idx.astype(jnp.int32)


def _outproj_kernel(ya_ref, yb_ref, yc_ref, h_ref, w_ref, nw_ref, rh_ref, rl_ref, rb_ref, tri_ref,
                    hm_ref, u_ref, rt_ref, cnt_ref):
    a0, a1 = HG_WIDTH, HG_WIDTH + ML_WIDTH
    h = h_ref[...]
    h = h + _dot(ya_ref[...], w_ref[0:a0, :])
    h = h + _dot(yb_ref[...], w_ref[a0:a1, :])
    h = h + _dot(yc_ref[...], w_ref[a1:, :])
    hm_ref[...] = h
    ms = jnp.mean(h * h, axis=-1, keepdims=True)
    u = h * lax.rsqrt(ms + EPS) * nw_ref[...]
    _pack_slab(u_ref, u)
    u_hi = u.astype(BF16)
    u_lo = (u - u_hi.astype(F32)).astype(BF16)
    lg = _dot(u_hi, rh_ref[...]) + (_dot(u_lo, rh_ref[...]) + _dot(u_hi, rl_ref[...])) + rb_ref[...]

    tm = lg.shape[0]
    lane = lax.broadcasted_iota(jnp.int32, (tm, 128), 1)
    g_mask = lane < N_GROUPS_MOE
    g_max, g_sel = _first_max(jnp.where(g_mask, lg, NEG), lane)
    g_gate = 1.0 / jnp.sum(jnp.where(g_mask, jnp.exp(lg - g_max), 0.0), axis=-1, keepdims=True)
    lo = LANE_E0 + g_sel * EXPERTS_PER_GROUP
    e_vals = jnp.where((lane >= lo) & (lane < lo + EXPERTS_PER_GROUP), lg, NEG)
    v1, i1 = _first_max(e_vals, lane)
    v2, i2 = _first_max(jnp.where(lane == i1, NEG, e_vals), lane)
    a = jnp.exp(v2 - v1)
    w1 = g_gate / (1.0 + a)
    w2 = w1 * a
    @pl.when(pl.program_id(0) == 0)
    def _():
        cnt_ref[...] = jnp.zeros_like(cnt_ref)

    hit1 = lane == i1
    hit2 = lane == i2
    onehot = jnp.where(hit1 | hit2, 1.0, 0.0)
    before = _dot(tri_ref[...], onehot.astype(BF16)) + cnt_ref[0:1, :]
    r1 = jnp.sum(jnp.where(hit1, before, 0.0), axis=-1, keepdims=True)
    r2 = jnp.sum(jnp.where(hit2, before, 0.0), axis=-1, keepdims=True)
    cnt_ref[...] = cnt_ref[...] + jnp.sum(onehot, axis=0, keepdims=True)
    rec = jnp.zeros((tm, 128), F32)
    for ln, val in ((RT_E, (i1 - LANE_E0).astype(F32)), (RT_E + 1, (i2 - LANE_E0).astype(F32)),
                    (RT_RANK, r1), (RT_RANK + 1, r2), (RT_W, w1), (RT_W + 1, w2)):
        rec = jnp.where(lane == ln, val, rec)
    rt_ref[...] = rec


def _outproj_call(ya, yb, yc, h, w, layer, nw, r_hi, r_lo, r_bias):
    t, d = h.shape
    tm = _row_tile(t, ROW_TILES["out_proj"])
    tri = jnp.asarray(np.tril(np.ones((tm, tm), np.float32), -1), BF16)
    row = lambda i: (i, 0)
    const = lambda shape: pl.BlockSpec(shape, lambda i: (0, 0))
    return pl.pallas_call(
        _outproj_kernel, grid=(t // tm,),
        in_specs=[pl.BlockSpec((tm, HG_WIDTH), row), pl.BlockSpec((tm, ML_WIDTH), row),
                  pl.BlockSpec((tm, SSM_WIDTH), row), pl.BlockSpec((tm, d), row),
                  pl.BlockSpec((None, D_MIX, d), lambda i: (layer, 0, 0)),
                  const((1, d)), const((d, 128)), const((d, 128)), const((1, 128)), const((tm, tm))],
        out_specs=[pl.BlockSpec((tm, d), row), pl.BlockSpec((tm * SLAB, 128), row), pl.BlockSpec((tm, 128), row),
                   const((8, 128))],
        out_shape=[jax.ShapeDtypeStruct((t, d), F32), jax.ShapeDtypeStruct((t * SLAB, 128), U32),
                   jax.ShapeDtypeStruct((t, 128), F32), jax.ShapeDtypeStruct((8, 128), F32)],
        compiler_params=_cparams(("arbitrary",)), name="out_proj_router",
    )(ya, yb, yc, h, w, nw.reshape(1, d), r_hi, r_lo, r_bias, tri)


MOE_RING = 3
WT_FIRST, WT_SLOT, WT_NEXT = 0, 1, 2


TAB_SRC2, TAB_DST_PREV, TAB_DST = 0, 1, 2


def _moe_kernel(be_ref, nu_ref, wt_ref, src01_ref, tab_ref, u_hbm,
                w1_hbm, w3_hbm, w2_hbm, o_hbm, xbuf, ybuf, st1, st3, st2, w1b, w3b, w2b, gsem, ssem, wsem,
                *, layer):
    i = pl.program_id(0)
    bm = MOE_BM
    n_used = nu_ref[0]
    xs = i % MOE_RING
    slot = xs
    other = (i + MOE_RING - 1) % MOE_RING

    def slab(idx):
        return pl.ds(pl.multiple_of(idx * SLAB, SLAB), SLAB)

    def row_in(tok, r, sl):
        return pltpu.make_async_copy(u_hbm.at[slab(tok), :], xbuf.at[sl, slab(r), :], gsem.at[sl])

    def row_out(r, row, sl):
        return pltpu.make_async_copy(ybuf.at[sl, slab(r), :], o_hbm.at[slab(row), :], ssem.at[sl])

    def gather(section, sl):
        def body(g, c):
            for j in range(8):
                r = g * 8 + j
                row_in(src01_ref[0, section * bm + r], r, sl).start(priority=j % 2)
            return c
        lax.fori_loop(0, bm // 8, body, 0)

    def scatter(sl):
        def body(g, c):
            for j in range(8):
                r = g * 8 + j
                row_out(r, tab_ref[0, TAB_DST * bm + r], sl).start(priority=j % 2)
            return c
        lax.fori_loop(0, bm // 8, body, 0)

    def wait_gather(sl):
        pltpu.make_async_copy(u_hbm.at[pl.ds(0, bm * SLAB), :], xbuf.at[sl], gsem.at[sl]).wait()

    def wait_scatter(sl):
        pltpu.make_async_copy(ybuf.at[sl], o_hbm.at[pl.ds(0, bm * SLAB), :], ssem.at[sl]).wait()

    def weight_copies(e, p):
        return [pltpu.make_async_copy(src.at[layer, e], dst.at[p], wsem.at[p])
                for src, dst in ((w1_hbm, st1), (w3_hbm, st3), (w2_hbm, st2))]

    n_real = o_hbm.shape[0] // SLAB - MOE_RING * bm

    @pl.when(i == 0)
    def _():
        for c in weight_copies(be_ref[0], 0):
            c.start()
        gather(0, 0)
        gather(1, 1)
        ybuf[...] = jnp.zeros_like(ybuf)
        for sl in range(2):
            pltpu.make_async_copy(ybuf.at[sl], o_hbm.at[pl.ds((n_real + sl * bm) * SLAB, bm * SLAB), :],
                                  ssem.at[sl]).start()

    @pl.when(i < n_used)
    def _():
        @pl.when(wt_ref[WT_FIRST, i] == 1)
        def _():
            p = wt_ref[WT_SLOT, i]
            nxt = wt_ref[WT_NEXT, i]

            @pl.when(nxt >= 0)
            def _():
                for c in weight_copies(nxt, 1 - p):
                    c.start(priority=1)

            for c in weight_copies(be_ref[i], p):
                c.wait()
            w1b[...] = st1[p].astype(BF16)
            w3b[...] = st3[p].astype(BF16)
            w2b[...] = st2[p].astype(BF16)

        wait_gather(xs)

        xn = (i + 2) % MOE_RING

        def issue(part, parts=4):
            for r in range(part * bm // parts, (part + 1) * bm // parts):
                row_in(tab_ref[0, TAB_SRC2 * bm + r], r, xn).start(priority=0)
                row_out(r, tab_ref[0, TAB_DST_PREV * bm + r], other).start(priority=1)

        f = w1b.shape[1]
        h1 = jnp.zeros((bm, f), F32)
        h3 = jnp.zeros((bm, f), F32)
        for j in range(SLAB):
            lo, hi = _unpack_slab(xbuf, j, bm, lead=(xs,))
            xk = jnp.concatenate([lo, hi], axis=1).astype(BF16)
            h1 = h1 + _dot(xk, w1b[j * 256:(j + 1) * 256, :])
            h3 = h3 + _dot(xk, w3b[j * 256:(j + 1) * 256, :])
            if j % 4 == 3:
                issue(j // 4)
        hid = (_silu(h1) * h3).astype(BF16)
        wait_scatter(slot)
        half = SLAB // 2
        _pack_slab(ybuf, _dot(hid, w2b[:, :half * 256]), lead=(slot,))
        issue(2)
        _pack_slab(ybuf, _dot(hid, w2b[:, half * 256:]), lead=(slot,), j0=half)
        issue(3)

    @pl.when(i == n_used - 1)
    def _():
        scatter(slot)
        for k in range(MOE_RING):
            wait_scatter(k)
        wait_gather((i + 1) % MOE_RING)
        wait_gather((i + 2) % MOE_RING)


def _moe_call(block_e, n_used, w_table, src_tok, dst_row, u, w1, w3, w2, layer, n_out_rows):
    d, f = w1.shape[-2:]
    assert d == SLAB * 256
    bm = MOE_BM
    nb = block_e.shape[0]
    src_tok = src_tok.reshape(nb, bm)
    dst_row = dst_row.reshape(nb, bm)
    lead = (n_out_rows - bm + jnp.arange(bm, dtype=jnp.int32)).reshape(1, bm)
    src_ahead = jnp.concatenate([src_tok[2:], src_tok[-1:], src_tok[-1:]], axis=0)
    dst_prev = jnp.concatenate([lead, dst_row[:-1]], axis=0)
    table = jnp.concatenate([src_ahead, dst_prev, dst_row], axis=1).reshape(nb, 1, 3 * bm)
    src01 = src_tok[:2].reshape(1, 1, 2 * bm)
    idx = lambda w, fn: pl.BlockSpec((None, 1, w), fn, memory_space=pltpu.SMEM)
    hbm = pl.BlockSpec(memory_space=pl.ANY)
    ring = pltpu.VMEM((MOE_RING, bm * SLAB, 128), U32)
    return pl.pallas_call(
        functools.partial(_moe_kernel, layer=layer),
        grid_spec=pltpu.PrefetchScalarGridSpec(
            num_scalar_prefetch=3, grid=(nb,),
            in_specs=[idx(2 * bm, lambda i, *_: (0, 0, 0)), idx(3 * bm, lambda i, *_: (i, 0, 0)),
                      hbm, hbm, hbm, hbm],
            out_specs=hbm,
            scratch_shapes=[ring, ring,
                            pltpu.VMEM((2, d, f), F32), pltpu.VMEM((2, d, f), F32), pltpu.VMEM((2, f, d), F32),
                            pltpu.VMEM((d, f), BF16), pltpu.VMEM((d, f), BF16), pltpu.VMEM((f, d), BF16),
                            pltpu.SemaphoreType.DMA((MOE_RING,)), pltpu.SemaphoreType.DMA((MOE_RING,)),
                            pltpu.SemaphoreType.DMA((2,))]),
        out_shape=jax.ShapeDtypeStruct((n_out_rows * SLAB, 128), U32),
        compiler_params=_cparams(("arbitrary",)), name="moe_ffn",
    )(block_e, n_used, w_table, src01, table, u, w1, w3, w2)


def _route_tables(rt, cnt, t):
    bm = MOE_BM
    tk = t * TOP_K
    e = rt[:, RT_E:RT_E + TOP_K].astype(jnp.int32)
    rank = rt[:, RT_RANK:RT_RANK + TOP_K].astype(jnp.int32)
    wts = rt[:, RT_W:RT_W + TOP_K]
    counts = cnt[0, LANE_E0:LANE_E0 + N_EXPERTS].astype(jnp.int32)
    padded = (counts + bm - 1) // bm * bm
    pad_ends = jnp.cumsum(padded)
    pad_starts = pad_ends - padded
    e_ids = jnp.arange(N_EXPERTS, dtype=jnp.int32)

    def lookup(table, idx):
        return jnp.sum(jnp.where(idx[..., None] == e_ids, table, 0), axis=-1)

    dest = (lookup(pad_starts, e) + rank).reshape(-1)
    n_blocks = -(-tk // bm) + N_EXPERTS
    n_rows = n_blocks * bm
    inv = jnp.full((n_rows,), -1, jnp.int32).at[dest].set(jnp.arange(tk, dtype=jnp.int32))
    tok, slot = inv // TOP_K, inv % TOP_K
    src_tok = jnp.where(inv >= 0, tok, 0)
    dst_row = jnp.where(inv >= 0, slot * t + tok, tk + (jnp.arange(n_rows, dtype=jnp.int32) % (MOE_RING * bm)))
    block_ids = jnp.arange(n_blocks, dtype=jnp.int32)
    block_e = jnp.minimum(jnp.sum(pad_ends[None, :] <= block_ids[:, None] * bm, axis=1), N_EXPERTS - 1).astype(jnp.int32)
    n_used = (pad_ends[-1] // bm).astype(jnp.int32).reshape(1)
    has_rows = counts > 0
    at_or_after = lax.cummin(jnp.where(has_rows, e_ids, N_EXPERTS)[::-1])[::-1]
    next_e = jnp.concatenate([at_or_after[1:], jnp.full((1,), N_EXPERTS, jnp.int32)])
    next_e = jnp.where(next_e >= N_EXPERTS, -1, next_e)
    stage = (jnp.cumsum(has_rows.astype(jnp.int32)) - 1) % 2
    first = jnp.concatenate([jnp.ones((1,), bool), block_e[1:] != block_e[:-1]]) & (block_ids < n_used[0])
    w_table = jnp.stack([first.astype(jnp.int32), lookup(stage, block_e), lookup(next_e, block_e)]).astype(jnp.int32)
    return block_e, n_used, w_table, src_tok, dst_row, wts, tk + MOE_RING * bm


def _lane_row(pairs):
    row = jnp.zeros((128,), F32)
    for lane0, vals in pairs:
        row = row.at[lane0:lane0 + vals.shape[0]].set(vals.astype(F32))
    return row


def kernel(x, meta_tokens, hg_lb_logits, norm_mix_w, w_in, hg_norm_w, ml_b_i, ml_b_f, ml_norm_w,
           ssm_conv_w, ssm_conv_b, ssm_dt_bias, ssm_a_log, ssm_d, ssm_norm_w, w_out, norm_ffn_w,
           moe_w_group, moe_b_group, moe_w_router, moe_b_router, moe_w1, moe_w3, moe_w2, final_norm_w):
    batch, seq, d = x.shape
    depth = w_in.shape[0]
    p = LEAD_PAD + N_META + seq
    t = batch * p
    meta = jnp.broadcast_to(meta_tokens.astype(x.dtype)[None], (batch, N_META, d))
    h = jnp.concatenate([jnp.zeros((batch, LEAD_PAD, d), x.dtype), meta, x], axis=1).reshape(t, d)

    lb_w = jax.nn.softmax(hg_lb_logits.astype(F32), axis=0)
    lower_bounds = jnp.cumsum(lb_w, axis=0) - lb_w[0]

    o_mq, o_mv, o_mi, o_sz = HG_KEY * 2 + HG_WIDTH * 2, 2560, 3584, 3592
    o_sx = o_sz + SSM_WIDTH
    o_dt = o_sx + SSM_WIDTH + 2 * SSM_BC
    w_perm = jnp.concatenate([
        w_in[..., :o_mq], w_in[..., o_sz:o_sx], w_in[..., o_sx:o_dt], w_in[..., o_mv:o_mi], w_in[..., o_mq:o_mv],
        w_in[..., o_mi:o_sz], w_in[..., o_dt:],
        jnp.zeros((depth, d, N_PROJ - C_SMALL - 2 * ML_HEADS - SSM_HEADS), w_in.dtype)], axis=-1).astype(BF16)
    w_out_b = w_out.astype(BF16)

    contrib = wts = None
    for layer in range(depth):
        lb = lower_bounds[layer]
        lbf = jnp.maximum(lb, LB_FLOOR)
        hg_par = jnp.zeros((8, HG_KEY), F32).at[0].set(lbf).at[1].set(1.0 - lb).at[2].set(lbf - lb).at[3].set(hg_norm_w[layer])
        ml_par = jnp.zeros((8, 128), F32).at[0].set(_lane_row([(LANE_MI, ml_b_i[layer]), (LANE_MF, ml_b_f[layer])]))
        ss_par = jnp.zeros((8, 128), F32).at[0].set(_lane_row([(LANE_DT, ssm_dt_bias[layer])]))
        ss_par = ss_par.at[1].set(_lane_row([(LANE_DT, ssm_a_log[layer])]))
        dskip = jnp.repeat(ssm_d[layer].astype(F32), SSM_HEADDIM).reshape(1, SSM_WIDTH)
        w_r = jnp.concatenate([moe_w_group[layer],
                               moe_w_router[layer].transpose(1, 0, 2).reshape(d, N_EXPERTS),
                               jnp.zeros((d, 128 - N_GROUPS_MOE - N_EXPERTS), F32)], axis=1)
        r_hi = w_r.astype(BF16)
        r_lo = (w_r - r_hi.astype(F32)).astype(BF16)

        if layer == 0:
            (u,) = _norm_call(h, norm_mix_w[layer], write_h=False, u_dtype=BF16)
        else:
            h, u = _norm_call(h, norm_mix_w[layer], contrib, wts, write_h=True, u_dtype=BF16)
        proj = _inproj_call(u, w_perm, layer)
        ya = _hgrn2_call(proj, hg_par, batch, p)
        yb = _mlstm_call(proj, ml_par, ml_norm_w[layer].reshape(1, ML_WIDTH), batch, p)
        yc = _ssd_call(proj, ssm_conv_w[layer], ssm_conv_b[layer].reshape(1, -1), ss_par, dskip,
                       ssm_norm_w[layer].reshape(1, SSM_WIDTH), batch, p)
        r_bias = _lane_row([(0, moe_b_group[layer]), (LANE_E0, moe_b_router[layer].reshape(-1))]).reshape(1, 128)
        h, u_ffn, rt, cnt = _outproj_call(ya, yb, yc, h, w_out_b, layer, norm_ffn_w[layer], r_hi, r_lo, r_bias)
        block_e, n_used, w_table, src_tok, dst_row, wts, n_out_rows = _route_tables(rt, cnt, t)
        contrib = _moe_call(block_e, n_used, w_table, src_tok, dst_row, u_ffn, moe_w1, moe_w3, moe_w2, layer,
                            n_out_rows)
    return _final_call(h, contrib, wts, final_norm_w, batch, p)
```

```python
import functools

import jax
import jax.numpy as jnp
import numpy as np
from jax import lax
from jax.experimental import pallas as pl
from jax.experimental.pallas import tpu as pltpu

F32 = jnp.float32
BF16 = jnp.bfloat16

D_MODEL = 2048
N_META = 16
CHUNK = 64
HG_CHUNK = 16
LEAD_PAD = CHUNK - N_META
EPS = 1e-6
NEG = -1e30
LB_FLOOR = 1e-30

HG_HEADS = 4
HG_KDIM = 128
HG_KEY = HG_HEADS * HG_KDIM
HG_WIDTH = HG_HEADS * 128

ML_HEADS = 4
ML_QK = 64
ML_V = 128
ML_QK_W = ML_HEADS * ML_QK
ML_WIDTH = ML_HEADS * ML_V
GATE_CAP = 15.0

SSM_HEADS = 16
SSM_HEADDIM = 64
SSM_WIDTH = SSM_HEADS * SSM_HEADDIM
SSM_STATE = 128
SSM_GROUPS = 4
SSM_HPG = SSM_HEADS // SSM_GROUPS
SSM_BC = SSM_GROUPS * SSM_STATE
CONV_W = 4

D_MIX = HG_WIDTH + ML_WIDTH + SSM_WIDTH

N_GROUPS_MOE = 4
EXPERTS_PER_GROUP = 8
N_EXPERTS = N_GROUPS_MOE * EXPERTS_PER_GROUP
TOP_K = 2
D_EXPERT = 512
MOE_BM = 128

C_HG = 0
C_SZ = 2048
C_SX = 3072
C_MV = 5120
C_MQ = 6144
C_SMALL = 6656
N_PROJ = 6912
LANE_MI = 0
LANE_MF = ML_HEADS
LANE_DT = 2 * ML_HEADS

VMEM_LIMIT = 56 * 1024 * 1024

ROW_TILES = {
    "norm": 264,
    "final": 256,
    "in_proj": 2112,
    "hgrn2": 528,
    "ssd": 704,
    "out_proj": 352,
}
IN_PROJ_COLS = 768


def _cparams(sem):
    return pltpu.CompilerParams(dimension_semantics=sem, vmem_limit_bytes=VMEM_LIMIT)


def _row_tile(n, target, mult=16):
    best = None
    for t in range(mult, min(n, target) + 1, mult):
        if n % t == 0:
            best = t
    assert best is not None, (n, target, mult)
    return best


def _split3(x):
    hi = x.astype(BF16)
    r = x - hi.astype(F32)
    mid = r.astype(BF16)
    lo = (r - mid.astype(F32)).astype(BF16)
    return hi, mid, lo


def _dot(a, b):
    return jnp.dot(a, b, preferred_element_type=F32)


def _sel_dot(sel, x):
    hi, mid, lo = _split3(x)
    return _dot(sel, hi) + _dot(sel, mid) + _dot(sel, lo)


def _dot_sel(x, sel):
    hi, mid, lo = _split3(x)
    return _dot(hi, sel) + _dot(mid, sel) + _dot(lo, sel)


def _dot_nt(a, b):
    return lax.dot_general(a, b, (((1,), (1,)), ((), ())), preferred_element_type=F32)


def _dot_tn(a, b):
    return lax.dot_general(a, b, (((0,), (0,)), ((), ())), preferred_element_type=F32)


def _log_sigmoid(x):
    return jnp.minimum(x, 0.0) - jnp.log1p(jnp.exp(-jnp.abs(x)))


def _sigmoid(x):
    return 1.0 / (1.0 + jnp.exp(-x))


def _silu(x):
    return x * _sigmoid(x)


SLAB = 8
U32 = jnp.uint32
HI_MASK = 0xFFFF0000


def _bf16_bits(x):
    return lax.bitcast_convert_type(x.astype(BF16).astype(F32), U32)


def _pack_slab(ref, val, lead=(), j0=0):
    rows = val.shape[0]
    for jj in range(val.shape[1] // 256):
        lo = _bf16_bits(val[:, jj * 256:jj * 256 + 128])
        hi = _bf16_bits(val[:, jj * 256 + 128:(jj + 1) * 256])
        ref[(*lead, pl.ds(j0 + jj, rows, stride=SLAB), slice(None))] = (lo >> 16) | (hi & U32(HI_MASK))


def _unpack_slab(ref, j, rows, lead=()):
    w = ref[(*lead, pl.ds(j, rows, stride=SLAB), slice(None))]
    return (lax.bitcast_convert_type(w << 16, F32), lax.bitcast_convert_type(w & U32(HI_MASK), F32))


def _combine_rows(h_ref, c0_ref, c1_ref, wt_ref):
    rows = h_ref.shape[0]
    wt = wt_ref[...]
    w0, w1 = wt[:, 0:1], wt[:, 1:2]
    pieces = []
    for j in range(SLAB):
        a0, b0 = _unpack_slab(c0_ref, j, rows)
        a1, b1 = _unpack_slab(c1_ref, j, rows)
        pieces += [w0 * a0 + w1 * a1, w0 * b0 + w1 * b1]
    return h_ref[...] + jnp.concatenate(pieces, axis=1)


def _norm_kernel(*refs, combine, write_h):
    if combine:
        h_ref, c0_ref, c1_ref, wt_ref, nw_ref = refs[:5]
        outs = refs[5:]
        h = _combine_rows(h_ref, c0_ref, c1_ref, wt_ref)
    else:
        h_ref, nw_ref = refs[:2]
        outs = refs[2:]
        h = h_ref[...]
    if write_h:
        outs[0][...] = h
    ms = jnp.mean(h * h, axis=-1, keepdims=True)
    u_ref = outs[-1]
    u_ref[...] = (h * lax.rsqrt(ms + EPS) * nw_ref[...]).astype(u_ref.dtype)


def _norm_call(h, nw, contrib=None, wts=None, *, write_h, u_dtype):
    t, d = h.shape
    tm = _row_tile(t, ROW_TILES["norm"])
    combine = contrib is not None
    row = lambda i: (i, 0)
    in_specs = [pl.BlockSpec((tm, d), row)]
    args = [h]
    if combine:
        in_specs += [pl.BlockSpec((tm * SLAB, 128), row), pl.BlockSpec((tm * SLAB, 128), lambda i: (t // tm + i, 0)),
                     pl.BlockSpec((tm, 2), row)]
        args += [contrib, contrib, wts]
    in_specs.append(pl.BlockSpec((1, d), lambda i: (0, 0)))
    args.append(nw.reshape(1, d))
    out_shape, out_specs = [], []
    if write_h:
        out_shape.append(jax.ShapeDtypeStruct((t, d), F32))
        out_specs.append(pl.BlockSpec((tm, d), row))
    out_shape.append(jax.ShapeDtypeStruct((t, d), u_dtype))
    out_specs.append(pl.BlockSpec((tm, d), row))
    return pl.pallas_call(
        functools.partial(_norm_kernel, combine=combine, write_h=write_h),
        grid=(t // tm,), in_specs=in_specs, out_specs=out_specs, out_shape=out_shape,
        compiler_params=_cparams(("arbitrary",)), name="combine_norm",
    )(*args)


def _final_kernel(h_ref, c0_ref, c1_ref, wt_ref, nw_ref, o_ref):
    h = _combine_rows(h_ref, c0_ref, c1_ref, wt_ref)
    ms = jnp.mean(h * h, axis=-1, keepdims=True)
    o_ref[...] = h * lax.rsqrt(ms + EPS) * nw_ref[...]


def _final_call(h, contrib, wts, nw, batch, p):
    t, d = h.shape
    seq = p - CHUNK
    tm = _row_tile(seq, ROW_TILES["final"])
    n_out = seq // tm
    row0 = lambda b, i: b * p + CHUNK + i * tm
    src = lambda b, i: (pl.multiple_of(row0(b, i), CHUNK), 0)
    slab0 = lambda b, i: (pl.multiple_of(row0(b, i) * SLAB, CHUNK), 0)
    slab1 = lambda b, i: (pl.multiple_of((t + row0(b, i)) * SLAB, CHUNK), 0)
    win = lambda r, w, fn: pl.BlockSpec((pl.Element(r), pl.Element(w)), fn)
    return pl.pallas_call(
        _final_kernel, grid=(batch, n_out),
        in_specs=[win(tm, d, src), win(tm * SLAB, 128, slab0), win(tm * SLAB, 128, slab1), win(tm, 2, src),
                  pl.BlockSpec((1, d), lambda b, i: (0, 0))],
        out_specs=pl.BlockSpec((None, tm, d), lambda b, i: (b, i, 0)),
        out_shape=jax.ShapeDtypeStruct((batch, seq, d), F32),
        compiler_params=_cparams(("arbitrary", "arbitrary")), name="final_norm",
    )(h, contrib, contrib, wts, nw.reshape(1, d))


def _matmul_kernel(x_ref, w_ref, o_ref):
    o_ref[...] = _dot(x_ref[...], w_ref[...])


def _inproj_call(u, w, layer):
    t, d = u.shape
    n = w.shape[-1]
    tm = _row_tile(t, ROW_TILES["in_proj"])
    tn = IN_PROJ_COLS
    assert n % tn == 0
    return pl.pallas_call(
        _matmul_kernel, grid=(t // tm, n // tn),
        in_specs=[pl.BlockSpec((tm, d), lambda i, j: (i, 0)),
                  pl.BlockSpec((None, d, tn), lambda i, j: (layer, 0, j))],
        out_specs=pl.BlockSpec((tm, tn), lambda i, j: (i, j)),
        out_shape=jax.ShapeDtypeStruct((t, n), F32),
        compiler_params=_cparams(("arbitrary", "arbitrary")), name="in_proj",
    )(u, w)


def _hgrn2_kernel(q_ref, f_ref, i_ref, g_ref, par_ref, o_ref, st_ref, *, rows):
    s = pl.program_id(1)

    @pl.when(s == 0)
    def _():
        st_ref[...] = jnp.zeros_like(st_ref)

    c = HG_CHUNK
    ones = jnp.ones((HG_KDIM, 128), BF16)
    rid = lax.broadcasted_iota(jnp.int32, (c, 128), 0)
    scale = HG_KDIM ** -0.5

    hc = c // 2

    def chunk(ci, carry):
        r0 = pl.multiple_of(ci * c, c)
        pad = (s * rows + r0 + rid) < LEAD_PAD

        def front(h):
            cols = slice(h * 128, (h + 1) * 128)
            a_lb = par_ref[0:1, cols]
            b_lb = par_ref[1:2, cols]
            c_lb = par_ref[2:3, cols]
            z = f_ref[pl.ds(r0, c), cols]
            sg = _sigmoid(z)
            f = a_lb + b_lb * sg
            log_f = jnp.where(pad, 0.0, jnp.log(f))
            k = jnp.where(pad, 0.0, b_lb * (1.0 - sg) - c_lb)
            q = q_ref[pl.ds(r0, c), cols] * scale
            v = i_ref[pl.ds(r0, c), cols]
            cum = log_f
            for sh in (1, 2, 4, 8):
                cum = cum + jnp.where(rid >= sh, pltpu.roll(cum, sh, axis=0), 0.0)
            parts = []
            for s_ in range(c):
                lo = 0 if s_ < hc else hc
                rel = jnp.where(rid[lo:] >= s_, cum[lo:] - cum[s_:s_ + 1, :], NEG)
                parts.append(q[lo:] * (k[s_:s_ + 1, :] * jnp.exp(rel)))
            sc = _dot(jnp.concatenate(parts, axis=0).astype(BF16), ones)
            st = st_ref[h]
            o_inter = _dot_nt((q * jnp.exp(cum)).astype(BF16), st.astype(BF16))
            last = cum[c - 1:c, :]
            kd = (k * jnp.exp(last - cum)).astype(BF16)
            st_ref[h] = st * jnp.exp(last) + _dot_tn(v.astype(BF16), kd)
            return sc, o_inter, v

        def back(h, sc, o_inter, v):
            cols = slice(h * 128, (h + 1) * 128)
            o_top = o_inter[:hc]
            o_bot = o_inter[hc:]
            for s_ in range(hc):
                o_top = o_top + sc[s_ * c:s_ * c + hc, :] * v[s_:s_ + 1, :]
                o_bot = o_bot + sc[s_ * c + hc:(s_ + 1) * c, :] * v[s_:s_ + 1, :]
            for s_ in range(hc, c):
                r_ = hc * c + (s_ - hc) * hc
                o_bot = o_bot + sc[r_:r_ + hc, :] * v[s_:s_ + 1, :]
            o = jnp.concatenate([o_top, o_bot], axis=0)
            ms = jnp.mean(o * o, axis=-1, keepdims=True)
            g = g_ref[pl.ds(r0, c), cols]
            o_ref[pl.ds(r0, c), cols] = (o * lax.rsqrt(ms + EPS) * par_ref[3:4, cols] * _silu(g)).astype(o_ref.dtype)

        pending = front(0)
        for h in range(1, HG_HEADS):
            nxt = front(h)
            back(h - 1, *pending)
            pending = nxt
        back(HG_HEADS - 1, *pending)
        return carry

    n_chunks = rows // c
    lax.fori_loop(0, n_chunks, chunk, 0, unroll=3 if n_chunks % 3 == 0 else 1)


def _hgrn2_call(proj, par, batch, p):
    t = proj.shape[0]
    rows = _row_tile(p, ROW_TILES["hgrn2"])
    nb = p // rows
    w = HG_KEY
    blk = lambda j: pl.BlockSpec((rows, w), lambda b, s, j=j: (b * nb + s, C_HG // w + j))
    return pl.pallas_call(
        functools.partial(_hgrn2_kernel, rows=rows), grid=(batch, nb),
        in_specs=[blk(0), blk(1), blk(2), blk(3), pl.BlockSpec((8, w), lambda b, s: (0, 0))],
        out_specs=pl.BlockSpec((rows, w), lambda b, s: (b * nb + s, 0)),
        out_shape=jax.ShapeDtypeStruct((t, HG_WIDTH), BF16),
        scratch_shapes=[pltpu.VMEM((HG_HEADS, 128, HG_KDIM), F32)],
        compiler_params=_cparams(("arbitrary", "arbitrary")), name="hgrn2",
    )(proj, proj, proj, proj, par)


def _mlstm_kernel(v_ref, o_ref, q_ref, k_ref, sm_ref, par_ref, nw_ref, tri_ref, sel_ref, dg_ref,
                  y_ref, c_ref, m_ref):
    s = pl.program_id(0)

    @pl.when(s == 0)
    def _():
        c_ref[...] = jnp.zeros_like(c_ref)
        m_ref[...] = jnp.zeros_like(m_ref)

    n = CHUNK
    rid = lax.broadcasted_iota(jnp.int32, (n, 128), 0)
    pad = (s * n + rid) < LEAD_PAD
    sel_i = sel_ref[0]
    sel_f = sel_ref[1]
    dg = dg_ref[...]
    w = ML_HEADS * n
    tt = lax.broadcasted_iota(jnp.int32, (n, w), 0)
    ss = lax.broadcasted_iota(jnp.int32, (n, w), 1) & (n - 1)
    causal = tt >= ss
    scale = ML_QK ** -0.5
    lane = lax.broadcasted_iota(jnp.int32, (n, 128), 1)
    one_col = jnp.where(lane == 0, 1.0, 0.0).astype(BF16)
    for b in range(v_ref.shape[0]):
        pre = sm_ref[b] + par_ref[0:1, :]
        cap = GATE_CAP * jnp.tanh(pre * (1.0 / GATE_CAP))
        log_i = jnp.where(pad, NEG, cap)
        log_f = jnp.where(pad, 0.0, _log_sigmoid(cap))
        cum = _sel_dot(tri_ref[...], log_f)
        cum_col = _dot_sel(cum, sel_f)
        cum_row = jnp.sum(cum_col * dg, axis=0, keepdims=True)
        li_row = jnp.sum(_dot_sel(log_i, sel_i) * dg, axis=0, keepdims=True)
        dmat = jnp.where(causal, cum_col - cum_row + li_row, NEG)
        hs = range(ML_HEADS)
        sh = [b * ML_HEADS + h for h in hs]
        d_h = [dmat[:, h * n:(h + 1) * n] for h in hs]
        cum_h = [cum[:, LANE_MF + h:LANE_MF + h + 1] for h in hs]
        li_h = [log_i[:, LANE_MI + h:LANE_MI + h + 1] for h in hs]
        m_st = [m_ref[sh[h]:sh[h] + 1, 0:1] for h in hs]
        q = [(q_ref[b, :, h * ML_QK:(h + 1) * ML_QK] * scale).astype(BF16) for h in hs]
        k = [k_ref[b, :, h * ML_QK:(h + 1) * ML_QK] for h in hs]
        v_aug = [jnp.concatenate([v_ref[b, :, h * ML_V:(h + 1) * ML_V].astype(BF16), one_col], axis=1) for h in hs]
        c_prev = [c_ref[sh[h]] for h in hs]
        inter = [cum_h[h] + m_st[h] for h in hs]
        m_t = [jnp.maximum(inter[h], jnp.max(d_h[h], axis=-1, keepdims=True)) for h in hs]
        qk = [_dot_nt(q[h], k[h].astype(BF16)) for h in hs]
        qc = [_dot(q[h], c_prev[h].astype(BF16)) for h in hs]
        pw = [(qk[h] * jnp.exp(d_h[h] - m_t[h])).astype(BF16) for h in hs]
        nd = [_dot(pw[h], v_aug[h]) + jnp.exp(inter[h] - m_t[h]) * qc[h] for h in hs]
        hh = [nd[h][:, :ML_V] / jnp.maximum(jnp.abs(nd[h][:, ML_V:ML_V + 1]), jnp.exp(-m_t[h])) for h in hs]
        tot = [cum_h[h][n - 1:n, :] for h in hs]
        to_end = [tot[h] - cum_h[h] + li_h[h] for h in hs]
        m_loc = [jnp.max(to_end[h], axis=0, keepdims=True) for h in hs]
        kw = [(k[h] * jnp.exp(to_end[h] - m_loc[h])).astype(BF16) for h in hs]
        c_loc = [_dot_tn(kw[h], v_aug[h]) for h in hs]
        m_new = [jnp.maximum(tot[h] + m_st[h], m_loc[h]) for h in hs]
        for h in hs:
            c_ref[sh[h]] = (jnp.exp(tot[h] + m_st[h] - m_new[h]) * c_prev[h]
                            + jnp.exp(m_loc[h] - m_new[h]) * c_loc[h])
            m_ref[sh[h]:sh[h] + 1, :] = jnp.broadcast_to(m_new[h], (1, 128))
        for h in hs:
            ms = jnp.mean(hh[h] * hh[h], axis=-1, keepdims=True)
            cols = slice(h * ML_V, (h + 1) * ML_V)
            y_ref[b, :, cols] = (hh[h] * lax.rsqrt(ms + EPS) * nw_ref[:, cols]
                                 * _sigmoid(o_ref[b, :, cols])).astype(y_ref.dtype)


def _lane_select(lane0, heads, width):
    m = np.zeros((128, heads * width), np.float32)
    for h in range(heads):
        m[lane0 + h, h * width:(h + 1) * width] = 1.0
    return m


def _diag_mask(n, heads):
    return np.tile(np.eye(n, dtype=np.float32), (1, heads))


def _mlstm_call(proj, par, nw, batch, p):
    t = proj.shape[0]
    n = CHUNK
    nc = p // n
    tri = jnp.asarray(np.tril(np.ones((n, n), np.float32)), BF16)
    sel = jnp.asarray(np.stack([_lane_select(LANE_MI, ML_HEADS, n), _lane_select(LANE_MF, ML_HEADS, n)]), BF16)
    dg = jnp.asarray(_diag_mask(n, ML_HEADS), F32)
    proj3 = proj.reshape(batch, p, proj.shape[1])
    blk = lambda w, off: pl.BlockSpec((batch, n, w), lambda s: (0, s, off // w))
    const = lambda shape: pl.BlockSpec(shape, lambda s: (0,) * len(shape))
    m_rows = -(-batch * ML_HEADS // 8) * 8
    y = pl.pallas_call(
        _mlstm_kernel, grid=(nc,),
        in_specs=[blk(ML_WIDTH, C_MV), blk(ML_WIDTH, C_MV + ML_WIDTH), blk(ML_QK_W, C_MQ),
                  blk(ML_QK_W, C_MQ + ML_QK_W), blk(128, C_SMALL),
                  const((8, 128)), const((1, ML_WIDTH)), const((n, n)), const((2, 128, ML_HEADS * n)),
                  const((n, ML_HEADS * n))],
        out_specs=pl.BlockSpec((batch, n, ML_WIDTH), lambda s: (0, s, 0)),
        out_shape=jax.ShapeDtypeStruct((batch, p, ML_WIDTH), BF16),
        scratch_shapes=[pltpu.VMEM((batch * ML_HEADS, ML_QK, 2 * ML_V), F32), pltpu.VMEM((m_rows, 128), F32)],
        compiler_params=_cparams(("arbitrary",)), name="mlstm",
    )(proj3, proj3, proj3, proj3, proj3, par, nw, tri, sel, dg)
    return y.reshape(t, ML_WIDTH)


def _ssd_kernel(z_ref, x_ref, b_ref, c_ref, sm_ref, cw_ref, cb_ref, par_ref, dsk_ref, nw_ref,
                tri_ref, sel_ref, dg_ref, y_ref, xs_ref, bs_ref, cs_ref, st_ref):
    s = pl.program_id(1)
    n = CHUNK
    tail = 8
    chunks = z_ref.shape[0] // n

    @pl.when(s == 0)
    def _():
        st_ref[...] = jnp.zeros_like(st_ref)
        xs_ref[0:tail, :] = jnp.zeros((tail, xs_ref.shape[1]), F32)
        bs_ref[0:tail, :] = jnp.zeros((tail, bs_ref.shape[1]), F32)
        cs_ref[0:tail, :] = jnp.zeros((tail, cs_ref.shape[1]), F32)

    def chunk(ci, carry):
        _ssd_chunk(ci, s * chunks + ci, z_ref, x_ref, b_ref, c_ref, sm_ref, cw_ref, cb_ref, par_ref, dsk_ref,
                   nw_ref, tri_ref, sel_ref, dg_ref, y_ref, xs_ref, bs_ref, cs_ref, st_ref)
        return carry

    lax.fori_loop(0, chunks, chunk, 0)


def _ssd_chunk(ci, chunk_id, z_ref, x_ref, b_ref, c_ref, sm_ref, cw_ref, cb_ref, par_ref, dsk_ref, nw_ref,
               tri_ref, sel_ref, dg_ref, y_ref, xs_ref, bs_ref, cs_ref, st_ref):
    n = CHUNK
    tail = 8
    rows = pl.ds(pl.multiple_of(ci * n, n), n)

    def conv_silu(src_ref, scr_ref, c0, width, rowmask):
        scr_ref[tail:tail + n, :] = src_ref[rows, :]
        acc = cb_ref[:, c0:c0 + width]
        for j in range(CONV_W):
            off = tail - (CONV_W - 1) + j
            acc = acc + cw_ref[j:j + 1, c0:c0 + width] * scr_ref[off:off + n, :]
        scr_ref[0:tail, :] = scr_ref[n:n + tail, :]
        return jnp.where(rowmask, 0.0, _silu(acc))

    def padmask(width):
        return (chunk_id * n + lax.broadcasted_iota(jnp.int32, (n, width), 0)) < LEAD_PAD

    x = conv_silu(x_ref, xs_ref, 0, SSM_WIDTH, padmask(SSM_WIDTH))
    bm = conv_silu(b_ref, bs_ref, SSM_WIDTH, SSM_BC, padmask(SSM_BC))
    cm = conv_silu(c_ref, cs_ref, SSM_WIDTH + SSM_BC, SSM_BC, padmask(SSM_BC))

    pre = sm_ref[rows, :] + par_ref[0:1, :]
    dt = jnp.maximum(pre, 0.0) + jnp.log1p(jnp.exp(-jnp.abs(pre)))
    dt = jnp.where(padmask(128), 0.0, dt)
    da = dt * (-jnp.exp(par_ref[1:2, :]))
    cum = _sel_dot(tri_ref[...], da)
    sel = sel_ref[...]
    dt_col = _dot_sel(dt, sel)
    cum_col = _dot_sel(cum, sel)
    cum_row = jnp.sum(cum_col * dg_ref[...], axis=0, keepdims=True)
    w = SSM_WIDTH
    tt = lax.broadcasted_iota(jnp.int32, (n, w), 0)
    ss = lax.broadcasted_iota(jnp.int32, (n, w), 1) & (n - 1)
    decay = jnp.exp(jnp.where(tt >= ss, cum_col - cum_row, NEG))
    xdt = x * dt_col
    last = cum_col[n - 1:n, :]
    wend = (xdt * jnp.exp(last - cum_col)).astype(BF16)
    chunk_decay = jnp.exp(last)
    ecum = jnp.exp(cum_col)
    gw = SSM_HPG * SSM_HEADDIM
    rr = lax.broadcasted_iota(jnp.int32, (gw, gw), 0) // SSM_HEADDIM
    cc = lax.broadcasted_iota(jnp.int32, (gw, gw), 1) // SSM_HEADDIM
    blockdiag = rr == cc
    ys = []
    for g in range(SSM_GROUPS):
        gl = slice(g * gw, (g + 1) * gw)
        sl = slice(g * SSM_STATE, (g + 1) * SSM_STATE)
        cm_g = cm[:, sl].astype(BF16)
        bm_g = bm[:, sl].astype(BF16)
        cb = _dot_nt(cm_g, jnp.concatenate([bm_g] * SSM_HPG, axis=0))
        m = (cb * decay[:, gl]).astype(BF16)
        xdt_g = xdt[:, gl]
        bd = jnp.where(blockdiag, jnp.concatenate([xdt_g] * SSM_HPG, axis=0), 0.0).astype(BF16)
        st = st_ref[g]
        y_g = _dot(m, bd) + ecum[:, gl] * _dot(cm_g, st.astype(BF16))
        st_ref[g] = st * chunk_decay[:, gl] + _dot_tn(bm_g, wend[:, gl])
        ys.append(y_g)
    y = jnp.concatenate(ys, axis=1) + x * dsk_ref[...]
    y = y * _silu(z_ref[rows, :])
    outs = []
    for g in range(SSM_GROUPS):
        gl = slice(g * gw, (g + 1) * gw)
        y_g = y[:, gl]
        ms = jnp.mean(y_g * y_g, axis=-1, keepdims=True)
        outs.append(y_g * lax.rsqrt(ms + EPS))
    y_ref[rows, :] = (jnp.concatenate(outs, axis=1) * nw_ref[...]).astype(y_ref.dtype)


def _ssd_call(proj, cw, cb, par, dsk, nw, batch, p):
    t = proj.shape[0]
    n = CHUNK
    nc = p // n
    tri = jnp.asarray(np.tril(np.ones((n, n), np.float32)), BF16)
    sel = jnp.asarray(_lane_select(LANE_DT, SSM_HEADS, SSM_HEADDIM), BF16)
    dg = jnp.asarray(_diag_mask(n, SSM_HEADS), F32)
    rows = _row_tile(p, ROW_TILES["ssd"], mult=n)
    nc = p // rows
    blk = lambda w, off: pl.BlockSpec((rows, w), lambda b, s: (b * nc + s, off // w))
    const = lambda shape: pl.BlockSpec(shape, lambda b, s: (0,) * len(shape))
    cch = SSM_WIDTH + 2 * SSM_BC
    return pl.pallas_call(
        _ssd_kernel, grid=(batch, nc),
        in_specs=[blk(SSM_WIDTH, C_SZ), blk(SSM_WIDTH, C_SX), blk(SSM_BC, C_SX + SSM_WIDTH),
                  blk(SSM_BC, C_SX + SSM_WIDTH + SSM_BC), blk(128, C_SMALL),
                  const((CONV_W, cch)), const((1, cch)), const((8, 128)), const((1, SSM_WIDTH)),
                  const((1, SSM_WIDTH)), const((n, n)), const((128, SSM_WIDTH)), const((n, SSM_WIDTH))],
        out_specs=pl.BlockSpec((rows, SSM_WIDTH), lambda b, s: (b * nc + s, 0)),
        out_shape=jax.ShapeDtypeStruct((t, SSM_WIDTH), BF16),
        scratch_shapes=[pltpu.VMEM((n + 8, SSM_WIDTH), F32), pltpu.VMEM((n + 8, SSM_BC), F32),
                        pltpu.VMEM((n + 8, SSM_BC), F32),
                        pltpu.VMEM((SSM_GROUPS, SSM_STATE, SSM_HPG * SSM_HEADDIM), F32)],
        compiler_params=_cparams(("arbitrary", "arbitrary")), name="ssd",
    )(proj, proj, proj, proj, proj, cw, cb, par, dsk, nw, tri, sel, dg)


LANE_E0 = N_GROUPS_MOE
RT_E, RT_RANK, RT_W = 0, 2, 4


def _first_max(vals, lane):
    m = jnp.max(vals, axis=-1, keepdims=True)
    idx = jnp.min(jnp.where(vals == m, lane.astype(F32), 128.0), axis=-1, keepdims=True)
    return m, idx.astype(jnp.int32)


def _outproj_kernel(ya_ref, yb_ref, yc_ref, h_ref, w_ref, nw_ref, rh_ref, rl_ref, rb_ref, tri_ref,
                    hm_ref, u_ref, rt_ref, cnt_ref):
    a0, a1 = HG_WIDTH, HG_WIDTH + ML_WIDTH
    h = h_ref[...]
    h = h + _dot(ya_ref[...], w_ref[0:a0, :])
    h = h + _dot(yb_ref[...], w_ref[a0:a1, :])
    h = h + _dot(yc_ref[...], w_ref[a1:, :])
    hm_ref[...] = h
    ms = jnp.mean(h * h, axis=-1, keepdims=True)
    u = h * lax.rsqrt(ms + EPS) * nw_ref[...]
    _pack_slab(u_ref, u)
    u_hi = u.astype(BF16)
    u_lo = (u - u_hi.astype(F32)).astype(BF16)
    lg = _dot(u_hi, rh_ref[...]) + (_dot(u_lo, rh_ref[...]) + _dot(u_hi, rl_ref[...])) + rb_ref[...]

    tm = lg.shape[0]
    lane = lax.broadcasted_iota(jnp.int32, (tm, 128), 1)
    g_mask = lane < N_GROUPS_MOE
    g_max, g_sel = _first_max(jnp.where(g_mask, lg, NEG), lane)
    g_gate = 1.0 / jnp.sum(jnp.where(g_mask, jnp.exp(lg - g_max), 0.0), axis=-1, keepdims=True)
    lo = LANE_E0 + g_sel * EXPERTS_PER_GROUP
    e_vals = jnp.where((lane >= lo) & (lane < lo + EXPERTS_PER_GROUP), lg, NEG)
    v1, i1 = _first_max(e_vals, lane)
    v2, i2 = _first_max(jnp.where(lane == i1, NEG, e_vals), lane)
    a = jnp.exp(v2 - v1)
    w1 = g_gate / (1.0 + a)
    w2 = w1 * a
    @pl.when(pl.program_id(0) == 0)
    def _():
        cnt_ref[...] = jnp.zeros_like(cnt_ref)

    hit1 = lane == i1
    hit2 = lane == i2
    onehot = jnp.where(hit1 | hit2, 1.0, 0.0)
    before = _dot(tri_ref[...], onehot.astype(BF16)) + cnt_ref[0:1, :]
    r1 = jnp.sum(jnp.where(hit1, before, 0.0), axis=-1, keepdims=True)
    r2 = jnp.sum(jnp.where(hit2, before, 0.0), axis=-1, keepdims=True)
    cnt_ref[...] = cnt_ref[...] + jnp.sum(onehot, axis=0, keepdims=True)
    rec = jnp.zeros((tm, 128), F32)
    for ln, val in ((RT_E, (i1 - LANE_E0).astype(F32)), (RT_E + 1, (i2 - LANE_E0).astype(F32)),
                    (RT_RANK, r1), (RT_RANK + 1, r2), (RT_W, w1), (RT_W + 1, w2)):
        rec = jnp.where(lane == ln, val, rec)
    rt_ref[...] = rec


def _outproj_call(ya, yb, yc, h, w, layer, nw, r_hi, r_lo, r_bias):
    t, d = h.shape
    tm = _row_tile(t, ROW_TILES["out_proj"])
    tri = jnp.asarray(np.tril(np.ones((tm, tm), np.float32), -1), BF16)
    row = lambda i: (i, 0)
    const = lambda shape: pl.BlockSpec(shape, lambda i: (0, 0))
    return pl.pallas_call(
        _outproj_kernel, grid=(t // tm,),
        in_specs=[pl.BlockSpec((tm, HG_WIDTH), row), pl.BlockSpec((tm, ML_WIDTH), row),
                  pl.BlockSpec((tm, SSM_WIDTH), row), pl.BlockSpec((tm, d), row),
                  pl.BlockSpec((None, D_MIX, d), lambda i: (layer, 0, 0)),
                  const((1, d)), const((d, 128)), const((d, 128)), const((1, 128)), const((tm, tm))],
        out_specs=[pl.BlockSpec((tm, d), row), pl.BlockSpec((tm * SLAB, 128), row), pl.BlockSpec((tm, 128), row),
                   const((8, 128))],
        out_shape=[jax.ShapeDtypeStruct((t, d), F32), jax.ShapeDtypeStruct((t * SLAB, 128), U32),
                   jax.ShapeDtypeStruct((t, 128), F32), jax.ShapeDtypeStruct((8, 128), F32)],
        compiler_params=_cparams(("arbitrary",)), name="out_proj_router",
    )(ya, yb, yc, h, w, nw.reshape(1, d), r_hi, r_lo, r_bias, tri)


MOE_RING = 3
WT_FIRST, WT_SLOT, WT_NEXT = 0, 1, 2


TAB_SRC2, TAB_DST_PREV, TAB_DST = 0, 1, 2


def _moe_kernel(be_ref, nu_ref, wt_ref, src01_ref, tab_ref, u_hbm,
                w1_hbm, w3_hbm, w2_hbm, o_hbm, xbuf, ybuf, st1, st3, st2, w1b, w3b, w2b, gsem, ssem, wsem,
                *, layer):
    i = pl.program_id(0)
    bm = MOE_BM
    n_used = nu_ref[0]
    xs = i % MOE_RING
    slot = xs
    other = (i + MOE_RING - 1) % MOE_RING

    def slab(idx):
        return pl.ds(pl.multiple_of(idx * SLAB, SLAB), SLAB)

    def row_in(tok, r, sl):
        return pltpu.make_async_copy(u_hbm.at[slab(tok), :], xbuf.at[sl, slab(r), :], gsem.at[sl])

    def row_out(r, row, sl):
        return pltpu.make_async_copy(ybuf.at[sl, slab(r), :], o_hbm.at[slab(row), :], ssem.at[sl])

    def gather(section, sl):
        def body(g, c):
            for j in range(8):
                r = g * 8 + j
                row_in(src01_ref[0, section * bm + r], r, sl).start(priority=j % 2)
            return c
        lax.fori_loop(0, bm // 8, body, 0)

    def scatter(sl):
        def body(g, c):
            for j in range(8):
                r = g * 8 + j
                row_out(r, tab_ref[0, TAB_DST * bm + r], sl).start(priority=j % 2)
            return c
        lax.fori_loop(0, bm // 8, body, 0)

    def wait_gather(sl):
        pltpu.make_async_copy(u_hbm.at[pl.ds(0, bm * SLAB), :], xbuf.at[sl], gsem.at[sl]).wait()

    def wait_scatter(sl):
        pltpu.make_async_copy(ybuf.at[sl], o_hbm.at[pl.ds(0, bm * SLAB), :], ssem.at[sl]).wait()

    def weight_copies(e, p):
        return [pltpu.make_async_copy(src.at[layer, e], dst.at[p], wsem.at[p])
                for src, dst in ((w1_hbm, st1), (w3_hbm, st3), (w2_hbm, st2))]

    n_real = o_hbm.shape[0] // SLAB - MOE_RING * bm

    @pl.when(i == 0)
    def _():
        for c in weight_copies(be_ref[0], 0):
            c.start()
        gather(0, 0)
        gather(1, 1)
        ybuf[...] = jnp.zeros_like(ybuf)
        for sl in range(2):
            pltpu.make_async_copy(ybuf.at[sl], o_hbm.at[pl.ds((n_real + sl * bm) * SLAB, bm * SLAB), :],
                                  ssem.at[sl]).start()

    @pl.when(i < n_used)
    def _():
        @pl.when(wt_ref[WT_FIRST, i] == 1)
        def _():
            p = wt_ref[WT_SLOT, i]
            nxt = wt_ref[WT_NEXT, i]

            @pl.when(nxt >= 0)
            def _():
                for k, c in enumerate(weight_copies(nxt, 1 - p)):
                    c.start(priority=1 if k < 2 else 0)

            for c in weight_copies(be_ref[i], p):
                c.wait()
            w1b[...] = st1[p].astype(BF16)
            w3b[...] = st3[p].astype(BF16)
            w2b[...] = st2[p].astype(BF16)

        wait_gather(xs)

        xn = (i + 2) % MOE_RING

        def issue(part, parts=4):
            for r in range(part * bm // parts, (part + 1) * bm // parts):
                row_in(tab_ref[0, TAB_SRC2 * bm + r], r, xn).start(priority=0)
                row_out(r, tab_ref[0, TAB_DST_PREV * bm + r], other).start(priority=1)

        f = w1b.shape[1]
        h1 = jnp.zeros((bm, f), F32)
        h3 = jnp.zeros((bm, f), F32)
        for j in range(SLAB):
            lo, hi = _unpack_slab(xbuf, j, bm, lead=(xs,))
            xk = jnp.concatenate([lo, hi], axis=1).astype(BF16)
            h1 = h1 + _dot(xk, w1b[j * 256:(j + 1) * 256, :])
            h3 = h3 + _dot(xk, w3b[j * 256:(j + 1) * 256, :])
            if j % 4 == 3:
                issue(j // 4)
        hid = (_silu(h1) * h3).astype(BF16)
        wait_scatter(slot)
        half = SLAB // 2
        _pack_slab(ybuf, _dot(hid, w2b[:, :half * 256]), lead=(slot,))
        issue(2)
        _pack_slab(ybuf, _dot(hid, w2b[:, half * 256:]), lead=(slot,), j0=half)
        issue(3)

    @pl.when(i == n_used - 1)
    def _():
        scatter(slot)
        for k in range(MOE_RING):
            wait_scatter(k)
        wait_gather((i + 1) % MOE_RING)
        wait_gather((i + 2) % MOE_RING)


def _moe_call(block_e, n_used, w_table, src_tok, dst_row, u, w1, w3, w2, layer, n_out_rows):
    d, f = w1.shape[-2:]
    assert d == SLAB * 256
    bm = MOE_BM
    nb = block_e.shape[0]
    src_tok = src_tok.reshape(nb, bm)
    dst_row = dst_row.reshape(nb, bm)
    lead = (n_out_rows - bm + jnp.arange(bm, dtype=jnp.int32)).reshape(1, bm)
    src_ahead = jnp.concatenate([src_tok[2:], src_tok[-1:], src_tok[-1:]], axis=0)
    dst_prev = jnp.concatenate([lead, dst_row[:-1]], axis=0)
    table = jnp.concatenate([src_ahead, dst_prev, dst_row], axis=1).reshape(nb, 1, 3 * bm)
    src01 = src_tok[:2].reshape(1, 1, 2 * bm)
    idx = lambda w, fn: pl.BlockSpec((None, 1, w), fn, memory_space=pltpu.SMEM)
    hbm = pl.BlockSpec(memory_space=pl.ANY)
    ring = pltpu.VMEM((MOE_RING, bm * SLAB, 128), U32)
    return pl.pallas_call(
        functools.partial(_moe_kernel, layer=layer),
        grid_spec=pltpu.PrefetchScalarGridSpec(
            num_scalar_prefetch=3, grid=(nb,),
            in_specs=[idx(2 * bm, lambda i, *_: (0, 0, 0)), idx(3 * bm, lambda i, *_: (i, 0, 0)),
                      hbm, hbm, hbm, hbm],
            out_specs=hbm,
            scratch_shapes=[ring, ring,
                            pltpu.VMEM((2, d, f), F32), pltpu.VMEM((2, d, f), F32), pltpu.VMEM((2, f, d), F32),
                            pltpu.VMEM((d, f), BF16), pltpu.VMEM((d, f), BF16), pltpu.VMEM((f, d), BF16),
                            pltpu.SemaphoreType.DMA((MOE_RING,)), pltpu.SemaphoreType.DMA((MOE_RING,)),
                            pltpu.SemaphoreType.DMA((2,))]),
        out_shape=jax.ShapeDtypeStruct((n_out_rows * SLAB, 128), U32),
        compiler_params=_cparams(("arbitrary",)), name="moe_ffn",
    )(block_e, n_used, w_table, src01, table, u, w1, w3, w2)


def _route_tables(rt, cnt, t):
    bm = MOE_BM
    tk = t * TOP_K
    e = rt[:, RT_E:RT_E + TOP_K].astype(jnp.int32)
    rank = rt[:, RT_RANK:RT_RANK + TOP_K].astype(jnp.int32)
    wts = rt[:, RT_W:RT_W + TOP_K]
    counts = cnt[0, LANE_E0:LANE_E0 + N_EXPERTS].astype(jnp.int32)
    padded = (counts + bm - 1) // bm * bm
    pad_ends = jnp.cumsum(padded)
    pad_starts = pad_ends - padded
    e_ids = jnp.arange(N_EXPERTS, dtype=jnp.int32)

    def lookup(table, idx):
        return jnp.sum(jnp.where(idx[..., None] == e_ids, table, 0), axis=-1)

    dest = (lookup(pad_starts, e) + rank).reshape(-1)
    n_blocks = -(-tk // bm) + N_EXPERTS
    n_rows = n_blocks * bm
    inv = jnp.full((n_rows,), -1, jnp.int32).at[dest].set(jnp.arange(tk, dtype=jnp.int32))
    tok, slot = inv // TOP_K, inv % TOP_K
    src_tok = jnp.where(inv >= 0, tok, 0)
    dst_row = jnp.where(inv >= 0, slot * t + tok, tk + (jnp.arange(n_rows, dtype=jnp.int32) % (MOE_RING * bm)))
    block_ids = jnp.arange(n_blocks, dtype=jnp.int32)
    block_e = jnp.minimum(jnp.sum(pad_ends[None, :] <= block_ids[:, None] * bm, axis=1), N_EXPERTS - 1).astype(jnp.int32)
    n_used = (pad_ends[-1] // bm).astype(jnp.int32).reshape(1)
    has_rows = counts > 0
    at_or_after = lax.cummin(jnp.where(has_rows, e_ids, N_EXPERTS)[::-1])[::-1]
    next_e = jnp.concatenate([at_or_after[1:], jnp.full((1,), N_EXPERTS, jnp.int32)])
    next_e = jnp.where(next_e >= N_EXPERTS, -1, next_e)
    stage = (jnp.cumsum(has_rows.astype(jnp.int32)) - 1) % 2
    first = jnp.concatenate([jnp.ones((1,), bool), block_e[1:] != block_e[:-1]]) & (block_ids < n_used[0])
    w_table = jnp.stack([first.astype(jnp.int32), lookup(stage, block_e), lookup(next_e, block_e)]).astype(jnp.int32)
    return block_e, n_used, w_table, src_tok, dst_row, wts, tk + MOE_RING * bm


def _lane_row(pairs):
    row = jnp.zeros((128,), F32)
    for lane0, vals in pairs:
        row = row.at[lane0:lane0 + vals.shape[0]].set(vals.astype(F32))
    return row


def kernel(x, meta_tokens, hg_lb_logits, norm_mix_w, w_in, hg_norm_w, ml_b_i, ml_b_f, ml_norm_w,
           ssm_conv_w, ssm_conv_b, ssm_dt_bias, ssm_a_log, ssm_d, ssm_norm_w, w_out, norm_ffn_w,
           moe_w_group, moe_b_group, moe_w_router, moe_b_router, moe_w1, moe_w3, moe_w2, final_norm_w):
    batch, seq, d = x.shape
    depth = w_in.shape[0]
    p = LEAD_PAD + N_META + seq
    t = batch * p
    meta = jnp.broadcast_to(meta_tokens.astype(x.dtype)[None], (batch, N_META, d))
    h = jnp.concatenate([jnp.zeros((batch, LEAD_PAD, d), x.dtype), meta, x], axis=1).reshape(t, d)

    lb_w = jax.nn.softmax(hg_lb_logits.astype(F32), axis=0)
    lower_bounds = jnp.cumsum(lb_w, axis=0) - lb_w[0]

    o_mq, o_mv, o_mi, o_sz = HG_KEY * 2 + HG_WIDTH * 2, 2560, 3584, 3592
    o_sx = o_sz + SSM_WIDTH
    o_dt = o_sx + SSM_WIDTH + 2 * SSM_BC
    w_perm = jnp.concatenate([
        w_in[..., :o_mq], w_in[..., o_sz:o_sx], w_in[..., o_sx:o_dt], w_in[..., o_mv:o_mi], w_in[..., o_mq:o_mv],
        w_in[..., o_mi:o_sz], w_in[..., o_dt:],
        jnp.zeros((depth, d, N_PROJ - C_SMALL - 2 * ML_HEADS - SSM_HEADS), w_in.dtype)], axis=-1).astype(BF16)
    w_out_b = w_out.astype(BF16)

    contrib = wts = None
    for layer in range(depth):
        lb = lower_bounds[layer]
        lbf = jnp.maximum(lb, LB_FLOOR)
        hg_par = jnp.zeros((8, HG_KEY), F32).at[0].set(lbf).at[1].set(1.0 - lb).at[2].set(lbf - lb).at[3].set(hg_norm_w[layer])
        ml_par = jnp.zeros((8, 128), F32).at[0].set(_lane_row([(LANE_MI, ml_b_i[layer]), (LANE_MF, ml_b_f[layer])]))
        ss_par = jnp.zeros((8, 128), F32).at[0].set(_lane_row([(LANE_DT, ssm_dt_bias[layer])]))
        ss_par = ss_par.at[1].set(_lane_row([(LANE_DT, ssm_a_log[layer])]))
        dskip = jnp.repeat(ssm_d[layer].astype(F32), SSM_HEADDIM).reshape(1, SSM_WIDTH)
        w_r = jnp.concatenate([moe_w_group[layer],
                               moe_w_router[layer].transpose(1, 0, 2).reshape(d, N_EXPERTS),
                               jnp.zeros((d, 128 - N_GROUPS_MOE - N_EXPERTS), F32)], axis=1)
        r_hi = w_r.astype(BF16)
        r_lo = (w_r - r_hi.astype(F32)).astype(BF16)

        if layer == 0:
            (u,) = _norm_call(h, norm_mix_w[layer], write_h=False, u_dtype=BF16)
        else:
            h, u = _norm_call(h, norm_mix_w[layer], contrib, wts, write_h=True, u_dtype=BF16)
        proj = _inproj_call(u, w_perm, layer)
        ya = _hgrn2_call(proj, hg_par, batch, p)
        yb = _mlstm_call(proj, ml_par, ml_norm_w[layer].reshape(1, ML_WIDTH), batch, p)
        yc = _ssd_call(proj, ssm_conv_w[layer], ssm_conv_b[layer].reshape(1, -1), ss_par, dskip,
                       ssm_norm_w[layer].reshape(1, SSM_WIDTH), batch, p)
        r_bias = _lane_row([(0, moe_b_group[layer]), (LANE_E0, moe_b_router[layer].reshape(-1))]).reshape(1, 128)
        h, u_ffn, rt, cnt = _outproj_call(ya, yb, yc, h, w_out_b, layer, norm_ffn_w[layer], r_hi, r_lo, r_bias)
        block_e, n_used, w_table, src_tok, dst_row, wts, n_out_rows = _route_tables(rt, cnt, t)
        contrib = _moe_call(block_e, n_used, w_table, src_tok, dst_row, u_ffn, moe_w1, moe_w3, moe_w2, layer,
                            n_out_rows)
    return _final_call(h, contrib, wts, final_norm_w, batch, p)
```
